```python
import math
import jax, jax.numpy as jnp
from jax import lax
import numpy as np

D_MODEL = 1024
BATCH = 4
SEQ = 4096
DEPTH = 2
DEC_BATCH = 128
DEC_SEQ = 1
PAST_LEN = 2048
PAGE_SIZE = 128

N_EVEN = (DEPTH + 1) // 2
N_ODD = DEPTH // 2
MIX_WIDTH = D_MODEL
A_GROUPS = 4
A_CH = MIX_WIDTH // 2 // A_GROUPS
CHUNK = 128
B_HEADS = 8
B_DH = MIX_WIDTH // 2 // B_HEADS
MOBA_BLOCK = 256
MOBA_TOPK = 3
C_HEADS = 8
C_DH = MIX_WIDTH // C_HEADS
C_KV_HEADS = 2
IDX_HEADS = 4
IDX_DH = 64
DSA_TOPK = 256
ROPE_THETA = 500000.0
ROPE_FRAC = 4
D_FF = 2816
CONV_W = 3
EPS = 1e-6
QBLOCK = 128

A_WIDTH = A_GROUPS * A_CH
B_WIDTH = B_HEADS * B_DH
IN0_COLS = 2 * A_WIDTH + 3 * B_WIDTH
C_Q = C_HEADS * C_DH
C_KV = C_KV_HEADS * C_DH
IDX_Q = IDX_HEADS * IDX_DH
IN1_COLS = C_Q + 2 * C_KV + IDX_Q + IDX_DH + IDX_HEADS

kernel_name = 'hybrid_gmlp_moba_dsa_convffn_step'

F32 = jnp.float32


def rmsnorm(x, g):
    xf = x.astype(F32)
    y = xf * lax.rsqrt(jnp.mean(xf * xf, axis=-1, keepdims=True) + EPS)
    return (y * g.astype(F32)).astype(x.dtype)


def rope_partial(x, pos):
    dh = x.shape[-1]
    rd = dh // ROPE_FRAC
    half = rd // 2
    inv = ROPE_THETA ** (-jnp.arange(half, dtype=F32) * (2.0 / rd))
    ang = pos.astype(F32)[:, None] * inv[None, :]
    cos = jnp.cos(ang)[:, None, :]
    sin = jnp.sin(ang)[:, None, :]
    x1 = x[..., :half].astype(F32)
    x2 = x[..., half:rd].astype(F32)
    r1 = (x1 * cos - x2 * sin).astype(x.dtype)
    r2 = (x2 * cos + x1 * sin).astype(x.dtype)
    return jnp.concatenate([r1, r2, x[..., rd:]], axis=-1)


def gather_pages(cache, page_table):
    g = cache[page_table]
    return g.reshape(g.shape[0], g.shape[1] * g.shape[2], *g.shape[3:])


def blocked_queries(fn, qs, pos):
    nb = pos.shape[0] // QBLOCK

    def split(a):
        return jnp.moveaxis(a.reshape(a.shape[0], nb, QBLOCK, *a.shape[2:]), 1, 0)

    out = lax.map(lambda args: fn(*args), tuple(split(a) for a in qs) + (pos.reshape(nb, QBLOCK),))
    out = jnp.moveaxis(out, 0, 1)
    return out.reshape(out.shape[0], nb * QBLOCK, *out.shape[3:])


def chunk_mlp(u, v, w_s, b_s):
    B, T, G, C = u.shape
    n = min(T, CHUNK)
    nc = T // n
    w = jnp.where(jnp.tril(jnp.ones((n, n), dtype=bool))[None], w_s[:, :n, :n], 0.0)
    vc = v.reshape(B, nc, n, G, C)
    mixed = jnp.einsum('gij,bnjgc->bnigc', w.astype(v.dtype), vc) + b_s[:, :n].T[None, None, :, :, None]
    return u * mixed.reshape(B, T, G, C)


def moba_attend(q, k, v, qpos):
    B, L, H, dh = k.shape
    Tq = q.shape[1]
    nblk = -(-L // MOBA_BLOCK)
    Lp = nblk * MOBA_BLOCK
    if Lp != L:
        pad = ((0, 0), (0, Lp - L), (0, 0), (0, 0))
        k = jnp.pad(k, pad)
        v = jnp.pad(v, pad)
    kb = k.reshape(B, nblk, MOBA_BLOCK, H, dh)
    vb = v.reshape(B, nblk, MOBA_BLOCK, H, dh)
    kmean = jnp.mean(kb.astype(F32), axis=2)
    own = qpos // MOBA_BLOCK
    gate = jnp.einsum('bthd,bnhd->bthn', q.astype(F32), kmean)
    past = jnp.arange(nblk)[None, :] < own[:, None]
    gate = jnp.where(past[None, :, None, :], gate, -jnp.inf)
    ksel = min(MOBA_TOPK, nblk)
    _, idx = lax.top_k(gate, ksel)
    own_b = jnp.broadcast_to(own[None, :, None, None], (B, Tq, H, 1)).astype(idx.dtype)
    blk = jnp.concatenate([idx, own_b], axis=-1)
    bi = jnp.arange(B)[:, None, None, None]
    hi = jnp.arange(H)[None, None, :, None]
    kg = kb[bi, blk, :, hi, :]
    vg = vb[bi, blk, :, hi, :]
    s = jnp.einsum('bthd,bthnsd->bthns', q, kg, preferred_element_type=F32) * (dh ** -0.5)
    keypos = blk[..., None] * MOBA_BLOCK + jnp.arange(MOBA_BLOCK)
    is_own = (jnp.arange(ksel + 1) == ksel)[:, None]
    allowed = jnp.where(is_own, keypos <= qpos[None, :, None, None, None],
                        (blk < own[None, :, None, None])[..., None])
    s = jnp.where(allowed, s, -jnp.inf)
    p = jax.nn.softmax(s.reshape(B, Tq, H, -1), axis=-1).reshape(s.shape)
    return jnp.einsum('bthns,bthnsd->bthd', p.astype(v.dtype), vg)


def dsa_attend(q, qi, wi, k, v, ki, qpos, n_keep):
    B, Tq, H, dh = q.shape
    L, KV = k.shape[1], k.shape[2]
    G = H // KV
    di = qi.shape[-1]
    sc = jnp.einsum('bthd,bsd->bths', qi.astype(F32), ki.astype(F32)) * (di ** -0.5)
    score = jnp.einsum('bths,bth->bts', jax.nn.relu(sc), wi.astype(F32))
    admissible = jnp.arange(L)[None, :] <= qpos[:, None]
    score = jnp.where(admissible[None], score, -jnp.inf)
    _, idx = lax.top_k(score, n_keep)
    bi = jnp.arange(B)[:, None, None]
    kg = k[bi, idx]
    vg = v[bi, idx]
    qg = q.reshape(B, Tq, KV, G, dh)
    s = jnp.einsum('btkgd,btnkd->btkgn', qg, kg, preferred_element_type=F32) * (dh ** -0.5)
    ok = idx <= qpos[None, :, None]
    s = jnp.where(ok[:, :, None, None, :], s, -jnp.inf)
    p = jax.nn.softmax(s, axis=-1)
    o = jnp.einsum('btkgn,btnkd->btkgd', p.astype(v.dtype), vg)
    return o.reshape(B, Tq, H, dh)


def even_mixer(h, pos, past_k, past_v, w_in, g_v, w_s, b_s, w_out):
    B, T, _ = h.shape
    z = h @ w_in
    zu, zv, zq, zk, zvb = jnp.split(
        z, [A_WIDTH, 2 * A_WIDTH, 2 * A_WIDTH + B_WIDTH, 2 * A_WIDTH + 2 * B_WIDTH], axis=-1)
    u = jax.nn.gelu(zu).reshape(B, T, A_GROUPS, A_CH)
    va = rmsnorm(jax.nn.gelu(zv).reshape(B, T, A_GROUPS, A_CH), g_v)
    a_out = chunk_mlp(u, va, w_s, b_s)
    q = rope_partial(zq.reshape(B, T, B_HEADS, B_DH), pos)
    k = rope_partial(zk.reshape(B, T, B_HEADS, B_DH), pos)
    vb = zvb.reshape(B, T, B_HEADS, B_DH)
    if past_k is None:
        b_out = blocked_queries(lambda qq, pp: moba_attend(qq, k, vb, pp), (q,), pos)
    else:
        kf = jnp.concatenate([past_k.astype(k.dtype), k], axis=1)
        vf = jnp.concatenate([past_v.astype(vb.dtype), vb], axis=1)
        b_out = moba_attend(q, kf, vf, pos)
    y = jnp.concatenate([a_out.reshape(B, T, A_WIDTH), b_out.reshape(B, T, B_WIDTH)], axis=-1) @ w_out
    return y, k, vb, va


def odd_mixer(h, pos, past, w_in, w_out, n_keep):
    B, T, _ = h.shape
    z = h @ w_in
    o1 = C_Q
    o2 = o1 + C_KV
    o3 = o2 + C_KV
    o4 = o3 + IDX_Q
    o5 = o4 + IDX_DH
    zq, zk, zv, zqi, zki, zw = jnp.split(z, [o1, o2, o3, o4, o5], axis=-1)
    q = rope_partial(zq.reshape(B, T, C_HEADS, C_DH), pos)
    k = rope_partial(zk.reshape(B, T, C_KV_HEADS, C_DH), pos)
    v = zv.reshape(B, T, C_KV_HEADS, C_DH)
    qi = rope_partial(zqi.reshape(B, T, IDX_HEADS, IDX_DH), pos)
    ki = rope_partial(zki.reshape(B, T, 1, IDX_DH), pos)[:, :, 0]
    wi = zw * (IDX_HEADS ** -0.5)
    if past is None:
        o = blocked_queries(lambda qq, qqi, ww, pp: dsa_attend(qq, qqi, ww, k, v, ki, pp, n_keep),
                            (q, qi, wi), pos)
    else:
        pk, pv, pki = past
        kf = jnp.concatenate([pk.astype(k.dtype), k], axis=1)
        vf = jnp.concatenate([pv.astype(v.dtype), v], axis=1)
        kif = jnp.concatenate([pki.astype(ki.dtype), ki], axis=1)
        o = dsa_attend(q, qi, wi, kf, vf, kif, pos, n_keep)
    y = o.reshape(B, T, C_Q) @ w_out
    return y, k, v, ki


def conv_ffn(h, prev, w_up, w_conv, b_conv, w_down):
    up = h @ w_up
    T = up.shape[1]
    ext = jnp.concatenate([prev.astype(up.dtype), up], axis=1)
    conv = b_conv
    for j in range(CONV_W):
        conv = conv + ext[:, j:j + T] * w_conv[j]
    a, g = jnp.split(conv, 2, axis=-1)
    out = (jax.nn.silu(g) * a) @ w_down
    return out, ext[:, ext.shape[1] - (CONV_W - 1):]


def setup_inputs(seed: int = 0) -> dict:
    key = jax.random.key(seed)
    ks = jax.random.split(key, 32)
    n_pages = PAST_LEN // PAGE_SIZE
    n_used = DEC_BATCH * n_pages
    n_pool = n_used + n_used // 4

    def nrm(k, shape, s):
        return jax.random.normal(k, shape, F32) * s

    page_table = jax.random.permutation(ks[0], n_pool)[:n_used].reshape(DEC_BATCH, n_pages).astype(jnp.int32)
    return {
        'x_prompt': nrm(ks[1], (BATCH, SEQ, D_MODEL), 1.0),
        'x_sample': nrm(ks[2], (DEC_BATCH, DEC_SEQ, D_MODEL), 1.0),
        'cache_moba_k': nrm(ks[3], (N_EVEN, n_pool, PAGE_SIZE, B_HEADS, B_DH), 1.0),
        'cache_moba_v': nrm(ks[4], (N_EVEN, n_pool, PAGE_SIZE, B_HEADS, B_DH), 1.0),
        'cache_dsa_k': nrm(ks[5], (N_ODD, n_pool, PAGE_SIZE, C_KV_HEADS, C_DH), 1.0),
        'cache_dsa_v': nrm(ks[6], (N_ODD, n_pool, PAGE_SIZE, C_KV_HEADS, C_DH), 1.0),
        'cache_dsa_kidx': nrm(ks[7], (N_ODD, n_pool, PAGE_SIZE, IDX_DH), 1.0),
        'state_ffn_conv': nrm(ks[8], (DEPTH, DEC_BATCH, CONV_W - 1, 2 * D_FF), 1.0),
        'page_table': page_table,
        'w_in_even': nrm(ks[9], (N_EVEN, D_MODEL, IN0_COLS), D_MODEL ** -0.5),
        'g_sgu': 1.0 + nrm(ks[10], (N_EVEN, A_GROUPS, A_CH), 0.05),
        'w_sgu': nrm(ks[11], (N_EVEN, A_GROUPS, CHUNK, CHUNK), CHUNK ** -0.5),
        'b_sgu': 1.0 + nrm(ks[12], (N_EVEN, A_GROUPS, CHUNK), 0.05),
        'w_out_even': nrm(ks[13], (N_EVEN, MIX_WIDTH, D_MODEL), MIX_WIDTH ** -0.5),
        'w_in_odd': nrm(ks[14], (N_ODD, D_MODEL, IN1_COLS), D_MODEL ** -0.5),
        'w_out_odd': nrm(ks[15], (N_ODD, C_Q, D_MODEL), C_Q ** -0.5),
        'g_mix': 1.0 + nrm(ks[16], (DEPTH, D_MODEL), 0.05),
        'g_ffn': 1.0 + nrm(ks[17], (DEPTH, D_MODEL), 0.05),
        'w_up': nrm(ks[18], (DEPTH, D_MODEL, 2 * D_FF), D_MODEL ** -0.5),
        'w_conv': nrm(ks[19], (DEPTH, CONV_W, 2 * D_FF), CONV_W ** -0.5),
        'b_conv': nrm(ks[20], (DEPTH, 2 * D_FF), 0.02),
        'w_down': nrm(ks[21], (DEPTH, D_FF, D_MODEL), D_FF ** -0.5),
        'g_final': 1.0 + nrm(ks[22], (D_MODEL,), 0.05),
    }


def reference(x_prompt, x_sample, cache_moba_k, cache_moba_v, cache_dsa_k, cache_dsa_v, cache_dsa_kidx,
              state_ffn_conv, page_table, w_in_even, g_sgu, w_sgu, b_sgu, w_out_even, w_in_odd, w_out_odd,
              g_mix, g_ffn, w_up, w_conv, b_conv, w_down, g_final):
    seq = x_prompt.shape[1]
    dec_seq = x_sample.shape[1]
    past_len = page_table.shape[1] * cache_moba_k.shape[2]
    pos_p = jnp.arange(seq, dtype=jnp.int32)
    pos_s = past_len + jnp.arange(dec_seq, dtype=jnp.int32)
    keep_p = min(DSA_TOPK, seq // 4)
    keep_s = min(DSA_TOPK, (past_len + dec_seq) // 4)
    xp, xs = x_prompt, x_sample
    moba_kp, moba_vp, moba_ks, moba_vs, sgu_vs = [], [], [], [], []
    dsa_kp, dsa_vp, dsa_kip, dsa_ks, dsa_vs, dsa_kis = [], [], [], [], [], []
    conv_p, conv_s = [], []
    for layer in range(DEPTH):
        if layer % 2 == 0:
            e = layer // 2
            wts = (w_in_even[e], g_sgu[e], w_sgu[e], b_sgu[e], w_out_even[e])
            yp, kp, vp, _ = even_mixer(rmsnorm(xp, g_mix[layer]), pos_p, None, None, *wts)
            ys, ks_, vs_, va_s = even_mixer(rmsnorm(xs, g_mix[layer]), pos_s,
                                            gather_pages(cache_moba_k[e], page_table),
                                            gather_pages(cache_moba_v[e], page_table), *wts)
            moba_kp.append(kp)
            moba_vp.append(vp)
            moba_ks.append(ks_)
            moba_vs.append(vs_)
            sgu_vs.append(va_s)
        else:
            o = layer // 2
            yp, kp, vp, kip = odd_mixer(rmsnorm(xp, g_mix[layer]), pos_p, None,
                                        w_in_odd[o], w_out_odd[o], keep_p)
            past = (gather_pages(cache_dsa_k[o], page_table), gather_pages(cache_dsa_v[o], page_table),
                    gather_pages(cache_dsa_kidx[o], page_table))
            ys, ks_, vs_, kis = odd_mixer(rmsnorm(xs, g_mix[layer]), pos_s, past,
                                          w_in_odd[o], w_out_odd[o], keep_s)
            dsa_kp.append(kp)
            dsa_vp.append(vp)
            dsa_kip.append(kip)
            dsa_ks.append(ks_)
            dsa_vs.append(vs_)
            dsa_kis.append(kis)
        xp = xp + yp
        xs = xs + ys
        zeros_prev = jnp.zeros((xp.shape[0], CONV_W - 1, 2 * D_FF), xp.dtype)
        fp, cp = conv_ffn(rmsnorm(xp, g_ffn[layer]), zeros_prev, w_up[layer], w_conv[layer], b_conv[layer], w_down[layer])
        fs, cs = conv_ffn(rmsnorm(xs, g_ffn[layer]), state_ffn_conv[layer], w_up[layer], w_conv[layer], b_conv[layer], w_down[layer])
        xp = xp + fp
        xs = xs + fs
        conv_p.append(cp)
        conv_s.append(cs)
    y_prompt = rmsnorm(xp, g_final)
    y_sample = rmsnorm(xs, g_final)
    return (y_prompt, y_sample,
            jnp.stack(moba_kp), jnp.stack(moba_vp), jnp.stack(moba_ks), jnp.stack(moba_vs), jnp.stack(sgu_vs),
            jnp.stack(dsa_kp), jnp.stack(dsa_vp), jnp.stack(dsa_kip),
            jnp.stack(dsa_ks), jnp.stack(dsa_vs), jnp.stack(dsa_kis),
            jnp.stack(conv_p), jnp.stack(conv_s))
```

```python
import functools

import numpy as np
import jax
import jax.numpy as jnp
from jax import lax
from jax.experimental import pallas as pl
from jax.experimental.pallas import tpu as pltpu

F32 = jnp.float32
BF16 = jnp.bfloat16

A_GROUPS = 4
CHUNK = 128
B_HEADS = 8
MOBA_BLOCK = 256
MOBA_TOPK = 3
C_HEADS = 8
C_KV_HEADS = 2
IDX_HEADS = 4
DSA_TOPK = 256
ROPE_THETA = 500000.0
ROPE_FRAC = 4
CONV_W = 3
EPS = 1e-6

LANE = 128
SUBLANE_BF16 = 16
VMEM_LIMIT_BYTES = 56 * 1024 * 1024
NEG_BIG = -1e30


def _params(*sem):
    return pltpu.CompilerParams(dimension_semantics=sem,
                                vmem_limit_bytes=VMEM_LIMIT_BYTES)


def _full(shape):
    nd = len(shape)
    return pl.BlockSpec(shape, lambda *_: (0,) * nd)


def _rmsnorm(x, g):
    ms = jnp.mean(x * x, axis=-1, keepdims=True)
    return x * lax.rsqrt(ms + EPS) * g


def _rope_apply(x, c, s1, s2, half):
    parts = []
    for j in range(x.shape[1] // LANE):
        xs = x[:, j * LANE:(j + 1) * LANE]
        parts.append(xs * c + pltpu.roll(xs, LANE - half, 1) * s1
                     + pltpu.roll(xs, half, 1) * s2)
    return parts[0] if len(parts) == 1 else jnp.concatenate(parts, axis=1)


def _head_of(index, dh):
    assert dh & (dh - 1) == 0
    return lax.shift_right_logical(index, dh.bit_length() - 1)


def _dot(a, b):
    return jnp.dot(a, b, preferred_element_type=F32)


def _dot_nt(a, b):
    return lax.dot_general(a, b, (((1,), (1,)), ((), ())),
                           preferred_element_type=F32)


def _rope_rows(dh):
    rd = dh // ROPE_FRAC
    half = rd // 2
    inv = ROPE_THETA ** (-jnp.arange(half, dtype=F32) * (2.0 / rd))
    r = np.arange(LANE) % dh
    j = np.where(r < half, r, np.where(r < rd, r - half, 0))
    inv_lane = jnp.where(jnp.asarray(r < rd), inv[j], 0.0).reshape(1, LANE)
    s1 = jnp.asarray(np.where(r < half, -1.0, 0.0), F32).reshape(1, LANE)
    s2 = jnp.asarray(np.where((r >= half) & (r < rd), 1.0, 0.0), F32).reshape(1, LANE)
    return inv_lane, s1, s2, half


def _rope_table_kernel(inv_ref, s1_ref, s2_ref, c_ref, a_ref, b_ref, *, base):
    rows = c_ref.shape[0]
    pos = lax.broadcasted_iota(jnp.int32, (rows, LANE), 0) + (base + pl.program_id(0) * rows)
    ang = pos.astype(F32) * inv_ref[...]
    s = jnp.sin(ang)
    c_ref[...] = jnp.cos(ang)
    a_ref[...] = s * s1_ref[...]
    b_ref[...] = s * s2_ref[...]


def _rope_tables(dh, n_pos, base):
    inv_lane, s1, s2, half = _rope_rows(dh)
    rows = min(n_pos, 512)
    assert n_pos % rows == 0
    out = jax.ShapeDtypeStruct((n_pos, LANE), F32)
    spec = pl.BlockSpec((rows, LANE), lambda i: (i, 0))
    c, a, b = pl.pallas_call(
        functools.partial(_rope_table_kernel, base=base),
        grid=(n_pos // rows,),
        in_specs=[_full((1, LANE))] * 3,
        out_specs=[spec] * 3,
        out_shape=[out] * 3,
        compiler_params=_params("arbitrary"),
        name="rope_tables",
    )(inv_lane, s1, s2)
    return (c, a, b), half


def _sgu_values(h, w_ref, gs_ref, aw):
    u = jax.nn.gelu(_dot(h, w_ref[:, 0:aw]))
    zv = jax.nn.gelu(_dot(h, w_ref[:, aw:2 * aw]))
    vs = []
    for g in range(A_GROUPS):
        vg = zv[:, g * LANE:(g + 1) * LANE]
        vs.append(_rmsnorm(vg, gs_ref[:, g * LANE:(g + 1) * LANE]))
    return u, vs


def _even_in_prompt_kernel(x_ref, g_ref, w_ref, c_ref, s1_ref, s2_ref, gs_ref, ws_ref, bs_ref,
                           a_ref, q_ref, k_ref, v_ref, kb_ref, vt_ref, km_ref, *, half, scale):
    tm = x_ref.shape[0]
    aw = A_GROUPS * LANE
    bw = q_ref.shape[1]
    h = _rmsnorm(x_ref[...], g_ref[...]).astype(BF16)
    u, vs = _sgu_values(h, w_ref, gs_ref, aw)
    tri = (lax.broadcasted_iota(jnp.int32, (CHUNK, CHUNK), 0)
           >= lax.broadcasted_iota(jnp.int32, (CHUNK, CHUNK), 1))
    for g in range(A_GROUPS):
        wg = jnp.where(tri, ws_ref[g], 0.0).astype(BF16)
        for cc in range(tm // CHUNK):
            rows = slice(cc * CHUNK, (cc + 1) * CHUNK)
            mixed = _dot(wg, vs[g][rows].astype(BF16)) + bs_ref[g]
            a_ref[rows, g * LANE:(g + 1) * LANE] = (u[rows, g * LANE:(g + 1) * LANE] * mixed).astype(BF16)
    c, s1, s2 = c_ref[...], s1_ref[...], s2_ref[...]
    q = _rope_apply(_dot(h, w_ref[:, 2 * aw:2 * aw + bw]), c, s1, s2, half)
    q_ref[...] = (q * scale).astype(BF16)
    k = _rope_apply(_dot(h, w_ref[:, 2 * aw + bw:2 * aw + 2 * bw]), c, s1, s2, half)
    k_ref[...] = k
    kb_ref[...] = k.astype(BF16)
    km_ref[0] = jnp.sum(k, axis=0, keepdims=True) * (1.0 / tm)
    v = _dot(h, w_ref[:, 2 * aw + 2 * bw:2 * aw + 3 * bw])
    v_ref[...] = v
    vt_ref[...] = v.T.astype(BF16)


def _even_in_decode_kernel(x_ref, g_ref, w_ref, c_ref, s1_ref, s2_ref, gs_ref, w00_ref, b0_ref,
                           a_ref, q_ref, k_ref, v_ref, va_ref, *, half, scale):
    aw = A_GROUPS * LANE
    bw = q_ref.shape[1]
    h = _rmsnorm(x_ref[...], g_ref[...]).astype(BF16)
    u, vs = _sgu_values(h, w_ref, gs_ref, aw)
    va = jnp.concatenate(vs, axis=1)
    va_ref[...] = va
    a_ref[...] = (u * (w00_ref[...] * va + b0_ref[...])).astype(BF16)
    c, s1, s2 = c_ref[...], s1_ref[...], s2_ref[...]
    q = _rope_apply(_dot(h, w_ref[:, 2 * aw:2 * aw + bw]), c, s1, s2, half)
    q_ref[...] = (q * scale).astype(BF16)
    k_ref[...] = _rope_apply(_dot(h, w_ref[:, 2 * aw + bw:2 * aw + 2 * bw]), c, s1, s2, half)
    v_ref[...] = _dot(h, w_ref[:, 2 * aw + 2 * bw:2 * aw + 3 * bw])


def _even_in_prompt(x, seq, g_mix, w_in, tables, half, g_sgu, w_sgu, b_sgu):
    n, d = x.shape
    tm = MOBA_BLOCK
    bw = (w_in.shape[1] - 2 * A_GROUPS * LANE) // 3
    dh = bw // B_HEADS
    tiles_per_seq = seq // tm
    row = lambda w, dt: (pl.BlockSpec((tm, w), lambda i: (i, 0)), jax.ShapeDtypeStruct((n, w), dt))
    tab = pl.BlockSpec((tm, LANE), lambda i: (i % tiles_per_seq, 0))
    outs = [row(A_GROUPS * LANE, BF16), row(bw, BF16), row(bw, F32), row(bw, F32), row(bw, BF16),
            (pl.BlockSpec((bw, tm), lambda i: (0, i)), jax.ShapeDtypeStruct((bw, n), BF16)),
            (pl.BlockSpec((1, 1, bw), lambda i: (i, 0, 0)), jax.ShapeDtypeStruct((n // tm, 1, bw), F32))]
    bs = jnp.broadcast_to(b_sgu[:, :, None], (A_GROUPS, CHUNK, LANE))
    return pl.pallas_call(
        functools.partial(_even_in_prompt_kernel, half=half, scale=dh ** -0.5),
        grid=(n // tm,),
        in_specs=[pl.BlockSpec((tm, d), lambda i: (i, 0)), _full((1, d)), _full(w_in.shape),
                  tab, tab, tab, _full((1, A_GROUPS * LANE)), _full(w_sgu.shape), _full(bs.shape)],
        out_specs=[o[0] for o in outs],
        out_shape=[o[1] for o in outs],
        compiler_params=_params("arbitrary"),
        name="even_in_prompt",
    )(x, g_mix.reshape(1, d), w_in, *tables, g_sgu.reshape(1, -1), w_sgu, bs)


def _even_in_decode(x, g_mix, w_in, tables, half, g_sgu, w_sgu, b_sgu):
    n, d = x.shape
    bw = (w_in.shape[1] - 2 * A_GROUPS * LANE) // 3
    dh = bw // B_HEADS
    aw = A_GROUPS * LANE
    w00 = jnp.repeat(w_sgu[:, 0, 0], LANE).reshape(1, aw)
    b0 = jnp.repeat(b_sgu[:, 0], LANE).reshape(1, aw)
    shapes = [(aw, BF16), (bw, BF16), (bw, F32), (bw, F32), (aw, F32)]
    return pl.pallas_call(
        functools.partial(_even_in_decode_kernel, half=half, scale=dh ** -0.5),
        grid=(1,),
        in_specs=[_full((n, d)), _full((1, d)), _full(w_in.shape)] + [_full((1, LANE))] * 3
                 + [_full((1, aw))] * 3,
        out_specs=[_full((n, w)) for w, _ in shapes],
        out_shape=[jax.ShapeDtypeStruct((n, w), dt) for w, dt in shapes],
        compiler_params=_params("arbitrary"),
        name="even_in_decode",
    )(x, g_mix.reshape(1, d), w_in, *tables, g_sgu.reshape(1, -1), w00, b0)


def _moba_prompt_kernel(q_ref, k_ref, vt_ref, km_ref, o_ref, sel_ref, *, dh):
    blk = MOBA_BLOCK
    tq = q_ref.shape[0]
    i = pl.program_id(2)
    nblk = km_ref.shape[1]
    heads = LANE // dh
    q = q_ref[...]
    lane = lax.broadcasted_iota(jnp.int32, (tq, LANE), 1)
    blk_id = lax.broadcasted_iota(jnp.int32, (nblk, tq), 0)
    kmean = km_ref[0].astype(BF16)
    neg_inf = jnp.float32(-jnp.inf)
    qms = []
    for h in range(heads):
        qm = jnp.where(_head_of(lane, dh) == h, q, jnp.zeros_like(q))
        qms.append(qm)
        gate = jnp.where(blk_id < i, _dot_nt(kmean, qm), neg_inf)
        sel = jnp.zeros((nblk, tq), F32)
        for _ in range(MOBA_TOPK):
            best = jnp.max(gate, axis=0, keepdims=True)
            first = jnp.min(jnp.where(gate == best, blk_id, nblk), axis=0, keepdims=True)
            hit = blk_id == first
            sel = jnp.where(hit, 1.0, sel)
            gate = jnp.where(hit, neg_inf, gate)
        sel_ref[h] = jnp.where(blk_id < i, sel, 0.0)

    causal = (lax.broadcasted_iota(jnp.int32, (blk, tq), 0)
              <= lax.broadcasted_iota(jnp.int32, (blk, tq), 1))
    start = pl.multiple_of(i * blk, blk)
    kb = k_ref[pl.ds(start, blk), :]
    vtb = vt_ref[:, pl.ds(start, blk)]
    carry = []
    for h in range(heads):
        s = jnp.where(causal, _dot_nt(kb, qms[h]), neg_inf)
        m = jnp.max(s, axis=0, keepdims=True)
        p = jnp.exp(s - m)
        carry += [m, jnp.sum(p, axis=0, keepdims=True), _dot(vtb, p.astype(BF16))]

    def body(n, carry):
        start = pl.multiple_of(n * blk, blk)
        kb = k_ref[pl.ds(start, blk), :]
        vtb = vt_ref[:, pl.ds(start, blk)]
        out = []
        for h in range(heads):
            m, l, acc = carry[3 * h:3 * h + 3]
            keep = sel_ref[h, pl.ds(n, 1), :] > 0.0
            s = jnp.where(keep, _dot_nt(kb, qms[h]), neg_inf)
            m_new = jnp.maximum(m, jnp.max(s, axis=0, keepdims=True))
            alpha = jnp.exp(m - m_new)
            p = jnp.exp(s - m_new)
            out += [m_new, l * alpha + jnp.sum(p, axis=0, keepdims=True),
                    acc * alpha + _dot(vtb, p.astype(BF16))]
        return tuple(out)

    carry = lax.fori_loop(0, i, body, tuple(carry))
    row = lax.broadcasted_iota(jnp.int32, (LANE, tq), 0)
    out_t = jnp.zeros((LANE, tq), F32)
    for h in range(heads):
        _, l, acc = carry[3 * h:3 * h + 3]
        out_t = jnp.where(_head_of(row, dh) == h, acc / l, out_t)
    o_ref[...] = out_t.T.astype(BF16)


def _moba_prompt(q, kb, vt, kmean, batch, seq, dh):
    n, bw = q.shape
    tq = MOBA_BLOCK
    nblk = seq // MOBA_BLOCK
    groups = bw // LANE
    heads = LANE // dh
    km = kmean.reshape(batch, nblk, bw)
    tiles = seq // tq
    return pl.pallas_call(
        functools.partial(_moba_prompt_kernel, dh=dh),
        grid=(batch, groups, tiles),
        in_specs=[pl.BlockSpec((tq, LANE), lambda b, g, i: (b * tiles + i, g)),
                  pl.BlockSpec((seq, LANE), lambda b, g, i: (b, g)),
                  pl.BlockSpec((LANE, seq), lambda b, g, i: (g, b)),
                  pl.BlockSpec((1, nblk, LANE), lambda b, g, i: (b, 0, g))],
        out_specs=pl.BlockSpec((tq, LANE), lambda b, g, i: (b * tiles + i, g)),
        out_shape=jax.ShapeDtypeStruct((n, bw), BF16),
        scratch_shapes=[pltpu.VMEM((heads, nblk, tq), F32)],
        compiler_params=_params("arbitrary", "arbitrary", "arbitrary"),
        name="moba_prompt",
    )(q, kb, vt, km)


def _residual_proj_kernel(*refs, n_in, norm):
    x_ref = refs[0]
    in_refs = refs[1:1 + n_in]
    w_ref = refs[1 + n_in]
    y = x_ref[...]
    lo = 0
    for r in in_refs:
        y = y + _dot(r[...], w_ref[lo:lo + r.shape[1], :])
        lo += r.shape[1]
    if norm == "final":
        g_ref, o_ref = refs[2 + n_in:]
        o_ref[...] = _rmsnorm(y, g_ref[...])
    else:
        g_ref, o_ref, h_ref = refs[2 + n_in:]
        o_ref[...] = y
        h_ref[...] = _rmsnorm(y, g_ref[...]).astype(BF16)


def _residual_proj(x, ins, w, g, norm, tm, name):
    n, d = x.shape
    tm = min(tm, n)
    row = lambda width: pl.BlockSpec((tm, width), lambda i: (i, 0))
    out_specs = [row(d)] if norm == "final" else [row(d), row(d)]
    out_shape = ([jax.ShapeDtypeStruct((n, d), F32)] if norm == "final"
                 else [jax.ShapeDtypeStruct((n, d), F32), jax.ShapeDtypeStruct((n, d), BF16)])
    return pl.pallas_call(
        functools.partial(_residual_proj_kernel, n_in=len(ins), norm=norm),
        grid=(n // tm,),
        in_specs=[row(d)] + [row(a.shape[1]) for a in ins] + [_full(w.shape), _full((1, d))],
        out_specs=out_specs,
        out_shape=out_shape,
        compiler_params=_params("arbitrary"),
        name=name,
    )(x, *ins, w, g.reshape(1, d))


FFN_COLS = 256
FFN_HALO = SUBLANE_BF16


def _silu_gate(gate, value):
    return gate * (1.0 / (1.0 + jnp.exp(-gate))) * value


def _ffn_up_prompt_kernel(h_ref, halo_ref, w_ref, wc_ref, bc_ref, act_ref, tail_ref, *, tiles_per_seq):
    tm = h_ref.shape[0]
    dff = act_ref.shape[1]
    seq_start = pl.program_id(0) % tiles_per_seq == 0
    halo = halo_ref[...]
    halo = jnp.where(seq_start, jnp.zeros_like(halo), halo)
    hh = jnp.concatenate([halo, h_ref[...]], axis=0)
    for c in range(dff // FFN_COLS):
        halves = []
        for lo in (c * FFN_COLS, dff + c * FFN_COLS):
            cols = slice(lo, lo + FFN_COLS)
            up = _dot(hh, w_ref[:, cols])
            tail_ref[0, :, cols] = up[tm + FFN_HALO - 8:, :]
            conv = bc_ref[:, cols] + up[FFN_HALO:, :] * wc_ref[CONV_W - 1:CONV_W, cols]
            for j in range(1, CONV_W):
                prev = pltpu.roll(up, j, 0)[FFN_HALO:, :]
                conv = conv + prev * wc_ref[CONV_W - 1 - j:CONV_W - j, cols]
            halves.append(conv)
        act_ref[:, c * FFN_COLS:(c + 1) * FFN_COLS] = _silu_gate(halves[1], halves[0]).astype(BF16)


def _ffn_up_decode_kernel(h_ref, s0_ref, s1_ref, w_ref, wc_ref, bc_ref, act_ref, up_ref):
    dff = act_ref.shape[1]
    h = h_ref[...]
    for c in range(dff // FFN_COLS):
        halves = []
        for lo in (c * FFN_COLS, dff + c * FFN_COLS):
            cols = slice(lo, lo + FFN_COLS)
            up = _dot(h, w_ref[:, cols])
            up_ref[:, cols] = up
            halves.append(bc_ref[:, cols] + s0_ref[:, cols] * wc_ref[0:1, cols]
                          + s1_ref[:, cols] * wc_ref[1:2, cols] + up * wc_ref[2:3, cols])
        act_ref[:, c * FFN_COLS:(c + 1) * FFN_COLS] = _silu_gate(halves[1], halves[0]).astype(BF16)


def _ffn_up_prompt(h, seq, w_up, w_conv, b_conv, tm):
    n, d = h.shape
    dff = w_up.shape[1] // 2
    assert dff % FFN_COLS == 0 and seq % tm == 0 and tm % FFN_HALO == 0
    ratio = tm // FFN_HALO
    act, tail = pl.pallas_call(
        functools.partial(_ffn_up_prompt_kernel, tiles_per_seq=seq // tm),
        grid=(n // tm,),
        in_specs=[pl.BlockSpec((tm, d), lambda i: (i, 0)),
                  pl.BlockSpec((FFN_HALO, d), lambda i: (jnp.maximum(i * ratio - 1, 0), 0)),
                  _full(w_up.shape), _full(w_conv.shape), _full((1, 2 * dff))],
        out_specs=[pl.BlockSpec((tm, dff), lambda i: (i, 0)),
                   pl.BlockSpec((1, 8, 2 * dff), lambda i: (i, 0, 0))],
        out_shape=[jax.ShapeDtypeStruct((n, dff), BF16),
                   jax.ShapeDtypeStruct((n // tm, 8, 2 * dff), F32)],
        compiler_params=_params("arbitrary"),
        name="ffn_up_prompt",
    )(h, h, w_up, w_conv, b_conv.reshape(1, -1))
    last = tail.reshape(n // seq, seq // tm, 8, 2 * dff)[:, -1, 8 - (CONV_W - 1):, :]
    return act, last


def _ffn_up_decode(h, state, w_up, w_conv, b_conv):
    n, d = h.shape
    dff = w_up.shape[1] // 2
    act, up = pl.pallas_call(
        _ffn_up_decode_kernel,
        grid=(1,),
        in_specs=[_full((n, d)), _full((n, 2 * dff)), _full((n, 2 * dff)),
                  _full(w_up.shape), _full(w_conv.shape), _full((1, 2 * dff))],
        out_specs=[_full((n, dff)), _full((n, 2 * dff))],
        out_shape=[jax.ShapeDtypeStruct((n, dff), BF16), jax.ShapeDtypeStruct((n, 2 * dff), F32)],
        compiler_params=_params("arbitrary"),
        name="ffn_up_decode",
    )(h, state[:, 0], state[:, 1], w_up, w_conv, b_conv.reshape(1, -1))
    return act, jnp.stack([state[:, 1], up], axis=1)


def _odd_in_kernel(h_ref, w_ref, c64_ref, a64_ref, b64_ref, c128_ref, a128_ref, b128_ref,
                   q_ref, k_ref, v_ref, kb_ref, vt_ref, qi_ref, kif_ref, ki2_ref, wit_ref,
                   *, half64, half128, idx_dh):
    cq = q_ref.shape[1]
    ckv = k_ref.shape[1]
    ciq = qi_ref.shape[1]
    h = h_ref[...]
    t64 = (c64_ref[...], a64_ref[...], b64_ref[...])
    t128 = (c128_ref[...], a128_ref[...], b128_ref[...])
    q_ref[...] = _rope_apply(_dot(h, w_ref[:, 0:cq]), *t128, half128).astype(BF16)
    k = _rope_apply(_dot(h, w_ref[:, cq:cq + ckv]), *t128, half128)
    k_ref[...] = k
    kb_ref[...] = k.astype(BF16)
    v = _dot(h, w_ref[:, cq + ckv:cq + 2 * ckv])
    v_ref[...] = v
    if vt_ref is not None:
        vt_ref[...] = v.T.astype(BF16)
    lo = cq + 2 * ckv
    qi_ref[...] = _rope_apply(_dot(h, w_ref[:, lo:lo + ciq]), *t64, half64).astype(BF16)
    tail = _dot(h, w_ref[:, lo + ciq:lo + ciq + LANE])
    ki = _rope_apply(tail, *t64, half64)
    kif_ref[...] = ki
    lane = lax.broadcasted_iota(jnp.int32, ki.shape, 1)
    ki2_ref[...] = jnp.where(lane < idx_dh, ki, pltpu.roll(ki, idx_dh, 1)).astype(BF16)
    wit_ref[...] = tail.T[idx_dh:idx_dh + 8, :] * (IDX_HEADS ** -0.5)


def _odd_in(h, seq, w_in, tabs64, half64, tabs128, half128, c_dh, idx_dh, tm, with_vt):
    n, d = h.shape
    tm = min(tm, n)
    cq = C_HEADS * c_dh
    ckv = C_KV_HEADS * c_dh
    ciq = IDX_HEADS * idx_dh
    cols = cq + 2 * ckv + ciq + LANE
    assert 2 * idx_dh == LANE and w_in.shape[1] <= cols
    w = jnp.pad(w_in, ((0, 0), (0, cols - w_in.shape[1]))).astype(BF16)
    if seq is None:
        tab = pl.BlockSpec((1, LANE), lambda i: (0, 0))
    else:
        tiles_per_seq = seq // tm
        tab = pl.BlockSpec((tm, LANE), lambda i: (i % tiles_per_seq, 0))
    row = lambda width, dt: (pl.BlockSpec((tm, width), lambda i: (i, 0)), jax.ShapeDtypeStruct((n, width), dt))
    col = lambda height, dt: (pl.BlockSpec((height, tm), lambda i: (0, i)), jax.ShapeDtypeStruct((height, n), dt))
    outs = [row(cq, BF16), row(ckv, F32), row(ckv, F32), row(ckv, BF16)]
    if with_vt:
        outs.append(col(ckv, BF16))
    outs += [row(ciq, BF16), row(LANE, F32), row(LANE, BF16), col(8, F32)]

    def body(*refs):
        refs = list(refs)
        if not with_vt:
            refs.insert(8 + 4, None)
        _odd_in_kernel(*refs, half64=half64, half128=half128, idx_dh=idx_dh)

    res = pl.pallas_call(
        body,
        grid=(n // tm,),
        in_specs=[pl.BlockSpec((tm, d), lambda i: (i, 0)), _full(w.shape)] + [tab] * 6,
        out_specs=[o[0] for o in outs],
        out_shape=[o[1] for o in outs],
        compiler_params=_params("arbitrary"),
        name="odd_in_prompt" if with_vt else "odd_in_decode",
    )(h, w, *tabs64, *tabs128)
    res = list(res)
    if not with_vt:
        res.insert(4, None)
    return res


BISECT_MAX_ITERS = 320
INDEX_BIG = 2 ** 30


def _ind(cond):
    return jnp.where(cond, 1.0, 0.0)


def _select_threshold(count_fn, lo, hi, active, keep, index_bits):
    keep_f = jnp.float32(keep)

    def is_open(lo, hi):
        mid = 0.5 * lo + 0.5 * hi
        return jnp.where(active, jnp.where(mid > lo, jnp.where(mid < hi, 1, 0), 0), 0)

    def cond(st):
        return st[3]

    def step(st):
        lo, hi, it, _ = st
        mid = 0.5 * lo + 0.5 * hi
        opened = is_open(lo, hi) > 0
        enough = count_fn(lambda s, idx: _ind(s >= mid)) >= keep_f
        lo = jnp.where(opened, jnp.where(enough, mid, lo), lo)
        hi = jnp.where(opened, jnp.where(enough, hi, mid), hi)
        go = jnp.logical_and(jnp.max(is_open(lo, hi)) > 0, it + 1 < BISECT_MAX_ITERS)
        return lo, hi, it + 1, go

    lo, hi, _, _ = lax.while_loop(cond, step, (lo, hi, jnp.int32(0), jnp.max(is_open(lo, hi)) > 0))
    at_max = count_fn(lambda s, idx: _ind(s >= hi)) >= keep_f
    thr = jnp.where(at_max, hi, lo)
    n_gt = count_fn(lambda s, idx: _ind(s > thr))
    n_eq = count_fn(lambda s, idx: _ind(s == thr))
    need = keep_f - n_gt
    excess = jnp.where(active, jnp.where(n_eq > need, 1, 0), 0)

    def tie_search():
        lo_i = jnp.full(thr.shape, -1, jnp.int32)
        hi_i = jnp.full(thr.shape, (1 << index_bits) - 1, jnp.int32)
        for _ in range(index_bits + 1):
            mid_i = lax.shift_right_arithmetic(lo_i + hi_i, 1)
            ok = count_fn(lambda s, idx: jnp.where(s == thr, _ind(idx <= mid_i), 0.0)) >= need
            hi_i = jnp.where(ok, mid_i, hi_i)
            lo_i = jnp.where(ok, lo_i, mid_i)
        return hi_i

    cut = lax.cond(jnp.max(excess) > 0, tie_search, lambda: jnp.full(thr.shape, INDEX_BIG, jnp.int32))
    thr = jnp.where(active, thr, -jnp.inf)
    cut = jnp.where(active, cut, -1)
    return thr, cut


def _selected(s, idx, thr, cut):
    return jnp.where(s > thr, 1.0, jnp.where(s == thr, _ind(idx <= cut), 0.0))


def _dsa_prompt_kernel(q_ref, qi_ref, wit_ref, kb_ref, vt_ref, ki2_ref, o_ref,
                       sc_ref, m_ref, l_ref, acc_ref, *, c_dh, idx_dh, keep, kc):
    tq = q_ref.shape[0]
    seq = kb_ref.shape[0]
    t0 = pl.program_id(1) * tq
    nch = (t0 + tq + kc - 1) // kc
    scale = c_dh ** -0.5
    neg_inf = jnp.float32(-jnp.inf)
    lane = lax.broadcasted_iota(jnp.int32, (tq, LANE), 1)
    key_iota = lax.broadcasted_iota(jnp.int32, (kc, tq), 0)
    qpos = t0 + lax.broadcasted_iota(jnp.int32, (1, tq), 1)

    qi = qi_ref[...]
    per_group = LANE // idx_dh
    qms = []
    for h in range(IDX_HEADS):
        grp = qi[:, (h // per_group) * LANE:(h // per_group + 1) * LANE]
        qms.append(jnp.where(_head_of(lane, idx_dh) == h % per_group, grp, jnp.zeros_like(grp)))
    wit = wit_ref[...]

    def score_chunk(c, carry):
        lo, hi = carry
        start = pl.multiple_of(c * kc, kc)
        kic = ki2_ref[pl.ds(start, kc), :]
        acc = jnp.zeros((kc, tq), F32)
        for h in range(IDX_HEADS):
            acc = acc + jnp.maximum(_dot_nt(kic, qms[h]), 0.0) * wit[h:h + 1, :]
        score = acc * (idx_dh ** -0.5)
        adm = start + key_iota <= qpos
        sc_ref[pl.ds(start, kc), :] = jnp.where(adm, score, neg_inf)
        lo = jnp.minimum(lo, jnp.min(jnp.where(adm, score, jnp.inf), axis=0, keepdims=True))
        hi = jnp.maximum(hi, jnp.max(jnp.where(adm, score, neg_inf), axis=0, keepdims=True))
        return lo, hi

    lo, hi = lax.fori_loop(0, nch, score_chunk,
                           (jnp.full((1, tq), jnp.inf, F32), jnp.full((1, tq), neg_inf, F32)))

    def count_fn(pred):
        def body(c, cnt):
            start = pl.multiple_of(c * kc, kc)
            hit = pred(sc_ref[pl.ds(start, kc), :], start + key_iota)
            return cnt + jnp.sum(hit.reshape(kc // 8, 8, tq), axis=0)
        part = lax.fori_loop(0, nch, body, jnp.zeros((8, tq), F32))
        return jnp.sum(part, axis=0, keepdims=True)

    thr, cut = _select_threshold(count_fn, lo, hi, qpos + 1 > keep, keep, (seq - 1).bit_length())

    m_ref[...] = jnp.full(m_ref.shape, NEG_BIG, F32)
    l_ref[...] = jnp.zeros(l_ref.shape, F32)
    acc_ref[...] = jnp.zeros(acc_ref.shape, F32)
    group = C_HEADS // C_KV_HEADS

    def attend_chunk(c, _):
        start = pl.multiple_of(c * kc, kc)
        keepm = _selected(sc_ref[pl.ds(start, kc), :], start + key_iota, thr, cut) > 0.0
        kbc = kb_ref[pl.ds(start, kc), :]
        vtc = vt_ref[:, pl.ds(start, kc)]
        for h in range(C_HEADS):
            kv = h // group
            s = _dot_nt(kbc[:, kv * c_dh:(kv + 1) * c_dh], q_ref[:, h * c_dh:(h + 1) * c_dh])
            s = jnp.where(keepm, s, neg_inf)
            m_old = m_ref[h:h + 1, :]
            m_new = jnp.maximum(m_old, jnp.max(s, axis=0, keepdims=True))
            alpha = jnp.exp((m_old - m_new) * scale)
            p = jnp.exp((s - m_new) * scale)
            m_ref[h:h + 1, :] = m_new
            l_ref[h:h + 1, :] = l_ref[h:h + 1, :] * alpha + jnp.sum(p, axis=0, keepdims=True)
            acc_ref[h] = acc_ref[h] * alpha + _dot(vtc[kv * c_dh:(kv + 1) * c_dh, :], p.astype(BF16))
        return 0

    lax.fori_loop(0, nch, attend_chunk, 0)
    for h in range(C_HEADS):
        o_ref[:, h * c_dh:(h + 1) * c_dh] = (acc_ref[h] / l_ref[h:h + 1, :]).T.astype(BF16)


def _dsa_prompt(q, qi, wit, kb, vt, ki2, batch, seq, c_dh, idx_dh, keep, tq, kc):
    n = q.shape[0]
    tiles = seq // tq
    assert c_dh == LANE and seq % kc == 0 and seq % tq == 0
    return pl.pallas_call(
        functools.partial(_dsa_prompt_kernel, c_dh=c_dh, idx_dh=idx_dh, keep=keep, kc=kc),
        grid=(batch, tiles),
        in_specs=[pl.BlockSpec((tq, q.shape[1]), lambda b, i: (b * tiles + i, 0)),
                  pl.BlockSpec((tq, qi.shape[1]), lambda b, i: (b * tiles + i, 0)),
                  pl.BlockSpec((8, tq), lambda b, i: (0, b * tiles + i)),
                  pl.BlockSpec((seq, kb.shape[1]), lambda b, i: (b, 0)),
                  pl.BlockSpec((vt.shape[0], seq), lambda b, i: (0, b)),
                  pl.BlockSpec((seq, LANE), lambda b, i: (b, 0))],
        out_specs=pl.BlockSpec((tq, q.shape[1]), lambda b, i: (b * tiles + i, 0)),
        out_shape=jax.ShapeDtypeStruct(q.shape, BF16),
        scratch_shapes=[pltpu.VMEM((seq, tq), F32), pltpu.VMEM((8, tq), F32), pltpu.VMEM((8, tq), F32),
                        pltpu.VMEM((C_HEADS, c_dh, tq), F32)],
        compiler_params=_params("arbitrary", "arbitrary"),
        name="dsa_prompt",
    )(q, qi, wit, kb, vt, ki2)


PAGE_ROWS = 128


def _head_expander(heads, dh):
    r = np.arange(LANE)[:, None]
    c = np.arange(heads * dh)[None, :]
    return jnp.asarray(c // dh == r, BF16)


def _block_diag_query(q_row, dh):
    w = q_row.shape[1]
    row = lax.broadcasted_iota(jnp.int32, (LANE, w), 0)
    lane_head = _head_of(lax.broadcasted_iota(jnp.int32, (LANE, w), 1), dh)
    q = jnp.broadcast_to(q_row.astype(F32), (LANE, w))
    return jnp.where(lane_head == row, q, 0.0).astype(BF16)


def _moba_decode_kernel(pt_ref, q_ref, kn_ref, vn_ref, e_ref, *rest, n_pages, dh):
    k_refs, v_refs, o_ref = rest[:n_pages], rest[n_pages:2 * n_pages], rest[2 * n_pages]
    bw = q_ref.shape[2]
    heads = bw // dh
    ppb = MOBA_BLOCK // PAGE_ROWS
    nblk = n_pages // ppb
    neg_inf = jnp.float32(-jnp.inf)
    qbd = _block_diag_query(q_ref[0], dh)
    s_pages = [_dot_nt(k_refs[j][...].astype(BF16), qbd) for j in range(n_pages)]
    s_own = _dot_nt(jnp.broadcast_to(kn_ref[0], (8, bw)).astype(BF16), qbd)[0:1]
    gates = []
    for n in range(nblk):
        g = jnp.zeros((1, LANE), F32)
        for j in range(n * ppb, (n + 1) * ppb):
            g = g + jnp.sum(s_pages[j], axis=0, keepdims=True)
        gates.append(g)
    sels = []
    for n in range(nblk):
        rank = jnp.zeros((1, LANE), F32)
        for m in range(nblk):
            if m != n:
                beats = gates[m] >= gates[n] if m < n else gates[m] > gates[n]
                rank = rank + jnp.where(beats, 1.0, 0.0)
        sels.append(rank < MOBA_TOPK)
    m = s_own
    masked = []
    for j in range(n_pages):
        sm = jnp.where(sels[j // ppb], s_pages[j], neg_inf)
        masked.append(sm)
        m = jnp.maximum(m, jnp.max(sm, axis=0, keepdims=True))
    e_own = jnp.exp(s_own - m)
    es = [jnp.exp(sm - m) for sm in masked]
    l = e_own
    for e in es:
        l = l + jnp.sum(e, axis=0, keepdims=True)
    rinv = 1.0 / l
    expand = e_ref[...]
    acc = jnp.zeros((PAGE_ROWS, bw), F32)
    for j in range(n_pages):
        acc = acc + _dot((es[j] * rinv).astype(BF16), expand) * v_refs[j][...]
    p_own = _dot(jnp.broadcast_to(e_own * rinv, (8, LANE)).astype(BF16), expand)[0:1]
    o_ref[0] = (jnp.sum(acc, axis=0, keepdims=True) + p_own * vn_ref[0]).astype(BF16)


def _moba_decode(q, k_new, v_new, cache_k, cache_v, page_table, dh):
    n, bw = q.shape
    n_pages = page_table.shape[1]
    assert cache_k.shape[1] == PAGE_ROWS and (n_pages * PAGE_ROWS) % MOBA_BLOCK == 0
    ck = cache_k.reshape(cache_k.shape[0], PAGE_ROWS, bw)
    cv = cache_v.reshape(cache_v.shape[0], PAGE_ROWS, bw)
    seq_row = pl.BlockSpec((1, 1, bw), lambda b, pt: (b, 0, 0))
    pages = [pl.BlockSpec((None, PAGE_ROWS, bw), lambda b, pt, j=j: (pt[b, j], 0, 0)) for j in range(n_pages)]
    out = pl.pallas_call(
        functools.partial(_moba_decode_kernel, n_pages=n_pages, dh=dh),
        grid_spec=pltpu.PrefetchScalarGridSpec(
            num_scalar_prefetch=1, grid=(n,),
            in_specs=[seq_row, seq_row, seq_row, pl.BlockSpec((LANE, bw), lambda b, pt: (0, 0))] + pages + pages,
            out_specs=seq_row),
        out_shape=jax.ShapeDtypeStruct((n, 1, bw), BF16),
        compiler_params=_params("arbitrary"),
        name="moba_decode",
    )(page_table, q.reshape(n, 1, bw), k_new.reshape(n, 1, bw), v_new.reshape(n, 1, bw),
      _head_expander(bw // dh, dh), *([ck] * n_pages), *([cv] * n_pages))
    return out.reshape(n, bw)


def _dsa_decode_score_kernel(pt_ref, qi_ref, wi_ref, kin_ref, *rest, n_pages, idx_dh):
    ki_refs, o_ref = rest[:n_pages], rest[n_pages]
    qi8 = qi_ref[0]
    wi8 = wi_ref[0]
    scale = idx_dh ** -0.5

    def score(keys):
        sc = jnp.maximum(_dot_nt(qi8, keys.astype(BF16)), 0.0)
        return jnp.sum(sc * wi8, axis=0, keepdims=True) * scale

    o_ref[0] = jnp.full(o_ref.shape[1:], -jnp.inf, F32)
    for j in range(n_pages):
        o_ref[0, j:j + 1, :] = score(ki_refs[j][...])
    own = score(jnp.broadcast_to(kin_ref[0], (PAGE_ROWS, idx_dh)))
    lane = lax.broadcasted_iota(jnp.int32, (1, PAGE_ROWS), 1)
    o_ref[0, n_pages:n_pages + 1, :] = jnp.where(lane == 0, own, -jnp.inf)


def _dsa_decode_select_kernel(s_ref, m_ref, *, keep, n_valid):
    s = s_ref[...]
    idx = lax.broadcasted_iota(jnp.int32, s.shape, 1)
    neg_inf = jnp.float32(-jnp.inf)
    lo = jnp.min(jnp.where(s > neg_inf, s, jnp.inf), axis=1, keepdims=True)
    hi = jnp.max(s, axis=1, keepdims=True)
    count_fn = lambda pred: jnp.sum(pred(s, idx), axis=1, keepdims=True)
    active = jnp.full(lo.shape, n_valid > keep, jnp.bool_)
    thr, cut = _select_threshold(count_fn, lo, hi, active, keep, (s.shape[1] - 1).bit_length())
    m_ref[...] = _selected(s, idx, thr, cut)


def _dsa_decode_attn_kernel(pt_ref, q_ref, kn_ref, vn_ref, mask_ref, e_ref, *rest, n_pages, c_dh):
    k_refs, v_refs, o_ref = rest[:n_pages], rest[n_pages:2 * n_pages], rest[2 * n_pages]
    ckv = kn_ref.shape[2]
    group = C_HEADS // C_KV_HEADS
    scale = c_dh ** -0.5
    neg_inf = jnp.float32(-jnp.inf)
    q8 = jnp.concatenate([q_ref[0].astype(F32)] * (ckv // c_dh), axis=1)
    lane_kv = _head_of(lax.broadcasted_iota(jnp.int32, q8.shape, 1), c_dh)
    row_kv = _head_of(lax.broadcasted_iota(jnp.int32, q8.shape, 0), group)
    qbd = jnp.where(lane_kv == row_kv, q8, 0.0)
    qbd = jnp.concatenate([qbd, jnp.zeros((LANE - C_HEADS, ckv), F32)], axis=0).astype(BF16)
    mask = mask_ref[0]
    mask_t = jnp.concatenate([mask, jnp.zeros((LANE - mask.shape[0], PAGE_ROWS), F32)], axis=0).T
    s_own = _dot_nt(jnp.broadcast_to(kn_ref[0], (8, ckv)).astype(BF16), qbd)[0:1]
    s_own = jnp.where(mask[n_pages:n_pages + 1, 0:1] > 0.0, s_own, neg_inf)
    m = jnp.maximum(s_own, NEG_BIG)
    masked = []
    for j in range(n_pages):
        sm = jnp.where(mask_t[:, j:j + 1] > 0.0, _dot_nt(k_refs[j][...].astype(BF16), qbd), neg_inf)
        masked.append(sm)
        m = jnp.maximum(m, jnp.max(sm, axis=0, keepdims=True))
    e_own = jnp.exp((s_own - m) * scale)
    es = [jnp.exp((sm - m) * scale) for sm in masked]
    l = e_own
    for e in es:
        l = l + jnp.sum(e, axis=0, keepdims=True)
    rinv = 1.0 / l
    expand = e_ref[...]

    def spread(v):
        return jnp.concatenate([v[:, (h // group) * c_dh:(h // group + 1) * c_dh] for h in range(C_HEADS)], axis=1)

    acc = jnp.zeros((PAGE_ROWS, C_HEADS * c_dh), F32)
    for j in range(n_pages):
        acc = acc + _dot((es[j] * rinv).astype(BF16), expand) * spread(v_refs[j][...])
    p_own = _dot(jnp.broadcast_to(e_own * rinv, (8, LANE)).astype(BF16), expand)[0:1]
    o_ref[0] = (jnp.sum(acc, axis=0, keepdims=True) + p_own * spread(vn_ref[0])).astype(BF16)


def _dsa_decode(q, qi, wit, k_new, v_new, ki_new, cache_k, cache_v, cache_ki, page_table, c_dh, idx_dh, keep):
    n = q.shape[0]
    n_pages = page_table.shape[1]
    ckv = k_new.shape[1]
    assert cache_k.shape[1] == PAGE_ROWS
    ck = cache_k.reshape(cache_k.shape[0], PAGE_ROWS, ckv)
    cv = cache_v.reshape(cache_v.shape[0], PAGE_ROWS, ckv)
    rows = -(-(n_pages + 1) // 8) * 8
    qi8 = jnp.pad(qi.reshape(n, IDX_HEADS, idx_dh), ((0, 0), (0, 8 - IDX_HEADS), (0, 0)))
    wi8 = wit.T.reshape(n, 8, 1)
    seq3 = lambda a, b: pl.BlockSpec((1, a, b), lambda s, pt: (s, 0, 0))
    pages = lambda w: [pl.BlockSpec((None, PAGE_ROWS, w), lambda s, pt, j=j: (pt[s, j], 0, 0)) for j in range(n_pages)]
    scores = pl.pallas_call(
        functools.partial(_dsa_decode_score_kernel, n_pages=n_pages, idx_dh=idx_dh),
        grid_spec=pltpu.PrefetchScalarGridSpec(
            num_scalar_prefetch=1, grid=(n,),
            in_specs=[seq3(8, idx_dh), seq3(8, 1), seq3(1, idx_dh)] + pages(idx_dh),
            out_specs=seq3(rows, PAGE_ROWS)),
        out_shape=jax.ShapeDtypeStruct((n, rows, PAGE_ROWS), F32),
        compiler_params=_params("arbitrary"),
        name="dsa_decode_scores",
    )(page_table, qi8, wi8, ki_new.reshape(n, 1, idx_dh), *([cache_ki] * n_pages))
    flat = rows * PAGE_ROWS
    mask = pl.pallas_call(
        functools.partial(_dsa_decode_select_kernel, keep=keep, n_valid=n_pages * PAGE_ROWS + 1),
        grid=(1,),
        in_specs=[_full((n, flat))],
        out_specs=_full((n, flat)),
        out_shape=jax.ShapeDtypeStruct((n, flat), F32),
        compiler_params=_params("arbitrary"),
        name="dsa_decode_select",
    )(scores.reshape(n, flat))
    cq = C_HEADS * c_dh
    out = pl.pallas_call(
        functools.partial(_dsa_decode_attn_kernel, n_pages=n_pages, c_dh=c_dh),
        grid_spec=pltpu.PrefetchScalarGridSpec(
            num_scalar_prefetch=1, grid=(n,),
            in_specs=[seq3(C_HEADS, c_dh), seq3(1, ckv), seq3(1, ckv), seq3(rows, PAGE_ROWS),
                      pl.BlockSpec((LANE, cq), lambda s, pt: (0, 0))] + pages(ckv) + pages(ckv),
            out_specs=seq3(1, cq)),
        out_shape=jax.ShapeDtypeStruct((n, 1, cq), BF16),
        compiler_params=_params("arbitrary"),
        name="dsa_decode_attn",
    )(page_table, q.reshape(n, C_HEADS, c_dh), k_new.reshape(n, 1, ckv), v_new.reshape(n, 1, ckv),
      mask.reshape(n, rows, PAGE_ROWS), _head_expander(C_HEADS, c_dh), *([ck] * n_pages), *([cv] * n_pages))
    return out.reshape(n, cq)


def kernel(x_prompt, x_sample, cache_moba_k, cache_moba_v, cache_dsa_k, cache_dsa_v, cache_dsa_kidx,
           state_ffn_conv, page_table, w_in_even, g_sgu, w_sgu, b_sgu, w_out_even, w_in_odd, w_out_odd,
           g_mix, g_ffn, w_up, w_conv, b_conv, w_down, g_final):
    batch, seq, d = x_prompt.shape
    n_dec, dec_seq, _ = x_sample.shape
    depth = g_mix.shape[0]
    n_pages = page_table.shape[1]
    past_len = n_pages * cache_moba_k.shape[2]
    assert depth == 2 and dec_seq == 1, "one even + one odd layer, one decode token per sequence"
    b_dh = cache_moba_k.shape[-1]
    c_dh = cache_dsa_k.shape[-1]
    idx_dh = cache_dsa_kidx.shape[-1]
    keep_p = min(DSA_TOPK, seq // 4)
    keep_s = min(DSA_TOPK, (past_len + dec_seq) // 4)
    xp = x_prompt.reshape(batch * seq, d)
    xs = x_sample.reshape(n_dec, d)
    bf = lambda w: w.astype(BF16)

    tp64, half64 = _rope_tables(b_dh, seq, 0)
    ts64, _ = _rope_tables(b_dh, 1, past_len)
    tp128, half128 = _rope_tables(c_dh, seq, 0)
    ts128, _ = _rope_tables(c_dh, 1, past_len)
    assert idx_dh == b_dh

    w_in0, w_out0 = bf(w_in_even[0]), bf(w_out_even[0])
    a_p, q_p, mk_p, mv_p, kb_p, vt_p, km_p = _even_in_prompt(
        xp, seq, g_mix[0], w_in0, tp64, half64, g_sgu[0], w_sgu[0], b_sgu[0])
    bo_p = _moba_prompt(q_p, kb_p, vt_p, km_p, batch, seq, b_dh)
    a_s, q_s, mk_s, mv_s, va_s = _even_in_decode(xs, g_mix[0], w_in0, ts64, half64, g_sgu[0], w_sgu[0], b_sgu[0])
    bo_s = _moba_decode(q_s, mk_s, mv_s, cache_moba_k[0], cache_moba_v[0], page_table, b_dh)
    w_up0, w_down0 = bf(w_up[0]), bf(w_down[0])
    xp, hp = _residual_proj(xp, [a_p, bo_p], w_out0, g_ffn[0], "next", 256, "out_even_prompt")
    xs, hs = _residual_proj(xs, [a_s, bo_s], w_out0, g_ffn[0], "next", 256, "out_even_decode")
    act_p, conv_p0 = _ffn_up_prompt(hp, seq, w_up0, w_conv[0], b_conv[0], 512)
    act_s, conv_s0 = _ffn_up_decode(hs, state_ffn_conv[0], w_up0, w_conv[0], b_conv[0])
    xp, hp = _residual_proj(xp, [act_p], w_down0, g_mix[1], "next", 256, "ffn_down0_prompt")
    xs, hs = _residual_proj(xs, [act_s], w_down0, g_mix[1], "next", 256, "ffn_down0_decode")

    w_out1 = bf(w_out_odd[0])
    q_p, dk_p, dv_p, kb_p, vt_p, qi_p, kif_p, ki2_p, wit_p = _odd_in(
        hp, seq, w_in_odd[0], tp64, half64, tp128, half128, c_dh, idx_dh, 256, True)
    o_p = _dsa_prompt(q_p, qi_p, wit_p, kb_p, vt_p, ki2_p, batch, seq, c_dh, idx_dh, keep_p, 128, 256)
    q_s, dk_s, dv_s, _, _, qi_s, kif_s, _, wit_s = _odd_in(
        hs, None, w_in_odd[0], ts64, half64, ts128, half128, c_dh, idx_dh, 256, False)
    o_s = _dsa_decode(q_s, qi_s, wit_s, dk_s, dv_s, kif_s[:, :idx_dh], cache_dsa_k[0], cache_dsa_v[0],
                      cache_dsa_kidx[0], page_table, c_dh, idx_dh, keep_s)
    w_up1, w_down1 = bf(w_up[1]), bf(w_down[1])
    xp, hp = _residual_proj(xp, [o_p], w_out1, g_ffn[1], "next", 256, "out_odd_prompt")
    xs, hs = _residual_proj(xs, [o_s], w_out1, g_ffn[1], "next", 256, "out_odd_decode")
    act_p, conv_p1 = _ffn_up_prompt(hp, seq, w_up1, w_conv[1], b_conv[1], 512)
    act_s, conv_s1 = _ffn_up_decode(hs, state_ffn_conv[1], w_up1, w_conv[1], b_conv[1])
    y_p, = _residual_proj(xp, [act_p], w_down1, g_final, "final", 256, "ffn_down1_prompt")
    y_s, = _residual_proj(xs, [act_s], w_down1, g_final, "final", 256, "ffn_down1_decode")

    ckv = C_KV_HEADS
    return (y_p.reshape(batch, seq, d), y_s.reshape(n_dec, dec_seq, d),
            mk_p.reshape(1, batch, seq, B_HEADS, b_dh), mv_p.reshape(1, batch, seq, B_HEADS, b_dh),
            mk_s.reshape(1, n_dec, dec_seq, B_HEADS, b_dh), mv_s.reshape(1, n_dec, dec_seq, B_HEADS, b_dh),
            va_s.reshape(1, n_dec, dec_seq, A_GROUPS, LANE),
            dk_p.reshape(1, batch, seq, ckv, c_dh), dv_p.reshape(1, batch, seq, ckv, c_dh),
            kif_p[:, :idx_dh].reshape(1, batch, seq, idx_dh),
            dk_s.reshape(1, n_dec, dec_seq, ckv, c_dh), dv_s.reshape(1, n_dec, dec_seq, ckv, c_dh),
            kif_s[:, :idx_dh].reshape(1, n_dec, dec_seq, idx_dh),
            jnp.stack([conv_p0, conv_p1]), jnp.stack([conv_s0, conv_s1]))
```

```python
import functools

import numpy as np
import jax
import jax.numpy as jnp
from jax import lax
from jax.experimental import pallas as pl
from jax.experimental.pallas import tpu as pltpu

F32 = jnp.float32
BF16 = jnp.bfloat16

A_GROUPS = 4
CHUNK = 128
B_HEADS = 8
MOBA_BLOCK = 256
MOBA_TOPK = 3
C_HEADS = 8
C_KV_HEADS = 2
IDX_HEADS = 4
DSA_TOPK = 256
ROPE_THETA = 500000.0
ROPE_FRAC = 4
CONV_W = 3
EPS = 1e-6

LANE = 128
SUBLANE_BF16 = 16
VMEM_LIMIT_BYTES = 56 * 1024 * 1024
NEG_BIG = -1e30


def _params(*sem):
    return pltpu.CompilerParams(dimension_semantics=sem,
                                vmem_limit_bytes=VMEM_LIMIT_BYTES)


def _full(shape):
    nd = len(shape)
    return pl.BlockSpec(shape, lambda *_: (0,) * nd)


def _rmsnorm(x, g):
    ms = jnp.mean(x * x, axis=-1, keepdims=True)
    return x * lax.rsqrt(ms + EPS) * g


def _rope_apply(x, c, s1, s2, half):
    parts = []
    for j in range(x.shape[1] // LANE):
        xs = x[:, j * LANE:(j + 1) * LANE]
        parts.append(xs * c + pltpu.roll(xs, LANE - half, 1) * s1
                     + pltpu.roll(xs, half, 1) * s2)
    return parts[0] if len(parts) == 1 else jnp.concatenate(parts, axis=1)


def _head_of(index, dh):
    assert dh & (dh - 1) == 0
    return lax.shift_right_logical(index, dh.bit_length() - 1)


def _dot(a, b):
    return jnp.dot(a, b, preferred_element_type=F32)


def _dot_nt(a, b):
    return lax.dot_general(a, b, (((1,), (1,)), ((), ())),
                           preferred_element_type=F32)


def _rope_rows(dh):
    rd = dh // ROPE_FRAC
    half = rd // 2
    inv = ROPE_THETA ** (-jnp.arange(half, dtype=F32) * (2.0 / rd))
    r = np.arange(LANE) % dh
    j = np.where(r < half, r, np.where(r < rd, r - half, 0))
    inv_lane = jnp.where(jnp.asarray(r < rd), inv[j], 0.0).reshape(1, LANE)
    s1 = jnp.asarray(np.where(r < half, -1.0, 0.0), F32).reshape(1, LANE)
    s2 = jnp.asarray(np.where((r >= half) & (r < rd), 1.0, 0.0), F32).reshape(1, LANE)
    return inv_lane, s1, s2, half


def _rope_table_kernel(inv_ref, s1_ref, s2_ref, c_ref, a_ref, b_ref, *, base):
    rows = c_ref.shape[0]
    pos = lax.broadcasted_iota(jnp.int32, (rows, LANE), 0) + (base + pl.program_id(0) * rows)
    ang = pos.astype(F32) * inv_ref[...]
    s = jnp.sin(ang)
    c_ref[...] = jnp.cos(ang)
    a_ref[...] = s * s1_ref[...]
    b_ref[...] = s * s2_ref[...]


def _rope_tables(dh, n_pos, base):
    inv_lane, s1, s2, half = _rope_rows(dh)
    rows = min(n_pos, 512)
    assert n_pos % rows == 0
    out = jax.ShapeDtypeStruct((n_pos, LANE), F32)
    spec = pl.BlockSpec((rows, LANE), lambda i: (i, 0))
    c, a, b = pl.pallas_call(
        functools.partial(_rope_table_kernel, base=base),
        grid=(n_pos // rows,),
        in_specs=[_full((1, LANE))] * 3,
        out_specs=[spec] * 3,
        out_shape=[out] * 3,
        compiler_params=_params("arbitrary"),
        name="rope_tables",
    )(inv_lane, s1, s2)
    return (c, a, b), half


def _sgu_values(h, w_ref, gs_ref, aw):
    u = jax.nn.gelu(_dot(h, w_ref[:, 0:aw]))
    zv = jax.nn.gelu(_dot(h, w_ref[:, aw:2 * aw]))
    vs = []
    for g in range(A_GROUPS):
        vg = zv[:, g * LANE:(g + 1) * LANE]
        vs.append(_rmsnorm(vg, gs_ref[:, g * LANE:(g + 1) * LANE]))
    return u, vs


def _even_in_prompt_kernel(x_ref, g_ref, w_ref, c_ref, s1_ref, s2_ref, gs_ref, ws_ref, bs_ref,
                           a_ref, q_ref, k_ref, v_ref, kb_ref, vt_ref, km_ref, *, half, scale):
    tm = x_ref.shape[0]
    aw = A_GROUPS * LANE
    bw = q_ref.shape[1]
    h = _rmsnorm(x_ref[...], g_ref[...]).astype(BF16)
    u, vs = _sgu_values(h, w_ref, gs_ref, aw)
    tri = (lax.broadcasted_iota(jnp.int32, (CHUNK, CHUNK), 0)
           >= lax.broadcasted_iota(jnp.int32, (CHUNK, CHUNK), 1))
    for g in range(A_GROUPS):
        wg = jnp.where(tri, ws_ref[g], 0.0).astype(BF16)
        for cc in range(tm // CHUNK):
            rows = slice(cc * CHUNK, (cc + 1) * CHUNK)
            mixed = _dot(wg, vs[g][rows].astype(BF16)) + bs_ref[g]
            a_ref[rows, g * LANE:(g + 1) * LANE] = (u[rows, g * LANE:(g + 1) * LANE] * mixed).astype(BF16)
    c, s1, s2 = c_ref[...], s1_ref[...], s2_ref[...]
    q = _rope_apply(_dot(h, w_ref[:, 2 * aw:2 * aw + bw]), c, s1, s2, half)
    q_ref[...] = (q * scale).astype(BF16)
    k = _rope_apply(_dot(h, w_ref[:, 2 * aw + bw:2 * aw + 2 * bw]), c, s1, s2, half)
    k_ref[...] = k
    kb_ref[...] = k.astype(BF16)
    km_ref[0] = jnp.sum(k, axis=0, keepdims=True) * (1.0 / tm)
    v = _dot(h, w_ref[:, 2 * aw + 2 * bw:2 * aw + 3 * bw])
    v_ref[...] = v
    vt_ref[...] = v.T.astype(BF16)


def _even_in_decode_kernel(x_ref, g_ref, w_ref, c_ref, s1_ref, s2_ref, gs_ref, w00_ref, b0_ref,
                           a_ref, q_ref, k_ref, v_ref, va_ref, *, half, scale):
    aw = A_GROUPS * LANE
    bw = q_ref.shape[1]
    h = _rmsnorm(x_ref[...], g_ref[...]).astype(BF16)
    u, vs = _sgu_values(h, w_ref, gs_ref, aw)
    va = jnp.concatenate(vs, axis=1)
    va_ref[...] = va
    a_ref[...] = (u * (w00_ref[...] * va + b0_ref[...])).astype(BF16)
    c, s1, s2 = c_ref[...], s1_ref[...], s2_ref[...]
    q = _rope_apply(_dot(h, w_ref[:, 2 * aw:2 * aw + bw]), c, s1, s2, half)
    q_ref[...] = (q * scale).astype(BF16)
    k_ref[...] = _rope_apply(_dot(h, w_ref[:, 2 * aw + bw:2 * aw + 2 * bw]), c, s1, s2, half)
    v_ref[...] = _dot(h, w_ref[:, 2 * aw + 2 * bw:2 * aw + 3 * bw])


def _even_in_prompt(x, seq, g_mix, w_in, tables, half, g_sgu, w_sgu, b_sgu):
    n, d = x.shape
    tm = MOBA_BLOCK
    bw = (w_in.shape[1] - 2 * A_GROUPS * LANE) // 3
    dh = bw // B_HEADS
    tiles_per_seq = seq // tm
    row = lambda w, dt: (pl.BlockSpec((tm, w), lambda i: (i, 0)), jax.ShapeDtypeStruct((n, w), dt))
    tab = pl.BlockSpec((tm, LANE), lambda i: (i % tiles_per_seq, 0))
    outs = [row(A_GROUPS * LANE, BF16), row(bw, BF16), row(bw, F32), row(bw, F32), row(bw, BF16),
            (pl.BlockSpec((bw, tm), lambda i: (0, i)), jax.ShapeDtypeStruct((bw, n), BF16)),
            (pl.BlockSpec((1, 1, bw), lambda i: (i, 0, 0)), jax.ShapeDtypeStruct((n // tm, 1, bw), F32))]
    bs = jnp.broadcast_to(b_sgu[:, :, None], (A_GROUPS, CHUNK, LANE))
    return pl.pallas_call(
        functools.partial(_even_in_prompt_kernel, half=half, scale=dh ** -0.5),
        grid=(n // tm,),
        in_specs=[pl.BlockSpec((tm, d), lambda i: (i, 0)), _full((1, d)), _full(w_in.shape),
                  tab, tab, tab, _full((1, A_GROUPS * LANE)), _full(w_sgu.shape), _full(bs.shape)],
        out_specs=[o[0] for o in outs],
        out_shape=[o[1] for o in outs],
        compiler_params=_params("arbitrary"),
        name="even_in_prompt",
    )(x, g_mix.reshape(1, d), w_in, *tables, g_sgu.reshape(1, -1), w_sgu, bs)


def _even_in_decode(x, g_mix, w_in, tables, half, g_sgu, w_sgu, b_sgu):
    n, d = x.shape
    bw = (w_in.shape[1] - 2 * A_GROUPS * LANE) // 3
    dh = bw // B_HEADS
    aw = A_GROUPS * LANE
    w00 = jnp.repeat(w_sgu[:, 0, 0], LANE).reshape(1, aw)
    b0 = jnp.repeat(b_sgu[:, 0], LANE).reshape(1, aw)
    shapes = [(aw, BF16), (bw, BF16), (bw, F32), (bw, F32), (aw, F32)]
    return pl.pallas_call(
        functools.partial(_even_in_decode_kernel, half=half, scale=dh ** -0.5),
        grid=(1,),
        in_specs=[_full((n, d)), _full((1, d)), _full(w_in.shape)] + [_full((1, LANE))] * 3
                 + [_full((1, aw))] * 3,
        out_specs=[_full((n, w)) for w, _ in shapes],
        out_shape=[jax.ShapeDtypeStruct((n, w), dt) for w, dt in shapes],
        compiler_params=_params("arbitrary"),
        name="even_in_decode",
    )(x, g_mix.reshape(1, d), w_in, *tables, g_sgu.reshape(1, -1), w00, b0)


def _moba_prompt_kernel(q_ref, k_ref, vt_ref, km_ref, o_ref, sel_ref, m_ref, l_ref, acc_ref, *, dh):
    blk = MOBA_BLOCK
    tq = q_ref.shape[0]
    i = pl.program_id(1)
    nblk = km_ref.shape[1]
    heads = q_ref.shape[1] // dh
    per_group = LANE // dh
    lane = lax.broadcasted_iota(jnp.int32, (tq, LANE), 1)
    blk_id = lax.broadcasted_iota(jnp.int32, (nblk, tq), 0)
    neg_inf = jnp.float32(-jnp.inf)
    qms = []
    for h in range(heads):
        grp = h // per_group
        q = q_ref[:, grp * LANE:(grp + 1) * LANE]
        qm = jnp.where(_head_of(lane, dh) == h % per_group, q, jnp.zeros_like(q))
        qms.append(qm)
        kmean = km_ref[0, :, grp * LANE:(grp + 1) * LANE].astype(BF16)
        gate = jnp.where(blk_id < i, _dot_nt(kmean, qm), neg_inf)
        sel = jnp.zeros((nblk, tq), F32)
        for _ in range(MOBA_TOPK):
            best = jnp.max(gate, axis=0, keepdims=True)
            first = jnp.min(jnp.where(gate == best, blk_id, nblk), axis=0, keepdims=True)
            hit = blk_id == first
            sel = jnp.where(hit, 1.0, sel)
            gate = jnp.where(hit, neg_inf, gate)
        sel_ref[h] = jnp.where(blk_id < i, sel, 0.0)

    causal = (lax.broadcasted_iota(jnp.int32, (blk, tq), 0)
              <= lax.broadcasted_iota(jnp.int32, (blk, tq), 1))

    head_row = lax.broadcasted_iota(jnp.int32, (heads, tq), 0)

    def attend(kb, vtb, keeps, first):
        if not first:
            m_old, l_old = m_ref[...], l_ref[...]
            acc_old = [acc_ref[h] for h in range(heads)]
        m_all = jnp.zeros((heads, tq), F32)
        l_all = jnp.zeros((heads, tq), F32)
        scores = [_dot_nt(kb[:, (h // per_group) * LANE:(h // per_group + 1) * LANE], qms[h])
                  for h in range(heads)]
        probs, alphas = [], []
        for h in range(heads):
            s = jnp.where(keeps[h], scores[h], neg_inf)
            s_max = jnp.max(s, axis=0, keepdims=True)
            m_new = s_max if first else jnp.maximum(m_old[h:h + 1, :], s_max)
            p = jnp.exp(s - m_new)
            l_new = jnp.sum(p, axis=0, keepdims=True)
            if not first:
                alpha = jnp.exp(m_old[h:h + 1, :] - m_new)
                alphas.append(alpha)
                l_new = l_old[h:h + 1, :] * alpha + l_new
            probs.append(p.astype(BF16))
            m_all = jnp.where(head_row == h, m_new, m_all)
            l_all = jnp.where(head_row == h, l_new, l_all)
        pvs = [_dot(vtb[h * dh:(h + 1) * dh, :], probs[h]) for h in range(heads)]
        m_ref[...] = m_all
        l_ref[...] = l_all
        for h in range(heads):
            acc_ref[h] = pvs[h] if first else acc_old[h] * alphas[h] + pvs[h]

    start = pl.multiple_of(i * blk, blk)
    attend(k_ref[pl.ds(start, blk), :], vt_ref[:, pl.ds(start, blk)], [causal] * heads, True)

    def body(n, _):
        start = pl.multiple_of(n * blk, blk)
        keeps = [sel_ref[h, pl.ds(n, 1), :] > 0.0 for h in range(heads)]
        attend(k_ref[pl.ds(start, blk), :], vt_ref[:, pl.ds(start, blk)], keeps, False)
        return 0

    lax.fori_loop(0, i, body, 0)
    outs = [acc_ref[h] / l_ref[h:h + 1, :] for h in range(heads)]
    o_ref[...] = jnp.concatenate(outs, axis=0).T.astype(BF16)


def _moba_prompt(q, kb, vt, kmean, batch, seq, dh):
    n, bw = q.shape
    tq = MOBA_BLOCK
    nblk = seq // MOBA_BLOCK
    heads = bw // dh
    km = kmean.reshape(batch, nblk, bw)
    tiles = seq // tq
    return pl.pallas_call(
        functools.partial(_moba_prompt_kernel, dh=dh),
        grid=(batch, tiles),
        in_specs=[pl.BlockSpec((tq, bw), lambda b, i: (b * tiles + i, 0)),
                  pl.BlockSpec((seq, bw), lambda b, i: (b, 0)),
                  pl.BlockSpec((bw, seq), lambda b, i: (0, b)),
                  pl.BlockSpec((1, nblk, bw), lambda b, i: (b, 0, 0))],
        out_specs=pl.BlockSpec((tq, bw), lambda b, i: (b * tiles + i, 0)),
        out_shape=jax.ShapeDtypeStruct((n, bw), BF16),
        scratch_shapes=[pltpu.VMEM((heads, nblk, tq), F32), pltpu.VMEM((heads, tq), F32),
                        pltpu.VMEM((heads, tq), F32), pltpu.VMEM((heads, dh, tq), F32)],
        compiler_params=_params("arbitrary", "arbitrary"),
        name="moba_prompt",
    )(q, kb, vt, km)


def _residual_proj_kernel(*refs, n_in, norm):
    x_ref = refs[0]
    in_refs = refs[1:1 + n_in]
    w_ref = refs[1 + n_in]
    y = x_ref[...]
    lo = 0
    for r in in_refs:
        y = y + _dot(r[...], w_ref[lo:lo + r.shape[1], :])
        lo += r.shape[1]
    if norm == "final":
        g_ref, o_ref = refs[2 + n_in:]
        o_ref[...] = _rmsnorm(y, g_ref[...])
    else:
        g_ref, o_ref, h_ref = refs[2 + n_in:]
        o_ref[...] = y
        h_ref[...] = _rmsnorm(y, g_ref[...]).astype(BF16)


def _residual_proj(x, ins, w, g, norm, tm, name):
    n, d = x.shape
    tm = min(tm, n)
    row = lambda width: pl.BlockSpec((tm, width), lambda i: (i, 0))
    out_specs = [row(d)] if norm == "final" else [row(d), row(d)]
    out_shape = ([jax.ShapeDtypeStruct((n, d), F32)] if norm == "final"
                 else [jax.ShapeDtypeStruct((n, d), F32), jax.ShapeDtypeStruct((n, d), BF16)])
    return pl.pallas_call(
        functools.partial(_residual_proj_kernel, n_in=len(ins), norm=norm),
        grid=(n // tm,),
        in_specs=[row(d)] + [row(a.shape[1]) for a in ins] + [_full(w.shape), _full((1, d))],
        out_specs=out_specs,
        out_shape=out_shape,
        compiler_params=_params("arbitrary"),
        name=name,
    )(x, *ins, w, g.reshape(1, d))


FFN_COLS = 256
FFN_HALO = SUBLANE_BF16


def _silu_gate(gate, value):
    return gate * (1.0 / (1.0 + jnp.exp(-gate))) * value


def _ffn_up_prompt_kernel(h_ref, halo_ref, w_ref, wc_ref, bc_ref, act_ref, tail_ref, *, tiles_per_seq):
    tm = h_ref.shape[0]
    dff = act_ref.shape[1]
    seq_start = pl.program_id(0) % tiles_per_seq == 0
    halo = halo_ref[...]
    halo = jnp.where(seq_start, jnp.zeros_like(halo), halo)
    hh = jnp.concatenate([halo, h_ref[...]], axis=0)
    for c in range(dff // FFN_COLS):
        halves = []
        for lo in (c * FFN_COLS, dff + c * FFN_COLS):
            cols = slice(lo, lo + FFN_COLS)
            up = _dot(hh, w_ref[:, cols])
            tail_ref[0, :, cols] = up[tm + FFN_HALO - 8:, :]
            conv = bc_ref[:, cols] + up[FFN_HALO:, :] * wc_ref[CONV_W - 1:CONV_W, cols]
            for j in range(1, CONV_W):
                prev = pltpu.roll(up, j, 0)[FFN_HALO:, :]
                conv = conv + prev * wc_ref[CONV_W - 1 - j:CONV_W - j, cols]
            halves.append(conv)
        act_ref[:, c * FFN_COLS:(c + 1) * FFN_COLS] = _silu_gate(halves[1], halves[0]).astype(BF16)


def _ffn_up_decode_kernel(h_ref, s0_ref, s1_ref, w_ref, wc_ref, bc_ref, act_ref, up_ref):
    dff = act_ref.shape[1]
    h = h_ref[...]
    for c in range(dff // FFN_COLS):
        halves = []
        for lo in (c * FFN_COLS, dff + c * FFN_COLS):
            cols = slice(lo, lo + FFN_COLS)
            up = _dot(h, w_ref[:, cols])
            up_ref[:, cols] = up
            halves.append(bc_ref[:, cols] + s0_ref[:, cols] * wc_ref[0:1, cols]
                          + s1_ref[:, cols] * wc_ref[1:2, cols] + up * wc_ref[2:3, cols])
        act_ref[:, c * FFN_COLS:(c + 1) * FFN_COLS] = _silu_gate(halves[1], halves[0]).astype(BF16)


def _ffn_up_prompt(h, seq, w_up, w_conv, b_conv, tm):
    n, d = h.shape
    dff = w_up.shape[1] // 2
    assert dff % FFN_COLS == 0 and seq % tm == 0 and tm % FFN_HALO == 0
    ratio = tm // FFN_HALO
    act, tail = pl.pallas_call(
        functools.partial(_ffn_up_prompt_kernel, tiles_per_seq=seq // tm),
        grid=(n // tm,),
        in_specs=[pl.BlockSpec((tm, d), lambda i: (i, 0)),
                  pl.BlockSpec((FFN_HALO, d), lambda i: (jnp.maximum(i * ratio - 1, 0), 0)),
                  _full(w_up.shape), _full(w_conv.shape), _full((1, 2 * dff))],
        out_specs=[pl.BlockSpec((tm, dff), lambda i: (i, 0)),
                   pl.BlockSpec((1, 8, 2 * dff), lambda i: (i, 0, 0))],
        out_shape=[jax.ShapeDtypeStruct((n, dff), BF16),
                   jax.ShapeDtypeStruct((n // tm, 8, 2 * dff), F32)],
        compiler_params=_params("arbitrary"),
        name="ffn_up_prompt",
    )(h, h, w_up, w_conv, b_conv.reshape(1, -1))
    last = tail.reshape(n // seq, seq // tm, 8, 2 * dff)[:, -1, 8 - (CONV_W - 1):, :]
    return act, last


def _ffn_up_decode(h, state, w_up, w_conv, b_conv):
    n, d = h.shape
    dff = w_up.shape[1] // 2
    act, up = pl.pallas_call(
        _ffn_up_decode_kernel,
        grid=(1,),
        in_specs=[_full((n, d)), _full((n, 2 * dff)), _full((n, 2 * dff)),
                  _full(w_up.shape), _full(w_conv.shape), _full((1, 2 * dff))],
        out_specs=[_full((n, dff)), _full((n, 2 * dff))],
        out_shape=[jax.ShapeDtypeStruct((n, dff), BF16), jax.ShapeDtypeStruct((n, 2 * dff), F32)],
        compiler_params=_params("arbitrary"),
        name="ffn_up_decode",
    )(h, state[:, 0], state[:, 1], w_up, w_conv, b_conv.reshape(1, -1))
    return act, jnp.stack([state[:, 1], up], axis=1)


def _odd_in_kernel(h_ref, w_ref, c64_ref, a64_ref, b64_ref, c128_ref, a128_ref, b128_ref,
                   q_ref, k_ref, v_ref, kb_ref, vt_ref, qi_ref, kif_ref, ki2_ref, wit_ref,
                   *, half64, half128, idx_dh):
    cq = q_ref.shape[1]
    ckv = k_ref.shape[1]
    ciq = qi_ref.shape[1]
    h = h_ref[...]
    t64 = (c64_ref[...], a64_ref[...], b64_ref[...])
    t128 = (c128_ref[...], a128_ref[...], b128_ref[...])
    q_ref[...] = _rope_apply(_dot(h, w_ref[:, 0:cq]), *t128, half128).astype(BF16)
    k = _rope_apply(_dot(h, w_ref[:, cq:cq + ckv]), *t128, half128)
    k_ref[...] = k
    kb_ref[...] = k.astype(BF16)
    v = _dot(h, w_ref[:, cq + ckv:cq + 2 * ckv])
    v_ref[...] = v
    if vt_ref is not None:
        vt_ref[...] = v.T.astype(BF16)
    lo = cq + 2 * ckv
    qi_ref[...] = _rope_apply(_dot(h, w_ref[:, lo:lo + ciq]), *t64, half64).astype(BF16)
    tail = _dot(h, w_ref[:, lo + ciq:lo + ciq + LANE])
    ki = _rope_apply(tail, *t64, half64)
    kif_ref[...] = ki
    lane = lax.broadcasted_iota(jnp.int32, ki.shape, 1)
    ki2_ref[...] = jnp.where(lane < idx_dh, ki, pltpu.roll(ki, idx_dh, 1)).astype(BF16)
    wit_ref[...] = tail.T[idx_dh:idx_dh + 8, :] * (IDX_HEADS ** -0.5)


def _odd_in(h, seq, w_in, tabs64, half64, tabs128, half128, c_dh, idx_dh, tm, with_vt):
    n, d = h.shape
    tm = min(tm, n)
    cq = C_HEADS * c_dh
    ckv = C_KV_HEADS * c_dh
    ciq = IDX_HEADS * idx_dh
    cols = cq + 2 * ckv + ciq + LANE
    assert 2 * idx_dh == LANE and w_in.shape[1] <= cols
    w = jnp.pad(w_in, ((0, 0), (0, cols - w_in.shape[1]))).astype(BF16)
    if seq is None:
        tab = pl.BlockSpec((1, LANE), lambda i: (0, 0))
    else:
        tiles_per_seq = seq // tm
        tab = pl.BlockSpec((tm, LANE), lambda i: (i % tiles_per_seq, 0))
    row = lambda width, dt: (pl.BlockSpec((tm, width), lambda i: (i, 0)), jax.ShapeDtypeStruct((n, width), dt))
    col = lambda height, dt: (pl.BlockSpec((height, tm), lambda i: (0, i)), jax.ShapeDtypeStruct((height, n), dt))
    outs = [row(cq, BF16), row(ckv, F32), row(ckv, F32), row(ckv, BF16)]
    if with_vt:
        outs.append(col(ckv, BF16))
    outs += [row(ciq, BF16), row(LANE, F32), row(LANE, BF16), col(8, F32)]

    def body(*refs):
        refs = list(refs)
        if not with_vt:
            refs.insert(8 + 4, None)
        _odd_in_kernel(*refs, half64=half64, half128=half128, idx_dh=idx_dh)

    res = pl.pallas_call(
        body,
        grid=(n // tm,),
        in_specs=[pl.BlockSpec((tm, d), lambda i: (i, 0)), _full(w.shape)] + [tab] * 6,
        out_specs=[o[0] for o in outs],
        out_shape=[o[1] for o in outs],
        compiler_params=_params("arbitrary"),
        name="odd_in_prompt" if with_vt else "odd_in_decode",
    )(h, w, *tabs64, *tabs128)
    res = list(res)
    if not with_vt:
        res.insert(4, None)
    return res


FLOAT_BITS = 32
INDEX_BIG = 2 ** 30


def _ind(cond):
    return jnp.where(cond, 1.0, 0.0)


def _ordered_bits(x, to):
    b = x if x.dtype == jnp.int32 else lax.bitcast_convert_type(x, jnp.int32)
    b = jnp.where(b < 0, b ^ 0x7FFFFFFF, b)
    return b if to == jnp.int32 else lax.bitcast_convert_type(b, F32)


def _select_threshold(count_fn, lo, hi, active, keep, index_bits):
    keep_f = jnp.float32(keep)

    def step(_, st):
        lo_k, hi_k = st
        mid_k = (lax.shift_right_arithmetic(lo_k, 1) + lax.shift_right_arithmetic(hi_k, 1)
                 + (lo_k & hi_k & 1))
        enough = count_fn(lambda s, idx: _ind(s >= _ordered_bits(mid_k, F32))) >= keep_f
        move = mid_k > lo_k
        lo_k = jnp.where(move, jnp.where(enough, mid_k, lo_k), lo_k)
        hi_k = jnp.where(move, jnp.where(enough, hi_k, mid_k), hi_k)
        return lo_k, hi_k

    lo_k, hi_k = lax.fori_loop(0, FLOAT_BITS, step, (_ordered_bits(lo, jnp.int32), _ordered_bits(hi, jnp.int32)))
    lo, hi = _ordered_bits(lo_k, F32), _ordered_bits(hi_k, F32)
    at_max = count_fn(lambda s, idx: _ind(s >= hi)) >= keep_f
    thr = jnp.where(at_max, hi, lo)
    n_gt = count_fn(lambda s, idx: _ind(s > thr))
    n_eq = count_fn(lambda s, idx: _ind(s == thr))
    need = keep_f - n_gt
    excess = jnp.where(active, jnp.where(n_eq > need, 1, 0), 0)

    def tie_search():
        lo_i = jnp.full(thr.shape, -1, jnp.int32)
        hi_i = jnp.full(thr.shape, (1 << index_bits) - 1, jnp.int32)
        for _ in range(index_bits + 1):
            mid_i = lax.shift_right_arithmetic(lo_i + hi_i, 1)
            ok = count_fn(lambda s, idx: jnp.where(s == thr, _ind(idx <= mid_i), 0.0)) >= need
            hi_i = jnp.where(ok, mid_i, hi_i)
            lo_i = jnp.where(ok, lo_i, mid_i)
        return hi_i

    cut = lax.cond(jnp.max(excess) > 0, tie_search, lambda: jnp.full(thr.shape, INDEX_BIG, jnp.int32))
    thr = jnp.where(active, thr, -jnp.inf)
    cut = jnp.where(active, cut, -1)
    return thr, cut


def _selected(s, idx, thr, cut):
    return jnp.where(s > thr, 1.0, jnp.where(s == thr, _ind(idx <= cut), 0.0))


def _dsa_prompt_kernel(q_ref, qi_ref, wit_ref, kb_ref, vt_ref, ki2_ref, o_ref,
                       sc_ref, m_ref, l_ref, acc_ref, *, c_dh, idx_dh, keep, kc):
    tq = q_ref.shape[0]
    seq = kb_ref.shape[0]
    t0 = pl.program_id(1) * tq
    nch = (t0 + tq + kc - 1) // kc
    scale = c_dh ** -0.5
    neg_inf = jnp.float32(-jnp.inf)
    lane = lax.broadcasted_iota(jnp.int32, (tq, LANE), 1)
    key_iota = lax.broadcasted_iota(jnp.int32, (kc, tq), 0)
    qpos = t0 + lax.broadcasted_iota(jnp.int32, (1, tq), 1)

    qi = qi_ref[...]
    per_group = LANE // idx_dh
    qms = []
    for h in range(IDX_HEADS):
        grp = qi[:, (h // per_group) * LANE:(h // per_group + 1) * LANE]
        qms.append(jnp.where(_head_of(lane, idx_dh) == h % per_group, grp, jnp.zeros_like(grp)))
    wit = wit_ref[...]

    def score_chunk(c, carry):
        lo, hi = carry
        start = pl.multiple_of(c * kc, kc)
        kic = ki2_ref[pl.ds(start, kc), :]
        acc = jnp.zeros((kc, tq), F32)
        for h in range(IDX_HEADS):
            acc = acc + jnp.maximum(_dot_nt(kic, qms[h]), 0.0) * wit[h:h + 1, :]
        score = acc * (idx_dh ** -0.5)
        adm = start + key_iota <= qpos
        sc_ref[pl.ds(start, kc), :] = jnp.where(adm, score, neg_inf)
        lo = jnp.minimum(lo, jnp.min(jnp.where(adm, score, jnp.inf), axis=0, keepdims=True))
        hi = jnp.maximum(hi, jnp.max(jnp.where(adm, score, neg_inf), axis=0, keepdims=True))
        return lo, hi

    lo, hi = lax.fori_loop(0, nch, score_chunk,
                           (jnp.full((1, tq), jnp.inf, F32), jnp.full((1, tq), neg_inf, F32)))

    def count_fn(pred):
        def body(c, cnt):
            start = pl.multiple_of(c * kc, kc)
            hit = pred(sc_ref[pl.ds(start, kc), :], start + key_iota)
            return cnt + jnp.sum(hit.reshape(kc // 8, 8, tq), axis=0)
        part = lax.fori_loop(0, nch, body, jnp.zeros((8, tq), F32))
        return jnp.sum(part, axis=0, keepdims=True)

    thr, cut = _select_threshold(count_fn, lo, hi, qpos + 1 > keep, keep, (seq - 1).bit_length())

    m_ref[...] = jnp.full(m_ref.shape, NEG_BIG, F32)
    l_ref[...] = jnp.zeros(l_ref.shape, F32)
    acc_ref[...] = jnp.zeros(acc_ref.shape, F32)
    group = C_HEADS // C_KV_HEADS
    head_row = lax.broadcasted_iota(jnp.int32, m_ref.shape, 0)

    def attend_chunk(c, _):
        start = pl.multiple_of(c * kc, kc)
        keepm = _selected(sc_ref[pl.ds(start, kc), :], start + key_iota, thr, cut) > 0.0
        kbc = kb_ref[pl.ds(start, kc), :]
        vtc = vt_ref[:, pl.ds(start, kc)]
        m_old, l_old = m_ref[...], l_ref[...]
        m_all = jnp.zeros(m_old.shape, F32)
        l_all = jnp.zeros(l_old.shape, F32)
        scores = [_dot_nt(kbc[:, (h // group) * c_dh:(h // group + 1) * c_dh], q_ref[:, h * c_dh:(h + 1) * c_dh])
                  for h in range(C_HEADS)]
        probs, alphas = [], []
        for h in range(C_HEADS):
            s = jnp.where(keepm, scores[h], neg_inf)
            m_new = jnp.maximum(m_old[h:h + 1, :], jnp.max(s, axis=0, keepdims=True))
            alpha = jnp.exp((m_old[h:h + 1, :] - m_new) * scale)
            p = jnp.exp((s - m_new) * scale)
            alphas.append(alpha)
            probs.append(p.astype(BF16))
            m_all = jnp.where(head_row == h, m_new, m_all)
            l_all = jnp.where(head_row == h, l_old[h:h + 1, :] * alpha + jnp.sum(p, axis=0, keepdims=True), l_all)
        pvs = [_dot(vtc[(h // group) * c_dh:(h // group + 1) * c_dh, :], probs[h]) for h in range(C_HEADS)]
        m_ref[...] = m_all
        l_ref[...] = l_all
        for h in range(C_HEADS):
            acc_ref[h] = acc_ref[h] * alphas[h] + pvs[h]
        return 0

    lax.fori_loop(0, nch, attend_chunk, 0)
    for h in range(C_HEADS):
        o_ref[:, h * c_dh:(h + 1) * c_dh] = (acc_ref[h] / l_ref[h:h + 1, :]).T.astype(BF16)


def _dsa_prompt(q, qi, wit, kb, vt, ki2, batch, seq, c_dh, idx_dh, keep, tq, kc):
    n = q.shape[0]
    tiles = seq // tq
    assert c_dh == LANE and seq % kc == 0 and seq % tq == 0
    return pl.pallas_call(
        functools.partial(_dsa_prompt_kernel, c_dh=c_dh, idx_dh=idx_dh, keep=keep, kc=kc),
        grid=(batch, tiles),
        in_specs=[pl.BlockSpec((tq, q.shape[1]), lambda b, i: (b * tiles + i, 0)),
                  pl.BlockSpec((tq, qi.shape[1]), lambda b, i: (b * tiles + i, 0)),
                  pl.BlockSpec((8, tq), lambda b, i: (0, b * tiles + i)),
                  pl.BlockSpec((seq, kb.shape[1]), lambda b, i: (b, 0)),
                  pl.BlockSpec((vt.shape[0], seq), lambda b, i: (0, b)),
                  pl.BlockSpec((seq, LANE), lambda b, i: (b, 0))],
        out_specs=pl.BlockSpec((tq, q.shape[1]), lambda b, i: (b * tiles + i, 0)),
        out_shape=jax.ShapeDtypeStruct(q.shape, BF16),
        scratch_shapes=[pltpu.VMEM((seq, tq), F32), pltpu.VMEM((8, tq), F32), pltpu.VMEM((8, tq), F32),
                        pltpu.VMEM((C_HEADS, c_dh, tq), F32)],
        compiler_params=_params("arbitrary", "arbitrary"),
        name="dsa_prompt",
    )(q, qi, wit, kb, vt, ki2)


PAGE_ROWS = 128


def _head_expander(heads, dh):
    r = np.arange(LANE)[:, None]
    c = np.arange(heads * dh)[None, :]
    return jnp.asarray(c // dh == r, BF16)


def _lane_bcast_col(row):
    return jnp.broadcast_to(row, (LANE, row.shape[1])).T


def _moba_decode_kernel(pt_ref, q_ref, kn_ref, vn_ref, *rest, n_pages, dh):
    kt_refs, vt_refs, o_ref = rest[:n_pages], rest[n_pages:2 * n_pages], rest[2 * n_pages]
    bw = q_ref.shape[2]
    heads = bw // dh
    ppb = MOBA_BLOCK // PAGE_ROWS
    nblk = n_pages // ppb
    neg_inf = jnp.float32(-jnp.inf)
    qb = _lane_bcast_col(q_ref[0].astype(F32))

    def head_sums(x):
        return jnp.sum(x.reshape(heads, dh, LANE), axis=1)

    def head_rows(x):
        return jnp.broadcast_to(x[:, None, :], (heads, dh, LANE)).reshape(bw, LANE)

    s_pages = [head_sums(kt_refs[j][...] * qb) for j in range(n_pages)]
    s_own = head_sums(_lane_bcast_col(kn_ref[0]) * qb)
    gates = []
    for n in range(nblk):
        g = jnp.zeros((heads, 1), F32)
        for j in range(n * ppb, (n + 1) * ppb):
            g = g + jnp.sum(s_pages[j], axis=1, keepdims=True)
        gates.append(g)
    sels = []
    for n in range(nblk):
        rank = jnp.zeros((heads, 1), F32)
        for m in range(nblk):
            if m != n:
                beats = gates[m] >= gates[n] if m < n else gates[m] > gates[n]
                rank = rank + jnp.where(beats, 1.0, 0.0)
        sels.append(rank < MOBA_TOPK)
    m = s_own
    masked = []
    for j in range(n_pages):
        sm = jnp.where(sels[j // ppb], s_pages[j], neg_inf)
        masked.append(sm)
        m = jnp.maximum(m, jnp.max(sm, axis=1, keepdims=True))
    e_own = jnp.exp(s_own - m)
    es = [jnp.exp(sm - m) for sm in masked]
    l = e_own
    for e in es:
        l = l + jnp.sum(e, axis=1, keepdims=True)
    rinv = 1.0 / l
    lane = lax.broadcasted_iota(jnp.int32, (bw, LANE), 1)
    acc = jnp.where(lane == 0, head_rows(e_own * rinv) * _lane_bcast_col(vn_ref[0]), 0.0)
    for j in range(n_pages):
        acc = acc + head_rows(es[j] * rinv) * vt_refs[j][...]
    out_col = jnp.sum(acc, axis=1, keepdims=True)
    o_ref[0] = jnp.broadcast_to(out_col, (bw, LANE)).T[0:1].astype(BF16)


def _moba_decode(q, k_new, v_new, cache_k, cache_v, page_table, dh):
    n, bw = q.shape
    n_pages = page_table.shape[1]
    assert cache_k.shape[1] == PAGE_ROWS == LANE and (n_pages * PAGE_ROWS) % MOBA_BLOCK == 0
    ckt = jnp.transpose(cache_k, (0, 2, 3, 1)).reshape(cache_k.shape[0], bw, PAGE_ROWS)
    cvt = jnp.transpose(cache_v, (0, 2, 3, 1)).reshape(cache_v.shape[0], bw, PAGE_ROWS)
    seq_row = pl.BlockSpec((1, 1, bw), lambda b, pt: (b, 0, 0))
    pages = [pl.BlockSpec((None, bw, PAGE_ROWS), lambda b, pt, j=j: (pt[b, j], 0, 0)) for j in range(n_pages)]
    out = pl.pallas_call(
        functools.partial(_moba_decode_kernel, n_pages=n_pages, dh=dh),
        grid_spec=pltpu.PrefetchScalarGridSpec(
            num_scalar_prefetch=1, grid=(n,),
            in_specs=[seq_row, seq_row, seq_row] + pages + pages,
            out_specs=seq_row),
        out_shape=jax.ShapeDtypeStruct((n, 1, bw), BF16),
        compiler_params=_params("arbitrary"),
        name="moba_decode",
    )(page_table, q.reshape(n, 1, bw), k_new.reshape(n, 1, bw), v_new.reshape(n, 1, bw),
      *([ckt] * n_pages), *([cvt] * n_pages))
    return out.reshape(n, bw)


def _dsa_decode_score_kernel(pt_ref, qi_ref, wi_ref, kin_ref, *rest, n_pages, idx_dh):
    kit_refs, o_ref = rest[:n_pages], rest[n_pages]
    qb = _lane_bcast_col(qi_ref[0].astype(F32))
    wi8 = wi_ref[0]
    scale = idx_dh ** -0.5

    def score(kit):
        total = jnp.zeros((1, LANE), F32)
        for h in range(IDX_HEADS):
            sc = jnp.sum(kit * qb[h * idx_dh:(h + 1) * idx_dh], axis=0, keepdims=True)
            total = total + jnp.maximum(sc, 0.0) * wi8[h:h + 1, :]
        return total * scale

    o_ref[0] = jnp.full(o_ref.shape[1:], -jnp.inf, F32)
    for j in range(n_pages):
        o_ref[0, j:j + 1, :] = score(kit_refs[j][...])
    own = score(_lane_bcast_col(kin_ref[0]))
    lane = lax.broadcasted_iota(jnp.int32, (1, PAGE_ROWS), 1)
    o_ref[0, n_pages:n_pages + 1, :] = jnp.where(lane == 0, own, -jnp.inf)


def _dsa_decode_select_kernel(s_ref, m_ref, *, keep, n_valid):
    s = s_ref[...]
    idx = lax.broadcasted_iota(jnp.int32, s.shape, 1)
    neg_inf = jnp.float32(-jnp.inf)
    lo = jnp.min(jnp.where(s > neg_inf, s, jnp.inf), axis=1, keepdims=True)
    hi = jnp.max(s, axis=1, keepdims=True)
    count_fn = lambda pred: jnp.sum(pred(s, idx), axis=1, keepdims=True)
    active = jnp.full(lo.shape, n_valid > keep, jnp.bool_)
    thr, cut = _select_threshold(count_fn, lo, hi, active, keep, (s.shape[1] - 1).bit_length())
    m_ref[...] = _selected(s, idx, thr, cut)


def _dsa_decode_attn_kernel(pt_ref, q_ref, kn_ref, vn_ref, mask_ref, e_ref, *rest, n_pages, c_dh):
    k_refs, v_refs, o_ref = rest[:n_pages], rest[n_pages:2 * n_pages], rest[2 * n_pages]
    rows = k_refs[0].shape[0]
    group = C_HEADS // C_KV_HEADS
    scale = c_dh ** -0.5
    neg_inf = jnp.float32(-jnp.inf)
    qm = jnp.concatenate([q_ref[0].astype(F32), jnp.zeros((LANE - C_HEADS, c_dh), F32)], axis=0).astype(BF16)
    row_id = lax.broadcasted_iota(jnp.int32, (rows, LANE), 0)
    lane_id = lax.broadcasted_iota(jnp.int32, (rows, LANE), 1)
    own_kv = (row_id & (C_KV_HEADS - 1)) == _head_of(lane_id, group)
    mask = mask_ref[0]
    mask_t = jnp.concatenate([mask, jnp.zeros((LANE - mask.shape[0], PAGE_ROWS), F32)], axis=0).T
    repeat = jnp.where(_head_of(row_id, C_KV_HEADS) == lane_id, 1.0, 0.0).astype(BF16)
    row_mask = _dot(repeat, mask_t.astype(BF16))
    kn = jnp.concatenate([kn_ref[0], jnp.zeros((8 - C_KV_HEADS, c_dh), F32)], axis=0)
    s_own_kv = _dot_nt(kn.astype(BF16), qm)
    lane_kv = _head_of(lax.broadcasted_iota(jnp.int32, (1, LANE), 1), group)
    s_own = jnp.zeros((1, LANE), F32)
    for kv in range(C_KV_HEADS):
        s_own = jnp.where(lane_kv == kv, s_own_kv[kv:kv + 1, :], s_own)
    s_own = jnp.where(mask[n_pages:n_pages + 1, 0:1] > 0.0, s_own, neg_inf)
    m = jnp.maximum(s_own, NEG_BIG)
    masked = []
    for j in range(n_pages):
        s = _dot_nt(k_refs[j][...].astype(BF16), qm)
        sm = jnp.where(own_kv, jnp.where(row_mask[:, j:j + 1] > 0.5, s, neg_inf), neg_inf)
        masked.append(sm)
        m = jnp.maximum(m, jnp.max(sm, axis=0, keepdims=True))
    e_own = jnp.exp((s_own - m) * scale)
    es = [jnp.exp((sm - m) * scale) for sm in masked]
    l = e_own
    for e in es:
        l = l + jnp.sum(e, axis=0, keepdims=True)
    rinv = jnp.where(l > 0.0, 1.0 / l, 0.0)
    expand = e_ref[...]
    acc = jnp.zeros((rows, C_HEADS * c_dh), F32)
    for j in range(n_pages):
        acc = acc + _dot((es[j] * rinv).astype(BF16), expand) * jnp.concatenate([v_refs[j][...]] * C_HEADS, axis=1)
    p_own = _dot(jnp.broadcast_to(e_own * rinv, (8, LANE)).astype(BF16), expand)[0:1]
    vn = vn_ref[0]
    vn_heads = jnp.concatenate([vn[h // group:h // group + 1, :] for h in range(C_HEADS)], axis=1)
    o_ref[0] = (jnp.sum(acc, axis=0, keepdims=True) + p_own * vn_heads).astype(BF16)


def _dsa_decode(q, qi, wit, k_new, v_new, ki_new, cache_k, cache_v, cache_ki, page_table, c_dh, idx_dh, keep):
    n = q.shape[0]
    n_pages = page_table.shape[1]
    assert cache_k.shape[1] == PAGE_ROWS == LANE
    pool = cache_k.shape[0]
    ck = cache_k.reshape(pool, PAGE_ROWS * C_KV_HEADS, c_dh)
    cv = cache_v.reshape(pool, PAGE_ROWS * C_KV_HEADS, c_dh)
    ckit = jnp.transpose(cache_ki, (0, 2, 1))
    rows = -(-(n_pages + 1) // 8) * 8
    wi8 = wit.T.reshape(n, 8, 1)
    seq3 = lambda a, b: pl.BlockSpec((1, a, b), lambda s, pt: (s, 0, 0))
    pages = lambda r, w: [pl.BlockSpec((None, r, w), lambda s, pt, j=j: (pt[s, j], 0, 0)) for j in range(n_pages)]
    scores = pl.pallas_call(
        functools.partial(_dsa_decode_score_kernel, n_pages=n_pages, idx_dh=idx_dh),
        grid_spec=pltpu.PrefetchScalarGridSpec(
            num_scalar_prefetch=1, grid=(n,),
            in_specs=[seq3(1, IDX_HEADS * idx_dh), seq3(8, 1), seq3(1, idx_dh)] + pages(idx_dh, PAGE_ROWS),
            out_specs=seq3(rows, PAGE_ROWS)),
        out_shape=jax.ShapeDtypeStruct((n, rows, PAGE_ROWS), F32),
        compiler_params=_params("arbitrary"),
        name="dsa_decode_scores",
    )(page_table, qi.reshape(n, 1, IDX_HEADS * idx_dh), wi8, ki_new.reshape(n, 1, idx_dh), *([ckit] * n_pages))
    flat = rows * PAGE_ROWS
    mask = pl.pallas_call(
        functools.partial(_dsa_decode_select_kernel, keep=keep, n_valid=n_pages * PAGE_ROWS + 1),
        grid=(1,),
        in_specs=[_full((n, flat))],
        out_specs=_full((n, flat)),
        out_shape=jax.ShapeDtypeStruct((n, flat), F32),
        compiler_params=_params("arbitrary"),
        name="dsa_decode_select",
    )(scores.reshape(n, flat))
    cq = C_HEADS * c_dh
    out = pl.pallas_call(
        functools.partial(_dsa_decode_attn_kernel, n_pages=n_pages, c_dh=c_dh),
        grid_spec=pltpu.PrefetchScalarGridSpec(
            num_scalar_prefetch=1, grid=(n,),
            in_specs=[seq3(C_HEADS, c_dh), seq3(C_KV_HEADS, c_dh), seq3(C_KV_HEADS, c_dh), seq3(rows, PAGE_ROWS),
                      pl.BlockSpec((LANE, cq), lambda s, pt: (0, 0))]
                     + pages(PAGE_ROWS * C_KV_HEADS, c_dh) + pages(PAGE_ROWS * C_KV_HEADS, c_dh),
            out_specs=seq3(1, cq)),
        out_shape=jax.ShapeDtypeStruct((n, 1, cq), BF16),
        compiler_params=_params("arbitrary"),
        name="dsa_decode_attn",
    )(page_table, q.reshape(n, C_HEADS, c_dh), k_new.reshape(n, C_KV_HEADS, c_dh),
      v_new.reshape(n, C_KV_HEADS, c_dh), mask.reshape(n, rows, PAGE_ROWS), _head_expander(C_HEADS, c_dh),
      *([ck] * n_pages), *([cv] * n_pages))
    return out.reshape(n, cq)


def kernel(x_prompt, x_sample, cache_moba_k, cache_moba_v, cache_dsa_k, cache_dsa_v, cache_dsa_kidx,
           state_ffn_conv, page_table, w_in_even, g_sgu, w_sgu, b_sgu, w_out_even, w_in_odd, w_out_odd,
           g_mix, g_ffn, w_up, w_conv, b_conv, w_down, g_final):
    batch, seq, d = x_prompt.shape
    n_dec, dec_seq, _ = x_sample.shape
    depth = g_mix.shape[0]
    n_pages = page_table.shape[1]
    past_len = n_pages * cache_moba_k.shape[2]
    assert depth == 2 and dec_seq == 1, "one even + one odd layer, one decode token per sequence"
    b_dh = cache_moba_k.shape[-1]
    c_dh = cache_dsa_k.shape[-1]
    idx_dh = cache_dsa_kidx.shape[-1]
    keep_p = min(DSA_TOPK, seq // 4)
    keep_s = min(DSA_TOPK, (past_len + dec_seq) // 4)
    xp = x_prompt.reshape(batch * seq, d)
    xs = x_sample.reshape(n_dec, d)
    bf = lambda w: w.astype(BF16)

    tp64, half64 = _rope_tables(b_dh, seq, 0)
    ts64, _ = _rope_tables(b_dh, 1, past_len)
    tp128, half128 = _rope_tables(c_dh, seq, 0)
    ts128, _ = _rope_tables(c_dh, 1, past_len)
    assert idx_dh == b_dh

    w_in0, w_out0 = bf(w_in_even[0]), bf(w_out_even[0])
    a_p, q_p, mk_p, mv_p, kb_p, vt_p, km_p = _even_in_prompt(
        xp, seq, g_mix[0], w_in0, tp64, half64, g_sgu[0], w_sgu[0], b_sgu[0])
    bo_p = _moba_prompt(q_p, kb_p, vt_p, km_p, batch, seq, b_dh)
    a_s, q_s, mk_s, mv_s, va_s = _even_in_decode(xs, g_mix[0], w_in0, ts64, half64, g_sgu[0], w_sgu[0], b_sgu[0])
    bo_s = _moba_decode(q_s, mk_s, mv_s, cache_moba_k[0], cache_moba_v[0], page_table, b_dh)
    w_up0, w_down0 = bf(w_up[0]), bf(w_down[0])
    xp, hp = _residual_proj(xp, [a_p, bo_p], w_out0, g_ffn[0], "next", 256, "out_even_prompt")
    xs, hs = _residual_proj(xs, [a_s, bo_s], w_out0, g_ffn[0], "next", 256, "out_even_decode")
    act_p, conv_p0 = _ffn_up_prompt(hp, seq, w_up0, w_conv[0], b_conv[0], 512)
    act_s, conv_s0 = _ffn_up_decode(hs, state_ffn_conv[0], w_up0, w_conv[0], b_conv[0])
    xp, hp = _residual_proj(xp, [act_p], w_down0, g_mix[1], "next", 256, "ffn_down0_prompt")
    xs, hs = _residual_proj(xs, [act_s], w_down0, g_mix[1], "next", 256, "ffn_down0_decode")

    w_out1 = bf(w_out_odd[0])
    q_p, dk_p, dv_p, kb_p, vt_p, qi_p, kif_p, ki2_p, wit_p = _odd_in(
        hp, seq, w_in_odd[0], tp64, half64, tp128, half128, c_dh, idx_dh, 256, True)
    o_p = _dsa_prompt(q_p, qi_p, wit_p, kb_p, vt_p, ki2_p, batch, seq, c_dh, idx_dh, keep_p, 256, 256)
    q_s, dk_s, dv_s, _, _, qi_s, kif_s, _, wit_s = _odd_in(
        hs, None, w_in_odd[0], ts64, half64, ts128, half128, c_dh, idx_dh, 256, False)
    o_s = _dsa_decode(q_s, qi_s, wit_s, dk_s, dv_s, kif_s[:, :idx_dh], cache_dsa_k[0], cache_dsa_v[0],
                      cache_dsa_kidx[0], page_table, c_dh, idx_dh, keep_s)
    w_up1, w_down1 = bf(w_up[1]), bf(w_down[1])
    xp, hp = _residual_proj(xp, [o_p], w_out1, g_ffn[1], "next", 256, "out_odd_prompt")
    xs, hs = _residual_proj(xs, [o_s], w_out1, g_ffn[1], "next", 256, "out_odd_decode")
    act_p, conv_p1 = _ffn_up_prompt(hp, seq, w_up1, w_conv[1], b_conv[1], 512)
    act_s, conv_s1 = _ffn_up_decode(hs, state_ffn_conv[1], w_up1, w_conv[1], b_conv[1])
    y_p, = _residual_proj(xp, [act_p], w_down1, g_final, "final", 256, "ffn_down1_prompt")
    y_s, = _residual_proj(xs, [act_s], w_down1, g_final, "final", 256, "ffn_down1_decode")

    ckv = C_KV_HEADS
    return (y_p.reshape(batch, seq, d), y_s.reshape(n_dec, dec_seq, d),
            mk_p.reshape(1, batch, seq, B_HEADS, b_dh), mv_p.reshape(1, batch, seq, B_HEADS, b_dh),
            mk_s.reshape(1, n_dec, dec_seq, B_HEADS, b_dh), mv_s.reshape(1, n_dec, dec_seq, B_HEADS, b_dh),
            va_s.reshape(1, n_dec, dec_seq, A_GROUPS, LANE),
            dk_p.reshape(1, batch, seq, ckv, c_dh), dv_p.reshape(1, batch, seq, ckv, c_dh),
            kif_p[:, :idx_dh].reshape(1, batch, seq, idx_dh),
            dk_s.reshape(1, n_dec, dec_seq, ckv, c_dh), dv_s.reshape(1, n_dec, dec_seq, ckv, c_dh),
            kif_s[:, :idx_dh].reshape(1, n_dec, dec_seq, idx_dh),
            jnp.stack([conv_p0, conv_p1]), jnp.stack([conv_s0, conv_s1]))
```

```python
import functools

import numpy as np
import jax
import jax.numpy as jnp
from jax import lax
from jax.experimental import pallas as pl
from jax.experimental.pallas import tpu as pltpu

F32 = jnp.float32
BF16 = jnp.bfloat16

A_GROUPS = 4
CHUNK = 128
B_HEADS = 8
MOBA_BLOCK = 256
MOBA_TOPK = 3
C_HEADS = 8
C_KV_HEADS = 2
IDX_HEADS = 4
DSA_TOPK = 256
ROPE_THETA = 500000.0
ROPE_FRAC = 4
CONV_W = 3
EPS = 1e-6

LANE = 128
SUBLANE_BF16 = 16
VMEM_LIMIT_BYTES = 56 * 1024 * 1024
NEG_BIG = -1e30
LOG2_E = 1.4426950408889634


def _params(*sem):
    return pltpu.CompilerParams(dimension_semantics=sem,
                                vmem_limit_bytes=VMEM_LIMIT_BYTES)


def _full(shape):
    nd = len(shape)
    return pl.BlockSpec(shape, lambda *_: (0,) * nd)


def _rmsnorm(x, g):
    ms = jnp.mean(x * x, axis=-1, keepdims=True)
    return x * lax.rsqrt(ms + EPS) * g


def _rope_apply(x, c, s1, s2, half):
    parts = []
    for j in range(x.shape[1] // LANE):
        xs = x[:, j * LANE:(j + 1) * LANE]
        parts.append(xs * c + pltpu.roll(xs, LANE - half, 1) * s1
                     + pltpu.roll(xs, half, 1) * s2)
    return parts[0] if len(parts) == 1 else jnp.concatenate(parts, axis=1)


def _head_of(index, dh):
    assert dh & (dh - 1) == 0
    return lax.shift_right_logical(index, dh.bit_length() - 1)


ONES_ROWS = SUBLANE_BF16


def _with_ones_rows(vt, dh):
    ones = jnp.ones((ONES_ROWS, vt.shape[1]), vt.dtype)
    parts = []
    for h in range(vt.shape[0] // dh):
        parts += [vt[h * dh:(h + 1) * dh], ones]
    return jnp.concatenate(parts, axis=0)


def _dot(a, b):
    return jnp.dot(a, b, preferred_element_type=F32)


def _dot_nt(a, b):
    return lax.dot_general(a, b, (((1,), (1,)), ((), ())),
                           preferred_element_type=F32)


def _rope_rows(dh):
    rd = dh // ROPE_FRAC
    half = rd // 2
    inv = ROPE_THETA ** (-jnp.arange(half, dtype=F32) * (2.0 / rd))
    r = np.arange(LANE) % dh
    j = np.where(r < half, r, np.where(r < rd, r - half, 0))
    inv_lane = jnp.where(jnp.asarray(r < rd), inv[j], 0.0).reshape(1, LANE)
    s1 = jnp.asarray(np.where(r < half, -1.0, 0.0), F32).reshape(1, LANE)
    s2 = jnp.asarray(np.where((r >= half) & (r < rd), 1.0, 0.0), F32).reshape(1, LANE)
    return inv_lane, s1, s2, half


def _rope_table_kernel(inv_ref, s1_ref, s2_ref, c_ref, a_ref, b_ref, *, base):
    rows = c_ref.shape[0]
    pos = lax.broadcasted_iota(jnp.int32, (rows, LANE), 0) + (base + pl.program_id(0) * rows)
    ang = pos.astype(F32) * inv_ref[...]
    s = jnp.sin(ang)
    c_ref[...] = jnp.cos(ang)
    a_ref[...] = s * s1_ref[...]
    b_ref[...] = s * s2_ref[...]


def _rope_tables(dh, n_pos, base):
    inv_lane, s1, s2, half = _rope_rows(dh)
    rows = min(n_pos, 512)
    assert n_pos % rows == 0
    out = jax.ShapeDtypeStruct((n_pos, LANE), F32)
    spec = pl.BlockSpec((rows, LANE), lambda i: (i, 0))
    c, a, b = pl.pallas_call(
        functools.partial(_rope_table_kernel, base=base),
        grid=(n_pos // rows,),
        in_specs=[_full((1, LANE))] * 3,
        out_specs=[spec] * 3,
        out_shape=[out] * 3,
        compiler_params=_params("arbitrary"),
        name="rope_tables",
    )(inv_lane, s1, s2)
    return (c, a, b), half


def _sgu_values(h, w_ref, gs_ref, aw):
    u = jax.nn.gelu(_dot(h, w_ref[:, 0:aw]))
    zv = jax.nn.gelu(_dot(h, w_ref[:, aw:2 * aw]))
    vs = []
    for g in range(A_GROUPS):
        vg = zv[:, g * LANE:(g + 1) * LANE]
        vs.append(_rmsnorm(vg, gs_ref[:, g * LANE:(g + 1) * LANE]))
    return u, vs


def _even_in_prompt_kernel(x_ref, g_ref, w_ref, c_ref, s1_ref, s2_ref, gs_ref, ws_ref, bs_ref,
                           a_ref, q_ref, kb_ref, kt_ref, vt_ref, vtb_ref, km_ref, v_scr, *, half, scale):
    tm = x_ref.shape[0]
    aw = A_GROUPS * LANE
    bw = q_ref.shape[1]
    h = _rmsnorm(x_ref[...], g_ref[...]).astype(BF16)
    u, vs = _sgu_values(h, w_ref, gs_ref, aw)
    tri = (lax.broadcasted_iota(jnp.int32, (CHUNK, CHUNK), 0)
           >= lax.broadcasted_iota(jnp.int32, (CHUNK, CHUNK), 1))
    for g in range(A_GROUPS):
        wg = jnp.where(tri, ws_ref[g], 0.0).astype(BF16)
        for cc in range(tm // CHUNK):
            rows = slice(cc * CHUNK, (cc + 1) * CHUNK)
            mixed = _dot(wg, vs[g][rows].astype(BF16)) + bs_ref[g]
            a_ref[rows, g * LANE:(g + 1) * LANE] = (u[rows, g * LANE:(g + 1) * LANE] * mixed).astype(BF16)
    c, s1, s2 = c_ref[...], s1_ref[...], s2_ref[...]
    q = _rope_apply(_dot(h, w_ref[:, 2 * aw:2 * aw + bw]), c, s1, s2, half)
    q_ref[...] = (q * scale).astype(BF16)
    k = _rope_apply(_dot(h, w_ref[:, 2 * aw + bw:2 * aw + 2 * bw]), c, s1, s2, half)
    kt_ref[...] = k.T
    kb_ref[...] = k.astype(BF16)
    km_ref[0] = jnp.sum(k, axis=0, keepdims=True) * (1.0 / tm)
    v_scr[...] = _dot(h, w_ref[:, 2 * aw + 2 * bw:2 * aw + 3 * bw])
    vt = v_scr[...].T
    vt_ref[...] = vt
    vtb_ref[...] = _with_ones_rows(vt, bw // B_HEADS).astype(BF16)


def _even_in_decode_kernel(x_ref, g_ref, w_ref, c_ref, s1_ref, s2_ref, gs_ref, w00_ref, b0_ref,
                           a_ref, q_ref, k_ref, v_ref, va_ref, *, half, scale):
    aw = A_GROUPS * LANE
    bw = q_ref.shape[1]
    h = _rmsnorm(x_ref[...], g_ref[...]).astype(BF16)
    u, vs = _sgu_values(h, w_ref, gs_ref, aw)
    va = jnp.concatenate(vs, axis=1)
    va_ref[...] = va
    a_ref[...] = (u * (w00_ref[...] * va + b0_ref[...])).astype(BF16)
    c, s1, s2 = c_ref[...], s1_ref[...], s2_ref[...]
    q = _rope_apply(_dot(h, w_ref[:, 2 * aw:2 * aw + bw]), c, s1, s2, half)
    q_ref[...] = (q * scale).astype(BF16)
    k_ref[...] = _rope_apply(_dot(h, w_ref[:, 2 * aw + bw:2 * aw + 2 * bw]), c, s1, s2, half)
    v_ref[...] = _dot(h, w_ref[:, 2 * aw + 2 * bw:2 * aw + 3 * bw])


def _even_in_prompt(x, seq, g_mix, w_in, tables, half, g_sgu, w_sgu, b_sgu):
    n, d = x.shape
    tm = MOBA_BLOCK
    bw = (w_in.shape[1] - 2 * A_GROUPS * LANE) // 3
    dh = bw // B_HEADS
    tiles_per_seq = seq // tm
    row = lambda w, dt: (pl.BlockSpec((tm, w), lambda i: (i, 0)), jax.ShapeDtypeStruct((n, w), dt))
    tab = pl.BlockSpec((tm, LANE), lambda i: (i % tiles_per_seq, 0))
    col = lambda h, dt: (pl.BlockSpec((None, h, tm), lambda i: (i // tiles_per_seq, 0, i % tiles_per_seq)),
                         jax.ShapeDtypeStruct((n // seq, h, seq), dt))
    outs = [row(A_GROUPS * LANE, BF16), row(bw, BF16), row(bw, BF16), col(bw, F32), col(bw, F32),
            col(bw + B_HEADS * ONES_ROWS, BF16),
            (pl.BlockSpec((1, 1, bw), lambda i: (i, 0, 0)), jax.ShapeDtypeStruct((n // tm, 1, bw), F32))]
    bs = jnp.broadcast_to(b_sgu[:, :, None], (A_GROUPS, CHUNK, LANE))
    return pl.pallas_call(
        functools.partial(_even_in_prompt_kernel, half=half, scale=dh ** -0.5),
        grid=(n // tm,),
        in_specs=[pl.BlockSpec((tm, d), lambda i: (i, 0)), _full((1, d)), _full(w_in.shape),
                  tab, tab, tab, _full((1, A_GROUPS * LANE)), _full(w_sgu.shape), _full(bs.shape)],
        out_specs=[o[0] for o in outs],
        out_shape=[o[1] for o in outs],
        scratch_shapes=[pltpu.VMEM((tm, bw), F32)],
        compiler_params=_params("arbitrary"),
        name="even_in_prompt",
    )(x, g_mix.reshape(1, d), w_in, *tables, g_sgu.reshape(1, -1), w_sgu, bs)


def _even_in_decode(x, g_mix, w_in, tables, half, g_sgu, w_sgu, b_sgu):
    n, d = x.shape
    bw = (w_in.shape[1] - 2 * A_GROUPS * LANE) // 3
    dh = bw // B_HEADS
    aw = A_GROUPS * LANE
    w00 = jnp.repeat(w_sgu[:, 0, 0], LANE).reshape(1, aw)
    b0 = jnp.repeat(b_sgu[:, 0], LANE).reshape(1, aw)
    shapes = [(aw, BF16), (bw, BF16), (bw, F32), (bw, F32), (aw, F32)]
    return pl.pallas_call(
        functools.partial(_even_in_decode_kernel, half=half, scale=dh ** -0.5),
        grid=(1,),
        in_specs=[_full((n, d)), _full((1, d)), _full(w_in.shape)] + [_full((1, LANE))] * 3
                 + [_full((1, aw))] * 3,
        out_specs=[_full((n, w)) for w, _ in shapes],
        out_shape=[jax.ShapeDtypeStruct((n, w), dt) for w, dt in shapes],
        compiler_params=_params("arbitrary"),
        name="even_in_decode",
    )(x, g_mix.reshape(1, d), w_in, *tables, g_sgu.reshape(1, -1), w00, b0)


def _moba_prompt_kernel(q_ref, k_ref, vt_ref, km_ref, o_ref, sel_ref, m_ref, acc_ref, *, dh):
    blk = MOBA_BLOCK
    tq = q_ref.shape[0]
    i = pl.program_id(1)
    nblk = km_ref.shape[1]
    heads = q_ref.shape[1] // dh
    ext = dh + ONES_ROWS
    per_group = LANE // dh
    lane = lax.broadcasted_iota(jnp.int32, (tq, LANE), 1)
    blk_id = lax.broadcasted_iota(jnp.int32, (nblk, tq), 0)
    neg_inf = jnp.float32(-jnp.inf)
    qms = []
    for h in range(heads):
        grp = h // per_group
        q = q_ref[:, grp * LANE:(grp + 1) * LANE]
        qm = jnp.where(_head_of(lane, dh) == h % per_group, q, jnp.zeros_like(q))
        qms.append(qm)
        kmean = km_ref[0, :, grp * LANE:(grp + 1) * LANE].astype(BF16)
        gate = jnp.where(blk_id < i, _dot_nt(kmean, qm), neg_inf)
        sel = jnp.zeros((nblk, tq), F32)
        for _ in range(MOBA_TOPK):
            best = jnp.max(gate, axis=0, keepdims=True)
            first = jnp.min(jnp.where(gate == best, blk_id, nblk), axis=0, keepdims=True)
            hit = blk_id == first
            sel = jnp.where(hit, 1.0, sel)
            gate = jnp.where(hit, neg_inf, gate)
        sel_ref[h] = jnp.where(blk_id < i, sel, 0.0)

    causal = (lax.broadcasted_iota(jnp.int32, (blk, tq), 0)
              <= lax.broadcasted_iota(jnp.int32, (blk, tq), 1))

    head_row = lax.broadcasted_iota(jnp.int32, (heads, tq), 0)

    def attend(kb, vtb, keeps, first):
        if not first:
            m_old = m_ref[...]
            acc_old = [acc_ref[h] for h in range(heads)]
        m_all = jnp.zeros((heads, tq), F32)
        scores = [_dot_nt(kb[:, (h // per_group) * LANE:(h // per_group + 1) * LANE], qms[h])
                  for h in range(heads)]
        probs, alphas = [], []
        for h in range(heads):
            s = jnp.where(keeps[h], scores[h], neg_inf)
            s_max = jnp.max(s, axis=0, keepdims=True)
            m_new = s_max if first else jnp.maximum(m_old[h:h + 1, :], s_max)
            if not first:
                alphas.append(jnp.exp(m_old[h:h + 1, :] - m_new))
            probs.append(jnp.exp(s - m_new).astype(BF16))
            m_all = jnp.where(head_row == h, m_new, m_all)
        pvs = [_dot(vtb[h * ext:(h + 1) * ext, :], probs[h]) for h in range(heads)]
        m_ref[...] = m_all
        for h in range(heads):
            acc_ref[h] = pvs[h] if first else acc_old[h] * alphas[h] + pvs[h]

    start = pl.multiple_of(i * blk, blk)
    attend(k_ref[pl.ds(start, blk), :], vt_ref[:, pl.ds(start, blk)], [causal] * heads, True)

    def body(n, _):
        start = pl.multiple_of(n * blk, blk)
        keeps = [sel_ref[h, pl.ds(n, 1), :] > 0.0 for h in range(heads)]
        attend(k_ref[pl.ds(start, blk), :], vt_ref[:, pl.ds(start, blk)], keeps, False)
        return 0

    lax.fori_loop(0, i, body, 0)
    outs = [acc_ref[h, 0:dh, :] / acc_ref[h, dh:dh + 1, :] for h in range(heads)]
    o_ref[...] = jnp.concatenate(outs, axis=0).T.astype(BF16)


def _moba_prompt(q, kb, vt, kmean, batch, seq, dh):
    n, bw = q.shape
    tq = MOBA_BLOCK
    nblk = seq // MOBA_BLOCK
    heads = bw // dh
    km = kmean.reshape(batch, nblk, bw)
    tiles = seq // tq
    return pl.pallas_call(
        functools.partial(_moba_prompt_kernel, dh=dh),
        grid=(batch, tiles),
        in_specs=[pl.BlockSpec((tq, bw), lambda b, i: (b * tiles + i, 0)),
                  pl.BlockSpec((seq, bw), lambda b, i: (b, 0)),
                  pl.BlockSpec((None, vt.shape[1], seq), lambda b, i: (b, 0, 0)),
                  pl.BlockSpec((1, nblk, bw), lambda b, i: (b, 0, 0))],
        out_specs=pl.BlockSpec((tq, bw), lambda b, i: (b * tiles + i, 0)),
        out_shape=jax.ShapeDtypeStruct((n, bw), BF16),
        scratch_shapes=[pltpu.VMEM((heads, nblk, tq), F32), pltpu.VMEM((heads, tq), F32),
                        pltpu.VMEM((heads, dh + ONES_ROWS, tq), F32)],
        compiler_params=_params("arbitrary", "arbitrary"),
        name="moba_prompt",
    )(q, kb, vt, km)


def _residual_proj_kernel(*refs, n_in, norm):
    x_ref = refs[0]
    in_refs = refs[1:1 + n_in]
    w_ref = refs[1 + n_in]
    y = x_ref[...]
    lo = 0
    for r in in_refs:
        y = y + _dot(r[...], w_ref[lo:lo + r.shape[1], :])
        lo += r.shape[1]
    if norm == "final":
        g_ref, o_ref = refs[2 + n_in:]
        o_ref[...] = _rmsnorm(y, g_ref[...])
    else:
        g_ref, o_ref, h_ref = refs[2 + n_in:]
        o_ref[...] = y
        h_ref[...] = _rmsnorm(y, g_ref[...]).astype(BF16)


def _residual_proj(x, ins, w, g, norm, tm, name):
    n, d = x.shape
    tm = min(tm, n)
    row = lambda width: pl.BlockSpec((tm, width), lambda i: (i, 0))
    out_specs = [row(d)] if norm == "final" else [row(d), row(d)]
    out_shape = ([jax.ShapeDtypeStruct((n, d), F32)] if norm == "final"
                 else [jax.ShapeDtypeStruct((n, d), F32), jax.ShapeDtypeStruct((n, d), BF16)])
    return pl.pallas_call(
        functools.partial(_residual_proj_kernel, n_in=len(ins), norm=norm),
        grid=(n // tm,),
        in_specs=[row(d)] + [row(a.shape[1]) for a in ins] + [_full(w.shape), _full((1, d))],
        out_specs=out_specs,
        out_shape=out_shape,
        compiler_params=_params("arbitrary"),
        name=name,
    )(x, *ins, w, g.reshape(1, d))


FFN_COLS = 256
FFN_HALO = SUBLANE_BF16


def _silu_gate(gate, value):
    return gate * (1.0 / (1.0 + jnp.exp(-gate))) * value


def _ffn_up_prompt_kernel(h_ref, halo_ref, w_ref, wc_ref, bc_ref, act_ref, tail_ref, up_scr, *, tiles_per_seq):
    tm = h_ref.shape[0]
    dff = act_ref.shape[1]
    seq_start = pl.program_id(0) % tiles_per_seq == 0
    halo = halo_ref[...]
    halo = jnp.where(seq_start, jnp.zeros_like(halo), halo)
    hh = jnp.concatenate([halo, h_ref[...]], axis=0)
    for c in range(dff // FFN_COLS):
        halves = []
        for k, lo in enumerate((c * FFN_COLS, dff + c * FFN_COLS)):
            cols = slice(lo, lo + FFN_COLS)
            up = _dot(hh, w_ref[:, cols])
            tail_ref[0, :, cols] = up[tm + FFN_HALO - 8:, :]
            up_scr[k] = up
            conv = bc_ref[:, cols] + up[FFN_HALO:, :] * wc_ref[CONV_W - 1:CONV_W, cols]
            for j in range(1, CONV_W):
                conv = conv + up_scr[k, pl.ds(FFN_HALO - j, tm), :] * wc_ref[CONV_W - 1 - j:CONV_W - j, cols]
            halves.append(conv)
        act_ref[:, c * FFN_COLS:(c + 1) * FFN_COLS] = _silu_gate(halves[1], halves[0]).astype(BF16)


def _ffn_up_decode_kernel(h_ref, s0_ref, s1_ref, w_ref, wc_ref, bc_ref, act_ref, up_ref):
    dff = act_ref.shape[1]
    h = h_ref[...]
    for c in range(dff // FFN_COLS):
        halves = []
        for lo in (c * FFN_COLS, dff + c * FFN_COLS):
            cols = slice(lo, lo + FFN_COLS)
            up = _dot(h, w_ref[:, cols])
            up_ref[:, cols] = up
            halves.append(bc_ref[:, cols] + s0_ref[:, cols] * wc_ref[0:1, cols]
                          + s1_ref[:, cols] * wc_ref[1:2, cols] + up * wc_ref[2:3, cols])
        act_ref[:, c * FFN_COLS:(c + 1) * FFN_COLS] = _silu_gate(halves[1], halves[0]).astype(BF16)


def _ffn_up_prompt(h, seq, w_up, w_conv, b_conv, tm):
    n, d = h.shape
    dff = w_up.shape[1] // 2
    assert dff % FFN_COLS == 0 and seq % tm == 0 and tm % FFN_HALO == 0
    ratio = tm // FFN_HALO
    act, tail = pl.pallas_call(
        functools.partial(_ffn_up_prompt_kernel, tiles_per_seq=seq // tm),
        grid=(n // tm,),
        in_specs=[pl.BlockSpec((tm, d), lambda i: (i, 0)),
                  pl.BlockSpec((FFN_HALO, d), lambda i: (jnp.maximum(i * ratio - 1, 0), 0)),
                  _full(w_up.shape), _full(w_conv.shape), _full((1, 2 * dff))],
        out_specs=[pl.BlockSpec((tm, dff), lambda i: (i, 0)),
                   pl.BlockSpec((1, 8, 2 * dff), lambda i: (i, 0, 0))],
        out_shape=[jax.ShapeDtypeStruct((n, dff), BF16),
                   jax.ShapeDtypeStruct((n // tm, 8, 2 * dff), F32)],
        scratch_shapes=[pltpu.VMEM((2, tm + FFN_HALO, FFN_COLS), F32)],
        compiler_params=_params("arbitrary"),
        name="ffn_up_prompt",
    )(h, h, w_up, w_conv, b_conv.reshape(1, -1))
    last = tail.reshape(n // seq, seq // tm, 8, 2 * dff)[:, -1, 8 - (CONV_W - 1):, :]
    return act, last


def _ffn_up_decode(h, state, w_up, w_conv, b_conv):
    n, d = h.shape
    dff = w_up.shape[1] // 2
    act, up = pl.pallas_call(
        _ffn_up_decode_kernel,
        grid=(1,),
        in_specs=[_full((n, d)), _full((n, 2 * dff)), _full((n, 2 * dff)),
                  _full(w_up.shape), _full(w_conv.shape), _full((1, 2 * dff))],
        out_specs=[_full((n, dff)), _full((n, 2 * dff))],
        out_shape=[jax.ShapeDtypeStruct((n, dff), BF16), jax.ShapeDtypeStruct((n, 2 * dff), F32)],
        compiler_params=_params("arbitrary"),
        name="ffn_up_decode",
    )(h, state[:, 0], state[:, 1], w_up, w_conv, b_conv.reshape(1, -1))
    return act, jnp.stack([state[:, 1], up], axis=1)


def _odd_in_kernel(h_ref, w_ref, c64_ref, a64_ref, b64_ref, c128_ref, a128_ref, b128_ref,
                   q_ref, k_ref, v_ref, kb_ref, vt_ref, qi_ref, kif_ref, ki2_ref, wit_ref,
                   *, half64, half128, idx_dh):
    cq = q_ref.shape[1]
    ckv = k_ref.shape[1]
    ciq = qi_ref.shape[1]
    h = h_ref[...]
    t64 = (c64_ref[...], a64_ref[...], b64_ref[...])
    t128 = (c128_ref[...], a128_ref[...], b128_ref[...])
    q_ref[...] = _rope_apply(_dot(h, w_ref[:, 0:cq]), *t128, half128).astype(BF16)
    k = _rope_apply(_dot(h, w_ref[:, cq:cq + ckv]), *t128, half128)
    k_ref[...] = k
    kb_ref[...] = k.astype(BF16)
    v = _dot(h, w_ref[:, cq + ckv:cq + 2 * ckv])
    v_ref[...] = v
    if vt_ref is not None:
        vt_ref[...] = _with_ones_rows(v.T, ckv // C_KV_HEADS).astype(BF16)
    lo = cq + 2 * ckv
    qi_ref[...] = _rope_apply(_dot(h, w_ref[:, lo:lo + ciq]), *t64, half64).astype(BF16)
    tail = _dot(h, w_ref[:, lo + ciq:lo + ciq + LANE])
    ki = _rope_apply(tail, *t64, half64)
    kif_ref[...] = ki
    lane = lax.broadcasted_iota(jnp.int32, ki.shape, 1)
    ki2_ref[...] = jnp.where(lane < idx_dh, ki, pltpu.roll(ki, idx_dh, 1)).astype(BF16)
    wit_ref[...] = tail.T[idx_dh:idx_dh + 8, :] * (IDX_HEADS ** -0.5)


def _odd_in(h, seq, w_in, tabs64, half64, tabs128, half128, c_dh, idx_dh, tm, with_vt):
    n, d = h.shape
    tm = min(tm, n)
    cq = C_HEADS * c_dh
    ckv = C_KV_HEADS * c_dh
    ciq = IDX_HEADS * idx_dh
    cols = cq + 2 * ckv + ciq + LANE
    assert 2 * idx_dh == LANE and w_in.shape[1] <= cols
    w = jnp.pad(w_in, ((0, 0), (0, cols - w_in.shape[1]))).astype(BF16)
    if seq is None:
        tab = pl.BlockSpec((1, LANE), lambda i: (0, 0))
    else:
        tiles_per_seq = seq // tm
        tab = pl.BlockSpec((tm, LANE), lambda i: (i % tiles_per_seq, 0))
    row = lambda width, dt: (pl.BlockSpec((tm, width), lambda i: (i, 0)), jax.ShapeDtypeStruct((n, width), dt))
    col = lambda height, dt: (pl.BlockSpec((height, tm), lambda i: (0, i)), jax.ShapeDtypeStruct((height, n), dt))
    outs = [row(cq, BF16), row(ckv, F32), row(ckv, F32), row(ckv, BF16)]
    if with_vt:
        outs.append(col(ckv + C_KV_HEADS * ONES_ROWS, BF16))
    outs += [row(ciq, BF16), row(LANE, F32), row(LANE, BF16), col(8, F32)]

    def body(*refs):
        refs = list(refs)
        if not with_vt:
            refs.insert(8 + 4, None)
        _odd_in_kernel(*refs, half64=half64, half128=half128, idx_dh=idx_dh)

    res = pl.pallas_call(
        body,
        grid=(n // tm,),
        in_specs=[pl.BlockSpec((tm, d), lambda i: (i, 0)), _full(w.shape)] + [tab] * 6,
        out_specs=[o[0] for o in outs],
        out_shape=[o[1] for o in outs],
        compiler_params=_params("arbitrary"),
        name="odd_in_prompt" if with_vt else "odd_in_decode",
    )(h, w, *tabs64, *tabs128)
    res = list(res)
    if not with_vt:
        res.insert(4, None)
    return res


FLOAT_BITS = 32
INDEX_BIG = 2 ** 30


def _ind(cond):
    return jnp.where(cond, 1.0, 0.0)


def _ordered_bits(x, to):
    b = x if x.dtype == jnp.int32 else lax.bitcast_convert_type(x, jnp.int32)
    b = jnp.where(b < 0, b ^ 0x7FFFFFFF, b)
    return b if to == jnp.int32 else lax.bitcast_convert_type(b, F32)


def _select_threshold(count_fn, lo, hi, active, keep, index_bits):
    keep_f = jnp.float32(keep)

    def step(_, st):
        lo_k, hi_k = st
        mid_k = (lax.shift_right_arithmetic(lo_k, 1) + lax.shift_right_arithmetic(hi_k, 1)
                 + (lo_k & hi_k & 1))
        cnt, = count_fn(lambda s, idx: (_ind(s >= _ordered_bits(mid_k, F32)),))
        move = mid_k > lo_k
        lo_k = jnp.where(move, jnp.where(cnt >= keep_f, mid_k, lo_k), lo_k)
        hi_k = jnp.where(move, jnp.where(cnt >= keep_f, hi_k, mid_k), hi_k)
        return lo_k, hi_k

    lo_k = _ordered_bits(lo, jnp.int32)
    hi_k = _ordered_bits(hi, jnp.int32) + 1
    lo_k, _ = lax.fori_loop(0, FLOAT_BITS, step, (lo_k, hi_k))
    thr = _ordered_bits(lo_k, F32)
    n_gt, n_eq = count_fn(lambda s, idx: (_ind(s > thr), _ind(s == thr)), 2)
    need = keep_f - n_gt
    excess = jnp.where(active, jnp.where(n_eq > need, 1, 0), 0)

    def tie_search():
        lo_i = jnp.full(thr.shape, -1, jnp.int32)
        hi_i = jnp.full(thr.shape, (1 << index_bits) - 1, jnp.int32)
        for _ in range(index_bits + 1):
            mid_i = lax.shift_right_arithmetic(lo_i + hi_i, 1)
            n_low, = count_fn(lambda s, idx: (jnp.where(s == thr, _ind(idx <= mid_i), 0.0),))
            ok = n_low >= need
            hi_i = jnp.where(ok, mid_i, hi_i)
            lo_i = jnp.where(ok, lo_i, mid_i)
        return hi_i

    cut = lax.cond(jnp.max(excess) > 0, tie_search, lambda: jnp.full(thr.shape, INDEX_BIG, jnp.int32))
    thr = jnp.where(active, thr, -jnp.inf)
    cut = jnp.where(active, cut, -1)
    return thr, cut


def _selected(s, idx, thr, cut):
    return jnp.where(s > thr, 1.0, jnp.where(s == thr, _ind(idx <= cut), 0.0))


def _dsa_prompt_kernel(q_ref, qi_ref, wit_ref, kb_ref, vt_ref, ki2_ref, o_ref,
                       sc_ref, m_ref, acc_ref, *, c_dh, idx_dh, keep, kc):
    tq = q_ref.shape[0]
    seq = kb_ref.shape[0]
    t0 = pl.program_id(1) * tq
    nch = (t0 + tq + kc - 1) // kc
    scale = c_dh ** -0.5
    neg_inf = jnp.float32(-jnp.inf)
    lane = lax.broadcasted_iota(jnp.int32, (tq, LANE), 1)
    key_iota = lax.broadcasted_iota(jnp.int32, (kc, tq), 0)
    qpos = t0 + lax.broadcasted_iota(jnp.int32, (1, tq), 1)

    qi = qi_ref[...]
    per_group = LANE // idx_dh
    qms = []
    for h in range(IDX_HEADS):
        grp = qi[:, (h // per_group) * LANE:(h // per_group + 1) * LANE]
        qms.append(jnp.where(_head_of(lane, idx_dh) == h % per_group, grp, jnp.zeros_like(grp)))
    wit = wit_ref[...]

    def score_chunk(c, carry):
        lo, hi = carry
        start = pl.multiple_of(c * kc, kc)
        kic = ki2_ref[pl.ds(start, kc), :]
        acc = jnp.zeros((kc, tq), F32)
        for h in range(IDX_HEADS):
            acc = acc + jnp.maximum(_dot_nt(kic, qms[h]), 0.0) * wit[h:h + 1, :]
        score = acc * (idx_dh ** -0.5)
        adm = start + key_iota <= qpos
        sc_ref[pl.ds(start, kc), :] = jnp.where(adm, score, neg_inf)
        lo = jnp.minimum(lo, jnp.min(jnp.where(adm, score, jnp.inf), axis=0, keepdims=True))
        hi = jnp.maximum(hi, jnp.max(jnp.where(adm, score, neg_inf), axis=0, keepdims=True))
        return lo, hi

    lo, hi = lax.fori_loop(0, nch, score_chunk,
                           (jnp.full((1, tq), jnp.inf, F32), jnp.full((1, tq), neg_inf, F32)))

    def count_fn(pred, n_out=1):
        def body(c, cnts):
            start = pl.multiple_of(c * kc, kc)
            hits = pred(sc_ref[pl.ds(start, kc), :], start + key_iota)
            return tuple(cnt + jnp.sum(hit.reshape(kc // 8, 8, tq), axis=0) for cnt, hit in zip(cnts, hits))
        parts = lax.fori_loop(0, nch, body, tuple(jnp.zeros((8, tq), F32) for _ in range(n_out)))
        return tuple(jnp.sum(part, axis=0, keepdims=True) for part in parts)

    thr, cut = _select_threshold(count_fn, lo, hi, qpos + 1 > keep, keep, (seq - 1).bit_length())

    m_ref[...] = jnp.full(m_ref.shape, NEG_BIG, F32)
    acc_ref[...] = jnp.zeros(acc_ref.shape, F32)
    group = C_HEADS // C_KV_HEADS
    ext = c_dh + ONES_ROWS
    exp2_scale = scale * LOG2_E
    head_row = lax.broadcasted_iota(jnp.int32, m_ref.shape, 0)

    def attend_chunk(c, _):
        start = pl.multiple_of(c * kc, kc)
        keepm = _selected(sc_ref[pl.ds(start, kc), :], start + key_iota, thr, cut) > 0.0
        kbc = kb_ref[pl.ds(start, kc), :]
        vtc = vt_ref[:, pl.ds(start, kc)]
        m_old = m_ref[...]
        m_all = jnp.zeros(m_old.shape, F32)
        scores = [_dot_nt(kbc[:, (h // group) * c_dh:(h // group + 1) * c_dh], q_ref[:, h * c_dh:(h + 1) * c_dh])
                  for h in range(C_HEADS)]
        probs, alphas = [], []
        for h in range(C_HEADS):
            s = jnp.where(keepm, scores[h], neg_inf)
            m_new = jnp.maximum(m_old[h:h + 1, :], jnp.max(s, axis=0, keepdims=True))
            alphas.append(jnp.exp2((m_old[h:h + 1, :] - m_new) * exp2_scale))
            probs.append(jnp.exp2((s - m_new) * exp2_scale).astype(BF16))
            m_all = jnp.where(head_row == h, m_new, m_all)
        pvs = [_dot(vtc[(h // group) * ext:(h // group + 1) * ext, :], probs[h]) for h in range(C_HEADS)]
        m_ref[...] = m_all
        for h in range(C_HEADS):
            acc_ref[h] = acc_ref[h] * alphas[h] + pvs[h]
        return 0

    lax.fori_loop(0, nch, attend_chunk, 0)
    for h in range(C_HEADS):
        out_t = acc_ref[h, 0:c_dh, :] / acc_ref[h, c_dh:c_dh + 1, :]
        o_ref[:, h * c_dh:(h + 1) * c_dh] = out_t.T.astype(BF16)


def _dsa_prompt(q, qi, wit, kb, vt, ki2, batch, seq, c_dh, idx_dh, keep, tq, kc):
    n = q.shape[0]
    tiles = seq // tq
    assert c_dh == LANE and seq % kc == 0 and seq % tq == 0
    return pl.pallas_call(
        functools.partial(_dsa_prompt_kernel, c_dh=c_dh, idx_dh=idx_dh, keep=keep, kc=kc),
        grid=(batch, tiles),
        in_specs=[pl.BlockSpec((tq, q.shape[1]), lambda b, i: (b * tiles + i, 0)),
                  pl.BlockSpec((tq, qi.shape[1]), lambda b, i: (b * tiles + i, 0)),
                  pl.BlockSpec((8, tq), lambda b, i: (0, b * tiles + i)),
                  pl.BlockSpec((seq, kb.shape[1]), lambda b, i: (b, 0)),
                  pl.BlockSpec((vt.shape[0], seq), lambda b, i: (0, b)),
                  pl.BlockSpec((seq, LANE), lambda b, i: (b, 0))],
        out_specs=pl.BlockSpec((tq, q.shape[1]), lambda b, i: (b * tiles + i, 0)),
        out_shape=jax.ShapeDtypeStruct(q.shape, BF16),
        scratch_shapes=[pltpu.VMEM((seq, tq), F32), pltpu.VMEM((8, tq), F32),
                        pltpu.VMEM((C_HEADS, c_dh + ONES_ROWS, tq), F32)],
        compiler_params=_params("arbitrary", "arbitrary"),
        name="dsa_prompt",
    )(q, qi, wit, kb, vt, ki2)


PAGE_ROWS = 128


def _lane_bcast_col(row):
    return jnp.broadcast_to(row, (LANE, row.shape[1])).T


def _moba_decode_kernel(pt_ref, q_ref, kn_ref, vn_ref, *rest, n_pages, dh):
    kt_refs, vt_refs, o_ref = rest[:n_pages], rest[n_pages:2 * n_pages], rest[2 * n_pages]
    bw = q_ref.shape[2]
    heads = bw // dh
    ppb = MOBA_BLOCK // PAGE_ROWS
    nblk = n_pages // ppb
    neg_inf = jnp.float32(-jnp.inf)
    qb = _lane_bcast_col(q_ref[0].astype(F32))

    def head_sums(x):
        return jnp.sum(x.reshape(heads, dh, LANE), axis=1)

    def head_rows(x):
        return jnp.broadcast_to(x[:, None, :], (heads, dh, LANE)).reshape(bw, LANE)

    s_pages = [head_sums(kt_refs[j][...] * qb) for j in range(n_pages)]
    s_own = head_sums(_lane_bcast_col(kn_ref[0]) * qb)
    gates = []
    for n in range(nblk):
        g = jnp.zeros((heads, 1), F32)
        for j in range(n * ppb, (n + 1) * ppb):
            g = g + jnp.sum(s_pages[j], axis=1, keepdims=True)
        gates.append(g)
    sels = []
    for n in range(nblk):
        rank = jnp.zeros((heads, 1), F32)
        for m in range(nblk):
            if m != n:
                beats = gates[m] >= gates[n] if m < n else gates[m] > gates[n]
                rank = rank + jnp.where(beats, 1.0, 0.0)
        sels.append(rank < MOBA_TOPK)
    m = s_own
    masked = []
    for j in range(n_pages):
        sm = jnp.where(sels[j // ppb], s_pages[j], neg_inf)
        masked.append(sm)
        m = jnp.maximum(m, jnp.max(sm, axis=1, keepdims=True))
    e_own = jnp.exp(s_own - m)
    es = [jnp.exp(sm - m) for sm in masked]
    l = e_own
    for e in es:
        l = l + jnp.sum(e, axis=1, keepdims=True)
    rinv = 1.0 / l
    lane = lax.broadcasted_iota(jnp.int32, (bw, LANE), 1)
    acc = jnp.where(lane == 0, head_rows(e_own * rinv) * _lane_bcast_col(vn_ref[0]), 0.0)
    for j in range(n_pages):
        acc = acc + head_rows(es[j] * rinv) * vt_refs[j][...]
    out_col = jnp.sum(acc, axis=1, keepdims=True)
    o_ref[0] = jnp.broadcast_to(out_col, (bw, LANE)).T[0:1].astype(BF16)


def _moba_decode(q, k_new, v_new, cache_k, cache_v, page_table, dh):
    n, bw = q.shape
    n_pages = page_table.shape[1]
    assert cache_k.shape[1] == PAGE_ROWS == LANE and (n_pages * PAGE_ROWS) % MOBA_BLOCK == 0
    ckt = jnp.transpose(cache_k, (0, 2, 3, 1)).reshape(cache_k.shape[0], bw, PAGE_ROWS)
    cvt = jnp.transpose(cache_v, (0, 2, 3, 1)).reshape(cache_v.shape[0], bw, PAGE_ROWS)
    seq_row = pl.BlockSpec((1, 1, bw), lambda b, pt: (b, 0, 0))
    pages = [pl.BlockSpec((None, bw, PAGE_ROWS), lambda b, pt, j=j: (pt[b, j], 0, 0)) for j in range(n_pages)]
    out = pl.pallas_call(
        functools.partial(_moba_decode_kernel, n_pages=n_pages, dh=dh),
        grid_spec=pltpu.PrefetchScalarGridSpec(
            num_scalar_prefetch=1, grid=(n,),
            in_specs=[seq_row, seq_row, seq_row] + pages + pages,
            out_specs=seq_row),
        out_shape=jax.ShapeDtypeStruct((n, 1, bw), BF16),
        compiler_params=_params("arbitrary"),
        name="moba_decode",
    )(page_table, q.reshape(n, 1, bw), k_new.reshape(n, 1, bw), v_new.reshape(n, 1, bw),
      *([ckt] * n_pages), *([cvt] * n_pages))
    return out.reshape(n, bw)


def _dsa_decode_score_kernel(pt_ref, qi_ref, wi_ref, kin_ref, *rest, n_pages, idx_dh):
    kit_refs, o_ref = rest[:n_pages], rest[n_pages]
    qb = _lane_bcast_col(qi_ref[0].astype(F32))
    wi8 = wi_ref[0]
    scale = idx_dh ** -0.5

    def score(kit):
        total = jnp.zeros((1, LANE), F32)
        for h in range(IDX_HEADS):
            sc = jnp.sum(kit * qb[h * idx_dh:(h + 1) * idx_dh], axis=0, keepdims=True)
            total = total + jnp.maximum(sc, 0.0) * wi8[h:h + 1, :]
        return total * scale

    o_ref[0] = jnp.full(o_ref.shape[1:], -jnp.inf, F32)
    for j in range(n_pages):
        o_ref[0, j:j + 1, :] = score(kit_refs[j][...])
    own = score(_lane_bcast_col(kin_ref[0]))
    lane = lax.broadcasted_iota(jnp.int32, (1, PAGE_ROWS), 1)
    o_ref[0, n_pages:n_pages + 1, :] = jnp.where(lane == 0, own, -jnp.inf)


def _dsa_decode_select_kernel(s_ref, m_ref, *, keep, n_valid):
    s = s_ref[...]
    idx = lax.broadcasted_iota(jnp.int32, s.shape, 1)
    neg_inf = jnp.float32(-jnp.inf)
    lo = jnp.min(jnp.where(s > neg_inf, s, jnp.inf), axis=1, keepdims=True)
    hi = jnp.max(s, axis=1, keepdims=True)
    count_fn = lambda pred, n_out=1: tuple(jnp.sum(hit, axis=1, keepdims=True) for hit in pred(s, idx))
    active = jnp.full(lo.shape, n_valid > keep, jnp.bool_)
    thr, cut = _select_threshold(count_fn, lo, hi, active, keep, (s.shape[1] - 1).bit_length())
    m_ref[...] = _selected(s, idx, thr, cut)


def _dsa_decode_attn_kernel(pt_ref, q_ref, kn_ref, vn_ref, mask_ref, *rest, n_pages, c_dh):
    k_refs, v_refs, o_ref = rest[:n_pages], rest[n_pages:2 * n_pages], rest[2 * n_pages]
    rows = k_refs[0].shape[0]
    group = C_HEADS // C_KV_HEADS
    scale = c_dh ** -0.5
    neg_inf = jnp.float32(-jnp.inf)
    qm = jnp.concatenate([q_ref[0].astype(F32), jnp.zeros((LANE - C_HEADS, c_dh), F32)], axis=0).astype(BF16)
    row_id = lax.broadcasted_iota(jnp.int32, (rows, LANE), 0)
    lane_id = lax.broadcasted_iota(jnp.int32, (rows, LANE), 1)
    own_kv = (row_id & (C_KV_HEADS - 1)) == _head_of(lane_id, group)
    mask = mask_ref[0]
    mask_t = jnp.concatenate([mask, jnp.zeros((LANE - mask.shape[0], PAGE_ROWS), F32)], axis=0).T
    repeat = jnp.where(_head_of(row_id, C_KV_HEADS) == lane_id, 1.0, 0.0).astype(BF16)
    row_mask = _dot(repeat, mask_t.astype(BF16))
    kn = jnp.concatenate([kn_ref[0], jnp.zeros((8 - C_KV_HEADS, c_dh), F32)], axis=0)
    s_own_kv = _dot_nt(kn.astype(BF16), qm)
    lane_kv = _head_of(lax.broadcasted_iota(jnp.int32, (1, LANE), 1), group)
    s_own = jnp.zeros((1, LANE), F32)
    for kv in range(C_KV_HEADS):
        s_own = jnp.where(lane_kv == kv, s_own_kv[kv:kv + 1, :], s_own)
    s_own = jnp.where(mask[n_pages:n_pages + 1, 0:1] > 0.0, s_own, neg_inf)
    m = jnp.maximum(s_own, NEG_BIG)
    masked = []
    for j in range(n_pages):
        s = _dot_nt(k_refs[j][...].astype(BF16), qm)
        sm = jnp.where(own_kv, jnp.where(row_mask[:, j:j + 1] > 0.5, s, neg_inf), neg_inf)
        masked.append(sm)
        m = jnp.maximum(m, jnp.max(sm, axis=0, keepdims=True))
    e_own = jnp.exp((s_own - m) * scale)
    es = [jnp.exp((sm - m) * scale) for sm in masked]
    l = e_own
    for e in es:
        l = l + jnp.sum(e, axis=0, keepdims=True)
    rinv = jnp.where(l > 0.0, 1.0 / l, 0.0)
    own_rows = lax.broadcasted_iota(jnp.int32, (SUBLANE_BF16, LANE), 0)
    p_own = jnp.where(own_rows == lane_kv, e_own * rinv, 0.0)
    v_own = jnp.concatenate([vn_ref[0], jnp.zeros((SUBLANE_BF16 - C_KV_HEADS, c_dh), F32)], axis=0)
    p_all = jnp.concatenate([e * rinv for e in es] + [p_own], axis=0).astype(BF16)
    v_all = jnp.concatenate([v_refs[j][...] for j in range(n_pages)] + [v_own], axis=0).astype(BF16)
    out = lax.dot_general(p_all, v_all, (((0,), (0,)), ((), ())), preferred_element_type=F32)
    o_ref[0] = out[0:C_HEADS].astype(BF16)


def _dsa_decode(q, qi, wit, k_new, v_new, ki_new, cache_k, cache_v, cache_ki, page_table, c_dh, idx_dh, keep):
    n = q.shape[0]
    n_pages = page_table.shape[1]
    assert cache_k.shape[1] == PAGE_ROWS == LANE
    pool = cache_k.shape[0]
    ck = cache_k.reshape(pool, PAGE_ROWS * C_KV_HEADS, c_dh)
    cv = cache_v.reshape(pool, PAGE_ROWS * C_KV_HEADS, c_dh)
    ckit = jnp.transpose(cache_ki, (0, 2, 1))
    rows = -(-(n_pages + 1) // 8) * 8
    wi8 = wit.T.reshape(n, 8, 1)
    seq3 = lambda a, b: pl.BlockSpec((1, a, b), lambda s, pt: (s, 0, 0))
    pages = lambda r, w: [pl.BlockSpec((None, r, w), lambda s, pt, j=j: (pt[s, j], 0, 0)) for j in range(n_pages)]
    scores = pl.pallas_call(
        functools.partial(_dsa_decode_score_kernel, n_pages=n_pages, idx_dh=idx_dh),
        grid_spec=pltpu.PrefetchScalarGridSpec(
            num_scalar_prefetch=1, grid=(n,),
            in_specs=[seq3(1, IDX_HEADS * idx_dh), seq3(8, 1), seq3(1, idx_dh)] + pages(idx_dh, PAGE_ROWS),
            out_specs=seq3(rows, PAGE_ROWS)),
        out_shape=jax.ShapeDtypeStruct((n, rows, PAGE_ROWS), F32),
        compiler_params=_params("arbitrary"),
        name="dsa_decode_scores",
    )(page_table, qi.reshape(n, 1, IDX_HEADS * idx_dh), wi8, ki_new.reshape(n, 1, idx_dh), *([ckit] * n_pages))
    flat = rows * PAGE_ROWS
    mask = pl.pallas_call(
        functools.partial(_dsa_decode_select_kernel, keep=keep, n_valid=n_pages * PAGE_ROWS + 1),
        grid=(1,),
        in_specs=[_full((n, flat))],
        out_specs=_full((n, flat)),
        out_shape=jax.ShapeDtypeStruct((n, flat), F32),
        compiler_params=_params("arbitrary"),
        name="dsa_decode_select",
    )(scores.reshape(n, flat))
    cq = C_HEADS * c_dh
    out = pl.pallas_call(
        functools.partial(_dsa_decode_attn_kernel, n_pages=n_pages, c_dh=c_dh),
        grid_spec=pltpu.PrefetchScalarGridSpec(
            num_scalar_prefetch=1, grid=(n,),
            in_specs=[seq3(C_HEADS, c_dh), seq3(C_KV_HEADS, c_dh), seq3(C_KV_HEADS, c_dh), seq3(rows, PAGE_ROWS)]
                     + pages(PAGE_ROWS * C_KV_HEADS, c_dh) + pages(PAGE_ROWS * C_KV_HEADS, c_dh),
            out_specs=seq3(C_HEADS, c_dh)),
        out_shape=jax.ShapeDtypeStruct((n, C_HEADS, c_dh), BF16),
        compiler_params=_params("arbitrary"),
        name="dsa_decode_attn",
    )(page_table, q.reshape(n, C_HEADS, c_dh), k_new.reshape(n, C_KV_HEADS, c_dh),
      v_new.reshape(n, C_KV_HEADS, c_dh), mask.reshape(n, rows, PAGE_ROWS),
      *([ck] * n_pages), *([cv] * n_pages))
    return out.reshape(n, cq)


def kernel(x_prompt, x_sample, cache_moba_k, cache_moba_v, cache_dsa_k, cache_dsa_v, cache_dsa_kidx,
           state_ffn_conv, page_table, w_in_even, g_sgu, w_sgu, b_sgu, w_out_even, w_in_odd, w_out_odd,
           g_mix, g_ffn, w_up, w_conv, b_conv, w_down, g_final):
    batch, seq, d = x_prompt.shape
    n_dec, dec_seq, _ = x_sample.shape
    depth = g_mix.shape[0]
    n_pages = page_table.shape[1]
    past_len = n_pages * cache_moba_k.shape[2]
    assert depth == 2 and dec_seq == 1, "one even + one odd layer, one decode token per sequence"
    b_dh = cache_moba_k.shape[-1]
    c_dh = cache_dsa_k.shape[-1]
    idx_dh = cache_dsa_kidx.shape[-1]
    keep_p = min(DSA_TOPK, seq // 4)
    keep_s = min(DSA_TOPK, (past_len + dec_seq) // 4)
    xp = x_prompt.reshape(batch * seq, d)
    xs = x_sample.reshape(n_dec, d)
    bf = lambda w: w.astype(BF16)

    tp64, half64 = _rope_tables(b_dh, seq, 0)
    ts64, _ = _rope_tables(b_dh, 1, past_len)
    tp128, half128 = _rope_tables(c_dh, seq, 0)
    ts128, _ = _rope_tables(c_dh, 1, past_len)
    assert idx_dh == b_dh

    w_in0, w_out0 = bf(w_in_even[0]), bf(w_out_even[0])
    a_p, q_p, kb_p, mkt_p, mvt_p, vt_p, km_p = _even_in_prompt(
        xp, seq, g_mix[0], w_in0, tp64, half64, g_sgu[0], w_sgu[0], b_sgu[0])
    bo_p = _moba_prompt(q_p, kb_p, vt_p, km_p, batch, seq, b_dh)
    cache_view = lambda t: jnp.transpose(t.reshape(batch, B_HEADS, b_dh, seq), (0, 3, 1, 2))[None]
    a_s, q_s, mk_s, mv_s, va_s = _even_in_decode(xs, g_mix[0], w_in0, ts64, half64, g_sgu[0], w_sgu[0], b_sgu[0])
    bo_s = _moba_decode(q_s, mk_s, mv_s, cache_moba_k[0], cache_moba_v[0], page_table, b_dh)
    w_up0, w_down0 = bf(w_up[0]), bf(w_down[0])
    xp, hp = _residual_proj(xp, [a_p, bo_p], w_out0, g_ffn[0], "next", 256, "out_even_prompt")
    xs, hs = _residual_proj(xs, [a_s, bo_s], w_out0, g_ffn[0], "next", 256, "out_even_decode")
    act_p, conv_p0 = _ffn_up_prompt(hp, seq, w_up0, w_conv[0], b_conv[0], 512)
    act_s, conv_s0 = _ffn_up_decode(hs, state_ffn_conv[0], w_up0, w_conv[0], b_conv[0])
    xp, hp = _residual_proj(xp, [act_p], w_down0, g_mix[1], "next", 256, "ffn_down0_prompt")
    xs, hs = _residual_proj(xs, [act_s], w_down0, g_mix[1], "next", 256, "ffn_down0_decode")

    w_out1 = bf(w_out_odd[0])
    q_p, dk_p, dv_p, kb_p, vt_p, qi_p, kif_p, ki2_p, wit_p = _odd_in(
        hp, seq, w_in_odd[0], tp64, half64, tp128, half128, c_dh, idx_dh, 256, True)
    o_p = _dsa_prompt(q_p, qi_p, wit_p, kb_p, vt_p, ki2_p, batch, seq, c_dh, idx_dh, keep_p, 256, 256)
    q_s, dk_s, dv_s, _, _, qi_s, kif_s, _, wit_s = _odd_in(
        hs, None, w_in_odd[0], ts64, half64, ts128, half128, c_dh, idx_dh, 256, False)
    o_s = _dsa_decode(q_s, qi_s, wit_s, dk_s, dv_s, kif_s[:, :idx_dh], cache_dsa_k[0], cache_dsa_v[0],
                      cache_dsa_kidx[0], page_table, c_dh, idx_dh, keep_s)
    w_up1, w_down1 = bf(w_up[1]), bf(w_down[1])
    xp, hp = _residual_proj(xp, [o_p], w_out1, g_ffn[1], "next", 256, "out_odd_prompt")
    xs, hs = _residual_proj(xs, [o_s], w_out1, g_ffn[1], "next", 256, "out_odd_decode")
    act_p, conv_p1 = _ffn_up_prompt(hp, seq, w_up1, w_conv[1], b_conv[1], 512)
    act_s, conv_s1 = _ffn_up_decode(hs, state_ffn_conv[1], w_up1, w_conv[1], b_conv[1])
    y_p, = _residual_proj(xp, [act_p], w_down1, g_final, "final", 256, "ffn_down1_prompt")
    y_s, = _residual_proj(xs, [act_s], w_down1, g_final, "final", 256, "ffn_down1_decode")

    ckv = C_KV_HEADS
    return (y_p.reshape(batch, seq, d), y_s.reshape(n_dec, dec_seq, d),
            cache_view(mkt_p), cache_view(mvt_p),
            mk_s.reshape(1, n_dec, dec_seq, B_HEADS, b_dh), mv_s.reshape(1, n_dec, dec_seq, B_HEADS, b_dh),
            va_s.reshape(1, n_dec, dec_seq, A_GROUPS, LANE),
            dk_p.reshape(1, batch, seq, ckv, c_dh), dv_p.reshape(1, batch, seq, ckv, c_dh),
            kif_p[:, :idx_dh].reshape(1, batch, seq, idx_dh),
            dk_s.reshape(1, n_dec, dec_seq, ckv, c_dh), dv_s.reshape(1, n_dec, dec_seq, ckv, c_dh),
            kif_s[:, :idx_dh].reshape(1, n_dec, dec_seq, idx_dh),
            jnp.stack([conv_p0, conv_p1]), jnp.stack([conv_s0, conv_s1]))
```

```python
import functools

import numpy as np
import jax
import jax.numpy as jnp
from jax import lax
from jax.experimental import pallas as pl
from jax.experimental.pallas import tpu as pltpu

F32 = jnp.float32
BF16 = jnp.bfloat16

A_GROUPS = 4
CHUNK = 128
B_HEADS = 8
MOBA_BLOCK = 256
MOBA_TOPK = 3
C_HEADS = 8
C_KV_HEADS = 2
IDX_HEADS = 4
DSA_TOPK = 256
ROPE_THETA = 500000.0
ROPE_FRAC = 4
CONV_W = 3
EPS = 1e-6

LANE = 128
SUBLANE_BF16 = 16
VMEM_LIMIT_BYTES = 56 * 1024 * 1024
NEG_BIG = -1e30
LOG2_E = 1.4426950408889634


def _params(*sem):
    return pltpu.CompilerParams(dimension_semantics=sem,
                                vmem_limit_bytes=VMEM_LIMIT_BYTES)


def _full(shape):
    nd = len(shape)
    return pl.BlockSpec(shape, lambda *_: (0,) * nd)


def _rmsnorm(x, g):
    ms = jnp.mean(x * x, axis=-1, keepdims=True)
    return x * lax.rsqrt(ms + EPS) * g


def _rope_apply(x, c, s1, s2, half):
    parts = []
    for j in range(x.shape[1] // LANE):
        xs = x[:, j * LANE:(j + 1) * LANE]
        parts.append(xs * c + pltpu.roll(xs, LANE - half, 1) * s1
                     + pltpu.roll(xs, half, 1) * s2)
    return parts[0] if len(parts) == 1 else jnp.concatenate(parts, axis=1)


def _head_of(index, dh):
    assert dh & (dh - 1) == 0
    return lax.shift_right_logical(index, dh.bit_length() - 1)


ONES_ROWS = SUBLANE_BF16


def _with_ones_rows(vt, dh):
    ones = jnp.ones((ONES_ROWS, vt.shape[1]), vt.dtype)
    parts = []
    for h in range(vt.shape[0] // dh):
        parts += [vt[h * dh:(h + 1) * dh], ones]
    return jnp.concatenate(parts, axis=0)


def _dot(a, b):
    return jnp.dot(a, b, preferred_element_type=F32)


def _dot_nt(a, b):
    return lax.dot_general(a, b, (((1,), (1,)), ((), ())),
                           preferred_element_type=F32)


def _rope_rows(dh):
    rd = dh // ROPE_FRAC
    half = rd // 2
    inv = ROPE_THETA ** (-jnp.arange(half, dtype=F32) * (2.0 / rd))
    r = np.arange(LANE) % dh
    j = np.where(r < half, r, np.where(r < rd, r - half, 0))
    inv_lane = jnp.where(jnp.asarray(r < rd), inv[j], 0.0).reshape(1, LANE)
    s1 = jnp.asarray(np.where(r < half, -1.0, 0.0), F32).reshape(1, LANE)
    s2 = jnp.asarray(np.where((r >= half) & (r < rd), 1.0, 0.0), F32).reshape(1, LANE)
    return inv_lane, s1, s2, half


def _rope_table_kernel(inv_ref, s1_ref, s2_ref, c_ref, a_ref, b_ref, *, base):
    rows = c_ref.shape[0]
    pos = lax.broadcasted_iota(jnp.int32, (rows, LANE), 0) + (base + pl.program_id(0) * rows)
    ang = pos.astype(F32) * inv_ref[...]
    s = jnp.sin(ang)
    c_ref[...] = jnp.cos(ang)
    a_ref[...] = s * s1_ref[...]
    b_ref[...] = s * s2_ref[...]


def _rope_tables(dh, n_pos, base):
    inv_lane, s1, s2, half = _rope_rows(dh)
    rows = min(n_pos, 512)
    assert n_pos % rows == 0
    out = jax.ShapeDtypeStruct((n_pos, LANE), F32)
    spec = pl.BlockSpec((rows, LANE), lambda i: (i, 0))
    c, a, b = pl.pallas_call(
        functools.partial(_rope_table_kernel, base=base),
        grid=(n_pos // rows,),
        in_specs=[_full((1, LANE))] * 3,
        out_specs=[spec] * 3,
        out_shape=[out] * 3,
        compiler_params=_params("arbitrary"),
        name="rope_tables",
    )(inv_lane, s1, s2)
    return (c, a, b), half


def _sgu_values(h, w_ref, gs_ref, aw):
    u = jax.nn.gelu(_dot(h, w_ref[:, 0:aw]))
    zv = jax.nn.gelu(_dot(h, w_ref[:, aw:2 * aw]))
    vs = []
    for g in range(A_GROUPS):
        vg = zv[:, g * LANE:(g + 1) * LANE]
        vs.append(_rmsnorm(vg, gs_ref[:, g * LANE:(g + 1) * LANE]))
    return u, vs


def _even_in_prompt_kernel(x_ref, g_ref, w_ref, c_ref, s1_ref, s2_ref, gs_ref, ws_ref, bs_ref,
                           a_ref, q_ref, kb_ref, kt_ref, vt_ref, vtb_ref, km_ref, v_scr, *, half, scale):
    tm = x_ref.shape[0]
    aw = A_GROUPS * LANE
    bw = q_ref.shape[1]
    h = _rmsnorm(x_ref[...], g_ref[...]).astype(BF16)
    u, vs = _sgu_values(h, w_ref, gs_ref, aw)
    tri = (lax.broadcasted_iota(jnp.int32, (CHUNK, CHUNK), 0)
           >= lax.broadcasted_iota(jnp.int32, (CHUNK, CHUNK), 1))
    for g in range(A_GROUPS):
        wg = jnp.where(tri, ws_ref[g], 0.0).astype(BF16)
        for cc in range(tm // CHUNK):
            rows = slice(cc * CHUNK, (cc + 1) * CHUNK)
            mixed = _dot(wg, vs[g][rows].astype(BF16)) + bs_ref[g]
            a_ref[rows, g * LANE:(g + 1) * LANE] = (u[rows, g * LANE:(g + 1) * LANE] * mixed).astype(BF16)
    c, s1, s2 = c_ref[...], s1_ref[...], s2_ref[...]
    q = _rope_apply(_dot(h, w_ref[:, 2 * aw:2 * aw + bw]), c, s1, s2, half)
    q_ref[...] = (q * scale).astype(BF16)
    k = _rope_apply(_dot(h, w_ref[:, 2 * aw + bw:2 * aw + 2 * bw]), c, s1, s2, half)
    kt_ref[...] = k.T
    kb_ref[...] = k.astype(BF16)
    km_ref[0] = jnp.sum(k, axis=0, keepdims=True) * (1.0 / tm)
    v_scr[...] = _dot(h, w_ref[:, 2 * aw + 2 * bw:2 * aw + 3 * bw])
    vt = v_scr[...].T
    vt_ref[...] = vt
    vtb_ref[...] = _with_ones_rows(vt, bw // B_HEADS).astype(BF16)


def _even_in_decode_kernel(x_ref, g_ref, w_ref, c_ref, s1_ref, s2_ref, gs_ref, w00_ref, b0_ref,
                           a_ref, q_ref, k_ref, v_ref, va_ref, *, half, scale):
    aw = A_GROUPS * LANE
    bw = q_ref.shape[1]
    h = _rmsnorm(x_ref[...], g_ref[...]).astype(BF16)
    u, vs = _sgu_values(h, w_ref, gs_ref, aw)
    va = jnp.concatenate(vs, axis=1)
    va_ref[...] = va
    a_ref[...] = (u * (w00_ref[...] * va + b0_ref[...])).astype(BF16)
    c, s1, s2 = c_ref[...], s1_ref[...], s2_ref[...]
    q = _rope_apply(_dot(h, w_ref[:, 2 * aw:2 * aw + bw]), c, s1, s2, half)
    q_ref[...] = (q * scale).astype(BF16)
    k_ref[...] = _rope_apply(_dot(h, w_ref[:, 2 * aw + bw:2 * aw + 2 * bw]), c, s1, s2, half)
    v_ref[...] = _dot(h, w_ref[:, 2 * aw + 2 * bw:2 * aw + 3 * bw])


def _even_in_prompt(x, seq, g_mix, w_in, tables, half, g_sgu, w_sgu, b_sgu):
    n, d = x.shape
    tm = MOBA_BLOCK
    bw = (w_in.shape[1] - 2 * A_GROUPS * LANE) // 3
    dh = bw // B_HEADS
    tiles_per_seq = seq // tm
    row = lambda w, dt: (pl.BlockSpec((tm, w), lambda i: (i, 0)), jax.ShapeDtypeStruct((n, w), dt))
    tab = pl.BlockSpec((tm, LANE), lambda i: (i % tiles_per_seq, 0))
    col = lambda h, dt: (pl.BlockSpec((None, h, tm), lambda i: (i // tiles_per_seq, 0, i % tiles_per_seq)),
                         jax.ShapeDtypeStruct((n // seq, h, seq), dt))
    outs = [row(A_GROUPS * LANE, BF16), row(bw, BF16), row(bw, BF16), col(bw, F32), col(bw, F32),
            col(bw + B_HEADS * ONES_ROWS, BF16),
            (pl.BlockSpec((1, 1, bw), lambda i: (i, 0, 0)), jax.ShapeDtypeStruct((n // tm, 1, bw), F32))]
    bs = jnp.broadcast_to(b_sgu[:, :, None], (A_GROUPS, CHUNK, LANE))
    return pl.pallas_call(
        functools.partial(_even_in_prompt_kernel, half=half, scale=dh ** -0.5),
        grid=(n // tm,),
        in_specs=[pl.BlockSpec((tm, d), lambda i: (i, 0)), _full((1, d)), _full(w_in.shape),
                  tab, tab, tab, _full((1, A_GROUPS * LANE)), _full(w_sgu.shape), _full(bs.shape)],
        out_specs=[o[0] for o in outs],
        out_shape=[o[1] for o in outs],
        scratch_shapes=[pltpu.VMEM((tm, bw), F32)],
        compiler_params=_params("arbitrary"),
        name="even_in_prompt",
    )(x, g_mix.reshape(1, d), w_in, *tables, g_sgu.reshape(1, -1), w_sgu, bs)


def _even_in_decode(x, g_mix, w_in, tables, half, g_sgu, w_sgu, b_sgu):
    n, d = x.shape
    bw = (w_in.shape[1] - 2 * A_GROUPS * LANE) // 3
    dh = bw // B_HEADS
    aw = A_GROUPS * LANE
    w00 = jnp.repeat(w_sgu[:, 0, 0], LANE).reshape(1, aw)
    b0 = jnp.repeat(b_sgu[:, 0], LANE).reshape(1, aw)
    shapes = [(aw, BF16), (bw, BF16), (bw, F32), (bw, F32), (aw, F32)]
    return pl.pallas_call(
        functools.partial(_even_in_decode_kernel, half=half, scale=dh ** -0.5),
        grid=(1,),
        in_specs=[_full((n, d)), _full((1, d)), _full(w_in.shape)] + [_full((1, LANE))] * 3
                 + [_full((1, aw))] * 3,
        out_specs=[_full((n, w)) for w, _ in shapes],
        out_shape=[jax.ShapeDtypeStruct((n, w), dt) for w, dt in shapes],
        compiler_params=_params("arbitrary"),
        name="even_in_decode",
    )(x, g_mix.reshape(1, d), w_in, *tables, g_sgu.reshape(1, -1), w00, b0)


def _moba_prompt_kernel(q_ref, k_ref, vt_ref, km_ref, o_ref, sel_ref, m_ref, acc_ref, *, dh):
    blk = MOBA_BLOCK
    tq = q_ref.shape[0]
    i = pl.program_id(1)
    nblk = km_ref.shape[1]
    heads = q_ref.shape[1] // dh
    ext = dh + ONES_ROWS
    per_group = LANE // dh
    lane = lax.broadcasted_iota(jnp.int32, (tq, LANE), 1)
    blk_id = lax.broadcasted_iota(jnp.int32, (nblk, tq), 0)
    neg_inf = jnp.float32(-jnp.inf)
    qms = []
    for h in range(heads):
        grp = h // per_group
        q = q_ref[:, grp * LANE:(grp + 1) * LANE]
        qm = jnp.where(_head_of(lane, dh) == h % per_group, q, jnp.zeros_like(q))
        qms.append(qm)
        kmean = km_ref[0, :, grp * LANE:(grp + 1) * LANE].astype(BF16)
        gate = jnp.where(blk_id < i, _dot_nt(kmean, qm), neg_inf)
        sel = jnp.zeros((nblk, tq), F32)
        for _ in range(MOBA_TOPK):
            best = jnp.max(gate, axis=0, keepdims=True)
            first = jnp.min(jnp.where(gate == best, blk_id, nblk), axis=0, keepdims=True)
            hit = blk_id == first
            sel = jnp.where(hit, 1.0, sel)
            gate = jnp.where(hit, neg_inf, gate)
        sel_ref[h] = jnp.where(blk_id < i, sel, 0.0)

    causal = (lax.broadcasted_iota(jnp.int32, (blk, tq), 0)
              <= lax.broadcasted_iota(jnp.int32, (blk, tq), 1))

    head_row = lax.broadcasted_iota(jnp.int32, (heads, tq), 0)

    def attend(kb, vtb, keeps, first):
        if not first:
            m_old = m_ref[...]
            acc_old = [acc_ref[h] for h in range(heads)]
        m_all = jnp.zeros((heads, tq), F32)
        scores = [_dot_nt(kb[:, (h // per_group) * LANE:(h // per_group + 1) * LANE], qms[h])
                  for h in range(heads)]
        probs, alphas = [], []
        for h in range(heads):
            s = jnp.where(keeps[h], scores[h], neg_inf)
            s_max = jnp.max(s, axis=0, keepdims=True)
            m_new = s_max if first else jnp.maximum(m_old[h:h + 1, :], s_max)
            if not first:
                alphas.append(jnp.exp(m_old[h:h + 1, :] - m_new))
            probs.append(jnp.exp(s - m_new).astype(BF16))
            m_all = jnp.where(head_row == h, m_new, m_all)
        pvs = [_dot(vtb[h * ext:(h + 1) * ext, :], probs[h]) for h in range(heads)]
        m_ref[...] = m_all
        for h in range(heads):
            acc_ref[h] = pvs[h] if first else acc_old[h] * alphas[h] + pvs[h]

    start = pl.multiple_of(i * blk, blk)
    attend(k_ref[pl.ds(start, blk), :], vt_ref[:, pl.ds(start, blk)], [causal] * heads, True)

    def body(n, _):
        start = pl.multiple_of(n * blk, blk)
        keeps = [sel_ref[h, pl.ds(n, 1), :] > 0.0 for h in range(heads)]
        attend(k_ref[pl.ds(start, blk), :], vt_ref[:, pl.ds(start, blk)], keeps, False)
        return 0

    lax.fori_loop(0, i, body, 0)
    outs = [acc_ref[h, 0:dh, :] / acc_ref[h, dh:dh + 1, :] for h in range(heads)]
    o_ref[...] = jnp.concatenate(outs, axis=0).T.astype(BF16)


def _moba_prompt(q, kb, vt, kmean, batch, seq, dh):
    n, bw = q.shape
    tq = MOBA_BLOCK
    nblk = seq // MOBA_BLOCK
    heads = bw // dh
    km = kmean.reshape(batch, nblk, bw)
    tiles = seq // tq
    return pl.pallas_call(
        functools.partial(_moba_prompt_kernel, dh=dh),
        grid=(batch, tiles),
        in_specs=[pl.BlockSpec((tq, bw), lambda b, i: (b * tiles + i, 0)),
                  pl.BlockSpec((seq, bw), lambda b, i: (b, 0)),
                  pl.BlockSpec((None, vt.shape[1], seq), lambda b, i: (b, 0, 0)),
                  pl.BlockSpec((1, nblk, bw), lambda b, i: (b, 0, 0))],
        out_specs=pl.BlockSpec((tq, bw), lambda b, i: (b * tiles + i, 0)),
        out_shape=jax.ShapeDtypeStruct((n, bw), BF16),
        scratch_shapes=[pltpu.VMEM((heads, nblk, tq), F32), pltpu.VMEM((heads, tq), F32),
                        pltpu.VMEM((heads, dh + ONES_ROWS, tq), F32)],
        compiler_params=_params("arbitrary", "arbitrary"),
        name="moba_prompt",
    )(q, kb, vt, km)


def _residual_proj_kernel(*refs, n_in, norm):
    x_ref = refs[0]
    in_refs = refs[1:1 + n_in]
    w_ref = refs[1 + n_in]
    y = x_ref[...]
    lo = 0
    for r in in_refs:
        y = y + _dot(r[...], w_ref[lo:lo + r.shape[1], :])
        lo += r.shape[1]
    if norm == "final":
        g_ref, o_ref = refs[2 + n_in:]
        o_ref[...] = _rmsnorm(y, g_ref[...])
    else:
        g_ref, o_ref, h_ref = refs[2 + n_in:]
        o_ref[...] = y
        h_ref[...] = _rmsnorm(y, g_ref[...]).astype(BF16)


def _residual_proj(x, ins, w, g, norm, tm, name):
    n, d = x.shape
    tm = min(tm, n)
    row = lambda width: pl.BlockSpec((tm, width), lambda i: (i, 0))
    out_specs = [row(d)] if norm == "final" else [row(d), row(d)]
    out_shape = ([jax.ShapeDtypeStruct((n, d), F32)] if norm == "final"
                 else [jax.ShapeDtypeStruct((n, d), F32), jax.ShapeDtypeStruct((n, d), BF16)])
    return pl.pallas_call(
        functools.partial(_residual_proj_kernel, n_in=len(ins), norm=norm),
        grid=(n // tm,),
        in_specs=[row(d)] + [row(a.shape[1]) for a in ins] + [_full(w.shape), _full((1, d))],
        out_specs=out_specs,
        out_shape=out_shape,
        compiler_params=_params("arbitrary"),
        name=name,
    )(x, *ins, w, g.reshape(1, d))


FFN_COLS = 256
FFN_HALO = SUBLANE_BF16


def _silu_gate(gate, value):
    return gate * (1.0 / (1.0 + jnp.exp(-gate))) * value


def _ffn_up_prompt_kernel(h_ref, halo_ref, w_ref, wc_ref, bc_ref, act_ref, tail_ref, up_scr, *, tiles_per_seq):
    tm = h_ref.shape[0]
    dff = act_ref.shape[1]
    seq_start = pl.program_id(0) % tiles_per_seq == 0
    halo = halo_ref[...]
    halo = jnp.where(seq_start, jnp.zeros_like(halo), halo)
    hh = jnp.concatenate([halo, h_ref[...]], axis=0)
    for c in range(dff // FFN_COLS):
        halves = []
        for k, lo in enumerate((c * FFN_COLS, dff + c * FFN_COLS)):
            cols = slice(lo, lo + FFN_COLS)
            up = _dot(hh, w_ref[:, cols])
            tail_ref[0, :, cols] = up[tm + FFN_HALO - 8:, :]
            up_scr[k] = up
            conv = bc_ref[:, cols] + up[FFN_HALO:, :] * wc_ref[CONV_W - 1:CONV_W, cols]
            for j in range(1, CONV_W):
                conv = conv + up_scr[k, pl.ds(FFN_HALO - j, tm), :] * wc_ref[CONV_W - 1 - j:CONV_W - j, cols]
            halves.append(conv)
        act_ref[:, c * FFN_COLS:(c + 1) * FFN_COLS] = _silu_gate(halves[1], halves[0]).astype(BF16)


def _ffn_up_decode_kernel(h_ref, s0_ref, s1_ref, w_ref, wc_ref, bc_ref, act_ref, up_ref):
    dff = act_ref.shape[1]
    h = h_ref[...]
    for c in range(dff // FFN_COLS):
        halves = []
        for lo in (c * FFN_COLS, dff + c * FFN_COLS):
            cols = slice(lo, lo + FFN_COLS)
            up = _dot(h, w_ref[:, cols])
            up_ref[:, cols] = up
            halves.append(bc_ref[:, cols] + s0_ref[:, cols] * wc_ref[0:1, cols]
                          + s1_ref[:, cols] * wc_ref[1:2, cols] + up * wc_ref[2:3, cols])
        act_ref[:, c * FFN_COLS:(c + 1) * FFN_COLS] = _silu_gate(halves[1], halves[0]).astype(BF16)


def _ffn_up_prompt(h, seq, w_up, w_conv, b_conv, tm):
    n, d = h.shape
    dff = w_up.shape[1] // 2
    assert dff % FFN_COLS == 0 and seq % tm == 0 and tm % FFN_HALO == 0
    ratio = tm // FFN_HALO
    act, tail = pl.pallas_call(
        functools.partial(_ffn_up_prompt_kernel, tiles_per_seq=seq // tm),
        grid=(n // tm,),
        in_specs=[pl.BlockSpec((tm, d), lambda i: (i, 0)),
                  pl.BlockSpec((FFN_HALO, d), lambda i: (jnp.maximum(i * ratio - 1, 0), 0)),
                  _full(w_up.shape), _full(w_conv.shape), _full((1, 2 * dff))],
        out_specs=[pl.BlockSpec((tm, dff), lambda i: (i, 0)),
                   pl.BlockSpec((1, 8, 2 * dff), lambda i: (i, 0, 0))],
        out_shape=[jax.ShapeDtypeStruct((n, dff), BF16),
                   jax.ShapeDtypeStruct((n // tm, 8, 2 * dff), F32)],
        scratch_shapes=[pltpu.VMEM((2, tm + FFN_HALO, FFN_COLS), F32)],
        compiler_params=_params("arbitrary"),
        name="ffn_up_prompt",
    )(h, h, w_up, w_conv, b_conv.reshape(1, -1))
    last = tail.reshape(n // seq, seq // tm, 8, 2 * dff)[:, -1, 8 - (CONV_W - 1):, :]
    return act, last


def _ffn_up_decode(h, state, w_up, w_conv, b_conv):
    n, d = h.shape
    dff = w_up.shape[1] // 2
    act, up = pl.pallas_call(
        _ffn_up_decode_kernel,
        grid=(1,),
        in_specs=[_full((n, d)), _full((n, 2 * dff)), _full((n, 2 * dff)),
                  _full(w_up.shape), _full(w_conv.shape), _full((1, 2 * dff))],
        out_specs=[_full((n, dff)), _full((n, 2 * dff))],
        out_shape=[jax.ShapeDtypeStruct((n, dff), BF16), jax.ShapeDtypeStruct((n, 2 * dff), F32)],
        compiler_params=_params("arbitrary"),
        name="ffn_up_decode",
    )(h, state[:, 0], state[:, 1], w_up, w_conv, b_conv.reshape(1, -1))
    return act, jnp.stack([state[:, 1], up], axis=1)


def _odd_in_kernel(h_ref, w_ref, c64_ref, a64_ref, b64_ref, c128_ref, a128_ref, b128_ref,
                   q_ref, k_ref, v_ref, kb_ref, vt_ref, qi_ref, kif_ref, ki2_ref, wit_ref,
                   *, half64, half128, idx_dh):
    cq = q_ref.shape[1]
    ckv = k_ref.shape[1]
    ciq = qi_ref.shape[1]
    h = h_ref[...]
    t64 = (c64_ref[...], a64_ref[...], b64_ref[...])
    t128 = (c128_ref[...], a128_ref[...], b128_ref[...])
    q_ref[...] = _rope_apply(_dot(h, w_ref[:, 0:cq]), *t128, half128).astype(BF16)
    k = _rope_apply(_dot(h, w_ref[:, cq:cq + ckv]), *t128, half128)
    k_ref[...] = k
    kb_ref[...] = k.astype(BF16)
    v = _dot(h, w_ref[:, cq + ckv:cq + 2 * ckv])
    v_ref[...] = v
    if vt_ref is not None:
        vt_ref[...] = _with_ones_rows(v.T, ckv // C_KV_HEADS).astype(BF16)
    lo = cq + 2 * ckv
    qi_ref[...] = _rope_apply(_dot(h, w_ref[:, lo:lo + ciq]), *t64, half64).astype(BF16)
    tail = _dot(h, w_ref[:, lo + ciq:lo + ciq + LANE])
    ki = _rope_apply(tail, *t64, half64)
    kif_ref[...] = ki
    lane = lax.broadcasted_iota(jnp.int32, ki.shape, 1)
    ki2_ref[...] = jnp.where(lane < idx_dh, ki, pltpu.roll(ki, idx_dh, 1)).astype(BF16)
    wit_ref[...] = tail.T[idx_dh:idx_dh + 8, :] * (IDX_HEADS ** -0.5)


def _odd_in(h, seq, w_in, tabs64, half64, tabs128, half128, c_dh, idx_dh, tm, with_vt):
    n, d = h.shape
    tm = min(tm, n)
    cq = C_HEADS * c_dh
    ckv = C_KV_HEADS * c_dh
    ciq = IDX_HEADS * idx_dh
    cols = cq + 2 * ckv + ciq + LANE
    assert 2 * idx_dh == LANE and w_in.shape[1] <= cols
    w = jnp.pad(w_in, ((0, 0), (0, cols - w_in.shape[1]))).astype(BF16)
    if seq is None:
        tab = pl.BlockSpec((1, LANE), lambda i: (0, 0))
    else:
        tiles_per_seq = seq // tm
        tab = pl.BlockSpec((tm, LANE), lambda i: (i % tiles_per_seq, 0))
    row = lambda width, dt: (pl.BlockSpec((tm, width), lambda i: (i, 0)), jax.ShapeDtypeStruct((n, width), dt))
    col = lambda height, dt: (pl.BlockSpec((height, tm), lambda i: (0, i)), jax.ShapeDtypeStruct((height, n), dt))
    outs = [row(cq, BF16), row(ckv, F32), row(ckv, F32), row(ckv, BF16)]
    if with_vt:
        outs.append(col(ckv + C_KV_HEADS * ONES_ROWS, BF16))
    outs += [row(ciq, BF16), row(LANE, F32), row(LANE, BF16), col(8, F32)]

    def body(*refs):
        refs = list(refs)
        if not with_vt:
            refs.insert(8 + 4, None)
        _odd_in_kernel(*refs, half64=half64, half128=half128, idx_dh=idx_dh)

    res = pl.pallas_call(
        body,
        grid=(n // tm,),
        in_specs=[pl.BlockSpec((tm, d), lambda i: (i, 0)), _full(w.shape)] + [tab] * 6,
        out_specs=[o[0] for o in outs],
        out_shape=[o[1] for o in outs],
        compiler_params=_params("arbitrary"),
        name="odd_in_prompt" if with_vt else "odd_in_decode",
    )(h, w, *tabs64, *tabs128)
    res = list(res)
    if not with_vt:
        res.insert(4, None)
    return res


FLOAT_BITS = 32
INDEX_BIG = 2 ** 30


def _ind(cond):
    return jnp.where(cond, 1.0, 0.0)


def _ordered_bits(x, to):
    b = x if x.dtype == jnp.int32 else lax.bitcast_convert_type(x, jnp.int32)
    b = jnp.where(b < 0, b ^ 0x7FFFFFFF, b)
    return b if to == jnp.int32 else lax.bitcast_convert_type(b, F32)


def _bisect_threshold(count_fn, lo, hi, keep):
    keep_f = jnp.float32(keep)

    def step(_, st):
        lo_k, hi_k, n_hi = st
        mid_k = (lax.shift_right_arithmetic(lo_k, 1) + lax.shift_right_arithmetic(hi_k, 1)
                 + (lo_k & hi_k & 1))
        cnt, = count_fn(lambda s, idx: (_ind(s >= _ordered_bits(mid_k, F32)),))
        raise_lo = jnp.logical_and(mid_k > lo_k, cnt >= keep_f)
        lower_hi = jnp.logical_and(mid_k > lo_k, cnt < keep_f)
        return (jnp.where(raise_lo, mid_k, lo_k), jnp.where(lower_hi, mid_k, hi_k),
                jnp.where(lower_hi, cnt, n_hi))

    lo_k = _ordered_bits(lo, jnp.int32)
    hi_k = _ordered_bits(hi, jnp.int32) + 1
    lo_k, _, n_gt = lax.fori_loop(0, FLOAT_BITS, step, (lo_k, hi_k, jnp.zeros(lo.shape, F32)))
    return _ordered_bits(lo_k, F32), n_gt


def _select_threshold(count_fn, lo, hi, active, keep, index_bits):
    thr, n_gt = _bisect_threshold(count_fn, lo, hi, keep)
    n_eq, = count_fn(lambda s, idx: (_ind(s == thr),))
    need = jnp.float32(keep) - n_gt
    excess = jnp.where(active, jnp.where(n_eq > need, 1, 0), 0)

    def tie_search():
        lo_i = jnp.full(thr.shape, -1, jnp.int32)
        hi_i = jnp.full(thr.shape, (1 << index_bits) - 1, jnp.int32)
        for _ in range(index_bits + 1):
            mid_i = lax.shift_right_arithmetic(lo_i + hi_i, 1)
            n_low, = count_fn(lambda s, idx: (jnp.where(s == thr, _ind(idx <= mid_i), 0.0),))
            ok = n_low >= need
            hi_i = jnp.where(ok, mid_i, hi_i)
            lo_i = jnp.where(ok, lo_i, mid_i)
        return hi_i

    cut = lax.cond(jnp.max(excess) > 0, tie_search, lambda: jnp.full(thr.shape, INDEX_BIG, jnp.int32))
    thr = jnp.where(active, thr, -jnp.inf)
    cut = jnp.where(active, cut, -1)
    return thr, cut


def _selected(s, idx, thr, cut):
    return jnp.where(s > thr, 1.0, jnp.where(s == thr, _ind(idx <= cut), 0.0))


def _dsa_prompt_kernel(q_ref, qi_ref, wit_ref, kb_ref, vt_ref, ki2_ref, o_ref,
                       sc_ref, m_ref, acc_ref, *, c_dh, idx_dh, keep, kc):
    tq = q_ref.shape[0]
    seq = kb_ref.shape[0]
    t0 = pl.program_id(1) * tq
    nch = (t0 + tq + kc - 1) // kc
    scale = c_dh ** -0.5
    neg_inf = jnp.float32(-jnp.inf)
    lane = lax.broadcasted_iota(jnp.int32, (tq, LANE), 1)
    key_iota = lax.broadcasted_iota(jnp.int32, (kc, tq), 0)
    qpos = t0 + lax.broadcasted_iota(jnp.int32, (1, tq), 1)

    qi = qi_ref[...]
    per_group = LANE // idx_dh
    qms = []
    for h in range(IDX_HEADS):
        grp = qi[:, (h // per_group) * LANE:(h // per_group + 1) * LANE]
        qms.append(jnp.where(_head_of(lane, idx_dh) == h % per_group, grp, jnp.zeros_like(grp)))
    wit = wit_ref[...]

    def score_chunk(c, carry):
        lo, hi = carry
        start = pl.multiple_of(c * kc, kc)
        kic = ki2_ref[pl.ds(start, kc), :]
        acc = jnp.zeros((kc, tq), F32)
        for h in range(IDX_HEADS):
            acc = acc + jnp.maximum(_dot_nt(kic, qms[h]), 0.0) * wit[h:h + 1, :]
        score = acc * (idx_dh ** -0.5)
        adm = start + key_iota <= qpos
        sc_ref[pl.ds(start, kc), :] = jnp.where(adm, score, neg_inf)
        lo = jnp.minimum(lo, jnp.min(jnp.where(adm, score, jnp.inf), axis=0, keepdims=True))
        hi = jnp.maximum(hi, jnp.max(jnp.where(adm, score, neg_inf), axis=0, keepdims=True))
        return lo, hi

    lo, hi = lax.fori_loop(0, nch, score_chunk,
                           (jnp.full((1, tq), jnp.inf, F32), jnp.full((1, tq), neg_inf, F32)))

    def count_fn(pred, n_out=1):
        def body(c, cnts):
            start = pl.multiple_of(c * kc, kc)
            hits = pred(sc_ref[pl.ds(start, kc), :], start + key_iota)
            return tuple(cnt + jnp.sum(hit.reshape(kc // 8, 8, tq), axis=0) for cnt, hit in zip(cnts, hits))
        parts = lax.fori_loop(0, nch, body, tuple(jnp.zeros((8, tq), F32) for _ in range(n_out)))
        return tuple(jnp.sum(part, axis=0, keepdims=True) for part in parts)

    thr, n_gt = _bisect_threshold(count_fn, lo, hi, keep)
    active = qpos + 1 > keep
    thr = jnp.where(active, thr, neg_inf)
    need = jnp.where(active, jnp.float32(keep) - n_gt, 0.0)
    prefix_ones = jnp.where(lax.broadcasted_iota(jnp.int32, (kc, kc), 0)
                            >= lax.broadcasted_iota(jnp.int32, (kc, kc), 1), 1.0, 0.0).astype(BF16)

    m_ref[...] = jnp.full(m_ref.shape, NEG_BIG, F32)
    acc_ref[...] = jnp.zeros(acc_ref.shape, F32)
    group = C_HEADS // C_KV_HEADS
    ext = c_dh + ONES_ROWS
    exp2_scale = scale * LOG2_E
    head_row = lax.broadcasted_iota(jnp.int32, m_ref.shape, 0)

    def attend_chunk(c, ties_before):
        start = pl.multiple_of(c * kc, kc)
        sc = sc_ref[pl.ds(start, kc), :]
        tie = _ind(sc == thr)
        tie_rank = ties_before + _dot(prefix_ones, tie.astype(BF16))
        keepm = jnp.where(sc > thr, 1.0, jnp.where(tie_rank <= need, tie, 0.0)) > 0.0
        ties_before = tie_rank[kc - 1:kc, :]
        kbc = kb_ref[pl.ds(start, kc), :]
        vtc = vt_ref[:, pl.ds(start, kc)]
        m_old = m_ref[...]
        m_all = jnp.zeros(m_old.shape, F32)
        scores = [_dot_nt(kbc[:, (h // group) * c_dh:(h // group + 1) * c_dh], q_ref[:, h * c_dh:(h + 1) * c_dh])
                  for h in range(C_HEADS)]
        probs, alphas = [], []
        for h in range(C_HEADS):
            s = jnp.where(keepm, scores[h], neg_inf)
            m_new = jnp.maximum(m_old[h:h + 1, :], jnp.max(s, axis=0, keepdims=True))
            alphas.append(jnp.exp2((m_old[h:h + 1, :] - m_new) * exp2_scale))
            probs.append(jnp.exp2((s - m_new) * exp2_scale).astype(BF16))
            m_all = jnp.where(head_row == h, m_new, m_all)
        pvs = [_dot(vtc[(h // group) * ext:(h // group + 1) * ext, :], probs[h]) for h in range(C_HEADS)]
        m_ref[...] = m_all
        for h in range(C_HEADS):
            acc_ref[h] = acc_ref[h] * alphas[h] + pvs[h]
        return ties_before

    lax.fori_loop(0, nch, attend_chunk, jnp.zeros((1, tq), F32))
    for h in range(C_HEADS):
        out_t = acc_ref[h, 0:c_dh, :] / acc_ref[h, c_dh:c_dh + 1, :]
        o_ref[:, h * c_dh:(h + 1) * c_dh] = out_t.T.astype(BF16)


def _dsa_prompt(q, qi, wit, kb, vt, ki2, batch, seq, c_dh, idx_dh, keep, tq, kc):
    n = q.shape[0]
    tiles = seq // tq
    assert c_dh == LANE and seq % kc == 0 and seq % tq == 0
    return pl.pallas_call(
        functools.partial(_dsa_prompt_kernel, c_dh=c_dh, idx_dh=idx_dh, keep=keep, kc=kc),
        grid=(batch, tiles),
        in_specs=[pl.BlockSpec((tq, q.shape[1]), lambda b, i: (b * tiles + i, 0)),
                  pl.BlockSpec((tq, qi.shape[1]), lambda b, i: (b * tiles + i, 0)),
                  pl.BlockSpec((8, tq), lambda b, i: (0, b * tiles + i)),
                  pl.BlockSpec((seq, kb.shape[1]), lambda b, i: (b, 0)),
                  pl.BlockSpec((vt.shape[0], seq), lambda b, i: (0, b)),
                  pl.BlockSpec((seq, LANE), lambda b, i: (b, 0))],
        out_specs=pl.BlockSpec((tq, q.shape[1]), lambda b, i: (b * tiles + i, 0)),
        out_shape=jax.ShapeDtypeStruct(q.shape, BF16),
        scratch_shapes=[pltpu.VMEM((seq, tq), F32), pltpu.VMEM((8, tq), F32),
                        pltpu.VMEM((C_HEADS, c_dh + ONES_ROWS, tq), F32)],
        compiler_params=_params("arbitrary", "arbitrary"),
        name="dsa_prompt",
    )(q, qi, wit, kb, vt, ki2)


PAGE_ROWS = 128
DECODE_SCORE_SEQS = 4
DECODE_ATTN_SEQS = 2


def _lane_bcast_col(row):
    return jnp.broadcast_to(row, (LANE, row.shape[1])).T


def _moba_decode_kernel(pt_ref, q_ref, kn_ref, vn_ref, *rest, n_pages, dh):
    kt_refs, vt_refs, o_ref = rest[:n_pages], rest[n_pages:2 * n_pages], rest[2 * n_pages]
    bw = q_ref.shape[2]
    heads = bw // dh
    ppb = MOBA_BLOCK // PAGE_ROWS
    nblk = n_pages // ppb
    neg_inf = jnp.float32(-jnp.inf)
    qb = _lane_bcast_col(q_ref[0].astype(F32))

    def head_sums(x):
        return jnp.sum(x.reshape(heads, dh, LANE), axis=1)

    def head_rows(x):
        return jnp.broadcast_to(x[:, None, :], (heads, dh, LANE)).reshape(bw, LANE)

    s_pages = [head_sums(kt_refs[j][...] * qb) for j in range(n_pages)]
    s_own = head_sums(_lane_bcast_col(kn_ref[0]) * qb)
    gates = []
    for n in range(nblk):
        g = jnp.zeros((heads, 1), F32)
        for j in range(n * ppb, (n + 1) * ppb):
            g = g + jnp.sum(s_pages[j], axis=1, keepdims=True)
        gates.append(g)
    sels = []
    for n in range(nblk):
        rank = jnp.zeros((heads, 1), F32)
        for m in range(nblk):
            if m != n:
                beats = gates[m] >= gates[n] if m < n else gates[m] > gates[n]
                rank = rank + jnp.where(beats, 1.0, 0.0)
        sels.append(rank < MOBA_TOPK)
    m = s_own
    masked = []
    for j in range(n_pages):
        sm = jnp.where(sels[j // ppb], s_pages[j], neg_inf)
        masked.append(sm)
        m = jnp.maximum(m, jnp.max(sm, axis=1, keepdims=True))
    e_own = jnp.exp(s_own - m)
    es = [jnp.exp(sm - m) for sm in masked]
    l = e_own
    for e in es:
        l = l + jnp.sum(e, axis=1, keepdims=True)
    rinv = 1.0 / l
    lane = lax.broadcasted_iota(jnp.int32, (bw, LANE), 1)
    acc = jnp.where(lane == 0, head_rows(e_own * rinv) * _lane_bcast_col(vn_ref[0]), 0.0)
    for j in range(n_pages):
        acc = acc + head_rows(es[j] * rinv) * vt_refs[j][...]
    out_col = jnp.sum(acc, axis=1, keepdims=True)
    o_ref[0] = jnp.broadcast_to(out_col, (bw, LANE)).T[0:1].astype(BF16)


def _moba_decode(q, k_new, v_new, cache_k, cache_v, page_table, dh):
    n, bw = q.shape
    n_pages = page_table.shape[1]
    assert cache_k.shape[1] == PAGE_ROWS == LANE and (n_pages * PAGE_ROWS) % MOBA_BLOCK == 0
    ckt = jnp.transpose(cache_k, (0, 2, 3, 1)).reshape(cache_k.shape[0], bw, PAGE_ROWS)
    cvt = jnp.transpose(cache_v, (0, 2, 3, 1)).reshape(cache_v.shape[0], bw, PAGE_ROWS)
    seq_row = pl.BlockSpec((1, 1, bw), lambda b, pt: (b, 0, 0))
    pages = [pl.BlockSpec((None, bw, PAGE_ROWS), lambda b, pt, j=j: (pt[b, j], 0, 0)) for j in range(n_pages)]
    out = pl.pallas_call(
        functools.partial(_moba_decode_kernel, n_pages=n_pages, dh=dh),
        grid_spec=pltpu.PrefetchScalarGridSpec(
            num_scalar_prefetch=1, grid=(n,),
            in_specs=[seq_row, seq_row, seq_row] + pages + pages,
            out_specs=seq_row),
        out_shape=jax.ShapeDtypeStruct((n, 1, bw), BF16),
        compiler_params=_params("arbitrary"),
        name="moba_decode",
    )(page_table, q.reshape(n, 1, bw), k_new.reshape(n, 1, bw), v_new.reshape(n, 1, bw),
      *([ckt] * n_pages), *([cvt] * n_pages))
    return out.reshape(n, bw)


def _dsa_decode_score_kernel(pt_ref, qi_ref, wi_ref, kin_ref, *rest, n_pages, idx_dh, seqs):
    for g in range(seqs):
        _dsa_decode_score_seq(g, qi_ref, wi_ref, kin_ref, rest[g * n_pages:(g + 1) * n_pages], rest[-1],
                              n_pages, idx_dh)


def _dsa_decode_score_seq(g, qi_ref, wi_ref, kin_ref, kit_refs, o_ref, n_pages, idx_dh):
    qb = _lane_bcast_col(qi_ref[g].astype(F32))
    wi8 = wi_ref[g]
    scale = idx_dh ** -0.5

    def score(kit):
        total = jnp.zeros((1, LANE), F32)
        for h in range(IDX_HEADS):
            sc = jnp.sum(kit * qb[h * idx_dh:(h + 1) * idx_dh], axis=0, keepdims=True)
            total = total + jnp.maximum(sc, 0.0) * wi8[h:h + 1, :]
        return total * scale

    o_ref[g] = jnp.full(o_ref.shape[1:], -jnp.inf, F32)
    for j in range(n_pages):
        o_ref[g, j:j + 1, :] = score(kit_refs[j][...])
    own = score(_lane_bcast_col(kin_ref[g]))
    lane = lax.broadcasted_iota(jnp.int32, (1, PAGE_ROWS), 1)
    o_ref[g, n_pages:n_pages + 1, :] = jnp.where(lane == 0, own, -jnp.inf)


def _dsa_decode_select_kernel(s_ref, m_ref, *, keep, n_valid):
    s = s_ref[...]
    idx = lax.broadcasted_iota(jnp.int32, s.shape, 1)
    neg_inf = jnp.float32(-jnp.inf)
    lo = jnp.min(jnp.where(s > neg_inf, s, jnp.inf), axis=1, keepdims=True)
    hi = jnp.max(s, axis=1, keepdims=True)
    count_fn = lambda pred, n_out=1: tuple(jnp.sum(hit, axis=1, keepdims=True) for hit in pred(s, idx))
    active = jnp.full(lo.shape, n_valid > keep, jnp.bool_)
    thr, cut = _select_threshold(count_fn, lo, hi, active, keep, (s.shape[1] - 1).bit_length())
    m_ref[...] = _selected(s, idx, thr, cut)


def _dsa_decode_attn_kernel(pt_ref, q_ref, kn_ref, vn_ref, mask_ref, *rest, n_pages, c_dh, seqs):
    staged = [_dsa_decode_attn_scores(g, q_ref, kn_ref, mask_ref, rest[g * n_pages:(g + 1) * n_pages],
                                      n_pages, c_dh) for g in range(seqs)]
    for g in range(seqs):
        _dsa_decode_attn_values(g, staged[g], vn_ref, rest[(seqs + g) * n_pages:(seqs + g + 1) * n_pages],
                                rest[-1], n_pages, c_dh)


def _dsa_decode_attn_scores(g, q_ref, kn_ref, mask_ref, k_refs, n_pages, c_dh):
    rows = k_refs[0].shape[0]
    group = C_HEADS // C_KV_HEADS
    scale = c_dh ** -0.5
    neg_inf = jnp.float32(-jnp.inf)
    qm = jnp.concatenate([q_ref[g].astype(F32), jnp.zeros((LANE - C_HEADS, c_dh), F32)], axis=0).astype(BF16)
    row_id = lax.broadcasted_iota(jnp.int32, (rows, LANE), 0)
    lane_id = lax.broadcasted_iota(jnp.int32, (rows, LANE), 1)
    own_kv = (row_id & (C_KV_HEADS - 1)) == _head_of(lane_id, group)
    mask = mask_ref[g]
    mask_t = jnp.concatenate([mask, jnp.zeros((LANE - mask.shape[0], PAGE_ROWS), F32)], axis=0).T
    repeat = jnp.where(_head_of(row_id, C_KV_HEADS) == lane_id, 1.0, 0.0).astype(BF16)
    row_mask = _dot(repeat, mask_t.astype(BF16))
    kn = jnp.concatenate([kn_ref[g], jnp.zeros((8 - C_KV_HEADS, c_dh), F32)], axis=0)
    s_own_kv = _dot_nt(kn.astype(BF16), qm)
    lane_kv = _head_of(lax.broadcasted_iota(jnp.int32, (1, LANE), 1), group)
    s_own = jnp.zeros((1, LANE), F32)
    for kv in range(C_KV_HEADS):
        s_own = jnp.where(lane_kv == kv, s_own_kv[kv:kv + 1, :], s_own)
    s_own = jnp.where(mask[n_pages:n_pages + 1, 0:1] > 0.0, s_own, neg_inf)
    m = jnp.maximum(s_own, NEG_BIG)
    masked = []
    for j in range(n_pages):
        s = _dot_nt(k_refs[j][...].astype(BF16), qm)
        sm = jnp.where(own_kv, jnp.where(row_mask[:, j:j + 1] > 0.5, s, neg_inf), neg_inf)
        masked.append(sm)
        m = jnp.maximum(m, jnp.max(sm, axis=0, keepdims=True))
    return masked, s_own, m, lane_kv


def _dsa_decode_attn_values(g, staged, vn_ref, v_refs, o_ref, n_pages, c_dh):
    masked, s_own, m, lane_kv = staged
    scale = c_dh ** -0.5
    own_rows = lax.broadcasted_iota(jnp.int32, (SUBLANE_BF16, LANE), 0)
    e_own = jnp.where(own_rows == lane_kv, jnp.exp((s_own - m) * scale), 0.0)
    v_own = jnp.concatenate([vn_ref[g], jnp.zeros((SUBLANE_BF16 - C_KV_HEADS, c_dh), F32)], axis=0)
    e_all = jnp.concatenate([jnp.exp((sm - m) * scale) for sm in masked] + [e_own], axis=0).astype(BF16)
    v_all = jnp.concatenate([v_refs[j][...] for j in range(n_pages)] + [v_own], axis=0).astype(BF16)
    v_all = jnp.concatenate([v_all, jnp.ones(v_all.shape, BF16)], axis=1)
    out = lax.dot_general(e_all, v_all, (((0,), (0,)), ((), ())), preferred_element_type=F32)
    o_ref[g] = (out[0:C_HEADS, 0:c_dh] / out[0:C_HEADS, c_dh:2 * c_dh]).astype(BF16)


def _dsa_decode(q, qi, wit, k_new, v_new, ki_new, cache_k, cache_v, cache_ki, page_table, c_dh, idx_dh, keep):
    n = q.shape[0]
    n_pages = page_table.shape[1]
    assert cache_k.shape[1] == PAGE_ROWS == LANE
    pool = cache_k.shape[0]
    ck = cache_k.reshape(pool, PAGE_ROWS * C_KV_HEADS, c_dh)
    cv = cache_v.reshape(pool, PAGE_ROWS * C_KV_HEADS, c_dh)
    ckit = jnp.transpose(cache_ki, (0, 2, 1))
    rows = -(-(n_pages + 1) // 8) * 8
    wi8 = wit.T.reshape(n, 8, 1)

    def seq3(g, a, b):
        return pl.BlockSpec((g, a, b), lambda s, pt: (s, 0, 0))

    def pages(g, r, w):
        return [pl.BlockSpec((None, r, w), lambda s, pt, j=j, k=k: (pt[s * g + k, j], 0, 0))
                for k in range(g) for j in range(n_pages)]

    gs = DECODE_SCORE_SEQS
    assert n % gs == 0 and n % DECODE_ATTN_SEQS == 0
    scores = pl.pallas_call(
        functools.partial(_dsa_decode_score_kernel, n_pages=n_pages, idx_dh=idx_dh, seqs=gs),
        grid_spec=pltpu.PrefetchScalarGridSpec(
            num_scalar_prefetch=1, grid=(n // gs,),
            in_specs=[seq3(gs, 1, IDX_HEADS * idx_dh), seq3(gs, 8, 1), seq3(gs, 1, idx_dh)]
                     + pages(gs, idx_dh, PAGE_ROWS),
            out_specs=seq3(gs, rows, PAGE_ROWS)),
        out_shape=jax.ShapeDtypeStruct((n, rows, PAGE_ROWS), F32),
        compiler_params=_params("arbitrary"),
        name="dsa_decode_scores",
    )(page_table, qi.reshape(n, 1, IDX_HEADS * idx_dh), wi8, ki_new.reshape(n, 1, idx_dh),
      *([ckit] * (gs * n_pages)))
    flat = rows * PAGE_ROWS
    mask = pl.pallas_call(
        functools.partial(_dsa_decode_select_kernel, keep=keep, n_valid=n_pages * PAGE_ROWS + 1),
        grid=(1,),
        in_specs=[_full((n, flat))],
        out_specs=_full((n, flat)),
        out_shape=jax.ShapeDtypeStruct((n, flat), F32),
        compiler_params=_params("arbitrary"),
        name="dsa_decode_select",
    )(scores.reshape(n, flat))
    cq = C_HEADS * c_dh
    ga = DECODE_ATTN_SEQS
    kv_rows = PAGE_ROWS * C_KV_HEADS
    out = pl.pallas_call(
        functools.partial(_dsa_decode_attn_kernel, n_pages=n_pages, c_dh=c_dh, seqs=ga),
        grid_spec=pltpu.PrefetchScalarGridSpec(
            num_scalar_prefetch=1, grid=(n // ga,),
            in_specs=[seq3(ga, C_HEADS, c_dh), seq3(ga, C_KV_HEADS, c_dh), seq3(ga, C_KV_HEADS, c_dh),
                      seq3(ga, rows, PAGE_ROWS)] + pages(ga, kv_rows, c_dh) + pages(ga, kv_rows, c_dh),
            out_specs=seq3(ga, C_HEADS, c_dh)),
        out_shape=jax.ShapeDtypeStruct((n, C_HEADS, c_dh), BF16),
        compiler_params=_params("arbitrary"),
        name="dsa_decode_attn",
    )(page_table, q.reshape(n, C_HEADS, c_dh), k_new.reshape(n, C_KV_HEADS, c_dh),
      v_new.reshape(n, C_KV_HEADS, c_dh), mask.reshape(n, rows, PAGE_ROWS),
      *([ck] * (ga * n_pages)), *([cv] * (ga * n_pages)))
    return out.reshape(n, cq)


def kernel(x_prompt, x_sample, cache_moba_k, cache_moba_v, cache_dsa_k, cache_dsa_v, cache_dsa_kidx,
           state_ffn_conv, page_table, w_in_even, g_sgu, w_sgu, b_sgu, w_out_even, w_in_odd, w_out_odd,
           g_mix, g_ffn, w_up, w_conv, b_conv, w_down, g_final):
    batch, seq, d = x_prompt.shape
    n_dec, dec_seq, _ = x_sample.shape
    depth = g_mix.shape[0]
    n_pages = page_table.shape[1]
    past_len = n_pages * cache_moba_k.shape[2]
    assert depth == 2 and dec_seq == 1, "one even + one odd layer, one decode token per sequence"
    b_dh = cache_moba_k.shape[-1]
    c_dh = cache_dsa_k.shape[-1]
    idx_dh = cache_dsa_kidx.shape[-1]
    keep_p = min(DSA_TOPK, seq // 4)
    keep_s = min(DSA_TOPK, (past_len + dec_seq) // 4)
    xp = x_prompt.reshape(batch * seq, d)
    xs = x_sample.reshape(n_dec, d)
    bf = lambda w: w.astype(BF16)

    tp64, half64 = _rope_tables(b_dh, seq, 0)
    ts64, _ = _rope_tables(b_dh, 1, past_len)
    tp128, half128 = _rope_tables(c_dh, seq, 0)
    ts128, _ = _rope_tables(c_dh, 1, past_len)
    assert idx_dh == b_dh

    w_in0, w_out0 = bf(w_in_even[0]), bf(w_out_even[0])
    a_p, q_p, kb_p, mkt_p, mvt_p, vt_p, km_p = _even_in_prompt(
        xp, seq, g_mix[0], w_in0, tp64, half64, g_sgu[0], w_sgu[0], b_sgu[0])
    bo_p = _moba_prompt(q_p, kb_p, vt_p, km_p, batch, seq, b_dh)
    cache_view = lambda t: jnp.transpose(t.reshape(batch, B_HEADS, b_dh, seq), (0, 3, 1, 2))[None]
    a_s, q_s, mk_s, mv_s, va_s = _even_in_decode(xs, g_mix[0], w_in0, ts64, half64, g_sgu[0], w_sgu[0], b_sgu[0])
    bo_s = _moba_decode(q_s, mk_s, mv_s, cache_moba_k[0], cache_moba_v[0], page_table, b_dh)
    w_up0, w_down0 = bf(w_up[0]), bf(w_down[0])
    xp, hp = _residual_proj(xp, [a_p, bo_p], w_out0, g_ffn[0], "next", 256, "out_even_prompt")
    xs, hs = _residual_proj(xs, [a_s, bo_s], w_out0, g_ffn[0], "next", 256, "out_even_decode")
    act_p, conv_p0 = _ffn_up_prompt(hp, seq, w_up0, w_conv[0], b_conv[0], 512)
    act_s, conv_s0 = _ffn_up_decode(hs, state_ffn_conv[0], w_up0, w_conv[0], b_conv[0])
    xp, hp = _residual_proj(xp, [act_p], w_down0, g_mix[1], "next", 256, "ffn_down0_prompt")
    xs, hs = _residual_proj(xs, [act_s], w_down0, g_mix[1], "next", 256, "ffn_down0_decode")

    w_out1 = bf(w_out_odd[0])
    q_p, dk_p, dv_p, kb_p, vt_p, qi_p, kif_p, ki2_p, wit_p = _odd_in(
        hp, seq, w_in_odd[0], tp64, half64, tp128, half128, c_dh, idx_dh, 256, True)
    o_p = _dsa_prompt(q_p, qi_p, wit_p, kb_p, vt_p, ki2_p, batch, seq, c_dh, idx_dh, keep_p, 256, 256)
    q_s, dk_s, dv_s, _, _, qi_s, kif_s, _, wit_s = _odd_in(
        hs, None, w_in_odd[0], ts64, half64, ts128, half128, c_dh, idx_dh, 256, False)
    o_s = _dsa_decode(q_s, qi_s, wit_s, dk_s, dv_s, kif_s[:, :idx_dh], cache_dsa_k[0], cache_dsa_v[0],
                      cache_dsa_kidx[0], page_table, c_dh, idx_dh, keep_s)
    w_up1, w_down1 = bf(w_up[1]), bf(w_down[1])
    xp, hp = _residual_proj(xp, [o_p], w_out1, g_ffn[1], "next", 256, "out_odd_prompt")
    xs, hs = _residual_proj(xs, [o_s], w_out1, g_ffn[1], "next", 256, "out_odd_decode")
    act_p, conv_p1 = _ffn_up_prompt(hp, seq, w_up1, w_conv[1], b_conv[1], 512)
    act_s, conv_s1 = _ffn_up_decode(hs, state_ffn_conv[1], w_up1, w_conv[1], b_conv[1])
    y_p, = _residual_proj(xp, [act_p], w_down1, g_final, "final", 256, "ffn_down1_prompt")
    y_s, = _residual_proj(xs, [act_s], w_down1, g_final, "final", 256, "ffn_down1_decode")

    ckv = C_KV_HEADS
    return (y_p.reshape(batch, seq, d), y_s.reshape(n_dec, dec_seq, d),
            cache_view(mkt_p), cache_view(mvt_p),
            mk_s.reshape(1, n_dec, dec_seq, B_HEADS, b_dh), mv_s.reshape(1, n_dec, dec_seq, B_HEADS, b_dh),
            va_s.reshape(1, n_dec, dec_seq, A_GROUPS, LANE),
            dk_p.reshape(1, batch, seq, ckv, c_dh), dv_p.reshape(1, batch, seq, ckv, c_dh),
            kif_p[:, :idx_dh].reshape(1, batch, seq, idx_dh),
            dk_s.reshape(1, n_dec, dec_seq, ckv, c_dh), dv_s.reshape(1, n_dec, dec_seq, ckv, c_dh),
            kif_s[:, :idx_dh].reshape(1, n_dec, dec_seq, idx_dh),
            jnp.stack([conv_p0, conv_p1]), jnp.stack([conv_s0, conv_s1]))
```

```python
import functools

import numpy as np
import jax
import jax.numpy as jnp
from jax import lax
from jax.experimental import pallas as pl
from jax.experimental.pallas import tpu as pltpu

F32 = jnp.float32
BF16 = jnp.bfloat16

A_GROUPS = 4
CHUNK = 128
B_HEADS = 8
MOBA_BLOCK = 256
MOBA_TOPK = 3
C_HEADS = 8
C_KV_HEADS = 2
IDX_HEADS = 4
DSA_TOPK = 256
ROPE_THETA = 500000.0
ROPE_FRAC = 4
CONV_W = 3
EPS = 1e-6

LANE = 128
SUBLANE_BF16 = 16
VMEM_LIMIT_BYTES = 56 * 1024 * 1024
NEG_BIG = -1e30
LOG2_E = 1.4426950408889634
ROW_TILE = 512


def _params(*sem):
    return pltpu.CompilerParams(dimension_semantics=sem,
                                vmem_limit_bytes=VMEM_LIMIT_BYTES)


def _full(shape):
    nd = len(shape)
    return pl.BlockSpec(shape, lambda *_: (0,) * nd)


def _rmsnorm(x, g):
    ms = jnp.mean(x * x, axis=-1, keepdims=True)
    return x * lax.rsqrt(ms + EPS) * g


def _rope_apply(x, c, s1, s2, half):
    parts = []
    for j in range(x.shape[1] // LANE):
        xs = x[:, j * LANE:(j + 1) * LANE]
        parts.append(xs * c + pltpu.roll(xs, LANE - half, 1) * s1
                     + pltpu.roll(xs, half, 1) * s2)
    return parts[0] if len(parts) == 1 else jnp.concatenate(parts, axis=1)


def _head_of(index, dh):
    assert dh & (dh - 1) == 0
    return lax.shift_right_logical(index, dh.bit_length() - 1)


ONES_ROWS = SUBLANE_BF16


def _with_ones_rows(vt, dh):
    ones = jnp.ones((ONES_ROWS, vt.shape[1]), vt.dtype)
    parts = []
    for h in range(vt.shape[0] // dh):
        parts += [vt[h * dh:(h + 1) * dh], ones]
    return jnp.concatenate(parts, axis=0)


def _dot(a, b):
    return jnp.dot(a, b, preferred_element_type=F32)


def _dot_nt(a, b):
    return lax.dot_general(a, b, (((1,), (1,)), ((), ())),
                           preferred_element_type=F32)


def _rope_rows(dh):
    rd = dh // ROPE_FRAC
    half = rd // 2
    inv = ROPE_THETA ** (-jnp.arange(half, dtype=F32) * (2.0 / rd))
    r = np.arange(LANE) % dh
    j = np.where(r < half, r, np.where(r < rd, r - half, 0))
    inv_lane = jnp.where(jnp.asarray(r < rd), inv[j], 0.0).reshape(1, LANE)
    s1 = jnp.asarray(np.where(r < half, -1.0, 0.0), F32).reshape(1, LANE)
    s2 = jnp.asarray(np.where((r >= half) & (r < rd), 1.0, 0.0), F32).reshape(1, LANE)
    return inv_lane, s1, s2, half


def _rope_table_kernel(inv_ref, s1_ref, s2_ref, c_ref, a_ref, b_ref, *, base):
    rows = c_ref.shape[0]
    pos = lax.broadcasted_iota(jnp.int32, (rows, LANE), 0) + (base + pl.program_id(0) * rows)
    ang = pos.astype(F32) * inv_ref[...]
    s = jnp.sin(ang)
    c_ref[...] = jnp.cos(ang)
    a_ref[...] = s * s1_ref[...]
    b_ref[...] = s * s2_ref[...]


def _rope_tables(dh, n_pos, base):
    inv_lane, s1, s2, half = _rope_rows(dh)
    rows = min(n_pos, 512)
    assert n_pos % rows == 0
    out = jax.ShapeDtypeStruct((n_pos, LANE), F32)
    spec = pl.BlockSpec((rows, LANE), lambda i: (i, 0))
    c, a, b = pl.pallas_call(
        functools.partial(_rope_table_kernel, base=base),
        grid=(n_pos // rows,),
        in_specs=[_full((1, LANE))] * 3,
        out_specs=[spec] * 3,
        out_shape=[out] * 3,
        compiler_params=_params("arbitrary"),
        name="rope_tables",
    )(inv_lane, s1, s2)
    return (c, a, b), half


def _sgu_values(h, w_ref, gs_ref, aw):
    u = jax.nn.gelu(_dot(h, w_ref[:, 0:aw]))
    zv = jax.nn.gelu(_dot(h, w_ref[:, aw:2 * aw]))
    vs = []
    for g in range(A_GROUPS):
        vg = zv[:, g * LANE:(g + 1) * LANE]
        vs.append(_rmsnorm(vg, gs_ref[:, g * LANE:(g + 1) * LANE]))
    return u, vs


def _even_in_prompt_kernel(x_ref, g_ref, w_ref, c_ref, s1_ref, s2_ref, gs_ref, ws_ref, bs_ref,
                           a_ref, q_ref, kb_ref, kt_ref, vt_ref, vtb_ref, km_ref, v_scr, *, half, scale):
    tm = x_ref.shape[0]
    aw = A_GROUPS * LANE
    bw = q_ref.shape[1]
    h = _rmsnorm(x_ref[...], g_ref[...]).astype(BF16)
    u, vs = _sgu_values(h, w_ref, gs_ref, aw)
    tri = (lax.broadcasted_iota(jnp.int32, (CHUNK, CHUNK), 0)
           >= lax.broadcasted_iota(jnp.int32, (CHUNK, CHUNK), 1))
    for g in range(A_GROUPS):
        wg = jnp.where(tri, ws_ref[g], 0.0).astype(BF16)
        for cc in range(tm // CHUNK):
            rows = slice(cc * CHUNK, (cc + 1) * CHUNK)
            mixed = _dot(wg, vs[g][rows].astype(BF16)) + bs_ref[g]
            a_ref[rows, g * LANE:(g + 1) * LANE] = (u[rows, g * LANE:(g + 1) * LANE] * mixed).astype(BF16)
    c, s1, s2 = c_ref[...], s1_ref[...], s2_ref[...]
    q = _rope_apply(_dot(h, w_ref[:, 2 * aw:2 * aw + bw]), c, s1, s2, half)
    q_ref[...] = (q * scale).astype(BF16)
    k = _rope_apply(_dot(h, w_ref[:, 2 * aw + bw:2 * aw + 2 * bw]), c, s1, s2, half)
    kt_ref[...] = k.T
    kb_ref[...] = k.astype(BF16)
    km_ref[0] = jnp.sum(k, axis=0, keepdims=True) * (1.0 / tm)
    v_scr[...] = _dot(h, w_ref[:, 2 * aw + 2 * bw:2 * aw + 3 * bw])
    vt = v_scr[...].T
    vt_ref[...] = vt
    vtb_ref[...] = _with_ones_rows(vt, bw // B_HEADS).astype(BF16)


def _even_in_decode_kernel(x_ref, g_ref, w_ref, c_ref, s1_ref, s2_ref, gs_ref, w00_ref, b0_ref,
                           a_ref, q_ref, k_ref, v_ref, va_ref, *, half, scale):
    aw = A_GROUPS * LANE
    bw = q_ref.shape[1]
    h = _rmsnorm(x_ref[...], g_ref[...]).astype(BF16)
    u, vs = _sgu_values(h, w_ref, gs_ref, aw)
    va = jnp.concatenate(vs, axis=1)
    va_ref[...] = va
    a_ref[...] = (u * (w00_ref[...] * va + b0_ref[...])).astype(BF16)
    c, s1, s2 = c_ref[...], s1_ref[...], s2_ref[...]
    q = _rope_apply(_dot(h, w_ref[:, 2 * aw:2 * aw + bw]), c, s1, s2, half)
    q_ref[...] = (q * scale).astype(BF16)
    k_ref[...] = _rope_apply(_dot(h, w_ref[:, 2 * aw + bw:2 * aw + 2 * bw]), c, s1, s2, half)
    v_ref[...] = _dot(h, w_ref[:, 2 * aw + 2 * bw:2 * aw + 3 * bw])


def _even_in_prompt(x, seq, g_mix, w_in, tables, half, g_sgu, w_sgu, b_sgu):
    n, d = x.shape
    tm = MOBA_BLOCK
    bw = (w_in.shape[1] - 2 * A_GROUPS * LANE) // 3
    dh = bw // B_HEADS
    tiles_per_seq = seq // tm
    row = lambda w, dt: (pl.BlockSpec((tm, w), lambda i: (i, 0)), jax.ShapeDtypeStruct((n, w), dt))
    tab = pl.BlockSpec((tm, LANE), lambda i: (i % tiles_per_seq, 0))
    col = lambda h, dt: (pl.BlockSpec((None, h, tm), lambda i: (i // tiles_per_seq, 0, i % tiles_per_seq)),
                         jax.ShapeDtypeStruct((n // seq, h, seq), dt))
    outs = [row(A_GROUPS * LANE, BF16), row(bw, BF16), row(bw, BF16), col(bw, F32), col(bw, F32),
            col(bw + B_HEADS * ONES_ROWS, BF16),
            (pl.BlockSpec((1, 1, bw), lambda i: (i, 0, 0)), jax.ShapeDtypeStruct((n // tm, 1, bw), F32))]
    bs = jnp.broadcast_to(b_sgu[:, :, None], (A_GROUPS, CHUNK, LANE))
    return pl.pallas_call(
        functools.partial(_even_in_prompt_kernel, half=half, scale=dh ** -0.5),
        grid=(n // tm,),
        in_specs=[pl.BlockSpec((tm, d), lambda i: (i, 0)), _full((1, d)), _full(w_in.shape),
                  tab, tab, tab, _full((1, A_GROUPS * LANE)), _full(w_sgu.shape), _full(bs.shape)],
        out_specs=[o[0] for o in outs],
        out_shape=[o[1] for o in outs],
        scratch_shapes=[pltpu.VMEM((tm, bw), F32)],
        compiler_params=_params("arbitrary"),
        name="even_in_prompt",
    )(x, g_mix.reshape(1, d), w_in, *tables, g_sgu.reshape(1, -1), w_sgu, bs)


def _even_in_decode(x, g_mix, w_in, tables, half, g_sgu, w_sgu, b_sgu):
    n, d = x.shape
    bw = (w_in.shape[1] - 2 * A_GROUPS * LANE) // 3
    dh = bw // B_HEADS
    aw = A_GROUPS * LANE
    w00 = jnp.repeat(w_sgu[:, 0, 0], LANE).reshape(1, aw)
    b0 = jnp.repeat(b_sgu[:, 0], LANE).reshape(1, aw)
    shapes = [(aw, BF16), (bw, BF16), (bw, F32), (bw, F32), (aw, F32)]
    return pl.pallas_call(
        functools.partial(_even_in_decode_kernel, half=half, scale=dh ** -0.5),
        grid=(1,),
        in_specs=[_full((n, d)), _full((1, d)), _full(w_in.shape)] + [_full((1, LANE))] * 3
                 + [_full((1, aw))] * 3,
        out_specs=[_full((n, w)) for w, _ in shapes],
        out_shape=[jax.ShapeDtypeStruct((n, w), dt) for w, dt in shapes],
        compiler_params=_params("arbitrary"),
        name="even_in_decode",
    )(x, g_mix.reshape(1, d), w_in, *tables, g_sgu.reshape(1, -1), w00, b0)


def _moba_prompt_kernel(q_ref, k_ref, vt_ref, km_ref, o_ref, sel_ref, m_ref, acc_ref, s_scr, *, dh):
    blk = MOBA_BLOCK
    tq = q_ref.shape[0]
    i = pl.program_id(1)
    nblk = km_ref.shape[1]
    heads = q_ref.shape[1] // dh
    ext = dh + ONES_ROWS
    per_group = LANE // dh
    lane = lax.broadcasted_iota(jnp.int32, (tq, LANE), 1)
    blk_id = lax.broadcasted_iota(jnp.int32, (nblk, tq), 0)
    neg_inf = jnp.float32(-jnp.inf)
    qms = []
    for h in range(heads):
        grp = h // per_group
        q = q_ref[:, grp * LANE:(grp + 1) * LANE]
        qm = jnp.where(_head_of(lane, dh) == h % per_group, q, jnp.zeros_like(q))
        qms.append(qm)
        kmean = km_ref[0, :, grp * LANE:(grp + 1) * LANE].astype(BF16)
        gate = jnp.where(blk_id < i, _dot_nt(kmean, qm), neg_inf)
        sel = jnp.zeros((nblk, tq), F32)
        for _ in range(MOBA_TOPK):
            best = jnp.max(gate, axis=0, keepdims=True)
            first = jnp.min(jnp.where(gate == best, blk_id, nblk), axis=0, keepdims=True)
            hit = blk_id == first
            sel = jnp.where(hit, 1.0, sel)
            gate = jnp.where(hit, neg_inf, gate)
        sel_ref[h] = jnp.where(blk_id < i, sel, 0.0)

    causal = (lax.broadcasted_iota(jnp.int32, (blk, tq), 0)
              <= lax.broadcasted_iota(jnp.int32, (blk, tq), 1))

    head_row = lax.broadcasted_iota(jnp.int32, (heads, tq), 0)

    def score_matmuls(block):
        start = pl.multiple_of(block * blk, blk)
        kb = k_ref[pl.ds(start, blk), :]
        return [_dot_nt(kb[:, (h // per_group) * LANE:(h // per_group + 1) * LANE], qms[h])
                for h in range(heads)]

    def attend(scores, vtb, keeps, first):
        if not first:
            m_old = m_ref[...]
            acc_old = [acc_ref[h] for h in range(heads)]
        m_all = jnp.zeros((heads, tq), F32)
        probs, alphas = [], []
        for h in range(heads):
            s = jnp.where(keeps[h], scores[h], neg_inf)
            s_max = jnp.max(s, axis=0, keepdims=True)
            m_new = s_max if first else jnp.maximum(m_old[h:h + 1, :], s_max)
            if not first:
                alphas.append(jnp.exp(m_old[h:h + 1, :] - m_new))
            probs.append(jnp.exp(s - m_new).astype(BF16))
            m_all = jnp.where(head_row == h, m_new, m_all)
        pvs = [_dot(vtb[h * ext:(h + 1) * ext, :], probs[h]) for h in range(heads)]
        m_ref[...] = m_all
        for h in range(heads):
            acc_ref[h] = pvs[h] if first else acc_old[h] * alphas[h] + pvs[h]

    start = pl.multiple_of(i * blk, blk)
    attend(score_matmuls(i), vt_ref[:, pl.ds(start, blk)], [causal] * heads, True)

    for h, s in enumerate(score_matmuls(0)):
        s_scr[h] = s

    def body(n, _):
        nxt = score_matmuls(jnp.minimum(n + 1, jnp.maximum(i - 1, 0)))
        start = pl.multiple_of(n * blk, blk)
        keeps = [sel_ref[h, pl.ds(n, 1), :] > 0.0 for h in range(heads)]
        attend([s_scr[h] for h in range(heads)], vt_ref[:, pl.ds(start, blk)], keeps, False)
        for h in range(heads):
            s_scr[h] = nxt[h]
        return 0

    lax.fori_loop(0, i, body, 0)
    outs = [acc_ref[h, 0:dh, :] / acc_ref[h, dh:dh + 1, :] for h in range(heads)]
    o_ref[...] = jnp.concatenate(outs, axis=0).T.astype(BF16)


def _moba_prompt(q, kb, vt, kmean, batch, seq, dh):
    n, bw = q.shape
    tq = MOBA_BLOCK
    nblk = seq // MOBA_BLOCK
    heads = bw // dh
    km = kmean.reshape(batch, nblk, bw)
    tiles = seq // tq
    return pl.pallas_call(
        functools.partial(_moba_prompt_kernel, dh=dh),
        grid=(batch, tiles),
        in_specs=[pl.BlockSpec((tq, bw), lambda b, i: (b * tiles + i, 0)),
                  pl.BlockSpec((seq, bw), lambda b, i: (b, 0)),
                  pl.BlockSpec((None, vt.shape[1], seq), lambda b, i: (b, 0, 0)),
                  pl.BlockSpec((1, nblk, bw), lambda b, i: (b, 0, 0))],
        out_specs=pl.BlockSpec((tq, bw), lambda b, i: (b * tiles + i, 0)),
        out_shape=jax.ShapeDtypeStruct((n, bw), BF16),
        scratch_shapes=[pltpu.VMEM((heads, nblk, tq), F32), pltpu.VMEM((heads, tq), F32),
                        pltpu.VMEM((heads, dh + ONES_ROWS, tq), F32), pltpu.VMEM((heads, MOBA_BLOCK, tq), F32)],
        compiler_params=_params("arbitrary", "arbitrary"),
        name="moba_prompt",
    )(q, kb, vt, km)


def _residual_proj_kernel(*refs, n_in, norm):
    x_ref = refs[0]
    in_refs = refs[1:1 + n_in]
    w_ref = refs[1 + n_in]
    y = x_ref[...]
    lo = 0
    for r in in_refs:
        y = y + _dot(r[...], w_ref[lo:lo + r.shape[1], :])
        lo += r.shape[1]
    if norm == "final":
        g_ref, o_ref = refs[2 + n_in:]
        o_ref[...] = _rmsnorm(y, g_ref[...])
    else:
        g_ref, o_ref, h_ref = refs[2 + n_in:]
        o_ref[...] = y
        h_ref[...] = _rmsnorm(y, g_ref[...]).astype(BF16)


def _residual_proj(x, ins, w, g, norm, tm, name):
    n, d = x.shape
    tm = min(tm, n)
    row = lambda width: pl.BlockSpec((tm, width), lambda i: (i, 0))
    out_specs = [row(d)] if norm == "final" else [row(d), row(d)]
    out_shape = ([jax.ShapeDtypeStruct((n, d), F32)] if norm == "final"
                 else [jax.ShapeDtypeStruct((n, d), F32), jax.ShapeDtypeStruct((n, d), BF16)])
    return pl.pallas_call(
        functools.partial(_residual_proj_kernel, n_in=len(ins), norm=norm),
        grid=(n // tm,),
        in_specs=[row(d)] + [row(a.shape[1]) for a in ins] + [_full(w.shape), _full((1, d))],
        out_specs=out_specs,
        out_shape=out_shape,
        compiler_params=_params("arbitrary"),
        name=name,
    )(x, *ins, w, g.reshape(1, d))


FFN_COLS = 256
FFN_HALO = SUBLANE_BF16


def _silu_gate(gate, value):
    return gate * (1.0 / (1.0 + jnp.exp(-gate))) * value


def _ffn_up_prompt_kernel(h_ref, halo_ref, w_ref, wc_ref, bc_ref, act_ref, tail_ref, up_scr, *, tiles_per_seq):
    tm = h_ref.shape[0]
    dff = act_ref.shape[1]
    seq_start = pl.program_id(0) % tiles_per_seq == 0
    halo = halo_ref[...]
    halo = jnp.where(seq_start, jnp.zeros_like(halo), halo)
    hh = jnp.concatenate([halo, h_ref[...]], axis=0)
    for c in range(dff // FFN_COLS):
        halves = []
        for k, lo in enumerate((c * FFN_COLS, dff + c * FFN_COLS)):
            cols = slice(lo, lo + FFN_COLS)
            up = _dot(hh, w_ref[:, cols])
            tail_ref[0, :, cols] = up[tm + FFN_HALO - 8:, :]
            up_scr[k] = up
            conv = bc_ref[:, cols] + up[FFN_HALO:, :] * wc_ref[CONV_W - 1:CONV_W, cols]
            for j in range(1, CONV_W):
                conv = conv + up_scr[k, pl.ds(FFN_HALO - j, tm), :] * wc_ref[CONV_W - 1 - j:CONV_W - j, cols]
            halves.append(conv)
        act_ref[:, c * FFN_COLS:(c + 1) * FFN_COLS] = _silu_gate(halves[1], halves[0]).astype(BF16)


def _ffn_up_decode_kernel(h_ref, s0_ref, s1_ref, w_ref, wc_ref, bc_ref, act_ref, up_ref):
    dff = act_ref.shape[1]
    h = h_ref[...]
    for c in range(dff // FFN_COLS):
        halves = []
        for lo in (c * FFN_COLS, dff + c * FFN_COLS):
            cols = slice(lo, lo + FFN_COLS)
            up = _dot(h, w_ref[:, cols])
            up_ref[:, cols] = up
            halves.append(bc_ref[:, cols] + s0_ref[:, cols] * wc_ref[0:1, cols]
                          + s1_ref[:, cols] * wc_ref[1:2, cols] + up * wc_ref[2:3, cols])
        act_ref[:, c * FFN_COLS:(c + 1) * FFN_COLS] = _silu_gate(halves[1], halves[0]).astype(BF16)


def _ffn_up_prompt(h, seq, w_up, w_conv, b_conv, tm):
    n, d = h.shape
    dff = w_up.shape[1] // 2
    assert dff % FFN_COLS == 0 and seq % tm == 0 and tm % FFN_HALO == 0
    ratio = tm // FFN_HALO
    act, tail = pl.pallas_call(
        functools.partial(_ffn_up_prompt_kernel, tiles_per_seq=seq // tm),
        grid=(n // tm,),
        in_specs=[pl.BlockSpec((tm, d), lambda i: (i, 0)),
                  pl.BlockSpec((FFN_HALO, d), lambda i: (jnp.maximum(i * ratio - 1, 0), 0)),
                  _full(w_up.shape), _full(w_conv.shape), _full((1, 2 * dff))],
        out_specs=[pl.BlockSpec((tm, dff), lambda i: (i, 0)),
                   pl.BlockSpec((1, 8, 2 * dff), lambda i: (i, 0, 0))],
        out_shape=[jax.ShapeDtypeStruct((n, dff), BF16),
                   jax.ShapeDtypeStruct((n // tm, 8, 2 * dff), F32)],
        scratch_shapes=[pltpu.VMEM((2, tm + FFN_HALO, FFN_COLS), F32)],
        compiler_params=_params("arbitrary"),
        name="ffn_up_prompt",
    )(h, h, w_up, w_conv, b_conv.reshape(1, -1))
    last = tail.reshape(n // seq, seq // tm, 8, 2 * dff)[:, -1, 8 - (CONV_W - 1):, :]
    return act, last


def _ffn_up_decode(h, state, w_up, w_conv, b_conv):
    n, d = h.shape
    dff = w_up.shape[1] // 2
    act, up = pl.pallas_call(
        _ffn_up_decode_kernel,
        grid=(1,),
        in_specs=[_full((n, d)), _full((n, 2 * dff)), _full((n, 2 * dff)),
                  _full(w_up.shape), _full(w_conv.shape), _full((1, 2 * dff))],
        out_specs=[_full((n, dff)), _full((n, 2 * dff))],
        out_shape=[jax.ShapeDtypeStruct((n, dff), BF16), jax.ShapeDtypeStruct((n, 2 * dff), F32)],
        compiler_params=_params("arbitrary"),
        name="ffn_up_decode",
    )(h, state[:, 0], state[:, 1], w_up, w_conv, b_conv.reshape(1, -1))
    return act, jnp.stack([state[:, 1], up], axis=1)


def _odd_in_kernel(h_ref, w_ref, c64_ref, a64_ref, b64_ref, c128_ref, a128_ref, b128_ref,
                   q_ref, k_ref, v_ref, kb_ref, vt_ref, qi_ref, kif_ref, ki2_ref, wit_ref,
                   *, half64, half128, idx_dh):
    cq = q_ref.shape[1]
    ckv = kb_ref.shape[1]
    ciq = qi_ref.shape[1]
    h = h_ref[...]
    t64 = (c64_ref[...], a64_ref[...], b64_ref[...])
    t128 = (c128_ref[...], a128_ref[...], b128_ref[...])
    q_ref[...] = _rope_apply(_dot(h, w_ref[:, 0:cq]), *t128, half128).astype(BF16)
    k = _rope_apply(_dot(h, w_ref[:, cq:cq + ckv]), *t128, half128)
    kb_ref[...] = k.astype(BF16)
    v = _dot(h, w_ref[:, cq + ckv:cq + 2 * ckv])
    c_dh = ckv // C_KV_HEADS
    for kv in range(C_KV_HEADS):
        k_ref[pl.ds(kv, h.shape[0], stride=C_KV_HEADS), :] = k[:, kv * c_dh:(kv + 1) * c_dh]
        v_ref[pl.ds(kv, h.shape[0], stride=C_KV_HEADS), :] = v[:, kv * c_dh:(kv + 1) * c_dh]
    if vt_ref is not None:
        vt_ref[...] = _with_ones_rows(v.T, ckv // C_KV_HEADS).astype(BF16)
    lo = cq + 2 * ckv
    qi_ref[...] = _rope_apply(_dot(h, w_ref[:, lo:lo + ciq]), *t64, half64).astype(BF16)
    tail = _dot(h, w_ref[:, lo + ciq:lo + ciq + LANE])
    ki = _rope_apply(tail, *t64, half64)
    kif_ref[...] = ki
    lane = lax.broadcasted_iota(jnp.int32, ki.shape, 1)
    ki2_ref[...] = jnp.where(lane < idx_dh, ki, pltpu.roll(ki, idx_dh, 1)).astype(BF16)
    wit_ref[...] = tail.T[idx_dh:idx_dh + 8, :] * (IDX_HEADS ** -0.5)


def _odd_in(h, seq, w_in, tabs64, half64, tabs128, half128, c_dh, idx_dh, tm, with_vt):
    n, d = h.shape
    tm = min(tm, n)
    cq = C_HEADS * c_dh
    ckv = C_KV_HEADS * c_dh
    ciq = IDX_HEADS * idx_dh
    cols = cq + 2 * ckv + ciq + LANE
    assert 2 * idx_dh == LANE and w_in.shape[1] <= cols
    w = jnp.pad(w_in, ((0, 0), (0, cols - w_in.shape[1]))).astype(BF16)
    if seq is None:
        tab = pl.BlockSpec((1, LANE), lambda i: (0, 0))
    else:
        tiles_per_seq = seq // tm
        tab = pl.BlockSpec((tm, LANE), lambda i: (i % tiles_per_seq, 0))
    row = lambda width, dt: (pl.BlockSpec((tm, width), lambda i: (i, 0)), jax.ShapeDtypeStruct((n, width), dt))
    col = lambda height, dt: (pl.BlockSpec((height, tm), lambda i: (0, i)), jax.ShapeDtypeStruct((height, n), dt))
    pair_rows = (pl.BlockSpec((tm * C_KV_HEADS, c_dh), lambda i: (i, 0)),
                 jax.ShapeDtypeStruct((n * C_KV_HEADS, c_dh), F32))
    outs = [row(cq, BF16), pair_rows, pair_rows, row(ckv, BF16)]
    if with_vt:
        outs.append(col(ckv + C_KV_HEADS * ONES_ROWS, BF16))
    outs += [row(ciq, BF16), row(LANE, F32), row(LANE, BF16), col(8, F32)]

    def body(*refs):
        refs = list(refs)
        if not with_vt:
            refs.insert(8 + 4, None)
        _odd_in_kernel(*refs, half64=half64, half128=half128, idx_dh=idx_dh)

    res = pl.pallas_call(
        body,
        grid=(n // tm,),
        in_specs=[pl.BlockSpec((tm, d), lambda i: (i, 0)), _full(w.shape)] + [tab] * 6,
        out_specs=[o[0] for o in outs],
        out_shape=[o[1] for o in outs],
        compiler_params=_params("arbitrary"),
        name="odd_in_prompt" if with_vt else "odd_in_decode",
    )(h, w, *tabs64, *tabs128)
    res = list(res)
    if not with_vt:
        res.insert(4, None)
    return res


FLOAT_BITS = 32
INDEX_BIG = 2 ** 30


def _ind(cond):
    return jnp.where(cond, 1.0, 0.0)


def _ordered_bits(x, to):
    b = x if x.dtype == jnp.int32 else lax.bitcast_convert_type(x, jnp.int32)
    b = jnp.where(b < 0, b ^ 0x7FFFFFFF, b)
    return b if to == jnp.int32 else lax.bitcast_convert_type(b, F32)


def _bisect_threshold(count_fn, lo, hi, keep):
    keep_f = jnp.float32(keep)

    def step(_, st):
        lo_k, hi_k, n_hi = st
        mid_k = (lax.shift_right_arithmetic(lo_k, 1) + lax.shift_right_arithmetic(hi_k, 1)
                 + (lo_k & hi_k & 1))
        cnt, = count_fn(lambda s, idx: (_ind(s >= _ordered_bits(mid_k, F32)),))
        raise_lo = jnp.logical_and(mid_k > lo_k, cnt >= keep_f)
        lower_hi = jnp.logical_and(mid_k > lo_k, cnt < keep_f)
        return (jnp.where(raise_lo, mid_k, lo_k), jnp.where(lower_hi, mid_k, hi_k),
                jnp.where(lower_hi, cnt, n_hi))

    lo_k = _ordered_bits(lo, jnp.int32)
    hi_k = _ordered_bits(hi, jnp.int32) + 1
    lo_k, _, n_gt = lax.fori_loop(0, FLOAT_BITS, step, (lo_k, hi_k, jnp.zeros(lo.shape, F32)))
    return _ordered_bits(lo_k, F32), n_gt


def _select_threshold(count_fn, lo, hi, active, keep, index_bits):
    thr, n_gt = _bisect_threshold(count_fn, lo, hi, keep)
    n_eq, = count_fn(lambda s, idx: (_ind(s == thr),))
    need = jnp.float32(keep) - n_gt
    excess = jnp.where(active, jnp.where(n_eq > need, 1, 0), 0)

    def tie_search():
        lo_i = jnp.full(thr.shape, -1, jnp.int32)
        hi_i = jnp.full(thr.shape, (1 << index_bits) - 1, jnp.int32)
        for _ in range(index_bits + 1):
            mid_i = lax.shift_right_arithmetic(lo_i + hi_i, 1)
            n_low, = count_fn(lambda s, idx: (jnp.where(s == thr, _ind(idx <= mid_i), 0.0),))
            ok = n_low >= need
            hi_i = jnp.where(ok, mid_i, hi_i)
            lo_i = jnp.where(ok, lo_i, mid_i)
        return hi_i

    cut = lax.cond(jnp.max(excess) > 0, tie_search, lambda: jnp.full(thr.shape, INDEX_BIG, jnp.int32))
    thr = jnp.where(active, thr, -jnp.inf)
    cut = jnp.where(active, cut, -1)
    return thr, cut


def _selected(s, idx, thr, cut):
    return jnp.where(s > thr, 1.0, jnp.where(s == thr, _ind(idx <= cut), 0.0))


def _dsa_prompt_kernel(q_ref, qi_ref, wit_ref, kb_ref, vt_ref, ki2_ref, o_ref,
                       sc_ref, m_ref, acc_ref, *, c_dh, idx_dh, keep, kc):
    tq = q_ref.shape[0]
    seq = kb_ref.shape[0]
    t0 = pl.program_id(1) * tq
    nch = (t0 + tq + kc - 1) // kc
    scale = c_dh ** -0.5
    neg_inf = jnp.float32(-jnp.inf)
    lane = lax.broadcasted_iota(jnp.int32, (tq, LANE), 1)
    key_iota = lax.broadcasted_iota(jnp.int32, (kc, tq), 0)
    qpos = t0 + lax.broadcasted_iota(jnp.int32, (1, tq), 1)

    qi = qi_ref[...]
    per_group = LANE // idx_dh
    qms = []
    for h in range(IDX_HEADS):
        grp = qi[:, (h // per_group) * LANE:(h // per_group + 1) * LANE]
        qms.append(jnp.where(_head_of(lane, idx_dh) == h % per_group, grp, jnp.zeros_like(grp)))
    wit = wit_ref[...]

    def score_chunk(c, carry):
        lo, hi = carry
        start = pl.multiple_of(c * kc, kc)
        kic = ki2_ref[pl.ds(start, kc), :]
        acc = jnp.zeros((kc, tq), F32)
        for h in range(IDX_HEADS):
            acc = acc + jnp.maximum(_dot_nt(kic, qms[h]), 0.0) * wit[h:h + 1, :]
        score = acc * (idx_dh ** -0.5)
        adm = start + key_iota <= qpos
        sc_ref[pl.ds(start, kc), :] = jnp.where(adm, score, neg_inf)
        lo = jnp.minimum(lo, jnp.min(jnp.where(adm, score, jnp.inf), axis=0, keepdims=True))
        hi = jnp.maximum(hi, jnp.max(jnp.where(adm, score, neg_inf), axis=0, keepdims=True))
        return lo, hi

    lo, hi = lax.fori_loop(0, nch, score_chunk,
                           (jnp.full((1, tq), jnp.inf, F32), jnp.full((1, tq), neg_inf, F32)))

    def count_fn(pred, n_out=1):
        def body(c, cnts):
            start = pl.multiple_of(c * kc, kc)
            hits = pred(sc_ref[pl.ds(start, kc), :], start + key_iota)
            return tuple(cnt + jnp.sum(hit.reshape(kc // 8, 8, tq), axis=0) for cnt, hit in zip(cnts, hits))
        parts = lax.fori_loop(0, nch, body, tuple(jnp.zeros((8, tq), F32) for _ in range(n_out)))
        return tuple(jnp.sum(part, axis=0, keepdims=True) for part in parts)

    thr, n_gt = _bisect_threshold(count_fn, lo, hi, keep)
    active = qpos + 1 > keep
    thr = jnp.where(active, thr, neg_inf)
    need = jnp.where(active, jnp.float32(keep) - n_gt, 0.0)
    prefix_ones = jnp.where(lax.broadcasted_iota(jnp.int32, (kc, kc), 0)
                            >= lax.broadcasted_iota(jnp.int32, (kc, kc), 1), 1.0, 0.0).astype(BF16)

    m_ref[...] = jnp.full(m_ref.shape, NEG_BIG, F32)
    acc_ref[...] = jnp.zeros(acc_ref.shape, F32)
    group = C_HEADS // C_KV_HEADS
    ext = c_dh + ONES_ROWS
    exp2_scale = scale * LOG2_E
    head_row = lax.broadcasted_iota(jnp.int32, m_ref.shape, 0)

    def attend_chunk(c, ties_before):
        start = pl.multiple_of(c * kc, kc)
        sc = sc_ref[pl.ds(start, kc), :]
        tie = _ind(sc == thr)
        tie_rank = ties_before + _dot(prefix_ones, tie.astype(BF16))
        keepm = jnp.where(sc > thr, 1.0, jnp.where(tie_rank <= need, tie, 0.0)) > 0.0
        ties_before = tie_rank[kc - 1:kc, :]
        kbc = kb_ref[pl.ds(start, kc), :]
        vtc = vt_ref[:, pl.ds(start, kc)]
        m_old = m_ref[...]
        m_all = jnp.zeros(m_old.shape, F32)
        scores = [_dot_nt(kbc[:, (h // group) * c_dh:(h // group + 1) * c_dh], q_ref[:, h * c_dh:(h + 1) * c_dh])
                  for h in range(C_HEADS)]
        probs, alphas = [], []
        for h in range(C_HEADS):
            s = jnp.where(keepm, scores[h], neg_inf)
            m_new = jnp.maximum(m_old[h:h + 1, :], jnp.max(s, axis=0, keepdims=True))
            alphas.append(jnp.exp2((m_old[h:h + 1, :] - m_new) * exp2_scale))
            probs.append(jnp.exp2((s - m_new) * exp2_scale).astype(BF16))
            m_all = jnp.where(head_row == h, m_new, m_all)
        pvs = [_dot(vtc[(h // group) * ext:(h // group + 1) * ext, :], probs[h]) for h in range(C_HEADS)]
        m_ref[...] = m_all
        for h in range(C_HEADS):
            acc_ref[h] = acc_ref[h] * alphas[h] + pvs[h]
        return ties_before

    lax.fori_loop(0, nch, attend_chunk, jnp.zeros((1, tq), F32))
    for h in range(C_HEADS):
        out_t = acc_ref[h, 0:c_dh, :] / acc_ref[h, c_dh:c_dh + 1, :]
        o_ref[:, h * c_dh:(h + 1) * c_dh] = out_t.T.astype(BF16)


def _dsa_prompt(q, qi, wit, kb, vt, ki2, batch, seq, c_dh, idx_dh, keep, tq, kc):
    n = q.shape[0]
    tiles = seq // tq
    assert c_dh == LANE and seq % kc == 0 and seq % tq == 0
    return pl.pallas_call(
        functools.partial(_dsa_prompt_kernel, c_dh=c_dh, idx_dh=idx_dh, keep=keep, kc=kc),
        grid=(batch, tiles),
        in_specs=[pl.BlockSpec((tq, q.shape[1]), lambda b, i: (b * tiles + i, 0)),
                  pl.BlockSpec((tq, qi.shape[1]), lambda b, i: (b * tiles + i, 0)),
                  pl.BlockSpec((8, tq), lambda b, i: (0, b * tiles + i)),
                  pl.BlockSpec((seq, kb.shape[1]), lambda b, i: (b, 0)),
                  pl.BlockSpec((vt.shape[0], seq), lambda b, i: (0, b)),
                  pl.BlockSpec((seq, LANE), lambda b, i: (b, 0))],
        out_specs=pl.BlockSpec((tq, q.shape[1]), lambda b, i: (b * tiles + i, 0)),
        out_shape=jax.ShapeDtypeStruct(q.shape, BF16),
        scratch_shapes=[pltpu.VMEM((seq, tq), F32), pltpu.VMEM((8, tq), F32),
                        pltpu.VMEM((C_HEADS, c_dh + ONES_ROWS, tq), F32)],
        compiler_params=_params("arbitrary", "arbitrary"),
        name="dsa_prompt",
    )(q, qi, wit, kb, vt, ki2)


PAGE_ROWS = 128
DECODE_SCORE_SEQS = 4
DECODE_ATTN_SEQS = 2


def _lane_bcast_col(row):
    return jnp.broadcast_to(row, (LANE, row.shape[1])).T


def _moba_decode_kernel(pt_ref, q_ref, kn_ref, vn_ref, *rest, n_pages, dh):
    kt_refs, vt_refs, o_ref = rest[:n_pages], rest[n_pages:2 * n_pages], rest[2 * n_pages]
    bw = q_ref.shape[2]
    heads = bw // dh
    ppb = MOBA_BLOCK // PAGE_ROWS
    nblk = n_pages // ppb
    neg_inf = jnp.float32(-jnp.inf)
    qb = _lane_bcast_col(q_ref[0].astype(F32))

    def head_sums(x):
        return jnp.sum(x.reshape(heads, dh, LANE), axis=1)

    def head_rows(x):
        return jnp.broadcast_to(x[:, None, :], (heads, dh, LANE)).reshape(bw, LANE)

    s_pages = [head_sums(kt_refs[j][...] * qb) for j in range(n_pages)]
    s_own = head_sums(_lane_bcast_col(kn_ref[0]) * qb)
    gates = []
    for n in range(nblk):
        g = jnp.zeros((heads, 1), F32)
        for j in range(n * ppb, (n + 1) * ppb):
            g = g + jnp.sum(s_pages[j], axis=1, keepdims=True)
        gates.append(g)
    sels = []
    for n in range(nblk):
        rank = jnp.zeros((heads, 1), F32)
        for m in range(nblk):
            if m != n:
                beats = gates[m] >= gates[n] if m < n else gates[m] > gates[n]
                rank = rank + jnp.where(beats, 1.0, 0.0)
        sels.append(rank < MOBA_TOPK)
    m = s_own
    masked = []
    for j in range(n_pages):
        sm = jnp.where(sels[j // ppb], s_pages[j], neg_inf)
        masked.append(sm)
        m = jnp.maximum(m, jnp.max(sm, axis=1, keepdims=True))
    e_own = jnp.exp(s_own - m)
    es = [jnp.exp(sm - m) for sm in masked]
    l = e_own
    for e in es:
        l = l + jnp.sum(e, axis=1, keepdims=True)
    rinv = 1.0 / l
    lane = lax.broadcasted_iota(jnp.int32, (bw, LANE), 1)
    acc = jnp.where(lane == 0, head_rows(e_own * rinv) * _lane_bcast_col(vn_ref[0]), 0.0)
    for j in range(n_pages):
        acc = acc + head_rows(es[j] * rinv) * vt_refs[j][...]
    out_col = jnp.sum(acc, axis=1, keepdims=True)
    o_ref[0] = jnp.broadcast_to(out_col, (bw, LANE)).T[0:1].astype(BF16)


def _moba_decode(q, k_new, v_new, cache_k, cache_v, page_table, dh):
    n, bw = q.shape
    n_pages = page_table.shape[1]
    assert cache_k.shape[1] == PAGE_ROWS == LANE and (n_pages * PAGE_ROWS) % MOBA_BLOCK == 0
    ckt = jnp.transpose(cache_k, (0, 2, 3, 1)).reshape(cache_k.shape[0], bw, PAGE_ROWS)
    cvt = jnp.transpose(cache_v, (0, 2, 3, 1)).reshape(cache_v.shape[0], bw, PAGE_ROWS)
    seq_row = pl.BlockSpec((1, 1, bw), lambda b, pt: (b, 0, 0))
    pages = [pl.BlockSpec((None, bw, PAGE_ROWS), lambda b, pt, j=j: (pt[b, j], 0, 0)) for j in range(n_pages)]
    out = pl.pallas_call(
        functools.partial(_moba_decode_kernel, n_pages=n_pages, dh=dh),
        grid_spec=pltpu.PrefetchScalarGridSpec(
            num_scalar_prefetch=1, grid=(n,),
            in_specs=[seq_row, seq_row, seq_row] + pages + pages,
            out_specs=seq_row),
        out_shape=jax.ShapeDtypeStruct((n, 1, bw), BF16),
        compiler_params=_params("arbitrary"),
        name="moba_decode",
    )(page_table, q.reshape(n, 1, bw), k_new.reshape(n, 1, bw), v_new.reshape(n, 1, bw),
      *([ckt] * n_pages), *([cvt] * n_pages))
    return out.reshape(n, bw)


def _dsa_decode_score_kernel(pt_ref, qi_ref, wi_ref, kin_ref, *rest, n_pages, idx_dh, seqs):
    for g in range(seqs):
        _dsa_decode_score_seq(g, qi_ref, wi_ref, kin_ref, rest[g * n_pages:(g + 1) * n_pages], rest[-1],
                              n_pages, idx_dh)


def _dsa_decode_score_seq(g, qi_ref, wi_ref, kin_ref, kit_refs, o_ref, n_pages, idx_dh):
    qb = _lane_bcast_col(qi_ref[g].astype(F32))
    wi8 = wi_ref[g]
    scale = idx_dh ** -0.5

    def score(kit):
        total = jnp.zeros((1, LANE), F32)
        for h in range(IDX_HEADS):
            sc = jnp.sum(kit * qb[h * idx_dh:(h + 1) * idx_dh], axis=0, keepdims=True)
            total = total + jnp.maximum(sc, 0.0) * wi8[h:h + 1, :]
        return total * scale

    o_ref[g] = jnp.full(o_ref.shape[1:], -jnp.inf, F32)
    for j in range(n_pages):
        o_ref[g, j:j + 1, :] = score(kit_refs[j][...])
    own = score(_lane_bcast_col(kin_ref[g]))
    lane = lax.broadcasted_iota(jnp.int32, (1, PAGE_ROWS), 1)
    o_ref[g, n_pages:n_pages + 1, :] = jnp.where(lane == 0, own, -jnp.inf)


def _dsa_decode_select_kernel(s_ref, m_ref, *, keep, n_valid):
    s = s_ref[...]
    idx = lax.broadcasted_iota(jnp.int32, s.shape, 1)
    neg_inf = jnp.float32(-jnp.inf)
    lo = jnp.min(jnp.where(s > neg_inf, s, jnp.inf), axis=1, keepdims=True)
    hi = jnp.max(s, axis=1, keepdims=True)
    count_fn = lambda pred, n_out=1: tuple(jnp.sum(hit, axis=1, keepdims=True) for hit in pred(s, idx))
    active = jnp.full(lo.shape, n_valid > keep, jnp.bool_)
    thr, cut = _select_threshold(count_fn, lo, hi, active, keep, (s.shape[1] - 1).bit_length())
    m_ref[...] = _selected(s, idx, thr, cut)


def _dsa_decode_attn_kernel(pt_ref, q_ref, kn_ref, vn_ref, mask_ref, *rest, n_pages, c_dh, seqs):
    staged = [_dsa_decode_attn_scores(g, q_ref, kn_ref, mask_ref, rest[g * n_pages:(g + 1) * n_pages],
                                      n_pages, c_dh) for g in range(seqs)]
    for g in range(seqs):
        _dsa_decode_attn_values(g, staged[g], vn_ref, rest[(seqs + g) * n_pages:(seqs + g + 1) * n_pages],
                                rest[-1], n_pages, c_dh)


def _dsa_decode_attn_scores(g, q_ref, kn_ref, mask_ref, k_refs, n_pages, c_dh):
    rows = k_refs[0].shape[0]
    group = C_HEADS // C_KV_HEADS
    scale = c_dh ** -0.5
    neg_inf = jnp.float32(-jnp.inf)
    qm = jnp.concatenate([q_ref[g].astype(F32), jnp.zeros((LANE - C_HEADS, c_dh), F32)], axis=0).astype(BF16)
    row_id = lax.broadcasted_iota(jnp.int32, (rows, LANE), 0)
    lane_id = lax.broadcasted_iota(jnp.int32, (rows, LANE), 1)
    own_kv = (row_id & (C_KV_HEADS - 1)) == _head_of(lane_id, group)
    mask = mask_ref[g]
    mask_t = jnp.concatenate([mask, jnp.zeros((LANE - mask.shape[0], PAGE_ROWS), F32)], axis=0).T
    repeat = jnp.where(_head_of(row_id, C_KV_HEADS) == lane_id, 1.0, 0.0).astype(BF16)
    row_mask = _dot(repeat, mask_t.astype(BF16))
    kn = jnp.concatenate([kn_ref[g], jnp.zeros((8 - C_KV_HEADS, c_dh), F32)], axis=0)
    s_own_kv = _dot_nt(kn.astype(BF16), qm)
    lane_kv = _head_of(lax.broadcasted_iota(jnp.int32, (1, LANE), 1), group)
    s_own = jnp.zeros((1, LANE), F32)
    for kv in range(C_KV_HEADS):
        s_own = jnp.where(lane_kv == kv, s_own_kv[kv:kv + 1, :], s_own)
    s_own = jnp.where(mask[n_pages:n_pages + 1, 0:1] > 0.0, s_own, neg_inf)
    m = jnp.maximum(s_own, NEG_BIG)
    masked = []
    for j in range(n_pages):
        s = _dot_nt(k_refs[j][...].astype(BF16), qm)
        sm = jnp.where(own_kv, jnp.where(row_mask[:, j:j + 1] > 0.5, s, neg_inf), neg_inf)
        masked.append(sm)
        m = jnp.maximum(m, jnp.max(sm, axis=0, keepdims=True))
    return masked, s_own, m, lane_kv


def _dsa_decode_attn_values(g, staged, vn_ref, v_refs, o_ref, n_pages, c_dh):
    masked, s_own, m, lane_kv = staged
    scale = c_dh ** -0.5
    own_rows = lax.broadcasted_iota(jnp.int32, (SUBLANE_BF16, LANE), 0)
    e_own = jnp.where(own_rows == lane_kv, jnp.exp((s_own - m) * scale), 0.0)
    v_own = jnp.concatenate([vn_ref[g], jnp.zeros((SUBLANE_BF16 - C_KV_HEADS, c_dh), F32)], axis=0)
    e_all = jnp.concatenate([jnp.exp((sm - m) * scale) for sm in masked] + [e_own], axis=0).astype(BF16)
    v_all = jnp.concatenate([v_refs[j][...] for j in range(n_pages)] + [v_own], axis=0).astype(BF16)
    v_all = jnp.concatenate([v_all, jnp.ones(v_all.shape, BF16)], axis=1)
    out = lax.dot_general(e_all, v_all, (((0,), (0,)), ((), ())), preferred_element_type=F32)
    o_ref[g] = (out[0:C_HEADS, 0:c_dh] / out[0:C_HEADS, c_dh:2 * c_dh]).astype(BF16)


def _dsa_decode(q, qi, wit, k_new, v_new, ki_new, cache_k, cache_v, cache_ki, page_table, c_dh, idx_dh, keep):
    n = q.shape[0]
    n_pages = page_table.shape[1]
    assert cache_k.shape[1] == PAGE_ROWS == LANE
    pool = cache_k.shape[0]
    ck = cache_k.reshape(pool, PAGE_ROWS * C_KV_HEADS, c_dh)
    cv = cache_v.reshape(pool, PAGE_ROWS * C_KV_HEADS, c_dh)
    ckit = jnp.transpose(cache_ki, (0, 2, 1))
    rows = -(-(n_pages + 1) // 8) * 8
    wi8 = wit.T.reshape(n, 8, 1)

    def seq3(g, a, b):
        return pl.BlockSpec((g, a, b), lambda s, pt: (s, 0, 0))

    def pages(g, r, w):
        return [pl.BlockSpec((None, r, w), lambda s, pt, j=j, k=k: (pt[s * g + k, j], 0, 0))
                for k in range(g) for j in range(n_pages)]

    gs = DECODE_SCORE_SEQS
    assert n % gs == 0 and n % DECODE_ATTN_SEQS == 0
    scores = pl.pallas_call(
        functools.partial(_dsa_decode_score_kernel, n_pages=n_pages, idx_dh=idx_dh, seqs=gs),
        grid_spec=pltpu.PrefetchScalarGridSpec(
            num_scalar_prefetch=1, grid=(n // gs,),
            in_specs=[seq3(gs, 1, IDX_HEADS * idx_dh), seq3(gs, 8, 1), seq3(gs, 1, idx_dh)]
                     + pages(gs, idx_dh, PAGE_ROWS),
            out_specs=seq3(gs, rows, PAGE_ROWS)),
        out_shape=jax.ShapeDtypeStruct((n, rows, PAGE_ROWS), F32),
        compiler_params=_params("arbitrary"),
        name="dsa_decode_scores",
    )(page_table, qi.reshape(n, 1, IDX_HEADS * idx_dh), wi8, ki_new.reshape(n, 1, idx_dh),
      *([ckit] * (gs * n_pages)))
    flat = rows * PAGE_ROWS
    mask = pl.pallas_call(
        functools.partial(_dsa_decode_select_kernel, keep=keep, n_valid=n_pages * PAGE_ROWS + 1),
        grid=(1,),
        in_specs=[_full((n, flat))],
        out_specs=_full((n, flat)),
        out_shape=jax.ShapeDtypeStruct((n, flat), F32),
        compiler_params=_params("arbitrary"),
        name="dsa_decode_select",
    )(scores.reshape(n, flat))
    cq = C_HEADS * c_dh
    ga = DECODE_ATTN_SEQS
    kv_rows = PAGE_ROWS * C_KV_HEADS
    out = pl.pallas_call(
        functools.partial(_dsa_decode_attn_kernel, n_pages=n_pages, c_dh=c_dh, seqs=ga),
        grid_spec=pltpu.PrefetchScalarGridSpec(
            num_scalar_prefetch=1, grid=(n // ga,),
            in_specs=[seq3(ga, C_HEADS, c_dh), seq3(ga, C_KV_HEADS, c_dh), seq3(ga, C_KV_HEADS, c_dh),
                      seq3(ga, rows, PAGE_ROWS)] + pages(ga, kv_rows, c_dh) + pages(ga, kv_rows, c_dh),
            out_specs=seq3(ga, C_HEADS, c_dh)),
        out_shape=jax.ShapeDtypeStruct((n, C_HEADS, c_dh), BF16),
        compiler_params=_params("arbitrary"),
        name="dsa_decode_attn",
    )(page_table, q.reshape(n, C_HEADS, c_dh), k_new.reshape(n, C_KV_HEADS, c_dh),
      v_new.reshape(n, C_KV_HEADS, c_dh), mask.reshape(n, rows, PAGE_ROWS),
      *([ck] * (ga * n_pages)), *([cv] * (ga * n_pages)))
    return out.reshape(n, cq)


def kernel(x_prompt, x_sample, cache_moba_k, cache_moba_v, cache_dsa_k, cache_dsa_v, cache_dsa_kidx,
           state_ffn_conv, page_table, w_in_even, g_sgu, w_sgu, b_sgu, w_out_even, w_in_odd, w_out_odd,
           g_mix, g_ffn, w_up, w_conv, b_conv, w_down, g_final):
    batch, seq, d = x_prompt.shape
    n_dec, dec_seq, _ = x_sample.shape
    depth = g_mix.shape[0]
    n_pages = page_table.shape[1]
    past_len = n_pages * cache_moba_k.shape[2]
    assert depth == 2 and dec_seq == 1, "one even + one odd layer, one decode token per sequence"
    b_dh = cache_moba_k.shape[-1]
    c_dh = cache_dsa_k.shape[-1]
    idx_dh = cache_dsa_kidx.shape[-1]
    keep_p = min(DSA_TOPK, seq // 4)
    keep_s = min(DSA_TOPK, (past_len + dec_seq) // 4)
    xp = x_prompt.reshape(batch * seq, d)
    xs = x_sample.reshape(n_dec, d)
    bf = lambda w: w.astype(BF16)

    tp64, half64 = _rope_tables(b_dh, seq, 0)
    ts64, _ = _rope_tables(b_dh, 1, past_len)
    tp128, half128 = _rope_tables(c_dh, seq, 0)
    ts128, _ = _rope_tables(c_dh, 1, past_len)
    assert idx_dh == b_dh

    w_in0, w_out0 = bf(w_in_even[0]), bf(w_out_even[0])
    a_p, q_p, kb_p, mkt_p, mvt_p, vt_p, km_p = _even_in_prompt(
        xp, seq, g_mix[0], w_in0, tp64, half64, g_sgu[0], w_sgu[0], b_sgu[0])
    bo_p = _moba_prompt(q_p, kb_p, vt_p, km_p, batch, seq, b_dh)
    cache_view = lambda t: jnp.transpose(t.reshape(batch, B_HEADS, b_dh, seq), (0, 3, 1, 2))[None]
    a_s, q_s, mk_s, mv_s, va_s = _even_in_decode(xs, g_mix[0], w_in0, ts64, half64, g_sgu[0], w_sgu[0], b_sgu[0])
    bo_s = _moba_decode(q_s, mk_s, mv_s, cache_moba_k[0], cache_moba_v[0], page_table, b_dh)
    w_up0, w_down0 = bf(w_up[0]), bf(w_down[0])
    xp, hp = _residual_proj(xp, [a_p, bo_p], w_out0, g_ffn[0], "next", ROW_TILE, "out_even_prompt")
    xs, hs = _residual_proj(xs, [a_s, bo_s], w_out0, g_ffn[0], "next", ROW_TILE, "out_even_decode")
    act_p, conv_p0 = _ffn_up_prompt(hp, seq, w_up0, w_conv[0], b_conv[0], ROW_TILE)
    act_s, conv_s0 = _ffn_up_decode(hs, state_ffn_conv[0], w_up0, w_conv[0], b_conv[0])
    xp, hp = _residual_proj(xp, [act_p], w_down0, g_mix[1], "next", ROW_TILE, "ffn_down0_prompt")
    xs, hs = _residual_proj(xs, [act_s], w_down0, g_mix[1], "next", ROW_TILE, "ffn_down0_decode")

    w_out1 = bf(w_out_odd[0])
    q_p, dk_p, dv_p, kb_p, vt_p, qi_p, kif_p, ki2_p, wit_p = _odd_in(
        hp, seq, w_in_odd[0], tp64, half64, tp128, half128, c_dh, idx_dh, ROW_TILE, True)
    o_p = _dsa_prompt(q_p, qi_p, wit_p, kb_p, vt_p, ki2_p, batch, seq, c_dh, idx_dh, keep_p, 256, 256)
    q_s, dk_s, dv_s, _, _, qi_s, kif_s, _, wit_s = _odd_in(
        hs, None, w_in_odd[0], ts64, half64, ts128, half128, c_dh, idx_dh, 256, False)
    o_s = _dsa_decode(q_s, qi_s, wit_s, dk_s, dv_s, kif_s[:, :idx_dh], cache_dsa_k[0], cache_dsa_v[0],
                      cache_dsa_kidx[0], page_table, c_dh, idx_dh, keep_s)
    w_up1, w_down1 = bf(w_up[1]), bf(w_down[1])
    xp, hp = _residual_proj(xp, [o_p], w_out1, g_ffn[1], "next", ROW_TILE, "out_odd_prompt")
    xs, hs = _residual_proj(xs, [o_s], w_out1, g_ffn[1], "next", ROW_TILE, "out_odd_decode")
    act_p, conv_p1 = _ffn_up_prompt(hp, seq, w_up1, w_conv[1], b_conv[1], ROW_TILE)
    act_s, conv_s1 = _ffn_up_decode(hs, state_ffn_conv[1], w_up1, w_conv[1], b_conv[1])
    y_p, = _residual_proj(xp, [act_p], w_down1, g_final, "final", ROW_TILE, "ffn_down1_prompt")
    y_s, = _residual_proj(xs, [act_s], w_down1, g_final, "final", ROW_TILE, "ffn_down1_decode")

    ckv = C_KV_HEADS
    return (y_p.reshape(batch, seq, d), y_s.reshape(n_dec, dec_seq, d),
            cache_view(mkt_p), cache_view(mvt_p),
            mk_s.reshape(1, n_dec, dec_seq, B_HEADS, b_dh), mv_s.reshape(1, n_dec, dec_seq, B_HEADS, b_dh),
            va_s.reshape(1, n_dec, dec_seq, A_GROUPS, LANE),
            dk_p.reshape(1, batch, seq, ckv, c_dh), dv_p.reshape(1, batch, seq, ckv, c_dh),
            kif_p[:, :idx_dh].reshape(1, batch, seq, idx_dh),
            dk_s.reshape(1, n_dec, dec_seq, ckv, c_dh), dv_s.reshape(1, n_dec, dec_seq, ckv, c_dh),
            kif_s[:, :idx_dh].reshape(1, n_dec, dec_seq, idx_dh),
            jnp.stack([conv_p0, conv_p1]), jnp.stack([conv_s0, conv_s1]))
```

```python
import functools

import numpy as np
import jax
import jax.numpy as jnp
from jax import lax
from jax.experimental import pallas as pl
from jax.experimental.pallas import tpu as pltpu

F32 = jnp.float32
BF16 = jnp.bfloat16

A_GROUPS = 4
CHUNK = 128
B_HEADS = 8
MOBA_BLOCK = 256
MOBA_TOPK = 3
C_HEADS = 8
C_KV_HEADS = 2
IDX_HEADS = 4
DSA_TOPK = 256
ROPE_THETA = 500000.0
ROPE_FRAC = 4
CONV_W = 3
EPS = 1e-6

LANE = 128
SUBLANE = 8
SUBLANE_BF16 = 16
VMEM_LIMIT_BYTES = 56 * 1024 * 1024
NEG_BIG = -1e30
LOG2_E = 1.4426950408889634
ROW_TILE = 512
DSA_QUERY_TILE = 256
DSA_KEY_CHUNK = 256


def _params(*sem):
    return pltpu.CompilerParams(dimension_semantics=sem,
                                vmem_limit_bytes=VMEM_LIMIT_BYTES)


def _full(shape):
    nd = len(shape)
    return pl.BlockSpec(shape, lambda *_: (0,) * nd)


def _rmsnorm(x, g):
    ms = jnp.mean(x * x, axis=-1, keepdims=True)
    return x * lax.rsqrt(ms + EPS) * g


def _rope_apply(x, c, s1, s2, half):
    parts = []
    for j in range(x.shape[1] // LANE):
        xs = x[:, j * LANE:(j + 1) * LANE]
        parts.append(xs * c + pltpu.roll(xs, LANE - half, 1) * s1
                     + pltpu.roll(xs, half, 1) * s2)
    return parts[0] if len(parts) == 1 else jnp.concatenate(parts, axis=1)


def _head_of(index, dh):
    assert dh & (dh - 1) == 0
    return lax.shift_right_logical(index, dh.bit_length() - 1)


ONES_ROWS = SUBLANE_BF16


def _with_ones_rows(vt, dh):
    ones = jnp.ones((ONES_ROWS, vt.shape[1]), vt.dtype)
    parts = []
    for h in range(vt.shape[0] // dh):
        parts += [vt[h * dh:(h + 1) * dh], ones]
    return jnp.concatenate(parts, axis=0)


def _dot(a, b):
    return jnp.dot(a, b, preferred_element_type=F32)


def _dot_nt(a, b):
    return lax.dot_general(a, b, (((1,), (1,)), ((), ())),
                           preferred_element_type=F32)


def _rope_rows(dh):
    rd = dh // ROPE_FRAC
    half = rd // 2
    inv = ROPE_THETA ** (-jnp.arange(half, dtype=F32) * (2.0 / rd))
    r = np.arange(LANE) % dh
    j = np.where(r < half, r, np.where(r < rd, r - half, 0))
    inv_lane = jnp.where(jnp.asarray(r < rd), inv[j], 0.0).reshape(1, LANE)
    s1 = jnp.asarray(np.where(r < half, -1.0, 0.0), F32).reshape(1, LANE)
    s2 = jnp.asarray(np.where((r >= half) & (r < rd), 1.0, 0.0), F32).reshape(1, LANE)
    return inv_lane, s1, s2, half


def _rope_table_kernel(inv_ref, s1_ref, s2_ref, c_ref, a_ref, b_ref, *, base):
    rows = c_ref.shape[0]
    pos = lax.broadcasted_iota(jnp.int32, (rows, LANE), 0) + (base + pl.program_id(0) * rows)
    ang = pos.astype(F32) * inv_ref[...]
    s = jnp.sin(ang)
    c_ref[...] = jnp.cos(ang)
    a_ref[...] = s * s1_ref[...]
    b_ref[...] = s * s2_ref[...]


def _rope_tables(dh, n_pos, base):
    inv_lane, s1, s2, half = _rope_rows(dh)
    rows = min(n_pos, ROW_TILE)
    assert n_pos % rows == 0
    out = jax.ShapeDtypeStruct((n_pos, LANE), F32)
    spec = pl.BlockSpec((rows, LANE), lambda i: (i, 0))
    c, a, b = pl.pallas_call(
        functools.partial(_rope_table_kernel, base=base),
        grid=(n_pos // rows,),
        in_specs=[_full((1, LANE))] * 3,
        out_specs=[spec] * 3,
        out_shape=[out] * 3,
        compiler_params=_params("arbitrary"),
        name="rope_tables",
    )(inv_lane, s1, s2)
    return (c, a, b), half


def _sgu_values(h, w_ref, gs_ref, aw):
    u = jax.nn.gelu(_dot(h, w_ref[:, 0:aw]))
    zv = jax.nn.gelu(_dot(h, w_ref[:, aw:2 * aw]))
    vs = []
    for g in range(A_GROUPS):
        vg = zv[:, g * LANE:(g + 1) * LANE]
        vs.append(_rmsnorm(vg, gs_ref[:, g * LANE:(g + 1) * LANE]))
    return u, vs


def _even_in_prompt_kernel(x_ref, g_ref, w_ref, c_ref, s1_ref, s2_ref, gs_ref, ws_ref, bs_ref,
                           a_ref, q_ref, kb_ref, kt_ref, vt_ref, vtb_ref, km_ref, v_scr, *, half, scale):
    tm = x_ref.shape[0]
    aw = A_GROUPS * LANE
    bw = q_ref.shape[1]
    h = _rmsnorm(x_ref[...], g_ref[...]).astype(BF16)
    u, vs = _sgu_values(h, w_ref, gs_ref, aw)
    tri = (lax.broadcasted_iota(jnp.int32, (CHUNK, CHUNK), 0)
           >= lax.broadcasted_iota(jnp.int32, (CHUNK, CHUNK), 1))
    for g in range(A_GROUPS):
        wg = jnp.where(tri, ws_ref[g], 0.0).astype(BF16)
        for cc in range(tm // CHUNK):
            rows = slice(cc * CHUNK, (cc + 1) * CHUNK)
            mixed = _dot(wg, vs[g][rows].astype(BF16)) + bs_ref[g]
            a_ref[rows, g * LANE:(g + 1) * LANE] = (u[rows, g * LANE:(g + 1) * LANE] * mixed).astype(BF16)
    c, s1, s2 = c_ref[...], s1_ref[...], s2_ref[...]
    q = _rope_apply(_dot(h, w_ref[:, 2 * aw:2 * aw + bw]), c, s1, s2, half)
    q_ref[...] = (q * scale).astype(BF16)
    k = _rope_apply(_dot(h, w_ref[:, 2 * aw + bw:2 * aw + 2 * bw]), c, s1, s2, half)
    kt_ref[...] = k.T
    kb_ref[...] = k.astype(BF16)
    km_ref[0] = jnp.sum(k, axis=0, keepdims=True) * (1.0 / tm)
    v_scr[...] = _dot(h, w_ref[:, 2 * aw + 2 * bw:2 * aw + 3 * bw])
    vt = v_scr[...].T
    vt_ref[...] = vt
    vtb_ref[...] = _with_ones_rows(vt, bw // B_HEADS).astype(BF16)


def _even_in_decode_kernel(x_ref, g_ref, w_ref, c_ref, s1_ref, s2_ref, gs_ref, w00_ref, b0_ref,
                           a_ref, q_ref, k_ref, v_ref, va_ref, *, half, scale):
    aw = A_GROUPS * LANE
    bw = q_ref.shape[1]
    h = _rmsnorm(x_ref[...], g_ref[...]).astype(BF16)
    u, vs = _sgu_values(h, w_ref, gs_ref, aw)
    va = jnp.concatenate(vs, axis=1)
    va_ref[...] = va
    a_ref[...] = (u * (w00_ref[...] * va + b0_ref[...])).astype(BF16)
    c, s1, s2 = c_ref[...], s1_ref[...], s2_ref[...]
    q = _rope_apply(_dot(h, w_ref[:, 2 * aw:2 * aw + bw]), c, s1, s2, half)
    q_ref[...] = (q * scale).astype(BF16)
    k_ref[...] = _rope_apply(_dot(h, w_ref[:, 2 * aw + bw:2 * aw + 2 * bw]), c, s1, s2, half)
    v_ref[...] = _dot(h, w_ref[:, 2 * aw + 2 * bw:2 * aw + 3 * bw])


def _even_in_prompt(x, seq, g_mix, w_in, tables, half, g_sgu, w_sgu, b_sgu):
    n, d = x.shape
    tm = MOBA_BLOCK
    bw = (w_in.shape[1] - 2 * A_GROUPS * LANE) // 3
    dh = bw // B_HEADS
    tiles_per_seq = seq // tm
    row = lambda w, dt: (pl.BlockSpec((tm, w), lambda i: (i, 0)), jax.ShapeDtypeStruct((n, w), dt))
    tab = pl.BlockSpec((tm, LANE), lambda i: (i % tiles_per_seq, 0))
    col = lambda h, dt: (pl.BlockSpec((None, h, tm), lambda i: (i // tiles_per_seq, 0, i % tiles_per_seq)),
                         jax.ShapeDtypeStruct((n // seq, h, seq), dt))
    outs = [row(A_GROUPS * LANE, BF16), row(bw, BF16), row(bw, BF16), col(bw, F32), col(bw, F32),
            col(bw + B_HEADS * ONES_ROWS, BF16),
            (pl.BlockSpec((1, 1, bw), lambda i: (i, 0, 0)), jax.ShapeDtypeStruct((n // tm, 1, bw), F32))]
    bs = jnp.broadcast_to(b_sgu[:, :, None], (A_GROUPS, CHUNK, LANE))
    return pl.pallas_call(
        functools.partial(_even_in_prompt_kernel, half=half, scale=dh ** -0.5),
        grid=(n // tm,),
        in_specs=[pl.BlockSpec((tm, d), lambda i: (i, 0)), _full((1, d)), _full(w_in.shape),
                  tab, tab, tab, _full((1, A_GROUPS * LANE)), _full(w_sgu.shape), _full(bs.shape)],
        out_specs=[o[0] for o in outs],
        out_shape=[o[1] for o in outs],
        scratch_shapes=[pltpu.VMEM((tm, bw), F32)],
        compiler_params=_params("arbitrary"),
        name="even_in_prompt",
    )(x, g_mix.reshape(1, d), w_in, *tables, g_sgu.reshape(1, -1), w_sgu, bs)


def _even_in_decode(x, g_mix, w_in, tables, half, g_sgu, w_sgu, b_sgu):
    n, d = x.shape
    bw = (w_in.shape[1] - 2 * A_GROUPS * LANE) // 3
    dh = bw // B_HEADS
    aw = A_GROUPS * LANE
    w00 = jnp.repeat(w_sgu[:, 0, 0], LANE).reshape(1, aw)
    b0 = jnp.repeat(b_sgu[:, 0], LANE).reshape(1, aw)
    shapes = [(aw, BF16), (bw, BF16), (bw, F32), (bw, F32), (aw, F32)]
    return pl.pallas_call(
        functools.partial(_even_in_decode_kernel, half=half, scale=dh ** -0.5),
        grid=(1,),
        in_specs=[_full((n, d)), _full((1, d)), _full(w_in.shape)] + [_full((1, LANE))] * 3
                 + [_full((1, aw))] * 3,
        out_specs=[_full((n, w)) for w, _ in shapes],
        out_shape=[jax.ShapeDtypeStruct((n, w), dt) for w, dt in shapes],
        compiler_params=_params("arbitrary"),
        name="even_in_decode",
    )(x, g_mix.reshape(1, d), w_in, *tables, g_sgu.reshape(1, -1), w00, b0)


def _moba_prompt_kernel(q_ref, k_ref, vt_ref, km_ref, o_ref, sel_ref, m_ref, acc_ref, s_scr, *, dh):
    blk = MOBA_BLOCK
    tq = q_ref.shape[0]
    i = pl.program_id(1)
    nblk = km_ref.shape[1]
    heads = q_ref.shape[1] // dh
    ext = dh + ONES_ROWS
    per_group = LANE // dh
    lane = lax.broadcasted_iota(jnp.int32, (tq, LANE), 1)
    blk_id = lax.broadcasted_iota(jnp.int32, (nblk, tq), 0)
    neg_inf = jnp.float32(-jnp.inf)
    qms = []
    for h in range(heads):
        grp = h // per_group
        q = q_ref[:, grp * LANE:(grp + 1) * LANE]
        qm = jnp.where(_head_of(lane, dh) == h % per_group, q, jnp.zeros_like(q))
        qms.append(qm)
        kmean = km_ref[0, :, grp * LANE:(grp + 1) * LANE].astype(BF16)
        gate = jnp.where(blk_id < i, _dot_nt(kmean, qm), neg_inf)
        sel = jnp.zeros((nblk, tq), F32)
        for _ in range(MOBA_TOPK):
            best = jnp.max(gate, axis=0, keepdims=True)
            first = jnp.min(jnp.where(gate == best, blk_id, nblk), axis=0, keepdims=True)
            hit = blk_id == first
            sel = jnp.where(hit, 1.0, sel)
            gate = jnp.where(hit, neg_inf, gate)
        sel_ref[h] = jnp.where(blk_id < i, sel, 0.0)

    causal = (lax.broadcasted_iota(jnp.int32, (blk, tq), 0)
              <= lax.broadcasted_iota(jnp.int32, (blk, tq), 1))

    head_row = lax.broadcasted_iota(jnp.int32, (heads, tq), 0)

    def score_matmuls(block):
        start = pl.multiple_of(block * blk, blk)
        kb = k_ref[pl.ds(start, blk), :]
        return [_dot_nt(kb[:, (h // per_group) * LANE:(h // per_group + 1) * LANE], qms[h])
                for h in range(heads)]

    def attend(scores, vtb, keeps, first):
        if not first:
            m_old = m_ref[...]
            acc_old = [acc_ref[h] for h in range(heads)]
        m_all = jnp.zeros((heads, tq), F32)
        probs, alphas = [], []
        for h in range(heads):
            s = jnp.where(keeps[h], scores[h], neg_inf)
            s_max = jnp.max(s, axis=0, keepdims=True)
            m_new = s_max if first else jnp.maximum(m_old[h:h + 1, :], s_max)
            if not first:
                alphas.append(jnp.exp(m_old[h:h + 1, :] - m_new))
            probs.append(jnp.exp(s - m_new).astype(BF16))
            m_all = jnp.where(head_row == h, m_new, m_all)
        pvs = [_dot(vtb[h * ext:(h + 1) * ext, :], probs[h]) for h in range(heads)]
        m_ref[...] = m_all
        for h in range(heads):
            acc_ref[h] = pvs[h] if first else acc_old[h] * alphas[h] + pvs[h]

    start = pl.multiple_of(i * blk, blk)
    attend(score_matmuls(i), vt_ref[:, pl.ds(start, blk)], [causal] * heads, True)

    for h, s in enumerate(score_matmuls(0)):
        s_scr[h] = s

    def body(n, _):
        nxt = score_matmuls(jnp.minimum(n + 1, jnp.maximum(i - 1, 0)))
        start = pl.multiple_of(n * blk, blk)
        keeps = [sel_ref[h, pl.ds(n, 1), :] > 0.0 for h in range(heads)]
        attend([s_scr[h] for h in range(heads)], vt_ref[:, pl.ds(start, blk)], keeps, False)
        for h in range(heads):
            s_scr[h] = nxt[h]
        return 0

    lax.fori_loop(0, i, body, 0)
    outs = [acc_ref[h, 0:dh, :] / acc_ref[h, dh:dh + 1, :] for h in range(heads)]
    o_ref[...] = jnp.concatenate(outs, axis=0).T.astype(BF16)


def _moba_prompt(q, kb, vt, kmean, batch, seq, dh):
    n, bw = q.shape
    tq = MOBA_BLOCK
    nblk = seq // MOBA_BLOCK
    heads = bw // dh
    km = kmean.reshape(batch, nblk, bw)
    tiles = seq // tq
    return pl.pallas_call(
        functools.partial(_moba_prompt_kernel, dh=dh),
        grid=(batch, tiles),
        in_specs=[pl.BlockSpec((tq, bw), lambda b, i: (b * tiles + i, 0)),
                  pl.BlockSpec((seq, bw), lambda b, i: (b, 0)),
                  pl.BlockSpec((None, vt.shape[1], seq), lambda b, i: (b, 0, 0)),
                  pl.BlockSpec((1, nblk, bw), lambda b, i: (b, 0, 0))],
        out_specs=pl.BlockSpec((tq, bw), lambda b, i: (b * tiles + i, 0)),
        out_shape=jax.ShapeDtypeStruct((n, bw), BF16),
        scratch_shapes=[pltpu.VMEM((heads, nblk, tq), F32), pltpu.VMEM((heads, tq), F32),
                        pltpu.VMEM((heads, dh + ONES_ROWS, tq), F32), pltpu.VMEM((heads, MOBA_BLOCK, tq), F32)],
        compiler_params=_params("arbitrary", "arbitrary"),
        name="moba_prompt",
    )(q, kb, vt, km)


def _residual_proj_kernel(*refs, n_in, norm):
    x_ref = refs[0]
    in_refs = refs[1:1 + n_in]
    w_ref = refs[1 + n_in]
    y = x_ref[...]
    lo = 0
    for r in in_refs:
        y = y + _dot(r[...], w_ref[lo:lo + r.shape[1], :])
        lo += r.shape[1]
    if norm == "final":
        g_ref, o_ref = refs[2 + n_in:]
        o_ref[...] = _rmsnorm(y, g_ref[...])
    else:
        g_ref, o_ref, h_ref = refs[2 + n_in:]
        o_ref[...] = y
        h_ref[...] = _rmsnorm(y, g_ref[...]).astype(BF16)


def _residual_proj(x, ins, w, g, norm, tm, name):
    n, d = x.shape
    tm = min(tm, n)
    row = lambda width: pl.BlockSpec((tm, width), lambda i: (i, 0))
    out_specs = [row(d)] if norm == "final" else [row(d), row(d)]
    out_shape = ([jax.ShapeDtypeStruct((n, d), F32)] if norm == "final"
                 else [jax.ShapeDtypeStruct((n, d), F32), jax.ShapeDtypeStruct((n, d), BF16)])
    return pl.pallas_call(
        functools.partial(_residual_proj_kernel, n_in=len(ins), norm=norm),
        grid=(n // tm,),
        in_specs=[row(d)] + [row(a.shape[1]) for a in ins] + [_full(w.shape), _full((1, d))],
        out_specs=out_specs,
        out_shape=out_shape,
        compiler_params=_params("arbitrary"),
        name=name,
    )(x, *ins, w, g.reshape(1, d))


FFN_COLS = 256
FFN_HALO = SUBLANE_BF16


def _silu_gate(gate, value):
    return gate * (1.0 / (1.0 + jnp.exp(-gate))) * value


def _ffn_up_prompt_kernel(h_ref, halo_ref, w_ref, wc_ref, bc_ref, act_ref, tail_ref, up_scr, *, tiles_per_seq):
    tm = h_ref.shape[0]
    dff = act_ref.shape[1]
    seq_start = pl.program_id(0) % tiles_per_seq == 0
    halo = halo_ref[...]
    halo = jnp.where(seq_start, jnp.zeros_like(halo), halo)
    hh = jnp.concatenate([halo, h_ref[...]], axis=0)
    for c in range(dff // FFN_COLS):
        halves = []
        for k, lo in enumerate((c * FFN_COLS, dff + c * FFN_COLS)):
            cols = slice(lo, lo + FFN_COLS)
            up = _dot(hh, w_ref[:, cols])
            tail_ref[0, :, cols] = up[tm + FFN_HALO - SUBLANE:, :]
            up_scr[k] = up
            conv = bc_ref[:, cols] + up[FFN_HALO:, :] * wc_ref[CONV_W - 1:CONV_W, cols]
            for j in range(1, CONV_W):
                conv = conv + up_scr[k, pl.ds(FFN_HALO - j, tm), :] * wc_ref[CONV_W - 1 - j:CONV_W - j, cols]
            halves.append(conv)
        act_ref[:, c * FFN_COLS:(c + 1) * FFN_COLS] = _silu_gate(halves[1], halves[0]).astype(BF16)


def _ffn_up_decode_kernel(h_ref, s0_ref, s1_ref, w_ref, wc_ref, bc_ref, act_ref, up_ref):
    dff = act_ref.shape[1]
    h = h_ref[...]
    for c in range(dff // FFN_COLS):
        halves = []
        for lo in (c * FFN_COLS, dff + c * FFN_COLS):
            cols = slice(lo, lo + FFN_COLS)
            up = _dot(h, w_ref[:, cols])
            up_ref[:, cols] = up
            halves.append(bc_ref[:, cols] + s0_ref[:, cols] * wc_ref[0:1, cols]
                          + s1_ref[:, cols] * wc_ref[1:2, cols] + up * wc_ref[2:3, cols])
        act_ref[:, c * FFN_COLS:(c + 1) * FFN_COLS] = _silu_gate(halves[1], halves[0]).astype(BF16)


def _ffn_up_prompt(h, seq, w_up, w_conv, b_conv, tm):
    n, d = h.shape
    dff = w_up.shape[1] // 2
    assert dff % FFN_COLS == 0 and seq % tm == 0 and tm % FFN_HALO == 0
    ratio = tm // FFN_HALO
    act, tail = pl.pallas_call(
        functools.partial(_ffn_up_prompt_kernel, tiles_per_seq=seq // tm),
        grid=(n // tm,),
        in_specs=[pl.BlockSpec((tm, d), lambda i: (i, 0)),
                  pl.BlockSpec((FFN_HALO, d), lambda i: (jnp.maximum(i * ratio - 1, 0), 0)),
                  _full(w_up.shape), _full(w_conv.shape), _full((1, 2 * dff))],
        out_specs=[pl.BlockSpec((tm, dff), lambda i: (i, 0)),
                   pl.BlockSpec((1, SUBLANE, 2 * dff), lambda i: (i, 0, 0))],
        out_shape=[jax.ShapeDtypeStruct((n, dff), BF16),
                   jax.ShapeDtypeStruct((n // tm, SUBLANE, 2 * dff), F32)],
        scratch_shapes=[pltpu.VMEM((2, tm + FFN_HALO, FFN_COLS), F32)],
        compiler_params=_params("arbitrary"),
        name="ffn_up_prompt",
    )(h, h, w_up, w_conv, b_conv.reshape(1, -1))
    last = tail.reshape(n // seq, seq // tm, SUBLANE, 2 * dff)[:, -1, SUBLANE - (CONV_W - 1):, :]
    return act, last


def _ffn_up_decode(h, state, w_up, w_conv, b_conv):
    n, d = h.shape
    dff = w_up.shape[1] // 2
    act, up = pl.pallas_call(
        _ffn_up_decode_kernel,
        grid=(1,),
        in_specs=[_full((n, d)), _full((n, 2 * dff)), _full((n, 2 * dff)),
                  _full(w_up.shape), _full(w_conv.shape), _full((1, 2 * dff))],
        out_specs=[_full((n, dff)), _full((n, 2 * dff))],
        out_shape=[jax.ShapeDtypeStruct((n, dff), BF16), jax.ShapeDtypeStruct((n, 2 * dff), F32)],
        compiler_params=_params("arbitrary"),
        name="ffn_up_decode",
    )(h, state[:, 0], state[:, 1], w_up, w_conv, b_conv.reshape(1, -1))
    return act, jnp.stack([state[:, 1], up], axis=1)


def _odd_in_kernel(h_ref, w_ref, c64_ref, a64_ref, b64_ref, c128_ref, a128_ref, b128_ref,
                   q_ref, k_ref, v_ref, kb_ref, vt_ref, qi_ref, kif_ref, ki2_ref, wit_ref,
                   *, half64, half128, idx_dh):
    cq = q_ref.shape[1]
    ckv = kb_ref.shape[1]
    ciq = qi_ref.shape[1]
    h = h_ref[...]
    t64 = (c64_ref[...], a64_ref[...], b64_ref[...])
    t128 = (c128_ref[...], a128_ref[...], b128_ref[...])
    q_ref[...] = _rope_apply(_dot(h, w_ref[:, 0:cq]), *t128, half128).astype(BF16)
    k = _rope_apply(_dot(h, w_ref[:, cq:cq + ckv]), *t128, half128)
    kb_ref[...] = k.astype(BF16)
    v = _dot(h, w_ref[:, cq + ckv:cq + 2 * ckv])
    c_dh = ckv // C_KV_HEADS
    for kv in range(C_KV_HEADS):
        k_ref[pl.ds(kv, h.shape[0], stride=C_KV_HEADS), :] = k[:, kv * c_dh:(kv + 1) * c_dh]
        v_ref[pl.ds(kv, h.shape[0], stride=C_KV_HEADS), :] = v[:, kv * c_dh:(kv + 1) * c_dh]
    if vt_ref is not None:
        vt_ref[...] = _with_ones_rows(v.T, ckv // C_KV_HEADS).astype(BF16)
    lo = cq + 2 * ckv
    qi_ref[...] = _rope_apply(_dot(h, w_ref[:, lo:lo + ciq]), *t64, half64).astype(BF16)
    tail = _dot(h, w_ref[:, lo + ciq:lo + ciq + LANE])
    ki = _rope_apply(tail, *t64, half64)
    kif_ref[...] = ki
    lane = lax.broadcasted_iota(jnp.int32, ki.shape, 1)
    ki2_ref[...] = jnp.where(lane < idx_dh, ki, pltpu.roll(ki, idx_dh, 1)).astype(BF16)
    wit_ref[...] = tail.T[idx_dh:idx_dh + 8, :] * (IDX_HEADS ** -0.5)


def _odd_in(h, seq, w_in, tabs64, half64, tabs128, half128, c_dh, idx_dh, tm, with_vt):
    n, d = h.shape
    tm = min(tm, n)
    cq = C_HEADS * c_dh
    ckv = C_KV_HEADS * c_dh
    ciq = IDX_HEADS * idx_dh
    cols = cq + 2 * ckv + ciq + LANE
    assert 2 * idx_dh == LANE and w_in.shape[1] <= cols
    w = jnp.pad(w_in, ((0, 0), (0, cols - w_in.shape[1]))).astype(BF16)
    if seq is None:
        tab = pl.BlockSpec((1, LANE), lambda i: (0, 0))
    else:
        tiles_per_seq = seq // tm
        tab = pl.BlockSpec((tm, LANE), lambda i: (i % tiles_per_seq, 0))
    row = lambda width, dt: (pl.BlockSpec((tm, width), lambda i: (i, 0)), jax.ShapeDtypeStruct((n, width), dt))
    col = lambda height, dt: (pl.BlockSpec((height, tm), lambda i: (0, i)), jax.ShapeDtypeStruct((height, n), dt))
    pair_rows = (pl.BlockSpec((tm * C_KV_HEADS, c_dh), lambda i: (i, 0)),
                 jax.ShapeDtypeStruct((n * C_KV_HEADS, c_dh), F32))
    outs = [row(cq, BF16), pair_rows, pair_rows, row(ckv, BF16)]
    if with_vt:
        outs.append(col(ckv + C_KV_HEADS * ONES_ROWS, BF16))
    outs += [row(ciq, BF16), row(LANE, F32), row(LANE, BF16), col(8, F32)]

    def body(*refs):
        refs = list(refs)
        if not with_vt:
            refs.insert(8 + 4, None)
        _odd_in_kernel(*refs, half64=half64, half128=half128, idx_dh=idx_dh)

    res = pl.pallas_call(
        body,
        grid=(n // tm,),
        in_specs=[pl.BlockSpec((tm, d), lambda i: (i, 0)), _full(w.shape)] + [tab] * 6,
        out_specs=[o[0] for o in outs],
        out_shape=[o[1] for o in outs],
        compiler_params=_params("arbitrary"),
        name="odd_in_prompt" if with_vt else "odd_in_decode",
    )(h, w, *tabs64, *tabs128)
    res = list(res)
    if not with_vt:
        res.insert(4, None)
    return res


FLOAT_BITS = 32
INDEX_BIG = 2 ** 30


def _ind(cond):
    return jnp.where(cond, 1.0, 0.0)


def _ordered_bits(x, to):
    b = x if x.dtype == jnp.int32 else lax.bitcast_convert_type(x, jnp.int32)
    b = jnp.where(b < 0, b ^ 0x7FFFFFFF, b)
    return b if to == jnp.int32 else lax.bitcast_convert_type(b, F32)


def _ordered_bits16(x):
    b = lax.shift_right_arithmetic(lax.bitcast_convert_type(x, jnp.int32), 16)
    return jnp.where(b < 0, b ^ 0x7FFF, b)


def _from_ordered_bits16(k):
    b = jnp.where(k < 0, k ^ 0x7FFF, k)
    return lax.bitcast_convert_type(lax.shift_left(b, 16), F32)


def _bf16_floor(x):
    b = lax.bitcast_convert_type(x, jnp.int32)
    low = b & 0xFFFF
    t = b - low
    t = jnp.where(b < 0, jnp.where(low != 0, t + 0x10000, t), t)
    return lax.bitcast_convert_type(t, F32)


def _bisect_threshold(count_fn, lo, hi, keep, coarse_count_fn=None):
    keep_f = jnp.float32(keep)

    def bisect(passes, pivot_count, st):
        def step(_, st):
            lo_k, hi_k, n_hi = st
            mid_k = (lax.shift_right_arithmetic(lo_k, 1) + lax.shift_right_arithmetic(hi_k, 1)
                     + (lo_k & hi_k & 1))
            cnt = pivot_count(mid_k)
            raise_lo = jnp.logical_and(mid_k > lo_k, cnt >= keep_f)
            lower_hi = jnp.logical_and(mid_k > lo_k, cnt < keep_f)
            return (jnp.where(raise_lo, mid_k, lo_k), jnp.where(lower_hi, mid_k, hi_k),
                    jnp.where(lower_hi, cnt, n_hi))
        return lax.fori_loop(0, passes, step, st)

    def fine_count(mid_k):
        return count_fn(lambda s, idx: (_ind(s >= _ordered_bits(mid_k, F32)),))[0]

    lo_k = _ordered_bits(lo, jnp.int32)
    hi_k = _ordered_bits(hi, jnp.int32) + 1
    n_hi = jnp.zeros(lo.shape, F32)
    if coarse_count_fn is None:
        lo_k, _, n_gt = bisect(FLOAT_BITS, fine_count, (lo_k, hi_k, n_hi))
    else:
        half = FLOAT_BITS // 2
        coarse = lambda mid_h: coarse_count_fn(_from_ordered_bits16(mid_h).astype(BF16))
        lo_h, _, n_hi = bisect(half, coarse, (_ordered_bits16(_bf16_floor(lo)),
                                              _ordered_bits16(_bf16_floor(hi)) + 1, n_hi))
        st = (_ordered_bits(_from_ordered_bits16(lo_h), jnp.int32),
              _ordered_bits(_from_ordered_bits16(lo_h + 1), jnp.int32), n_hi)
        lo_k, _, n_gt = bisect(half, fine_count, st)
    return _ordered_bits(lo_k, F32), n_gt


def _select_threshold(count_fn, lo, hi, active, keep, index_bits):
    thr, n_gt = _bisect_threshold(count_fn, lo, hi, keep)
    n_eq, = count_fn(lambda s, idx: (_ind(s == thr),))
    need = jnp.float32(keep) - n_gt
    excess = jnp.where(active, jnp.where(n_eq > need, 1, 0), 0)

    def tie_search():
        lo_i = jnp.full(thr.shape, -1, jnp.int32)
        hi_i = jnp.full(thr.shape, (1 << index_bits) - 1, jnp.int32)
        for _ in range(index_bits + 1):
            mid_i = lax.shift_right_arithmetic(lo_i + hi_i, 1)
            n_low, = count_fn(lambda s, idx: (jnp.where(s == thr, _ind(idx <= mid_i), 0.0),))
            ok = n_low >= need
            hi_i = jnp.where(ok, mid_i, hi_i)
            lo_i = jnp.where(ok, lo_i, mid_i)
        return hi_i

    cut = lax.cond(jnp.max(excess) > 0, tie_search, lambda: jnp.full(thr.shape, INDEX_BIG, jnp.int32))
    thr = jnp.where(active, thr, -jnp.inf)
    cut = jnp.where(active, cut, -1)
    return thr, cut


def _selected(s, idx, thr, cut):
    return jnp.where(s > thr, 1.0, jnp.where(s == thr, _ind(idx <= cut), 0.0))


def _dsa_prompt_kernel(q_ref, qi_ref, wit_ref, kb_ref, vt_ref, ki2_ref, o_ref,
                       sc_ref, sb_ref, m_ref, acc_ref, *, c_dh, idx_dh, keep, kc):
    tq = q_ref.shape[0]
    seq = kb_ref.shape[0]
    t0 = pl.program_id(1) * tq
    nch = (t0 + tq + kc - 1) // kc
    scale = c_dh ** -0.5
    neg_inf = jnp.float32(-jnp.inf)
    lane = lax.broadcasted_iota(jnp.int32, (tq, LANE), 1)
    key_iota = lax.broadcasted_iota(jnp.int32, (kc, tq), 0)
    qpos = t0 + lax.broadcasted_iota(jnp.int32, (1, tq), 1)

    qi = qi_ref[...]
    per_group = LANE // idx_dh
    qms = []
    for h in range(IDX_HEADS):
        grp = qi[:, (h // per_group) * LANE:(h // per_group + 1) * LANE]
        qms.append(jnp.where(_head_of(lane, idx_dh) == h % per_group, grp, jnp.zeros_like(grp)))
    wit = wit_ref[...]

    def score_chunk(c, carry):
        lo, hi = carry
        start = pl.multiple_of(c * kc, kc)
        kic = ki2_ref[pl.ds(start, kc), :]
        acc = jnp.zeros((kc, tq), F32)
        for h in range(IDX_HEADS):
            acc = acc + jnp.maximum(_dot_nt(kic, qms[h]), 0.0) * wit[h:h + 1, :]
        score = acc * (idx_dh ** -0.5)
        adm = start + key_iota <= qpos
        masked = jnp.where(adm, score, neg_inf)
        sc_ref[pl.ds(start, kc), :] = masked
        sb_ref[pl.ds(start, kc), :] = _bf16_floor(masked).astype(BF16)
        lo = jnp.minimum(lo, jnp.min(jnp.where(adm, score, jnp.inf), axis=0, keepdims=True))
        hi = jnp.maximum(hi, jnp.max(jnp.where(adm, score, neg_inf), axis=0, keepdims=True))
        return lo, hi

    lo, hi = lax.fori_loop(0, nch, score_chunk,
                           (jnp.full((1, tq), jnp.inf, F32), jnp.full((1, tq), neg_inf, F32)))

    def count_fn(pred, n_out=1):
        def body(c, cnts):
            start = pl.multiple_of(c * kc, kc)
            hits = pred(sc_ref[pl.ds(start, kc), :], start + key_iota)
            return tuple(cnt + jnp.sum(hit.reshape(kc // SUBLANE, SUBLANE, tq), axis=0) for cnt, hit in zip(cnts, hits))
        parts = lax.fori_loop(0, nch, body, tuple(jnp.zeros((SUBLANE, tq), F32) for _ in range(n_out)))
        return tuple(jnp.sum(part, axis=0, keepdims=True) for part in parts)

    def coarse_count_fn(pivot):
        one, zero = jnp.ones((), BF16), jnp.zeros((), BF16)
        rows = SUBLANE_BF16

        def body(c, cnt):
            hit = jnp.where(sb_ref[pl.ds(pl.multiple_of(c * kc, kc), kc), :] >= pivot, one, zero)
            part = hit[0:rows]
            for r in range(1, kc // rows):
                part = part + hit[r * rows:(r + 1) * rows]
            return cnt + part.astype(F32)
        part = lax.fori_loop(0, nch, body, jnp.zeros((rows, tq), F32))
        return jnp.sum(part, axis=0, keepdims=True)

    thr, n_gt = _bisect_threshold(count_fn, lo, hi, keep, coarse_count_fn)
    active = qpos + 1 > keep
    thr = jnp.where(active, thr, neg_inf)
    need = jnp.where(active, jnp.float32(keep) - n_gt, 0.0)
    prefix_ones = jnp.where(lax.broadcasted_iota(jnp.int32, (kc, kc), 0)
                            >= lax.broadcasted_iota(jnp.int32, (kc, kc), 1), 1.0, 0.0).astype(BF16)

    m_ref[...] = jnp.full(m_ref.shape, NEG_BIG, F32)
    acc_ref[...] = jnp.zeros(acc_ref.shape, F32)
    group = C_HEADS // C_KV_HEADS
    ext = c_dh + ONES_ROWS
    exp2_scale = scale * LOG2_E
    head_row = lax.broadcasted_iota(jnp.int32, m_ref.shape, 0)

    def attend_chunk(c, ties_before):
        start = pl.multiple_of(c * kc, kc)
        sc = sc_ref[pl.ds(start, kc), :]
        tie = _ind(sc == thr)
        tie_rank = ties_before + _dot(prefix_ones, tie.astype(BF16))
        keepm = jnp.where(sc > thr, 1.0, jnp.where(tie_rank <= need, tie, 0.0)) > 0.0
        ties_before = tie_rank[kc - 1:kc, :]
        kbc = kb_ref[pl.ds(start, kc), :]
        vtc = vt_ref[:, pl.ds(start, kc)]
        m_old = m_ref[...]
        m_all = jnp.zeros(m_old.shape, F32)
        scores = [_dot_nt(kbc[:, (h // group) * c_dh:(h // group + 1) * c_dh], q_ref[:, h * c_dh:(h + 1) * c_dh])
                  for h in range(C_HEADS)]
        probs, alphas = [], []
        for h in range(C_HEADS):
            s = jnp.where(keepm, scores[h], neg_inf)
            m_new = jnp.maximum(m_old[h:h + 1, :], jnp.max(s, axis=0, keepdims=True))
            alphas.append(jnp.exp2((m_old[h:h + 1, :] - m_new) * exp2_scale))
            probs.append(jnp.exp2((s - m_new) * exp2_scale).astype(BF16))
            m_all = jnp.where(head_row == h, m_new, m_all)
        pvs = [_dot(vtc[(h // group) * ext:(h // group + 1) * ext, :], probs[h]) for h in range(C_HEADS)]
        m_ref[...] = m_all
        for h in range(C_HEADS):
            acc_ref[h] = acc_ref[h] * alphas[h] + pvs[h]
        return ties_before

    lax.fori_loop(0, nch, attend_chunk, jnp.zeros((1, tq), F32))
    for h in range(C_HEADS):
        out_t = acc_ref[h, 0:c_dh, :] / acc_ref[h, c_dh:c_dh + 1, :]
        o_ref[:, h * c_dh:(h + 1) * c_dh] = out_t.T.astype(BF16)


def _dsa_prompt(q, qi, wit, kb, vt, ki2, batch, seq, c_dh, idx_dh, keep, tq, kc):
    n = q.shape[0]
    tiles = seq // tq
    assert c_dh == LANE and seq % kc == 0 and seq % tq == 0
    return pl.pallas_call(
        functools.partial(_dsa_prompt_kernel, c_dh=c_dh, idx_dh=idx_dh, keep=keep, kc=kc),
        grid=(batch, tiles),
        in_specs=[pl.BlockSpec((tq, q.shape[1]), lambda b, i: (b * tiles + i, 0)),
                  pl.BlockSpec((tq, qi.shape[1]), lambda b, i: (b * tiles + i, 0)),
                  pl.BlockSpec((SUBLANE, tq), lambda b, i: (0, b * tiles + i)),
                  pl.BlockSpec((seq, kb.shape[1]), lambda b, i: (b, 0)),
                  pl.BlockSpec((vt.shape[0], seq), lambda b, i: (0, b)),
                  pl.BlockSpec((seq, LANE), lambda b, i: (b, 0))],
        out_specs=pl.BlockSpec((tq, q.shape[1]), lambda b, i: (b * tiles + i, 0)),
        out_shape=jax.ShapeDtypeStruct(q.shape, BF16),
        scratch_shapes=[pltpu.VMEM((seq, tq), F32), pltpu.VMEM((seq, tq), BF16), pltpu.VMEM((SUBLANE, tq), F32),
                        pltpu.VMEM((C_HEADS, c_dh + ONES_ROWS, tq), F32)],
        compiler_params=_params("arbitrary", "arbitrary"),
        name="dsa_prompt",
    )(q, qi, wit, kb, vt, ki2)


PAGE_ROWS = 128
DECODE_SCORE_SEQS = 4
DECODE_ATTN_SEQS = 2


def _lane_bcast_col(row):
    return jnp.broadcast_to(row, (LANE, row.shape[1])).T


def _moba_decode_kernel(pt_ref, q_ref, kn_ref, vn_ref, *rest, n_pages, dh):
    kt_refs, vt_refs, o_ref = rest[:n_pages], rest[n_pages:2 * n_pages], rest[2 * n_pages]
    bw = q_ref.shape[2]
    heads = bw // dh
    ppb = MOBA_BLOCK // PAGE_ROWS
    nblk = n_pages // ppb
    neg_inf = jnp.float32(-jnp.inf)
    qb = _lane_bcast_col(q_ref[0].astype(F32))

    def head_sums(x):
        return jnp.sum(x.reshape(heads, dh, LANE), axis=1)

    def head_rows(x):
        return jnp.broadcast_to(x[:, None, :], (heads, dh, LANE)).reshape(bw, LANE)

    s_pages = [head_sums(kt_refs[j][...] * qb) for j in range(n_pages)]
    s_own = head_sums(_lane_bcast_col(kn_ref[0]) * qb)
    gates = []
    for n in range(nblk):
        g = jnp.zeros((heads, 1), F32)
        for j in range(n * ppb, (n + 1) * ppb):
            g = g + jnp.sum(s_pages[j], axis=1, keepdims=True)
        gates.append(g)
    sels = []
    for n in range(nblk):
        rank = jnp.zeros((heads, 1), F32)
        for m in range(nblk):
            if m != n:
                beats = gates[m] >= gates[n] if m < n else gates[m] > gates[n]
                rank = rank + jnp.where(beats, 1.0, 0.0)
        sels.append(rank < MOBA_TOPK)
    m = s_own
    masked = []
    for j in range(n_pages):
        sm = jnp.where(sels[j // ppb], s_pages[j], neg_inf)
        masked.append(sm)
        m = jnp.maximum(m, jnp.max(sm, axis=1, keepdims=True))
    e_own = jnp.exp(s_own - m)
    es = [jnp.exp(sm - m) for sm in masked]
    l = e_own
    for e in es:
        l = l + jnp.sum(e, axis=1, keepdims=True)
    rinv = 1.0 / l
    lane = lax.broadcasted_iota(jnp.int32, (bw, LANE), 1)
    acc = jnp.where(lane == 0, head_rows(e_own * rinv) * _lane_bcast_col(vn_ref[0]), 0.0)
    for j in range(n_pages):
        acc = acc + head_rows(es[j] * rinv) * vt_refs[j][...]
    out_col = jnp.sum(acc, axis=1, keepdims=True)
    o_ref[0] = jnp.broadcast_to(out_col, (bw, LANE)).T[0:1].astype(BF16)


def _moba_decode(q, k_new, v_new, cache_k, cache_v, page_table, dh):
    n, bw = q.shape
    n_pages = page_table.shape[1]
    assert cache_k.shape[1] == PAGE_ROWS == LANE and (n_pages * PAGE_ROWS) % MOBA_BLOCK == 0
    ckt = jnp.transpose(cache_k, (0, 2, 3, 1)).reshape(cache_k.shape[0], bw, PAGE_ROWS)
    cvt = jnp.transpose(cache_v, (0, 2, 3, 1)).reshape(cache_v.shape[0], bw, PAGE_ROWS)
    seq_row = pl.BlockSpec((1, 1, bw), lambda b, pt: (b, 0, 0))
    pages = [pl.BlockSpec((None, bw, PAGE_ROWS), lambda b, pt, j=j: (pt[b, j], 0, 0)) for j in range(n_pages)]
    out = pl.pallas_call(
        functools.partial(_moba_decode_kernel, n_pages=n_pages, dh=dh),
        grid_spec=pltpu.PrefetchScalarGridSpec(
            num_scalar_prefetch=1, grid=(n,),
            in_specs=[seq_row, seq_row, seq_row] + pages + pages,
            out_specs=seq_row),
        out_shape=jax.ShapeDtypeStruct((n, 1, bw), BF16),
        compiler_params=_params("arbitrary"),
        name="moba_decode",
    )(page_table, q.reshape(n, 1, bw), k_new.reshape(n, 1, bw), v_new.reshape(n, 1, bw),
      *([ckt] * n_pages), *([cvt] * n_pages))
    return out.reshape(n, bw)


def _dsa_decode_score_kernel(pt_ref, qi_ref, wi_ref, kin_ref, *rest, n_pages, idx_dh, seqs):
    for g in range(seqs):
        _dsa_decode_score_seq(g, qi_ref, wi_ref, kin_ref, rest[g * n_pages:(g + 1) * n_pages], rest[-1],
                              n_pages, idx_dh)


def _dsa_decode_score_seq(g, qi_ref, wi_ref, kin_ref, kit_refs, o_ref, n_pages, idx_dh):
    qb = _lane_bcast_col(qi_ref[g].astype(F32))
    wi8 = wi_ref[g]
    scale = idx_dh ** -0.5

    def score(kit):
        total = jnp.zeros((1, LANE), F32)
        for h in range(IDX_HEADS):
            sc = jnp.sum(kit * qb[h * idx_dh:(h + 1) * idx_dh], axis=0, keepdims=True)
            total = total + jnp.maximum(sc, 0.0) * wi8[h:h + 1, :]
        return total * scale

    o_ref[g] = jnp.full(o_ref.shape[1:], -jnp.inf, F32)
    for j in range(n_pages):
        o_ref[g, j:j + 1, :] = score(kit_refs[j][...])
    own = score(_lane_bcast_col(kin_ref[g]))
    lane = lax.broadcasted_iota(jnp.int32, (1, PAGE_ROWS), 1)
    o_ref[g, n_pages:n_pages + 1, :] = jnp.where(lane == 0, own, -jnp.inf)


def _dsa_decode_select_kernel(s_ref, m_ref, *, keep, n_valid):
    s = s_ref[...]
    idx = lax.broadcasted_iota(jnp.int32, s.shape, 1)
    neg_inf = jnp.float32(-jnp.inf)
    lo = jnp.min(jnp.where(s > neg_inf, s, jnp.inf), axis=1, keepdims=True)
    hi = jnp.max(s, axis=1, keepdims=True)
    count_fn = lambda pred, n_out=1: tuple(jnp.sum(hit, axis=1, keepdims=True) for hit in pred(s, idx))
    active = jnp.full(lo.shape, n_valid > keep, jnp.bool_)
    thr, cut = _select_threshold(count_fn, lo, hi, active, keep, (s.shape[1] - 1).bit_length())
    m_ref[...] = _selected(s, idx, thr, cut)


def _dsa_decode_attn_kernel(pt_ref, q_ref, kn_ref, vn_ref, mask_ref, *rest, n_pages, c_dh, seqs):
    staged = [_dsa_decode_attn_scores(g, q_ref, kn_ref, mask_ref, rest[g * n_pages:(g + 1) * n_pages],
                                      n_pages, c_dh) for g in range(seqs)]
    for g in range(seqs):
        _dsa_decode_attn_values(g, staged[g], vn_ref, rest[(seqs + g) * n_pages:(seqs + g + 1) * n_pages],
                                rest[-1], n_pages, c_dh)


def _dsa_decode_attn_scores(g, q_ref, kn_ref, mask_ref, k_refs, n_pages, c_dh):
    rows = k_refs[0].shape[0]
    group = C_HEADS // C_KV_HEADS
    scale = c_dh ** -0.5
    neg_inf = jnp.float32(-jnp.inf)
    qm = jnp.concatenate([q_ref[g].astype(F32), jnp.zeros((LANE - C_HEADS, c_dh), F32)], axis=0).astype(BF16)
    row_id = lax.broadcasted_iota(jnp.int32, (rows, LANE), 0)
    lane_id = lax.broadcasted_iota(jnp.int32, (rows, LANE), 1)
    own_kv = (row_id & (C_KV_HEADS - 1)) == _head_of(lane_id, group)
    mask = mask_ref[g]
    mask_t = jnp.concatenate([mask, jnp.zeros((LANE - mask.shape[0], PAGE_ROWS), F32)], axis=0).T
    repeat = jnp.where(_head_of(row_id, C_KV_HEADS) == lane_id, 1.0, 0.0).astype(BF16)
    row_mask = _dot(repeat, mask_t.astype(BF16))
    kn = jnp.concatenate([kn_ref[g], jnp.zeros((8 - C_KV_HEADS, c_dh), F32)], axis=0)
    s_own_kv = _dot_nt(kn.astype(BF16), qm)
    lane_kv = _head_of(lax.broadcasted_iota(jnp.int32, (1, LANE), 1), group)
    s_own = jnp.zeros((1, LANE), F32)
    for kv in range(C_KV_HEADS):
        s_own = jnp.where(lane_kv == kv, s_own_kv[kv:kv + 1, :], s_own)
    s_own = jnp.where(mask[n_pages:n_pages + 1, 0:1] > 0.0, s_own, neg_inf)
    m = jnp.maximum(s_own, NEG_BIG)
    masked = []
    for j in range(n_pages):
        s = _dot_nt(k_refs[j][...].astype(BF16), qm)
        sm = jnp.where(own_kv, jnp.where(row_mask[:, j:j + 1] > 0.5, s, neg_inf), neg_inf)
        masked.append(sm)
        m = jnp.maximum(m, jnp.max(sm, axis=0, keepdims=True))
    return masked, s_own, m, lane_kv


def _dsa_decode_attn_values(g, staged, vn_ref, v_refs, o_ref, n_pages, c_dh):
    masked, s_own, m, lane_kv = staged
    scale = c_dh ** -0.5
    own_rows = lax.broadcasted_iota(jnp.int32, (SUBLANE_BF16, LANE), 0)
    e_own = jnp.where(own_rows == lane_kv, jnp.exp((s_own - m) * scale), 0.0)
    v_own = jnp.concatenate([vn_ref[g], jnp.zeros((SUBLANE_BF16 - C_KV_HEADS, c_dh), F32)], axis=0)
    e_all = jnp.concatenate([jnp.exp((sm - m) * scale) for sm in masked] + [e_own], axis=0).astype(BF16)
    v_all = jnp.concatenate([v_refs[j][...] for j in range(n_pages)] + [v_own], axis=0).astype(BF16)
    v_all = jnp.concatenate([v_all, jnp.ones(v_all.shape, BF16)], axis=1)
    out = lax.dot_general(e_all, v_all, (((0,), (0,)), ((), ())), preferred_element_type=F32)
    o_ref[g] = (out[0:C_HEADS, 0:c_dh] / out[0:C_HEADS, c_dh:2 * c_dh]).astype(BF16)


def _dsa_decode(q, qi, wit, k_new, v_new, ki_new, cache_k, cache_v, cache_ki, page_table, c_dh, idx_dh, keep):
    n = q.shape[0]
    n_pages = page_table.shape[1]
    assert cache_k.shape[1] == PAGE_ROWS == LANE
    pool = cache_k.shape[0]
    ck = cache_k.reshape(pool, PAGE_ROWS * C_KV_HEADS, c_dh)
    cv = cache_v.reshape(pool, PAGE_ROWS * C_KV_HEADS, c_dh)
    ckit = jnp.transpose(cache_ki, (0, 2, 1))
    rows = -(-(n_pages + 1) // SUBLANE) * SUBLANE
    wi8 = wit.T.reshape(n, 8, 1)

    def seq3(g, a, b):
        return pl.BlockSpec((g, a, b), lambda s, pt: (s, 0, 0))

    def pages(g, r, w):
        return [pl.BlockSpec((None, r, w), lambda s, pt, j=j, k=k: (pt[s * g + k, j], 0, 0))
                for k in range(g) for j in range(n_pages)]

    gs = DECODE_SCORE_SEQS
    assert n % gs == 0 and n % DECODE_ATTN_SEQS == 0
    scores = pl.pallas_call(
        functools.partial(_dsa_decode_score_kernel, n_pages=n_pages, idx_dh=idx_dh, seqs=gs),
        grid_spec=pltpu.PrefetchScalarGridSpec(
            num_scalar_prefetch=1, grid=(n // gs,),
            in_specs=[seq3(gs, 1, IDX_HEADS * idx_dh), seq3(gs, 8, 1), seq3(gs, 1, idx_dh)]
                     + pages(gs, idx_dh, PAGE_ROWS),
            out_specs=seq3(gs, rows, PAGE_ROWS)),
        out_shape=jax.ShapeDtypeStruct((n, rows, PAGE_ROWS), F32),
        compiler_params=_params("arbitrary"),
        name="dsa_decode_scores",
    )(page_table, qi.reshape(n, 1, IDX_HEADS * idx_dh), wi8, ki_new.reshape(n, 1, idx_dh),
      *([ckit] * (gs * n_pages)))
    flat = rows * PAGE_ROWS
    mask = pl.pallas_call(
        functools.partial(_dsa_decode_select_kernel, keep=keep, n_valid=n_pages * PAGE_ROWS + 1),
        grid=(1,),
        in_specs=[_full((n, flat))],
        out_specs=_full((n, flat)),
        out_shape=jax.ShapeDtypeStruct((n, flat), F32),
        compiler_params=_params("arbitrary"),
        name="dsa_decode_select",
    )(scores.reshape(n, flat))
    cq = C_HEADS * c_dh
    ga = DECODE_ATTN_SEQS
    kv_rows = PAGE_ROWS * C_KV_HEADS
    out = pl.pallas_call(
        functools.partial(_dsa_decode_attn_kernel, n_pages=n_pages, c_dh=c_dh, seqs=ga),
        grid_spec=pltpu.PrefetchScalarGridSpec(
            num_scalar_prefetch=1, grid=(n // ga,),
            in_specs=[seq3(ga, C_HEADS, c_dh), seq3(ga, C_KV_HEADS, c_dh), seq3(ga, C_KV_HEADS, c_dh),
                      seq3(ga, rows, PAGE_ROWS)] + pages(ga, kv_rows, c_dh) + pages(ga, kv_rows, c_dh),
            out_specs=seq3(ga, C_HEADS, c_dh)),
        out_shape=jax.ShapeDtypeStruct((n, C_HEADS, c_dh), BF16),
        compiler_params=_params("arbitrary"),
        name="dsa_decode_attn",
    )(page_table, q.reshape(n, C_HEADS, c_dh), k_new.reshape(n, C_KV_HEADS, c_dh),
      v_new.reshape(n, C_KV_HEADS, c_dh), mask.reshape(n, rows, PAGE_ROWS),
      *([ck] * (ga * n_pages)), *([cv] * (ga * n_pages)))
    return out.reshape(n, cq)


def kernel(x_prompt, x_sample, cache_moba_k, cache_moba_v, cache_dsa_k, cache_dsa_v, cache_dsa_kidx,
           state_ffn_conv, page_table, w_in_even, g_sgu, w_sgu, b_sgu, w_out_even, w_in_odd, w_out_odd,
           g_mix, g_ffn, w_up, w_conv, b_conv, w_down, g_final):
    batch, seq, d = x_prompt.shape
    n_dec, dec_seq, _ = x_sample.shape
    depth = g_mix.shape[0]
    n_pages = page_table.shape[1]
    past_len = n_pages * cache_moba_k.shape[2]
    assert depth == 2 and dec_seq == 1, "one even + one odd layer, one decode token per sequence"
    b_dh = cache_moba_k.shape[-1]
    c_dh = cache_dsa_k.shape[-1]
    idx_dh = cache_dsa_kidx.shape[-1]
    keep_p = min(DSA_TOPK, seq // 4)
    keep_s = min(DSA_TOPK, (past_len + dec_seq) // 4)
    xp = x_prompt.reshape(batch * seq, d)
    xs = x_sample.reshape(n_dec, d)
    bf = lambda w: w.astype(BF16)

    tp64, half64 = _rope_tables(b_dh, seq, 0)
    ts64, _ = _rope_tables(b_dh, 1, past_len)
    tp128, half128 = _rope_tables(c_dh, seq, 0)
    ts128, _ = _rope_tables(c_dh, 1, past_len)
    assert idx_dh == b_dh

    w_in0, w_out0 = bf(w_in_even[0]), bf(w_out_even[0])
    a_p, q_p, kb_p, mkt_p, mvt_p, vt_p, km_p = _even_in_prompt(
        xp, seq, g_mix[0], w_in0, tp64, half64, g_sgu[0], w_sgu[0], b_sgu[0])
    bo_p = _moba_prompt(q_p, kb_p, vt_p, km_p, batch, seq, b_dh)
    cache_view = lambda t: jnp.transpose(t.reshape(batch, B_HEADS, b_dh, seq), (0, 3, 1, 2))[None]
    a_s, q_s, mk_s, mv_s, va_s = _even_in_decode(xs, g_mix[0], w_in0, ts64, half64, g_sgu[0], w_sgu[0], b_sgu[0])
    bo_s = _moba_decode(q_s, mk_s, mv_s, cache_moba_k[0], cache_moba_v[0], page_table, b_dh)
    w_up0, w_down0 = bf(w_up[0]), bf(w_down[0])
    xp, hp = _residual_proj(xp, [a_p, bo_p], w_out0, g_ffn[0], "next", ROW_TILE, "out_even_prompt")
    xs, hs = _residual_proj(xs, [a_s, bo_s], w_out0, g_ffn[0], "next", ROW_TILE, "out_even_decode")
    act_p, conv_p0 = _ffn_up_prompt(hp, seq, w_up0, w_conv[0], b_conv[0], ROW_TILE)
    act_s, conv_s0 = _ffn_up_decode(hs, state_ffn_conv[0], w_up0, w_conv[0], b_conv[0])
    xp, hp = _residual_proj(xp, [act_p], w_down0, g_mix[1], "next", ROW_TILE, "ffn_down0_prompt")
    xs, hs = _residual_proj(xs, [act_s], w_down0, g_mix[1], "next", ROW_TILE, "ffn_down0_decode")

    w_out1 = bf(w_out_odd[0])
    q_p, dk_p, dv_p, kb_p, vt_p, qi_p, kif_p, ki2_p, wit_p = _odd_in(
        hp, seq, w_in_odd[0], tp64, half64, tp128, half128, c_dh, idx_dh, ROW_TILE, True)
    o_p = _dsa_prompt(q_p, qi_p, wit_p, kb_p, vt_p, ki2_p, batch, seq, c_dh, idx_dh, keep_p, DSA_QUERY_TILE, DSA_KEY_CHUNK)
    q_s, dk_s, dv_s, _, _, qi_s, kif_s, _, wit_s = _odd_in(
        hs, None, w_in_odd[0], ts64, half64, ts128, half128, c_dh, idx_dh, 256, False)
    o_s = _dsa_decode(q_s, qi_s, wit_s, dk_s, dv_s, kif_s[:, :idx_dh], cache_dsa_k[0], cache_dsa_v[0],
                      cache_dsa_kidx[0], page_table, c_dh, idx_dh, keep_s)
    w_up1, w_down1 = bf(w_up[1]), bf(w_down[1])
    xp, hp = _residual_proj(xp, [o_p], w_out1, g_ffn[1], "next", ROW_TILE, "out_odd_prompt")
    xs, hs = _residual_proj(xs, [o_s], w_out1, g_ffn[1], "next", ROW_TILE, "out_odd_decode")
    act_p, conv_p1 = _ffn_up_prompt(hp, seq, w_up1, w_conv[1], b_conv[1], ROW_TILE)
    act_s, conv_s1 = _ffn_up_decode(hs, state_ffn_conv[1], w_up1, w_conv[1], b_conv[1])
    y_p, = _residual_proj(xp, [act_p], w_down1, g_final, "final", ROW_TILE, "ffn_down1_prompt")
    y_s, = _residual_proj(xs, [act_s], w_down1, g_final, "final", ROW_TILE, "ffn_down1_decode")

    ckv = C_KV_HEADS
    return (y_p.reshape(batch, seq, d), y_s.reshape(n_dec, dec_seq, d),
            cache_view(mkt_p), cache_view(mvt_p),
            mk_s.reshape(1, n_dec, dec_seq, B_HEADS, b_dh), mv_s.reshape(1, n_dec, dec_seq, B_HEADS, b_dh),
            va_s.reshape(1, n_dec, dec_seq, A_GROUPS, LANE),
            dk_p.reshape(1, batch, seq, ckv, c_dh), dv_p.reshape(1, batch, seq, ckv, c_dh),
            kif_p[:, :idx_dh].reshape(1, batch, seq, idx_dh),
            dk_s.reshape(1, n_dec, dec_seq, ckv, c_dh), dv_s.reshape(1, n_dec, dec_seq, ckv, c_dh),
            kif_s[:, :idx_dh].reshape(1, n_dec, dec_seq, idx_dh),
            jnp.stack([conv_p0, conv_p1]), jnp.stack([conv_s0, conv_s1]))
```

```python
import functools

import numpy as np
import jax
import jax.numpy as jnp
from jax import lax
from jax.experimental import pallas as pl
from jax.experimental.pallas import tpu as pltpu

F32 = jnp.float32
BF16 = jnp.bfloat16

A_GROUPS = 4
CHUNK = 128
B_HEADS = 8
MOBA_BLOCK = 256
MOBA_TOPK = 3
C_HEADS = 8
C_KV_HEADS = 2
IDX_HEADS = 4
DSA_TOPK = 256
ROPE_THETA = 500000.0
ROPE_FRAC = 4
CONV_W = 3
EPS = 1e-6

LANE = 128
SUBLANE = 8
SUBLANE_BF16 = 16
VMEM_LIMIT_BYTES = 56 * 1024 * 1024
NEG_BIG = -1e30
LOG2_E = 1.4426950408889634
ROW_TILE = 512
DSA_QUERY_TILE = 256
DSA_KEY_CHUNK = 256


def _params(*sem):
    return pltpu.CompilerParams(dimension_semantics=sem,
                                vmem_limit_bytes=VMEM_LIMIT_BYTES)


def _full(shape):
    nd = len(shape)
    return pl.BlockSpec(shape, lambda *_: (0,) * nd)


def _rmsnorm(x, g):
    ms = jnp.mean(x * x, axis=-1, keepdims=True)
    return x * lax.rsqrt(ms + EPS) * g


def _rope_apply(x, c, s1, s2, half):
    parts = []
    for j in range(x.shape[1] // LANE):
        xs = x[:, j * LANE:(j + 1) * LANE]
        parts.append(xs * c + pltpu.roll(xs, LANE - half, 1) * s1
                     + pltpu.roll(xs, half, 1) * s2)
    return parts[0] if len(parts) == 1 else jnp.concatenate(parts, axis=1)


def _head_of(index, dh):
    assert dh & (dh - 1) == 0
    return lax.shift_right_logical(index, dh.bit_length() - 1)


ONES_ROWS = SUBLANE_BF16


def _with_ones_rows(vt, dh):
    ones = jnp.ones((ONES_ROWS, vt.shape[1]), vt.dtype)
    parts = []
    for h in range(vt.shape[0] // dh):
        parts += [vt[h * dh:(h + 1) * dh], ones]
    return jnp.concatenate(parts, axis=0)


def _dot(a, b):
    return jnp.dot(a, b, preferred_element_type=F32)


def _dot_nt(a, b):
    return lax.dot_general(a, b, (((1,), (1,)), ((), ())),
                           preferred_element_type=F32)


def _rope_rows(dh):
    rd = dh // ROPE_FRAC
    half = rd // 2
    inv = ROPE_THETA ** (-jnp.arange(half, dtype=F32) * (2.0 / rd))
    r = np.arange(LANE) % dh
    j = np.where(r < half, r, np.where(r < rd, r - half, 0))
    inv_lane = jnp.where(jnp.asarray(r < rd), inv[j], 0.0).reshape(1, LANE)
    s1 = jnp.asarray(np.where(r < half, -1.0, 0.0), F32).reshape(1, LANE)
    s2 = jnp.asarray(np.where((r >= half) & (r < rd), 1.0, 0.0), F32).reshape(1, LANE)
    return inv_lane, s1, s2, half


def _rope_table_kernel(inv_ref, s1_ref, s2_ref, c_ref, a_ref, b_ref, *, base):
    rows = c_ref.shape[0]
    pos = lax.broadcasted_iota(jnp.int32, (rows, LANE), 0) + (base + pl.program_id(0) * rows)
    ang = pos.astype(F32) * inv_ref[...]
    s = jnp.sin(ang)
    c_ref[...] = jnp.cos(ang)
    a_ref[...] = s * s1_ref[...]
    b_ref[...] = s * s2_ref[...]


def _rope_tables(dh, n_pos, base):
    inv_lane, s1, s2, half = _rope_rows(dh)
    rows = min(n_pos, ROW_TILE)
    assert n_pos % rows == 0
    out = jax.ShapeDtypeStruct((n_pos, LANE), F32)
    spec = pl.BlockSpec((rows, LANE), lambda i: (i, 0))
    c, a, b = pl.pallas_call(
        functools.partial(_rope_table_kernel, base=base),
        grid=(n_pos // rows,),
        in_specs=[_full((1, LANE))] * 3,
        out_specs=[spec] * 3,
        out_shape=[out] * 3,
        compiler_params=_params("arbitrary"),
        name="rope_tables",
    )(inv_lane, s1, s2)
    return (c, a, b), half


def _sgu_values(h, w_ref, gs_ref, aw):
    u = jax.nn.gelu(_dot(h, w_ref[:, 0:aw]))
    zv = jax.nn.gelu(_dot(h, w_ref[:, aw:2 * aw]))
    vs = []
    for g in range(A_GROUPS):
        vg = zv[:, g * LANE:(g + 1) * LANE]
        vs.append(_rmsnorm(vg, gs_ref[:, g * LANE:(g + 1) * LANE]))
    return u, vs


def _even_in_prompt_kernel(x_ref, g_ref, w_ref, c_ref, s1_ref, s2_ref, gs_ref, ws_ref, bs_ref,
                           a_ref, q_ref, kb_ref, kt_ref, vt_ref, vtb_ref, km_ref, v_scr, *, half, scale):
    tm = x_ref.shape[0]
    aw = A_GROUPS * LANE
    bw = q_ref.shape[1]
    h = _rmsnorm(x_ref[...], g_ref[...]).astype(BF16)
    u, vs = _sgu_values(h, w_ref, gs_ref, aw)
    tri = (lax.broadcasted_iota(jnp.int32, (CHUNK, CHUNK), 0)
           >= lax.broadcasted_iota(jnp.int32, (CHUNK, CHUNK), 1))
    for g in range(A_GROUPS):
        wg = jnp.where(tri, ws_ref[g], 0.0).astype(BF16)
        for cc in range(tm // CHUNK):
            rows = slice(cc * CHUNK, (cc + 1) * CHUNK)
            mixed = _dot(wg, vs[g][rows].astype(BF16)) + bs_ref[g]
            a_ref[rows, g * LANE:(g + 1) * LANE] = (u[rows, g * LANE:(g + 1) * LANE] * mixed).astype(BF16)
    c, s1, s2 = c_ref[...], s1_ref[...], s2_ref[...]
    q = _rope_apply(_dot(h, w_ref[:, 2 * aw:2 * aw + bw]), c, s1, s2, half)
    q_ref[...] = (q * scale).astype(BF16)
    k = _rope_apply(_dot(h, w_ref[:, 2 * aw + bw:2 * aw + 2 * bw]), c, s1, s2, half)
    kt_ref[...] = k.T
    kb_ref[...] = k.astype(BF16)
    for r in range(tm // MOBA_BLOCK):
        block = k[r * MOBA_BLOCK:(r + 1) * MOBA_BLOCK]
        km_ref[r] = jnp.sum(block, axis=0, keepdims=True) * (1.0 / MOBA_BLOCK)
    v_scr[...] = _dot(h, w_ref[:, 2 * aw + 2 * bw:2 * aw + 3 * bw])
    vt = v_scr[...].T
    vt_ref[...] = vt
    vtb_ref[...] = _with_ones_rows(vt, bw // B_HEADS).astype(BF16)


def _even_in_decode_kernel(x_ref, g_ref, w_ref, c_ref, s1_ref, s2_ref, gs_ref, w00_ref, b0_ref,
                           a_ref, q_ref, k_ref, v_ref, va_ref, *, half, scale):
    aw = A_GROUPS * LANE
    bw = q_ref.shape[1]
    h = _rmsnorm(x_ref[...], g_ref[...]).astype(BF16)
    u, vs = _sgu_values(h, w_ref, gs_ref, aw)
    va = jnp.concatenate(vs, axis=1)
    va_ref[...] = va
    a_ref[...] = (u * (w00_ref[...] * va + b0_ref[...])).astype(BF16)
    c, s1, s2 = c_ref[...], s1_ref[...], s2_ref[...]
    q = _rope_apply(_dot(h, w_ref[:, 2 * aw:2 * aw + bw]), c, s1, s2, half)
    q_ref[...] = (q * scale).astype(BF16)
    k_ref[...] = _rope_apply(_dot(h, w_ref[:, 2 * aw + bw:2 * aw + 2 * bw]), c, s1, s2, half)
    v_ref[...] = _dot(h, w_ref[:, 2 * aw + 2 * bw:2 * aw + 3 * bw])


def _even_in_prompt(x, seq, g_mix, w_in, tables, half, g_sgu, w_sgu, b_sgu):
    n, d = x.shape
    tm = ROW_TILE
    assert tm % MOBA_BLOCK == 0 and seq % tm == 0
    blocks = tm // MOBA_BLOCK
    bw = (w_in.shape[1] - 2 * A_GROUPS * LANE) // 3
    dh = bw // B_HEADS
    tiles_per_seq = seq // tm
    row = lambda w, dt: (pl.BlockSpec((tm, w), lambda i: (i, 0)), jax.ShapeDtypeStruct((n, w), dt))
    tab = pl.BlockSpec((tm, LANE), lambda i: (i % tiles_per_seq, 0))
    col = lambda h, dt: (pl.BlockSpec((None, h, tm), lambda i: (i // tiles_per_seq, 0, i % tiles_per_seq)),
                         jax.ShapeDtypeStruct((n // seq, h, seq), dt))
    outs = [row(A_GROUPS * LANE, BF16), row(bw, BF16), row(bw, BF16), col(bw, F32), col(bw, F32),
            col(bw + B_HEADS * ONES_ROWS, BF16),
            (pl.BlockSpec((blocks, 1, bw), lambda i: (i, 0, 0)), jax.ShapeDtypeStruct((n // MOBA_BLOCK, 1, bw), F32))]
    bs = jnp.broadcast_to(b_sgu[:, :, None], (A_GROUPS, CHUNK, LANE))
    return pl.pallas_call(
        functools.partial(_even_in_prompt_kernel, half=half, scale=dh ** -0.5),
        grid=(n // tm,),
        in_specs=[pl.BlockSpec((tm, d), lambda i: (i, 0)), _full((1, d)), _full(w_in.shape),
                  tab, tab, tab, _full((1, A_GROUPS * LANE)), _full(w_sgu.shape), _full(bs.shape)],
        out_specs=[o[0] for o in outs],
        out_shape=[o[1] for o in outs],
        scratch_shapes=[pltpu.VMEM((tm, bw), F32)],
        compiler_params=_params("arbitrary"),
        name="even_in_prompt",
    )(x, g_mix.reshape(1, d), w_in, *tables, g_sgu.reshape(1, -1), w_sgu, bs)


def _even_in_decode(x, g_mix, w_in, tables, half, g_sgu, w_sgu, b_sgu):
    n, d = x.shape
    bw = (w_in.shape[1] - 2 * A_GROUPS * LANE) // 3
    dh = bw // B_HEADS
    aw = A_GROUPS * LANE
    w00 = jnp.repeat(w_sgu[:, 0, 0], LANE).reshape(1, aw)
    b0 = jnp.repeat(b_sgu[:, 0], LANE).reshape(1, aw)
    shapes = [(aw, BF16), (bw, BF16), (bw, F32), (bw, F32), (aw, F32)]
    return pl.pallas_call(
        functools.partial(_even_in_decode_kernel, half=half, scale=dh ** -0.5),
        grid=(1,),
        in_specs=[_full((n, d)), _full((1, d)), _full(w_in.shape)] + [_full((1, LANE))] * 3
                 + [_full((1, aw))] * 3,
        out_specs=[_full((n, w)) for w, _ in shapes],
        out_shape=[jax.ShapeDtypeStruct((n, w), dt) for w, dt in shapes],
        compiler_params=_params("arbitrary"),
        name="even_in_decode",
    )(x, g_mix.reshape(1, d), w_in, *tables, g_sgu.reshape(1, -1), w00, b0)


def _moba_prompt_kernel(q_ref, k_ref, vt_ref, km_ref, o_ref, sel_ref, m_ref, acc_ref, s_scr, *, dh):
    blk = MOBA_BLOCK
    tq = q_ref.shape[0]
    i = pl.program_id(1)
    nblk = km_ref.shape[1]
    heads = q_ref.shape[1] // dh
    ext = dh + ONES_ROWS
    per_group = LANE // dh
    lane = lax.broadcasted_iota(jnp.int32, (tq, LANE), 1)
    blk_id = lax.broadcasted_iota(jnp.int32, (nblk, tq), 0)
    neg_inf = jnp.float32(-jnp.inf)
    qms = []
    for h in range(heads):
        grp = h // per_group
        q = q_ref[:, grp * LANE:(grp + 1) * LANE]
        qm = jnp.where(_head_of(lane, dh) == h % per_group, q, jnp.zeros_like(q))
        qms.append(qm)
        kmean = km_ref[0, :, grp * LANE:(grp + 1) * LANE].astype(BF16)
        gate = jnp.where(blk_id < i, _dot_nt(kmean, qm), neg_inf)
        sel = jnp.zeros((nblk, tq), F32)
        for _ in range(MOBA_TOPK):
            best = jnp.max(gate, axis=0, keepdims=True)
            first = jnp.min(jnp.where(gate == best, blk_id, nblk), axis=0, keepdims=True)
            hit = blk_id == first
            sel = jnp.where(hit, 1.0, sel)
            gate = jnp.where(hit, neg_inf, gate)
        sel_ref[h] = jnp.where(blk_id < i, sel, 0.0)

    causal = (lax.broadcasted_iota(jnp.int32, (blk, tq), 0)
              <= lax.broadcasted_iota(jnp.int32, (blk, tq), 1))

    head_row = lax.broadcasted_iota(jnp.int32, (heads, tq), 0)

    def score_matmuls(block):
        start = pl.multiple_of(block * blk, blk)
        kb = k_ref[pl.ds(start, blk), :]
        return [_dot_nt(kb[:, (h // per_group) * LANE:(h // per_group + 1) * LANE], qms[h])
                for h in range(heads)]

    def attend(scores, vtb, keeps, first):
        if not first:
            m_old = m_ref[...]
            acc_old = [acc_ref[h] for h in range(heads)]
        m_all = jnp.zeros((heads, tq), F32)
        probs, alphas = [], []
        for h in range(heads):
            s = jnp.where(keeps[h], scores[h], neg_inf)
            s_max = jnp.max(s, axis=0, keepdims=True)
            m_new = s_max if first else jnp.maximum(m_old[h:h + 1, :], s_max)
            if not first:
                alphas.append(jnp.exp(m_old[h:h + 1, :] - m_new))
            probs.append(jnp.exp(s - m_new).astype(BF16))
            m_all = jnp.where(head_row == h, m_new, m_all)
        pvs = [_dot(vtb[h * ext:(h + 1) * ext, :], probs[h]) for h in range(heads)]
        m_ref[...] = m_all
        for h in range(heads):
            acc_ref[h] = pvs[h] if first else acc_old[h] * alphas[h] + pvs[h]

    start = pl.multiple_of(i * blk, blk)
    attend(score_matmuls(i), vt_ref[:, pl.ds(start, blk)], [causal] * heads, True)

    for h, s in enumerate(score_matmuls(0)):
        s_scr[h] = s

    def body(n, _):
        nxt = score_matmuls(jnp.minimum(n + 1, jnp.maximum(i - 1, 0)))
        start = pl.multiple_of(n * blk, blk)
        keeps = [sel_ref[h, pl.ds(n, 1), :] > 0.0 for h in range(heads)]
        attend([s_scr[h] for h in range(heads)], vt_ref[:, pl.ds(start, blk)], keeps, False)
        for h in range(heads):
            s_scr[h] = nxt[h]
        return 0

    lax.fori_loop(0, i, body, 0)
    outs = [acc_ref[h, 0:dh, :] / acc_ref[h, dh:dh + 1, :] for h in range(heads)]
    o_ref[...] = jnp.concatenate(outs, axis=0).T.astype(BF16)


def _moba_prompt(q, kb, vt, kmean, batch, seq, dh):
    n, bw = q.shape
    tq = MOBA_BLOCK
    nblk = seq // MOBA_BLOCK
    heads = bw // dh
    km = kmean.reshape(batch, nblk, bw)
    tiles = seq // tq
    return pl.pallas_call(
        functools.partial(_moba_prompt_kernel, dh=dh),
        grid=(batch, tiles),
        in_specs=[pl.BlockSpec((tq, bw), lambda b, i: (b * tiles + i, 0)),
                  pl.BlockSpec((seq, bw), lambda b, i: (b, 0)),
                  pl.BlockSpec((None, vt.shape[1], seq), lambda b, i: (b, 0, 0)),
                  pl.BlockSpec((1, nblk, bw), lambda b, i: (b, 0, 0))],
        out_specs=pl.BlockSpec((tq, bw), lambda b, i: (b * tiles + i, 0)),
        out_shape=jax.ShapeDtypeStruct((n, bw), BF16),
        scratch_shapes=[pltpu.VMEM((heads, nblk, tq), F32), pltpu.VMEM((heads, tq), F32),
                        pltpu.VMEM((heads, dh + ONES_ROWS, tq), F32), pltpu.VMEM((heads, MOBA_BLOCK, tq), F32)],
        compiler_params=_params("arbitrary", "arbitrary"),
        name="moba_prompt",
    )(q, kb, vt, km)


def _residual_proj_kernel(*refs, n_in, norm):
    x_ref = refs[0]
    in_refs = refs[1:1 + n_in]
    w_ref = refs[1 + n_in]
    y = x_ref[...]
    lo = 0
    for r in in_refs:
        y = y + _dot(r[...], w_ref[lo:lo + r.shape[1], :])
        lo += r.shape[1]
    if norm == "final":
        g_ref, o_ref = refs[2 + n_in:]
        o_ref[...] = _rmsnorm(y, g_ref[...])
    else:
        g_ref, o_ref, h_ref = refs[2 + n_in:]
        o_ref[...] = y
        h_ref[...] = _rmsnorm(y, g_ref[...]).astype(BF16)


def _residual_proj(x, ins, w, g, norm, tm, name):
    n, d = x.shape
    tm = min(tm, n)
    row = lambda width: pl.BlockSpec((tm, width), lambda i: (i, 0))
    out_specs = [row(d)] if norm == "final" else [row(d), row(d)]
    out_shape = ([jax.ShapeDtypeStruct((n, d), F32)] if norm == "final"
                 else [jax.ShapeDtypeStruct((n, d), F32), jax.ShapeDtypeStruct((n, d), BF16)])
    return pl.pallas_call(
        functools.partial(_residual_proj_kernel, n_in=len(ins), norm=norm),
        grid=(n // tm,),
        in_specs=[row(d)] + [row(a.shape[1]) for a in ins] + [_full(w.shape), _full((1, d))],
        out_specs=out_specs,
        out_shape=out_shape,
        compiler_params=_params("arbitrary"),
        name=name,
    )(x, *ins, w, g.reshape(1, d))


FFN_COLS = 256
FFN_HALO = SUBLANE_BF16


def _silu_gate(gate, value):
    return gate * (1.0 / (1.0 + jnp.exp(-gate))) * value


def _ffn_up_prompt_kernel(h_ref, halo_ref, w_ref, wc_ref, bc_ref, act_ref, tail_ref, up_scr, *, tiles_per_seq):
    tm = h_ref.shape[0]
    dff = act_ref.shape[1]
    seq_start = pl.program_id(0) % tiles_per_seq == 0
    halo = halo_ref[...]
    halo = jnp.where(seq_start, jnp.zeros_like(halo), halo)
    hh = jnp.concatenate([halo, h_ref[...]], axis=0)
    for c in range(dff // FFN_COLS):
        halves = []
        for k, lo in enumerate((c * FFN_COLS, dff + c * FFN_COLS)):
            cols = slice(lo, lo + FFN_COLS)
            up = _dot(hh, w_ref[:, cols])
            tail_ref[0, :, cols] = up[tm + FFN_HALO - SUBLANE:, :]
            up_scr[k] = up
            conv = bc_ref[:, cols] + up[FFN_HALO:, :] * wc_ref[CONV_W - 1:CONV_W, cols]
            for j in range(1, CONV_W):
                conv = conv + up_scr[k, pl.ds(FFN_HALO - j, tm), :] * wc_ref[CONV_W - 1 - j:CONV_W - j, cols]
            halves.append(conv)
        act_ref[:, c * FFN_COLS:(c + 1) * FFN_COLS] = _silu_gate(halves[1], halves[0]).astype(BF16)


def _ffn_up_decode_kernel(h_ref, s0_ref, s1_ref, w_ref, wc_ref, bc_ref, act_ref, up_ref):
    dff = act_ref.shape[1]
    h = h_ref[...]
    for c in range(dff // FFN_COLS):
        halves = []
        for lo in (c * FFN_COLS, dff + c * FFN_COLS):
            cols = slice(lo, lo + FFN_COLS)
            up = _dot(h, w_ref[:, cols])
            up_ref[:, cols] = up
            halves.append(bc_ref[:, cols] + s0_ref[:, cols] * wc_ref[0:1, cols]
                          + s1_ref[:, cols] * wc_ref[1:2, cols] + up * wc_ref[2:3, cols])
        act_ref[:, c * FFN_COLS:(c + 1) * FFN_COLS] = _silu_gate(halves[1], halves[0]).astype(BF16)


def _ffn_up_prompt(h, seq, w_up, w_conv, b_conv, tm):
    n, d = h.shape
    dff = w_up.shape[1] // 2
    assert dff % FFN_COLS == 0 and seq % tm == 0 and tm % FFN_HALO == 0
    ratio = tm // FFN_HALO
    act, tail = pl.pallas_call(
        functools.partial(_ffn_up_prompt_kernel, tiles_per_seq=seq // tm),
        grid=(n // tm,),
        in_specs=[pl.BlockSpec((tm, d), lambda i: (i, 0)),
                  pl.BlockSpec((FFN_HALO, d), lambda i: (jnp.maximum(i * ratio - 1, 0), 0)),
                  _full(w_up.shape), _full(w_conv.shape), _full((1, 2 * dff))],
        out_specs=[pl.BlockSpec((tm, dff), lambda i: (i, 0)),
                   pl.BlockSpec((1, SUBLANE, 2 * dff), lambda i: (i, 0, 0))],
        out_shape=[jax.ShapeDtypeStruct((n, dff), BF16),
                   jax.ShapeDtypeStruct((n // tm, SUBLANE, 2 * dff), F32)],
        scratch_shapes=[pltpu.VMEM((2, tm + FFN_HALO, FFN_COLS), F32)],
        compiler_params=_params("arbitrary"),
        name="ffn_up_prompt",
    )(h, h, w_up, w_conv, b_conv.reshape(1, -1))
    last = tail.reshape(n // seq, seq // tm, SUBLANE, 2 * dff)[:, -1, SUBLANE - (CONV_W - 1):, :]
    return act, last


def _ffn_up_decode(h, state, w_up, w_conv, b_conv):
    n, d = h.shape
    dff = w_up.shape[1] // 2
    act, up = pl.pallas_call(
        _ffn_up_decode_kernel,
        grid=(1,),
        in_specs=[_full((n, d)), _full((n, 2 * dff)), _full((n, 2 * dff)),
                  _full(w_up.shape), _full(w_conv.shape), _full((1, 2 * dff))],
        out_specs=[_full((n, dff)), _full((n, 2 * dff))],
        out_shape=[jax.ShapeDtypeStruct((n, dff), BF16), jax.ShapeDtypeStruct((n, 2 * dff), F32)],
        compiler_params=_params("arbitrary"),
        name="ffn_up_decode",
    )(h, state[:, 0], state[:, 1], w_up, w_conv, b_conv.reshape(1, -1))
    return act, jnp.stack([state[:, 1], up], axis=1)


def _odd_in_kernel(h_ref, w_ref, c64_ref, a64_ref, b64_ref, c128_ref, a128_ref, b128_ref,
                   q_ref, k_ref, v_ref, kb_ref, vt_ref, qi_ref, kif_ref, ki2_ref, wit_ref,
                   *, half64, half128, idx_dh):
    cq = q_ref.shape[1]
    ckv = kb_ref.shape[1]
    ciq = qi_ref.shape[1]
    h = h_ref[...]
    t64 = (c64_ref[...], a64_ref[...], b64_ref[...])
    t128 = (c128_ref[...], a128_ref[...], b128_ref[...])
    q_ref[...] = _rope_apply(_dot(h, w_ref[:, 0:cq]), *t128, half128).astype(BF16)
    k = _rope_apply(_dot(h, w_ref[:, cq:cq + ckv]), *t128, half128)
    kb_ref[...] = k.astype(BF16)
    v = _dot(h, w_ref[:, cq + ckv:cq + 2 * ckv])
    c_dh = ckv // C_KV_HEADS
    for kv in range(C_KV_HEADS):
        k_ref[pl.ds(kv, h.shape[0], stride=C_KV_HEADS), :] = k[:, kv * c_dh:(kv + 1) * c_dh]
        v_ref[pl.ds(kv, h.shape[0], stride=C_KV_HEADS), :] = v[:, kv * c_dh:(kv + 1) * c_dh]
    if vt_ref is not None:
        vt_ref[...] = _with_ones_rows(v.T, ckv // C_KV_HEADS).astype(BF16)
    lo = cq + 2 * ckv
    qi_ref[...] = _rope_apply(_dot(h, w_ref[:, lo:lo + ciq]), *t64, half64).astype(BF16)
    tail = _dot(h, w_ref[:, lo + ciq:lo + ciq + LANE])
    ki = _rope_apply(tail, *t64, half64)
    kif_ref[...] = ki
    lane = lax.broadcasted_iota(jnp.int32, ki.shape, 1)
    ki2_ref[...] = jnp.where(lane < idx_dh, ki, pltpu.roll(ki, idx_dh, 1)).astype(BF16)
    wit_ref[...] = tail.T[idx_dh:idx_dh + 8, :] * (IDX_HEADS ** -0.5)


def _odd_in(h, seq, w_in, tabs64, half64, tabs128, half128, c_dh, idx_dh, tm, with_vt):
    n, d = h.shape
    tm = min(tm, n)
    cq = C_HEADS * c_dh
    ckv = C_KV_HEADS * c_dh
    ciq = IDX_HEADS * idx_dh
    cols = cq + 2 * ckv + ciq + LANE
    assert 2 * idx_dh == LANE and w_in.shape[1] <= cols
    w = jnp.pad(w_in, ((0, 0), (0, cols - w_in.shape[1]))).astype(BF16)
    if seq is None:
        tab = pl.BlockSpec((1, LANE), lambda i: (0, 0))
    else:
        tiles_per_seq = seq // tm
        tab = pl.BlockSpec((tm, LANE), lambda i: (i % tiles_per_seq, 0))
    row = lambda width, dt: (pl.BlockSpec((tm, width), lambda i: (i, 0)), jax.ShapeDtypeStruct((n, width), dt))
    col = lambda height, dt: (pl.BlockSpec((height, tm), lambda i: (0, i)), jax.ShapeDtypeStruct((height, n), dt))
    pair_rows = (pl.BlockSpec((tm * C_KV_HEADS, c_dh), lambda i: (i, 0)),
                 jax.ShapeDtypeStruct((n * C_KV_HEADS, c_dh), F32))
    outs = [row(cq, BF16), pair_rows, pair_rows, row(ckv, BF16)]
    if with_vt:
        outs.append(col(ckv + C_KV_HEADS * ONES_ROWS, BF16))
    outs += [row(ciq, BF16), row(LANE, F32), row(LANE, BF16), col(8, F32)]

    def body(*refs):
        refs = list(refs)
        if not with_vt:
            refs.insert(8 + 4, None)
        _odd_in_kernel(*refs, half64=half64, half128=half128, idx_dh=idx_dh)

    res = pl.pallas_call(
        body,
        grid=(n // tm,),
        in_specs=[pl.BlockSpec((tm, d), lambda i: (i, 0)), _full(w.shape)] + [tab] * 6,
        out_specs=[o[0] for o in outs],
        out_shape=[o[1] for o in outs],
        compiler_params=_params("arbitrary"),
        name="odd_in_prompt" if with_vt else "odd_in_decode",
    )(h, w, *tabs64, *tabs128)
    res = list(res)
    if not with_vt:
        res.insert(4, None)
    return res


FLOAT_BITS = 32
INDEX_BIG = 2 ** 30


def _ind(cond):
    return jnp.where(cond, 1.0, 0.0)


def _ordered_bits(x, to):
    b = x if x.dtype == jnp.int32 else lax.bitcast_convert_type(x, jnp.int32)
    b = jnp.where(b < 0, b ^ 0x7FFFFFFF, b)
    return b if to == jnp.int32 else lax.bitcast_convert_type(b, F32)


def _ordered_bits16(x):
    b = lax.shift_right_arithmetic(lax.bitcast_convert_type(x, jnp.int32), 16)
    return jnp.where(b < 0, b ^ 0x7FFF, b)


def _from_ordered_bits16(k):
    b = jnp.where(k < 0, k ^ 0x7FFF, k)
    return lax.bitcast_convert_type(lax.shift_left(b, 16), F32)


def _bf16_floor(x):
    b = lax.bitcast_convert_type(x, jnp.int32)
    low = b & 0xFFFF
    t = b - low
    t = jnp.where(b < 0, jnp.where(low != 0, t + 0x10000, t), t)
    return lax.bitcast_convert_type(t, F32)


def _bisect_threshold(count_fn, shape, keep, coarse_count_fn=None):
    keep_f = jnp.float32(keep)

    def bisect(passes, pivot_count, st):
        def step(_, st):
            lo_k, hi_k, n_hi = st
            mid_k = (lax.shift_right_arithmetic(lo_k, 1) + lax.shift_right_arithmetic(hi_k, 1)
                     + (lo_k & hi_k & 1))
            cnt = pivot_count(mid_k)
            raise_lo = jnp.logical_and(mid_k > lo_k, cnt >= keep_f)
            lower_hi = jnp.logical_and(mid_k > lo_k, cnt < keep_f)
            return (jnp.where(raise_lo, mid_k, lo_k), jnp.where(lower_hi, mid_k, hi_k),
                    jnp.where(lower_hi, cnt, n_hi))
        return lax.fori_loop(0, passes, step, st)

    def fine_count(mid_k):
        return count_fn(lambda s, idx: (_ind(s >= _ordered_bits(mid_k, F32)),))[0]

    lo = jnp.full(shape, -jnp.inf, F32)
    hi = jnp.full(shape, jnp.inf, F32)
    n_hi = jnp.zeros(shape, F32)
    if coarse_count_fn is None:
        lo_k, _, n_gt = bisect(FLOAT_BITS, fine_count,
                               (_ordered_bits(lo, jnp.int32), _ordered_bits(hi, jnp.int32) + 1, n_hi))
    else:
        half = FLOAT_BITS // 2
        coarse = lambda mid_h: coarse_count_fn(_from_ordered_bits16(mid_h).astype(BF16))
        lo_h, _, n_hi = bisect(half, coarse, (_ordered_bits16(lo), _ordered_bits16(hi) + 1, n_hi))
        st = (_ordered_bits(_from_ordered_bits16(lo_h), jnp.int32),
              _ordered_bits(_from_ordered_bits16(lo_h + 1), jnp.int32), n_hi)
        lo_k, _, n_gt = bisect(half, fine_count, st)
    return _ordered_bits(lo_k, F32), n_gt


def _select_threshold(count_fn, active, keep, index_bits):
    thr, n_gt = _bisect_threshold(count_fn, active.shape, keep)
    n_eq, = count_fn(lambda s, idx: (_ind(s == thr),))
    need = jnp.float32(keep) - n_gt
    excess = jnp.where(active, jnp.where(n_eq > need, 1, 0), 0)

    def tie_search():
        lo_i = jnp.full(thr.shape, -1, jnp.int32)
        hi_i = jnp.full(thr.shape, (1 << index_bits) - 1, jnp.int32)
        for _ in range(index_bits + 1):
            mid_i = lax.shift_right_arithmetic(lo_i + hi_i, 1)
            n_low, = count_fn(lambda s, idx: (jnp.where(s == thr, _ind(idx <= mid_i), 0.0),))
            ok = n_low >= need
            hi_i = jnp.where(ok, mid_i, hi_i)
            lo_i = jnp.where(ok, lo_i, mid_i)
        return hi_i

    cut = lax.cond(jnp.max(excess) > 0, tie_search, lambda: jnp.full(thr.shape, INDEX_BIG, jnp.int32))
    thr = jnp.where(active, thr, -jnp.inf)
    cut = jnp.where(active, cut, -1)
    return thr, cut


def _selected(s, idx, thr, cut):
    return jnp.where(s > thr, 1.0, jnp.where(s == thr, _ind(idx <= cut), 0.0))


def _dsa_prompt_kernel(q_ref, qi_ref, wit_ref, kb_ref, vt_ref, ki2_ref, o_ref,
                       sc_ref, sb_ref, m_ref, acc_ref, *, c_dh, idx_dh, keep, kc):
    tq = q_ref.shape[0]
    seq = kb_ref.shape[0]
    t0 = pl.program_id(1) * tq
    nch = (t0 + tq + kc - 1) // kc
    scale = c_dh ** -0.5
    neg_inf = jnp.float32(-jnp.inf)
    lane = lax.broadcasted_iota(jnp.int32, (tq, LANE), 1)
    key_iota = lax.broadcasted_iota(jnp.int32, (kc, tq), 0)
    qpos = t0 + lax.broadcasted_iota(jnp.int32, (1, tq), 1)

    qi = qi_ref[...]
    per_group = LANE // idx_dh
    qms = []
    for h in range(IDX_HEADS):
        grp = qi[:, (h // per_group) * LANE:(h // per_group + 1) * LANE]
        qms.append(jnp.where(_head_of(lane, idx_dh) == h % per_group, grp, jnp.zeros_like(grp)))
    wit = wit_ref[...]

    def score_chunk(c, _):
        start = pl.multiple_of(c * kc, kc)
        kic = ki2_ref[pl.ds(start, kc), :]
        acc = jnp.zeros((kc, tq), F32)
        for h in range(IDX_HEADS):
            acc = acc + jnp.maximum(_dot_nt(kic, qms[h]), 0.0) * wit[h:h + 1, :]
        score = acc * (idx_dh ** -0.5)
        adm = start + key_iota <= qpos
        masked = jnp.where(adm, score, neg_inf)
        sc_ref[pl.ds(start, kc), :] = masked
        sb_ref[pl.ds(start, kc), :] = _bf16_floor(masked).astype(BF16)
        return 0

    lax.fori_loop(0, nch, score_chunk, 0)

    def count_fn(pred, n_out=1):
        def body(c, cnts):
            start = pl.multiple_of(c * kc, kc)
            hits = pred(sc_ref[pl.ds(start, kc), :], start + key_iota)
            return tuple(cnt + jnp.sum(hit.reshape(kc // SUBLANE, SUBLANE, tq), axis=0) for cnt, hit in zip(cnts, hits))
        parts = lax.fori_loop(0, nch, body, tuple(jnp.zeros((SUBLANE, tq), F32) for _ in range(n_out)))
        return tuple(jnp.sum(part, axis=0, keepdims=True) for part in parts)

    def coarse_count_fn(pivot):
        one, zero = jnp.ones((), BF16), jnp.zeros((), BF16)
        rows = SUBLANE_BF16

        def body(c, cnt):
            hit = jnp.where(sb_ref[pl.ds(pl.multiple_of(c * kc, kc), kc), :] >= pivot, one, zero)
            part = hit[0:rows]
            for r in range(1, kc // rows):
                part = part + hit[r * rows:(r + 1) * rows]
            return cnt + part.astype(F32)
        part = lax.fori_loop(0, nch, body, jnp.zeros((rows, tq), F32))
        return jnp.sum(part, axis=0, keepdims=True)

    thr, n_gt = _bisect_threshold(count_fn, (1, tq), keep, coarse_count_fn)
    active = qpos + 1 > keep
    thr = jnp.where(active, thr, neg_inf)
    need = jnp.where(active, jnp.float32(keep) - n_gt, 0.0)
    prefix_ones = jnp.where(lax.broadcasted_iota(jnp.int32, (kc, kc), 0)
                            >= lax.broadcasted_iota(jnp.int32, (kc, kc), 1), 1.0, 0.0).astype(BF16)

    m_ref[...] = jnp.full(m_ref.shape, NEG_BIG, F32)
    acc_ref[...] = jnp.zeros(acc_ref.shape, F32)
    group = C_HEADS // C_KV_HEADS
    ext = c_dh + ONES_ROWS
    exp2_scale = scale * LOG2_E
    head_row = lax.broadcasted_iota(jnp.int32, m_ref.shape, 0)

    def attend_chunk(c, ties_before):
        start = pl.multiple_of(c * kc, kc)
        sc = sc_ref[pl.ds(start, kc), :]
        tie = _ind(sc == thr)
        tie_rank = ties_before + _dot(prefix_ones, tie.astype(BF16))
        keepm = jnp.where(sc > thr, 1.0, jnp.where(tie_rank <= need, tie, 0.0)) > 0.0
        ties_before = tie_rank[kc - 1:kc, :]
        kbc = kb_ref[pl.ds(start, kc), :]
        vtc = vt_ref[:, pl.ds(start, kc)]
        m_old = m_ref[...]
        m_all = jnp.zeros(m_old.shape, F32)
        scores = [_dot_nt(kbc[:, (h // group) * c_dh:(h // group + 1) * c_dh], q_ref[:, h * c_dh:(h + 1) * c_dh])
                  for h in range(C_HEADS)]
        probs, alphas = [], []
        for h in range(C_HEADS):
            s = jnp.where(keepm, scores[h], neg_inf)
            m_new = jnp.maximum(m_old[h:h + 1, :], jnp.max(s, axis=0, keepdims=True))
            alphas.append(jnp.exp2((m_old[h:h + 1, :] - m_new) * exp2_scale))
            probs.append(jnp.exp2((s - m_new) * exp2_scale).astype(BF16))
            m_all = jnp.where(head_row == h, m_new, m_all)
        pvs = [_dot(vtc[(h // group) * ext:(h // group + 1) * ext, :], probs[h]) for h in range(C_HEADS)]
        m_ref[...] = m_all
        for h in range(C_HEADS):
            acc_ref[h] = acc_ref[h] * alphas[h] + pvs[h]
        return ties_before

    lax.fori_loop(0, nch, attend_chunk, jnp.zeros((1, tq), F32))
    for h in range(C_HEADS):
        out_t = acc_ref[h, 0:c_dh, :] / acc_ref[h, c_dh:c_dh + 1, :]
        o_ref[:, h * c_dh:(h + 1) * c_dh] = out_t.T.astype(BF16)


def _dsa_prompt(q, qi, wit, kb, vt, ki2, batch, seq, c_dh, idx_dh, keep, tq, kc):
    n = q.shape[0]
    tiles = seq // tq
    assert c_dh == LANE and seq % kc == 0 and seq % tq == 0
    return pl.pallas_call(
        functools.partial(_dsa_prompt_kernel, c_dh=c_dh, idx_dh=idx_dh, keep=keep, kc=kc),
        grid=(batch, tiles),
        in_specs=[pl.BlockSpec((tq, q.shape[1]), lambda b, i: (b * tiles + i, 0)),
                  pl.BlockSpec((tq, qi.shape[1]), lambda b, i: (b * tiles + i, 0)),
                  pl.BlockSpec((SUBLANE, tq), lambda b, i: (0, b * tiles + i)),
                  pl.BlockSpec((seq, kb.shape[1]), lambda b, i: (b, 0)),
                  pl.BlockSpec((vt.shape[0], seq), lambda b, i: (0, b)),
                  pl.BlockSpec((seq, LANE), lambda b, i: (b, 0))],
        out_specs=pl.BlockSpec((tq, q.shape[1]), lambda b, i: (b * tiles + i, 0)),
        out_shape=jax.ShapeDtypeStruct(q.shape, BF16),
        scratch_shapes=[pltpu.VMEM((seq, tq), F32), pltpu.VMEM((seq, tq), BF16), pltpu.VMEM((SUBLANE, tq), F32),
                        pltpu.VMEM((C_HEADS, c_dh + ONES_ROWS, tq), F32)],
        compiler_params=_params("arbitrary", "arbitrary"),
        name="dsa_prompt",
    )(q, qi, wit, kb, vt, ki2)


PAGE_ROWS = 128
DECODE_SCORE_SEQS = 4
DECODE_ATTN_SEQS = 2


def _lane_bcast_col(row):
    return jnp.broadcast_to(row, (LANE, row.shape[1])).T


def _moba_decode_kernel(pt_ref, q_ref, kn_ref, vn_ref, *rest, n_pages, dh):
    kt_refs, vt_refs, o_ref = rest[:n_pages], rest[n_pages:2 * n_pages], rest[2 * n_pages]
    bw = q_ref.shape[2]
    heads = bw // dh
    ppb = MOBA_BLOCK // PAGE_ROWS
    nblk = n_pages // ppb
    neg_inf = jnp.float32(-jnp.inf)
    qb = _lane_bcast_col(q_ref[0].astype(F32))

    def head_sums(x):
        return jnp.sum(x.reshape(heads, dh, LANE), axis=1)

    def head_rows(x):
        return jnp.broadcast_to(x[:, None, :], (heads, dh, LANE)).reshape(bw, LANE)

    s_pages = [head_sums(kt_refs[j][...] * qb) for j in range(n_pages)]
    s_own = head_sums(_lane_bcast_col(kn_ref[0]) * qb)
    gates = []
    for n in range(nblk):
        g = jnp.zeros((heads, 1), F32)
        for j in range(n * ppb, (n + 1) * ppb):
            g = g + jnp.sum(s_pages[j], axis=1, keepdims=True)
        gates.append(g)
    sels = []
    for n in range(nblk):
        rank = jnp.zeros((heads, 1), F32)
        for m in range(nblk):
            if m != n:
                beats = gates[m] >= gates[n] if m < n else gates[m] > gates[n]
                rank = rank + jnp.where(beats, 1.0, 0.0)
        sels.append(rank < MOBA_TOPK)
    m = s_own
    masked = []
    for j in range(n_pages):
        sm = jnp.where(sels[j // ppb], s_pages[j], neg_inf)
        masked.append(sm)
        m = jnp.maximum(m, jnp.max(sm, axis=1, keepdims=True))
    e_own = jnp.exp(s_own - m)
    es = [jnp.exp(sm - m) for sm in masked]
    l = e_own
    for e in es:
        l = l + jnp.sum(e, axis=1, keepdims=True)
    rinv = 1.0 / l
    lane = lax.broadcasted_iota(jnp.int32, (bw, LANE), 1)
    acc = jnp.where(lane == 0, head_rows(e_own * rinv) * _lane_bcast_col(vn_ref[0]), 0.0)
    for j in range(n_pages):
        acc = acc + head_rows(es[j] * rinv) * vt_refs[j][...]
    out_col = jnp.sum(acc, axis=1, keepdims=True)
    o_ref[0] = jnp.broadcast_to(out_col, (bw, LANE)).T[0:1].astype(BF16)


def _moba_decode(q, k_new, v_new, cache_k, cache_v, page_table, dh):
    n, bw = q.shape
    n_pages = page_table.shape[1]
    assert cache_k.shape[1] == PAGE_ROWS == LANE and (n_pages * PAGE_ROWS) % MOBA_BLOCK == 0
    ckt = jnp.transpose(cache_k, (0, 2, 3, 1)).reshape(cache_k.shape[0], bw, PAGE_ROWS)
    cvt = jnp.transpose(cache_v, (0, 2, 3, 1)).reshape(cache_v.shape[0], bw, PAGE_ROWS)
    seq_row = pl.BlockSpec((1, 1, bw), lambda b, pt: (b, 0, 0))
    pages = [pl.BlockSpec((None, bw, PAGE_ROWS), lambda b, pt, j=j: (pt[b, j], 0, 0)) for j in range(n_pages)]
    out = pl.pallas_call(
        functools.partial(_moba_decode_kernel, n_pages=n_pages, dh=dh),
        grid_spec=pltpu.PrefetchScalarGridSpec(
            num_scalar_prefetch=1, grid=(n,),
            in_specs=[seq_row, seq_row, seq_row] + pages + pages,
            out_specs=seq_row),
        out_shape=jax.ShapeDtypeStruct((n, 1, bw), BF16),
        compiler_params=_params("arbitrary"),
        name="moba_decode",
    )(page_table, q.reshape(n, 1, bw), k_new.reshape(n, 1, bw), v_new.reshape(n, 1, bw),
      *([ckt] * n_pages), *([cvt] * n_pages))
    return out.reshape(n, bw)


def _dsa_decode_score_kernel(pt_ref, qi_ref, wi_ref, kin_ref, *rest, n_pages, idx_dh, seqs):
    for g in range(seqs):
        _dsa_decode_score_seq(g, qi_ref, wi_ref, kin_ref, rest[g * n_pages:(g + 1) * n_pages], rest[-1],
                              n_pages, idx_dh)


def _dsa_decode_score_seq(g, qi_ref, wi_ref, kin_ref, kit_refs, o_ref, n_pages, idx_dh):
    qb = _lane_bcast_col(qi_ref[g].astype(F32))
    wi8 = wi_ref[g]
    scale = idx_dh ** -0.5

    def score(kit):
        total = jnp.zeros((1, LANE), F32)
        for h in range(IDX_HEADS):
            sc = jnp.sum(kit * qb[h * idx_dh:(h + 1) * idx_dh], axis=0, keepdims=True)
            total = total + jnp.maximum(sc, 0.0) * wi8[h:h + 1, :]
        return total * scale

    o_ref[g] = jnp.full(o_ref.shape[1:], -jnp.inf, F32)
    for j in range(n_pages):
        o_ref[g, j:j + 1, :] = score(kit_refs[j][...])
    own = score(_lane_bcast_col(kin_ref[g]))
    lane = lax.broadcasted_iota(jnp.int32, (1, PAGE_ROWS), 1)
    o_ref[g, n_pages:n_pages + 1, :] = jnp.where(lane == 0, own, -jnp.inf)


def _dsa_decode_select_kernel(s_ref, m_ref, *, keep, n_valid):
    s = s_ref[...]
    idx = lax.broadcasted_iota(jnp.int32, s.shape, 1)
    count_fn = lambda pred, n_out=1: tuple(jnp.sum(hit, axis=1, keepdims=True) for hit in pred(s, idx))
    active = jnp.full((s.shape[0], 1), n_valid > keep, jnp.bool_)
    thr, cut = _select_threshold(count_fn, active, keep, (s.shape[1] - 1).bit_length())
    m_ref[...] = _selected(s, idx, thr, cut)


def _dsa_decode_attn_kernel(pt_ref, q_ref, kn_ref, vn_ref, mask_ref, *rest, n_pages, c_dh, seqs):
    staged = [_dsa_decode_attn_scores(g, q_ref, kn_ref, mask_ref, rest[g * n_pages:(g + 1) * n_pages],
                                      n_pages, c_dh) for g in range(seqs)]
    for g in range(seqs):
        _dsa_decode_attn_values(g, staged[g], vn_ref, rest[(seqs + g) * n_pages:(seqs + g + 1) * n_pages],
                                rest[-1], n_pages, c_dh)


def _dsa_decode_attn_scores(g, q_ref, kn_ref, mask_ref, k_refs, n_pages, c_dh):
    rows = k_refs[0].shape[0]
    group = C_HEADS // C_KV_HEADS
    scale = c_dh ** -0.5
    neg_inf = jnp.float32(-jnp.inf)
    qm = jnp.concatenate([q_ref[g].astype(F32), jnp.zeros((LANE - C_HEADS, c_dh), F32)], axis=0).astype(BF16)
    row_id = lax.broadcasted_iota(jnp.int32, (rows, LANE), 0)
    lane_id = lax.broadcasted_iota(jnp.int32, (rows, LANE), 1)
    own_kv = (row_id & (C_KV_HEADS - 1)) == _head_of(lane_id, group)
    mask = mask_ref[g]
    mask_t = jnp.concatenate([mask, jnp.zeros((LANE - mask.shape[0], PAGE_ROWS), F32)], axis=0).T
    repeat = jnp.where(_head_of(row_id, C_KV_HEADS) == lane_id, 1.0, 0.0).astype(BF16)
    row_mask = _dot(repeat, mask_t.astype(BF16))
    kn = jnp.concatenate([kn_ref[g], jnp.zeros((8 - C_KV_HEADS, c_dh), F32)], axis=0)
    s_own_kv = _dot_nt(kn.astype(BF16), qm)
    lane_kv = _head_of(lax.broadcasted_iota(jnp.int32, (1, LANE), 1), group)
    s_own = jnp.zeros((1, LANE), F32)
    for kv in range(C_KV_HEADS):
        s_own = jnp.where(lane_kv == kv, s_own_kv[kv:kv + 1, :], s_own)
    s_own = jnp.where(mask[n_pages:n_pages + 1, 0:1] > 0.0, s_own, neg_inf)
    m = jnp.maximum(s_own, NEG_BIG)
    masked = []
    for j in range(n_pages):
        s = _dot_nt(k_refs[j][...].astype(BF16), qm)
        sm = jnp.where(own_kv, jnp.where(row_mask[:, j:j + 1] > 0.5, s, neg_inf), neg_inf)
        masked.append(sm)
        m = jnp.maximum(m, jnp.max(sm, axis=0, keepdims=True))
    return masked, s_own, m, lane_kv


def _dsa_decode_attn_values(g, staged, vn_ref, v_refs, o_ref, n_pages, c_dh):
    masked, s_own, m, lane_kv = staged
    scale = c_dh ** -0.5
    own_rows = lax.broadcasted_iota(jnp.int32, (SUBLANE_BF16, LANE), 0)
    e_own = jnp.where(own_rows == lane_kv, jnp.exp((s_own - m) * scale), 0.0)
    v_own = jnp.concatenate([vn_ref[g], jnp.zeros((SUBLANE_BF16 - C_KV_HEADS, c_dh), F32)], axis=0)
    e_all = jnp.concatenate([jnp.exp((sm - m) * scale) for sm in masked] + [e_own], axis=0).astype(BF16)
    v_all = jnp.concatenate([v_refs[j][...] for j in range(n_pages)] + [v_own], axis=0).astype(BF16)
    v_all = jnp.concatenate([v_all, jnp.ones(v_all.shape, BF16)], axis=1)
    out = lax.dot_general(e_all, v_all, (((0,), (0,)), ((), ())), preferred_element_type=F32)
    o_ref[g] = (out[0:C_HEADS, 0:c_dh] / out[0:C_HEADS, c_dh:2 * c_dh]).astype(BF16)


def _dsa_decode(q, qi, wit, k_new, v_new, ki_new, cache_k, cache_v, cache_ki, page_table, c_dh, idx_dh, keep):
    n = q.shape[0]
    n_pages = page_table.shape[1]
    assert cache_k.shape[1] == PAGE_ROWS == LANE
    pool = cache_k.shape[0]
    ck = cache_k.reshape(pool, PAGE_ROWS * C_KV_HEADS, c_dh)
    cv = cache_v.reshape(pool, PAGE_ROWS * C_KV_HEADS, c_dh)
    ckit = jnp.transpose(cache_ki, (0, 2, 1))
    rows = -(-(n_pages + 1) // SUBLANE) * SUBLANE
    wi8 = wit.T.reshape(n, 8, 1)

    def seq3(g, a, b):
        return pl.BlockSpec((g, a, b), lambda s, pt: (s, 0, 0))

    def pages(g, r, w):
        return [pl.BlockSpec((None, r, w), lambda s, pt, j=j, k=k: (pt[s * g + k, j], 0, 0))
                for k in range(g) for j in range(n_pages)]

    gs = DECODE_SCORE_SEQS
    assert n % gs == 0 and n % DECODE_ATTN_SEQS == 0
    scores = pl.pallas_call(
        functools.partial(_dsa_decode_score_kernel, n_pages=n_pages, idx_dh=idx_dh, seqs=gs),
        grid_spec=pltpu.PrefetchScalarGridSpec(
            num_scalar_prefetch=1, grid=(n // gs,),
            in_specs=[seq3(gs, 1, IDX_HEADS * idx_dh), seq3(gs, 8, 1), seq3(gs, 1, idx_dh)]
                     + pages(gs, idx_dh, PAGE_ROWS),
            out_specs=seq3(gs, rows, PAGE_ROWS)),
        out_shape=jax.ShapeDtypeStruct((n, rows, PAGE_ROWS), F32),
        compiler_params=_params("arbitrary"),
        name="dsa_decode_scores",
    )(page_table, qi.reshape(n, 1, IDX_HEADS * idx_dh), wi8, ki_new.reshape(n, 1, idx_dh),
      *([ckit] * (gs * n_pages)))
    flat = rows * PAGE_ROWS
    mask = pl.pallas_call(
        functools.partial(_dsa_decode_select_kernel, keep=keep, n_valid=n_pages * PAGE_ROWS + 1),
        grid=(1,),
        in_specs=[_full((n, flat))],
        out_specs=_full((n, flat)),
        out_shape=jax.ShapeDtypeStruct((n, flat), F32),
        compiler_params=_params("arbitrary"),
        name="dsa_decode_select",
    )(scores.reshape(n, flat))
    cq = C_HEADS * c_dh
    ga = DECODE_ATTN_SEQS
    kv_rows = PAGE_ROWS * C_KV_HEADS
    out = pl.pallas_call(
        functools.partial(_dsa_decode_attn_kernel, n_pages=n_pages, c_dh=c_dh, seqs=ga),
        grid_spec=pltpu.PrefetchScalarGridSpec(
            num_scalar_prefetch=1, grid=(n // ga,),
            in_specs=[seq3(ga, C_HEADS, c_dh), seq3(ga, C_KV_HEADS, c_dh), seq3(ga, C_KV_HEADS, c_dh),
                      seq3(ga, rows, PAGE_ROWS)] + pages(ga, kv_rows, c_dh) + pages(ga, kv_rows, c_dh),
            out_specs=seq3(ga, C_HEADS, c_dh)),
        out_shape=jax.ShapeDtypeStruct((n, C_HEADS, c_dh), BF16),
        compiler_params=_params("arbitrary"),
        name="dsa_decode_attn",
    )(page_table, q.reshape(n, C_HEADS, c_dh), k_new.reshape(n, C_KV_HEADS, c_dh),
      v_new.reshape(n, C_KV_HEADS, c_dh), mask.reshape(n, rows, PAGE_ROWS),
      *([ck] * (ga * n_pages)), *([cv] * (ga * n_pages)))
    return out.reshape(n, cq)


def kernel(x_prompt, x_sample, cache_moba_k, cache_moba_v, cache_dsa_k, cache_dsa_v, cache_dsa_kidx,
           state_ffn_conv, page_table, w_in_even, g_sgu, w_sgu, b_sgu, w_out_even, w_in_odd, w_out_odd,
           g_mix, g_ffn, w_up, w_conv, b_conv, w_down, g_final):
    batch, seq, d = x_prompt.shape
    n_dec, dec_seq, _ = x_sample.shape
    depth = g_mix.shape[0]
    n_pages = page_table.shape[1]
    past_len = n_pages * cache_moba_k.shape[2]
    assert depth == 2 and dec_seq == 1, "one even + one odd layer, one decode token per sequence"
    b_dh = cache_moba_k.shape[-1]
    c_dh = cache_dsa_k.shape[-1]
    idx_dh = cache_dsa_kidx.shape[-1]
    keep_p = min(DSA_TOPK, seq // 4)
    keep_s = min(DSA_TOPK, (past_len + dec_seq) // 4)
    xp = x_prompt.reshape(batch * seq, d)
    xs = x_sample.reshape(n_dec, d)
    bf = lambda w: w.astype(BF16)

    tp64, half64 = _rope_tables(b_dh, seq, 0)
    ts64, _ = _rope_tables(b_dh, 1, past_len)
    tp128, half128 = _rope_tables(c_dh, seq, 0)
    ts128, _ = _rope_tables(c_dh, 1, past_len)
    assert idx_dh == b_dh

    w_in0, w_out0 = bf(w_in_even[0]), bf(w_out_even[0])
    a_p, q_p, kb_p, mkt_p, mvt_p, vt_p, km_p = _even_in_prompt(
        xp, seq, g_mix[0], w_in0, tp64, half64, g_sgu[0], w_sgu[0], b_sgu[0])
    bo_p = _moba_prompt(q_p, kb_p, vt_p, km_p, batch, seq, b_dh)
    cache_view = lambda t: jnp.transpose(t.reshape(batch, B_HEADS, b_dh, seq), (0, 3, 1, 2))[None]
    a_s, q_s, mk_s, mv_s, va_s = _even_in_decode(xs, g_mix[0], w_in0, ts64, half64, g_sgu[0], w_sgu[0], b_sgu[0])
    bo_s = _moba_decode(q_s, mk_s, mv_s, cache_moba_k[0], cache_moba_v[0], page_table, b_dh)
    w_up0, w_down0 = bf(w_up[0]), bf(w_down[0])
    xp, hp = _residual_proj(xp, [a_p, bo_p], w_out0, g_ffn[0], "next", ROW_TILE, "out_even_prompt")
    xs, hs = _residual_proj(xs, [a_s, bo_s], w_out0, g_ffn[0], "next", ROW_TILE, "out_even_decode")
    act_p, conv_p0 = _ffn_up_prompt(hp, seq, w_up0, w_conv[0], b_conv[0], ROW_TILE)
    act_s, conv_s0 = _ffn_up_decode(hs, state_ffn_conv[0], w_up0, w_conv[0], b_conv[0])
    xp, hp = _residual_proj(xp, [act_p], w_down0, g_mix[1], "next", ROW_TILE, "ffn_down0_prompt")
    xs, hs = _residual_proj(xs, [act_s], w_down0, g_mix[1], "next", ROW_TILE, "ffn_down0_decode")

    w_out1 = bf(w_out_odd[0])
    q_p, dk_p, dv_p, kb_p, vt_p, qi_p, kif_p, ki2_p, wit_p = _odd_in(
        hp, seq, w_in_odd[0], tp64, half64, tp128, half128, c_dh, idx_dh, ROW_TILE, True)
    o_p = _dsa_prompt(q_p, qi_p, wit_p, kb_p, vt_p, ki2_p, batch, seq, c_dh, idx_dh, keep_p, DSA_QUERY_TILE, DSA_KEY_CHUNK)
    q_s, dk_s, dv_s, _, _, qi_s, kif_s, _, wit_s = _odd_in(
        hs, None, w_in_odd[0], ts64, half64, ts128, half128, c_dh, idx_dh, 256, False)
    o_s = _dsa_decode(q_s, qi_s, wit_s, dk_s, dv_s, kif_s[:, :idx_dh], cache_dsa_k[0], cache_dsa_v[0],
                      cache_dsa_kidx[0], page_table, c_dh, idx_dh, keep_s)
    w_up1, w_down1 = bf(w_up[1]), bf(w_down[1])
    xp, hp = _residual_proj(xp, [o_p], w_out1, g_ffn[1], "next", ROW_TILE, "out_odd_prompt")
    xs, hs = _residual_proj(xs, [o_s], w_out1, g_ffn[1], "next", ROW_TILE, "out_odd_decode")
    act_p, conv_p1 = _ffn_up_prompt(hp, seq, w_up1, w_conv[1], b_conv[1], ROW_TILE)
    act_s, conv_s1 = _ffn_up_decode(hs, state_ffn_conv[1], w_up1, w_conv[1], b_conv[1])
    y_p, = _residual_proj(xp, [act_p], w_down1, g_final, "final", ROW_TILE, "ffn_down1_prompt")
    y_s, = _residual_proj(xs, [act_s], w_down1, g_final, "final", ROW_TILE, "ffn_down1_decode")

    ckv = C_KV_HEADS
    return (y_p.reshape(batch, seq, d), y_s.reshape(n_dec, dec_seq, d),
            cache_view(mkt_p), cache_view(mvt_p),
            mk_s.reshape(1, n_dec, dec_seq, B_HEADS, b_dh), mv_s.reshape(1, n_dec, dec_seq, B_HEADS, b_dh),
            va_s.reshape(1, n_dec, dec_seq, A_GROUPS, LANE),
            dk_p.reshape(1, batch, seq, ckv, c_dh), dv_p.reshape(1, batch, seq, ckv, c_dh),
            kif_p[:, :idx_dh].reshape(1, batch, seq, idx_dh),
            dk_s.reshape(1, n_dec, dec_seq, ckv, c_dh), dv_s.reshape(1, n_dec, dec_seq, ckv, c_dh),
            kif_s[:, :idx_dh].reshape(1, n_dec, dec_seq, idx_dh),
            jnp.stack([conv_p0, conv_p1]), jnp.stack([conv_s0, conv_s1]))
```

```python
import functools

import numpy as np
import jax
import jax.numpy as jnp
from jax import lax
from jax.experimental import pallas as pl
from jax.experimental.pallas import tpu as pltpu

F32 = jnp.float32
BF16 = jnp.bfloat16

A_GROUPS = 4
CHUNK = 128
B_HEADS = 8
MOBA_BLOCK = 256
MOBA_TOPK = 3
C_HEADS = 8
C_KV_HEADS = 2
IDX_HEADS = 4
DSA_TOPK = 256
ROPE_THETA = 500000.0
ROPE_FRAC = 4
CONV_W = 3
EPS = 1e-6

LANE = 128
SUBLANE = 8
SUBLANE_BF16 = 16
VMEM_LIMIT_BYTES = 56 * 1024 * 1024
NEG_BIG = -1e30
LOG2_E = 1.4426950408889634
ROW_TILE = 512
DSA_QUERY_TILE = 256
DSA_KEY_CHUNK = 256


def _params(*sem):
    return pltpu.CompilerParams(dimension_semantics=sem,
                                vmem_limit_bytes=VMEM_LIMIT_BYTES)


def _full(shape):
    nd = len(shape)
    return pl.BlockSpec(shape, lambda *_: (0,) * nd)


def _rmsnorm(x, g):
    ms = jnp.mean(x * x, axis=-1, keepdims=True)
    return x * lax.rsqrt(ms + EPS) * g


def _rope_apply(x, c, s1, s2, half):
    parts = []
    for j in range(x.shape[1] // LANE):
        xs = x[:, j * LANE:(j + 1) * LANE]
        parts.append(xs * c + pltpu.roll(xs, LANE - half, 1) * s1
                     + pltpu.roll(xs, half, 1) * s2)
    return parts[0] if len(parts) == 1 else jnp.concatenate(parts, axis=1)


def _head_of(index, dh):
    assert dh & (dh - 1) == 0
    return lax.shift_right_logical(index, dh.bit_length() - 1)


ONES_ROWS = SUBLANE_BF16


def _with_ones_rows(vt, dh):
    ones = jnp.ones((ONES_ROWS, vt.shape[1]), vt.dtype)
    parts = []
    for h in range(vt.shape[0] // dh):
        parts += [vt[h * dh:(h + 1) * dh], ones]
    return jnp.concatenate(parts, axis=0)


def _dot(a, b):
    return jnp.dot(a, b, preferred_element_type=F32)


def _dot_nt(a, b):
    return lax.dot_general(a, b, (((1,), (1,)), ((), ())),
                           preferred_element_type=F32)


def _rope_rows(dh):
    rd = dh // ROPE_FRAC
    half = rd // 2
    inv = ROPE_THETA ** (-jnp.arange(half, dtype=F32) * (2.0 / rd))
    r = np.arange(LANE) % dh
    j = np.where(r < half, r, np.where(r < rd, r - half, 0))
    inv_lane = jnp.where(jnp.asarray(r < rd), inv[j], 0.0).reshape(1, LANE)
    s1 = jnp.asarray(np.where(r < half, -1.0, 0.0), F32).reshape(1, LANE)
    s2 = jnp.asarray(np.where((r >= half) & (r < rd), 1.0, 0.0), F32).reshape(1, LANE)
    return inv_lane, s1, s2, half


def _rope_table_kernel(inv_ref, s1_ref, s2_ref, c_ref, a_ref, b_ref, *, base):
    rows = c_ref.shape[0]
    pos = lax.broadcasted_iota(jnp.int32, (rows, LANE), 0) + (base + pl.program_id(0) * rows)
    ang = pos.astype(F32) * inv_ref[...]
    s = jnp.sin(ang)
    c_ref[...] = jnp.cos(ang)
    a_ref[...] = s * s1_ref[...]
    b_ref[...] = s * s2_ref[...]


def _rope_tables(dh, n_pos, base):
    inv_lane, s1, s2, half = _rope_rows(dh)
    rows = min(n_pos, ROW_TILE)
    assert n_pos % rows == 0
    out = jax.ShapeDtypeStruct((n_pos, LANE), F32)
    spec = pl.BlockSpec((rows, LANE), lambda i: (i, 0))
    c, a, b = pl.pallas_call(
        functools.partial(_rope_table_kernel, base=base),
        grid=(n_pos // rows,),
        in_specs=[_full((1, LANE))] * 3,
        out_specs=[spec] * 3,
        out_shape=[out] * 3,
        compiler_params=_params("arbitrary"),
        name="rope_tables",
    )(inv_lane, s1, s2)
    return (c, a, b), half


def _sgu_values(h, w_ref, gs_ref, aw):
    u = jax.nn.gelu(_dot(h, w_ref[:, 0:aw]))
    zv = jax.nn.gelu(_dot(h, w_ref[:, aw:2 * aw]))
    vs = []
    for g in range(A_GROUPS):
        vg = zv[:, g * LANE:(g + 1) * LANE]
        vs.append(_rmsnorm(vg, gs_ref[:, g * LANE:(g + 1) * LANE]))
    return u, vs


def _even_in_prompt_kernel(x_ref, g_ref, w_ref, c_ref, s1_ref, s2_ref, gs_ref, ws_ref, bs_ref,
                           a_ref, q_ref, kb_ref, kt_ref, vt_ref, vtb_ref, km_ref, v_scr, *, half, scale):
    tm = x_ref.shape[0]
    aw = A_GROUPS * LANE
    bw = q_ref.shape[1]
    h = _rmsnorm(x_ref[...], g_ref[...]).astype(BF16)
    u, vs = _sgu_values(h, w_ref, gs_ref, aw)
    tri = (lax.broadcasted_iota(jnp.int32, (CHUNK, CHUNK), 0)
           >= lax.broadcasted_iota(jnp.int32, (CHUNK, CHUNK), 1))
    for g in range(A_GROUPS):
        wg = jnp.where(tri, ws_ref[g], 0.0).astype(BF16)
        for cc in range(tm // CHUNK):
            rows = slice(cc * CHUNK, (cc + 1) * CHUNK)
            mixed = _dot(wg, vs[g][rows].astype(BF16)) + bs_ref[g]
            a_ref[rows, g * LANE:(g + 1) * LANE] = (u[rows, g * LANE:(g + 1) * LANE] * mixed).astype(BF16)
    c, s1, s2 = c_ref[...], s1_ref[...], s2_ref[...]
    q = _rope_apply(_dot(h, w_ref[:, 2 * aw:2 * aw + bw]), c, s1, s2, half)
    q_ref[...] = (q * scale).astype(BF16)
    k = _rope_apply(_dot(h, w_ref[:, 2 * aw + bw:2 * aw + 2 * bw]), c, s1, s2, half)
    kt_ref[...] = k.T
    kb_ref[...] = k.astype(BF16)
    for r in range(tm // MOBA_BLOCK):
        block = k[r * MOBA_BLOCK:(r + 1) * MOBA_BLOCK]
        km_ref[r] = jnp.sum(block, axis=0, keepdims=True) * (1.0 / MOBA_BLOCK)
    v_scr[...] = _dot(h, w_ref[:, 2 * aw + 2 * bw:2 * aw + 3 * bw])
    vt = v_scr[...].T
    vt_ref[...] = vt
    vtb_ref[...] = _with_ones_rows(vt, bw // B_HEADS).astype(BF16)


def _even_in_decode_kernel(x_ref, g_ref, w_ref, c_ref, s1_ref, s2_ref, gs_ref, w00_ref, b0_ref,
                           a_ref, q_ref, k_ref, v_ref, va_ref, *, half, scale):
    aw = A_GROUPS * LANE
    bw = q_ref.shape[1]
    h = _rmsnorm(x_ref[...], g_ref[...]).astype(BF16)
    u, vs = _sgu_values(h, w_ref, gs_ref, aw)
    va = jnp.concatenate(vs, axis=1)
    va_ref[...] = va
    a_ref[...] = (u * (w00_ref[...] * va + b0_ref[...])).astype(BF16)
    c, s1, s2 = c_ref[...], s1_ref[...], s2_ref[...]
    q = _rope_apply(_dot(h, w_ref[:, 2 * aw:2 * aw + bw]), c, s1, s2, half)
    q_ref[...] = (q * scale).astype(BF16)
    k_ref[...] = _rope_apply(_dot(h, w_ref[:, 2 * aw + bw:2 * aw + 2 * bw]), c, s1, s2, half)
    v_ref[...] = _dot(h, w_ref[:, 2 * aw + 2 * bw:2 * aw + 3 * bw])


def _even_in_prompt(x, seq, g_mix, w_in, tables, half, g_sgu, w_sgu, b_sgu):
    n, d = x.shape
    tm = ROW_TILE
    assert tm % MOBA_BLOCK == 0 and seq % tm == 0
    blocks = tm // MOBA_BLOCK
    bw = (w_in.shape[1] - 2 * A_GROUPS * LANE) // 3
    dh = bw // B_HEADS
    tiles_per_seq = seq // tm
    row = lambda w, dt: (pl.BlockSpec((tm, w), lambda i: (i, 0)), jax.ShapeDtypeStruct((n, w), dt))
    tab = pl.BlockSpec((tm, LANE), lambda i: (i % tiles_per_seq, 0))
    col = lambda h, dt: (pl.BlockSpec((None, h, tm), lambda i: (i // tiles_per_seq, 0, i % tiles_per_seq)),
                         jax.ShapeDtypeStruct((n // seq, h, seq), dt))
    outs = [row(A_GROUPS * LANE, BF16), row(bw, BF16), row(bw, BF16), col(bw, F32), col(bw, F32),
            col(bw + B_HEADS * ONES_ROWS, BF16),
            (pl.BlockSpec((blocks, 1, bw), lambda i: (i, 0, 0)), jax.ShapeDtypeStruct((n // MOBA_BLOCK, 1, bw), F32))]
    bs = jnp.broadcast_to(b_sgu[:, :, None], (A_GROUPS, CHUNK, LANE))
    return pl.pallas_call(
        functools.partial(_even_in_prompt_kernel, half=half, scale=dh ** -0.5),
        grid=(n // tm,),
        in_specs=[pl.BlockSpec((tm, d), lambda i: (i, 0)), _full((1, d)), _full(w_in.shape),
                  tab, tab, tab, _full((1, A_GROUPS * LANE)), _full(w_sgu.shape), _full(bs.shape)],
        out_specs=[o[0] for o in outs],
        out_shape=[o[1] for o in outs],
        scratch_shapes=[pltpu.VMEM((tm, bw), F32)],
        compiler_params=_params("arbitrary"),
        name="even_in_prompt",
    )(x, g_mix.reshape(1, d), w_in, *tables, g_sgu.reshape(1, -1), w_sgu, bs)


def _even_in_decode(x, g_mix, w_in, tables, half, g_sgu, w_sgu, b_sgu):
    n, d = x.shape
    bw = (w_in.shape[1] - 2 * A_GROUPS * LANE) // 3
    dh = bw // B_HEADS
    aw = A_GROUPS * LANE
    w00 = jnp.repeat(w_sgu[:, 0, 0], LANE).reshape(1, aw)
    b0 = jnp.repeat(b_sgu[:, 0], LANE).reshape(1, aw)
    shapes = [(aw, BF16), (bw, BF16), (bw, F32), (bw, F32), (aw, F32)]
    return pl.pallas_call(
        functools.partial(_even_in_decode_kernel, half=half, scale=dh ** -0.5),
        grid=(1,),
        in_specs=[_full((n, d)), _full((1, d)), _full(w_in.shape)] + [_full((1, LANE))] * 3
                 + [_full((1, aw))] * 3,
        out_specs=[_full((n, w)) for w, _ in shapes],
        out_shape=[jax.ShapeDtypeStruct((n, w), dt) for w, dt in shapes],
        compiler_params=_params("arbitrary"),
        name="even_in_decode",
    )(x, g_mix.reshape(1, d), w_in, *tables, g_sgu.reshape(1, -1), w00, b0)


def _moba_prompt_kernel(q_ref, k_ref, vt_ref, km_ref, o_ref, sel_ref, m_ref, acc_ref, s_a, s_b, *, dh):
    blk = MOBA_BLOCK
    tq = q_ref.shape[0]
    i = pl.program_id(1)
    nblk = km_ref.shape[1]
    heads = q_ref.shape[1] // dh
    ext = dh + ONES_ROWS
    per_group = LANE // dh
    lane = lax.broadcasted_iota(jnp.int32, (tq, LANE), 1)
    blk_id = lax.broadcasted_iota(jnp.int32, (nblk, tq), 0)
    neg_inf = jnp.float32(-jnp.inf)
    qms = []
    for h in range(heads):
        grp = h // per_group
        q = q_ref[:, grp * LANE:(grp + 1) * LANE]
        qm = jnp.where(_head_of(lane, dh) == h % per_group, q, jnp.zeros_like(q))
        qms.append(qm)
        kmean = km_ref[0, :, grp * LANE:(grp + 1) * LANE].astype(BF16)
        gate = jnp.where(blk_id < i, _dot_nt(kmean, qm), neg_inf)
        sel = jnp.zeros((nblk, tq), F32)
        for _ in range(MOBA_TOPK):
            best = jnp.max(gate, axis=0, keepdims=True)
            first = jnp.min(jnp.where(gate == best, blk_id, nblk), axis=0, keepdims=True)
            hit = blk_id == first
            sel = jnp.where(hit, 1.0, sel)
            gate = jnp.where(hit, neg_inf, gate)
        sel_ref[h] = jnp.where(blk_id < i, sel, 0.0)

    causal = (lax.broadcasted_iota(jnp.int32, (blk, tq), 0)
              <= lax.broadcasted_iota(jnp.int32, (blk, tq), 1))

    head_row = lax.broadcasted_iota(jnp.int32, (heads, tq), 0)

    def score_matmuls(block):
        start = pl.multiple_of(block * blk, blk)
        kb = k_ref[pl.ds(start, blk), :]
        return [_dot_nt(kb[:, (h // per_group) * LANE:(h // per_group + 1) * LANE], qms[h])
                for h in range(heads)]

    def attend(scores, vtb, keeps, first):
        if not first:
            m_old = m_ref[...]
            acc_old = [acc_ref[h] for h in range(heads)]
        m_all = jnp.zeros((heads, tq), F32)
        probs, alphas = [], []
        for h in range(heads):
            if first:
                s = jnp.where(keeps[h], scores[h], neg_inf)
                m_new = jnp.max(s, axis=0, keepdims=True)
                shift = m_new
            else:
                s = scores[h]
                s_max = jnp.where(keeps[h], jnp.max(s, axis=0, keepdims=True), neg_inf)
                m_new = jnp.maximum(m_old[h:h + 1, :], s_max)
                alphas.append(jnp.exp(m_old[h:h + 1, :] - m_new))
                shift = jnp.where(keeps[h], m_new, jnp.inf)
            probs.append(jnp.exp(s - shift).astype(BF16))
            m_all = jnp.where(head_row == h, m_new, m_all)
        pvs = [_dot(vtb[h * ext:(h + 1) * ext, :], probs[h]) for h in range(heads)]
        m_ref[...] = m_all
        for h in range(heads):
            acc_ref[h] = pvs[h] if first else acc_old[h] * alphas[h] + pvs[h]

    start = pl.multiple_of(i * blk, blk)
    attend(score_matmuls(i), vt_ref[:, pl.ds(start, blk)], [causal] * heads, True)

    for h, s in enumerate(score_matmuls(0)):
        s_a[h] = s

    def step(n, read_scr, write_scr):
        for h, s in enumerate(score_matmuls(jnp.minimum(n + 1, jnp.maximum(i - 1, 0)))):
            write_scr[h] = s
        start = pl.multiple_of(n * blk, blk)
        keeps = [sel_ref[h, pl.ds(n, 1), :] > 0.0 for h in range(heads)]
        attend([read_scr[h] for h in range(heads)], vt_ref[:, pl.ds(start, blk)], keeps, False)

    def body(pair, _):
        step(2 * pair, s_a, s_b)

        @pl.when(2 * pair + 1 < i)
        def _():
            step(2 * pair + 1, s_b, s_a)
        return 0

    lax.fori_loop(0, (i + 1) // 2, body, 0)
    outs = [acc_ref[h, 0:dh, :] / acc_ref[h, dh:dh + 1, :] for h in range(heads)]
    o_ref[...] = jnp.concatenate(outs, axis=0).T.astype(BF16)


def _moba_prompt(q, kb, vt, kmean, batch, seq, dh):
    n, bw = q.shape
    tq = MOBA_BLOCK
    nblk = seq // MOBA_BLOCK
    heads = bw // dh
    km = kmean.reshape(batch, nblk, bw)
    tiles = seq // tq
    return pl.pallas_call(
        functools.partial(_moba_prompt_kernel, dh=dh),
        grid=(batch, tiles),
        in_specs=[pl.BlockSpec((tq, bw), lambda b, i: (b * tiles + i, 0)),
                  pl.BlockSpec((seq, bw), lambda b, i: (b, 0)),
                  pl.BlockSpec((None, vt.shape[1], seq), lambda b, i: (b, 0, 0)),
                  pl.BlockSpec((1, nblk, bw), lambda b, i: (b, 0, 0))],
        out_specs=pl.BlockSpec((tq, bw), lambda b, i: (b * tiles + i, 0)),
        out_shape=jax.ShapeDtypeStruct((n, bw), BF16),
        scratch_shapes=[pltpu.VMEM((heads, nblk, tq), F32), pltpu.VMEM((heads, tq), F32),
                        pltpu.VMEM((heads, dh + ONES_ROWS, tq), F32),
                        pltpu.VMEM((heads, MOBA_BLOCK, tq), F32), pltpu.VMEM((heads, MOBA_BLOCK, tq), F32)],
        compiler_params=_params("arbitrary", "arbitrary"),
        name="moba_prompt",
    )(q, kb, vt, km)


def _residual_proj_kernel(*refs, n_in, norm):
    x_ref = refs[0]
    in_refs = refs[1:1 + n_in]
    w_ref = refs[1 + n_in]
    y = x_ref[...]
    lo = 0
    for r in in_refs:
        y = y + _dot(r[...], w_ref[lo:lo + r.shape[1], :])
        lo += r.shape[1]
    if norm == "final":
        g_ref, o_ref = refs[2 + n_in:]
        o_ref[...] = _rmsnorm(y, g_ref[...])
    else:
        g_ref, o_ref, h_ref = refs[2 + n_in:]
        o_ref[...] = y
        h_ref[...] = _rmsnorm(y, g_ref[...]).astype(BF16)


def _residual_proj(x, ins, w, g, norm, tm, name):
    n, d = x.shape
    tm = min(tm, n)
    row = lambda width: pl.BlockSpec((tm, width), lambda i: (i, 0))
    out_specs = [row(d)] if norm == "final" else [row(d), row(d)]
    out_shape = ([jax.ShapeDtypeStruct((n, d), F32)] if norm == "final"
                 else [jax.ShapeDtypeStruct((n, d), F32), jax.ShapeDtypeStruct((n, d), BF16)])
    return pl.pallas_call(
        functools.partial(_residual_proj_kernel, n_in=len(ins), norm=norm),
        grid=(n // tm,),
        in_specs=[row(d)] + [row(a.shape[1]) for a in ins] + [_full(w.shape), _full((1, d))],
        out_specs=out_specs,
        out_shape=out_shape,
        compiler_params=_params("arbitrary"),
        name=name,
    )(x, *ins, w, g.reshape(1, d))


FFN_COLS = 256
FFN_HALO = SUBLANE_BF16


def _silu_gate(gate, value):
    return gate * (1.0 / (1.0 + jnp.exp(-gate))) * value


def _ffn_up_prompt_kernel(h_ref, halo_ref, w_ref, wc_ref, bc_ref, act_ref, tail_ref, up_scr, *, tiles_per_seq):
    tm = h_ref.shape[0]
    dff = act_ref.shape[1]
    seq_start = pl.program_id(0) % tiles_per_seq == 0
    halo = halo_ref[...]
    halo = jnp.where(seq_start, jnp.zeros_like(halo), halo)
    hh = jnp.concatenate([halo, h_ref[...]], axis=0)
    for c in range(dff // FFN_COLS):
        halves = []
        for k, lo in enumerate((c * FFN_COLS, dff + c * FFN_COLS)):
            cols = slice(lo, lo + FFN_COLS)
            up = _dot(hh, w_ref[:, cols])
            tail_ref[0, :, cols] = up[tm + FFN_HALO - SUBLANE:, :]
            up_scr[k] = up
            conv = bc_ref[:, cols] + up[FFN_HALO:, :] * wc_ref[CONV_W - 1:CONV_W, cols]
            for j in range(1, CONV_W):
                conv = conv + up_scr[k, pl.ds(FFN_HALO - j, tm), :] * wc_ref[CONV_W - 1 - j:CONV_W - j, cols]
            halves.append(conv)
        act_ref[:, c * FFN_COLS:(c + 1) * FFN_COLS] = _silu_gate(halves[1], halves[0]).astype(BF16)


def _ffn_up_decode_kernel(h_ref, s0_ref, s1_ref, w_ref, wc_ref, bc_ref, act_ref, up_ref):
    dff = act_ref.shape[1]
    h = h_ref[...]
    for c in range(dff // FFN_COLS):
        halves = []
        for lo in (c * FFN_COLS, dff + c * FFN_COLS):
            cols = slice(lo, lo + FFN_COLS)
            up = _dot(h, w_ref[:, cols])
            up_ref[:, cols] = up
            halves.append(bc_ref[:, cols] + s0_ref[:, cols] * wc_ref[0:1, cols]
                          + s1_ref[:, cols] * wc_ref[1:2, cols] + up * wc_ref[2:3, cols])
        act_ref[:, c * FFN_COLS:(c + 1) * FFN_COLS] = _silu_gate(halves[1], halves[0]).astype(BF16)


def _ffn_up_prompt(h, seq, w_up, w_conv, b_conv, tm):
    n, d = h.shape
    dff = w_up.shape[1] // 2
    assert dff % FFN_COLS == 0 and seq % tm == 0 and tm % FFN_HALO == 0
    ratio = tm // FFN_HALO
    act, tail = pl.pallas_call(
        functools.partial(_ffn_up_prompt_kernel, tiles_per_seq=seq // tm),
        grid=(n // tm,),
        in_specs=[pl.BlockSpec((tm, d), lambda i: (i, 0)),
                  pl.BlockSpec((FFN_HALO, d), lambda i: (jnp.maximum(i * ratio - 1, 0), 0)),
                  _full(w_up.shape), _full(w_conv.shape), _full((1, 2 * dff))],
        out_specs=[pl.BlockSpec((tm, dff), lambda i: (i, 0)),
                   pl.BlockSpec((1, SUBLANE, 2 * dff), lambda i: (i, 0, 0))],
        out_shape=[jax.ShapeDtypeStruct((n, dff), BF16),
                   jax.ShapeDtypeStruct((n // tm, SUBLANE, 2 * dff), F32)],
        scratch_shapes=[pltpu.VMEM((2, tm + FFN_HALO, FFN_COLS), F32)],
        compiler_params=_params("arbitrary"),
        name="ffn_up_prompt",
    )(h, h, w_up, w_conv, b_conv.reshape(1, -1))
    last = tail.reshape(n // seq, seq // tm, SUBLANE, 2 * dff)[:, -1, SUBLANE - (CONV_W - 1):, :]
    return act, last


def _ffn_up_decode(h, state, w_up, w_conv, b_conv):
    n, d = h.shape
    dff = w_up.shape[1] // 2
    act, up = pl.pallas_call(
        _ffn_up_decode_kernel,
        grid=(1,),
        in_specs=[_full((n, d)), _full((n, 2 * dff)), _full((n, 2 * dff)),
                  _full(w_up.shape), _full(w_conv.shape), _full((1, 2 * dff))],
        out_specs=[_full((n, dff)), _full((n, 2 * dff))],
        out_shape=[jax.ShapeDtypeStruct((n, dff), BF16), jax.ShapeDtypeStruct((n, 2 * dff), F32)],
        compiler_params=_params("arbitrary"),
        name="ffn_up_decode",
    )(h, state[:, 0], state[:, 1], w_up, w_conv, b_conv.reshape(1, -1))
    return act, jnp.stack([state[:, 1], up], axis=1)


def _odd_in_kernel(h_ref, w_ref, c64_ref, a64_ref, b64_ref, c128_ref, a128_ref, b128_ref,
                   q_ref, k_ref, v_ref, kb_ref, vt_ref, qi_ref, kif_ref, ki2_ref, wit_ref,
                   *, half64, half128, idx_dh):
    cq = q_ref.shape[1]
    ckv = kb_ref.shape[1]
    ciq = qi_ref.shape[1]
    h = h_ref[...]
    t64 = (c64_ref[...], a64_ref[...], b64_ref[...])
    t128 = (c128_ref[...], a128_ref[...], b128_ref[...])
    q_ref[...] = _rope_apply(_dot(h, w_ref[:, 0:cq]), *t128, half128).astype(BF16)
    k = _rope_apply(_dot(h, w_ref[:, cq:cq + ckv]), *t128, half128)
    kb_ref[...] = k.astype(BF16)
    v = _dot(h, w_ref[:, cq + ckv:cq + 2 * ckv])
    c_dh = ckv // C_KV_HEADS
    for kv in range(C_KV_HEADS):
        k_ref[pl.ds(kv, h.shape[0], stride=C_KV_HEADS), :] = k[:, kv * c_dh:(kv + 1) * c_dh]
        v_ref[pl.ds(kv, h.shape[0], stride=C_KV_HEADS), :] = v[:, kv * c_dh:(kv + 1) * c_dh]
    if vt_ref is not None:
        vt_ref[...] = _with_ones_rows(v.T, ckv // C_KV_HEADS).astype(BF16)
    lo = cq + 2 * ckv
    qi_ref[...] = _rope_apply(_dot(h, w_ref[:, lo:lo + ciq]), *t64, half64).astype(BF16)
    tail = _dot(h, w_ref[:, lo + ciq:lo + ciq + LANE])
    ki = _rope_apply(tail, *t64, half64)
    kif_ref[...] = ki
    lane = lax.broadcasted_iota(jnp.int32, ki.shape, 1)
    ki2_ref[...] = jnp.where(lane < idx_dh, ki, pltpu.roll(ki, idx_dh, 1)).astype(BF16)
    wit_ref[...] = tail.T[idx_dh:idx_dh + 8, :] * (IDX_HEADS ** -0.5)


def _odd_in(h, seq, w_in, tabs64, half64, tabs128, half128, c_dh, idx_dh, tm, with_vt):
    n, d = h.shape
    tm = min(tm, n)
    cq = C_HEADS * c_dh
    ckv = C_KV_HEADS * c_dh
    ciq = IDX_HEADS * idx_dh
    cols = cq + 2 * ckv + ciq + LANE
    assert 2 * idx_dh == LANE and w_in.shape[1] <= cols
    w = jnp.pad(w_in, ((0, 0), (0, cols - w_in.shape[1]))).astype(BF16)
    if seq is None:
        tab = pl.BlockSpec((1, LANE), lambda i: (0, 0))
    else:
        tiles_per_seq = seq // tm
        tab = pl.BlockSpec((tm, LANE), lambda i: (i % tiles_per_seq, 0))
    row = lambda width, dt: (pl.BlockSpec((tm, width), lambda i: (i, 0)), jax.ShapeDtypeStruct((n, width), dt))
    col = lambda height, dt: (pl.BlockSpec((height, tm), lambda i: (0, i)), jax.ShapeDtypeStruct((height, n), dt))
    pair_rows = (pl.BlockSpec((tm * C_KV_HEADS, c_dh), lambda i: (i, 0)),
                 jax.ShapeDtypeStruct((n * C_KV_HEADS, c_dh), F32))
    outs = [row(cq, BF16), pair_rows, pair_rows, row(ckv, BF16)]
    if with_vt:
        outs.append(col(ckv + C_KV_HEADS * ONES_ROWS, BF16))
    outs += [row(ciq, BF16), row(LANE, F32), row(LANE, BF16), col(8, F32)]

    def body(*refs):
        refs = list(refs)
        if not with_vt:
            refs.insert(8 + 4, None)
        _odd_in_kernel(*refs, half64=half64, half128=half128, idx_dh=idx_dh)

    res = pl.pallas_call(
        body,
        grid=(n // tm,),
        in_specs=[pl.BlockSpec((tm, d), lambda i: (i, 0)), _full(w.shape)] + [tab] * 6,
        out_specs=[o[0] for o in outs],
        out_shape=[o[1] for o in outs],
        compiler_params=_params("arbitrary"),
        name="odd_in_prompt" if with_vt else "odd_in_decode",
    )(h, w, *tabs64, *tabs128)
    res = list(res)
    if not with_vt:
        res.insert(4, None)
    return res


FLOAT_BITS = 32
INDEX_BIG = 2 ** 30


def _ind(cond):
    return jnp.where(cond, 1.0, 0.0)


def _ordered_bits(x, to):
    b = x if x.dtype == jnp.int32 else lax.bitcast_convert_type(x, jnp.int32)
    b = jnp.where(b < 0, b ^ 0x7FFFFFFF, b)
    return b if to == jnp.int32 else lax.bitcast_convert_type(b, F32)


def _ordered_bits16(x):
    b = lax.shift_right_arithmetic(lax.bitcast_convert_type(x, jnp.int32), 16)
    return jnp.where(b < 0, b ^ 0x7FFF, b)


def _from_ordered_bits16(k):
    b = jnp.where(k < 0, k ^ 0x7FFF, k)
    return lax.bitcast_convert_type(lax.shift_left(b, 16), F32)


def _bf16_floor(x):
    r = x.astype(BF16).astype(F32)
    b = lax.bitcast_convert_type(r, jnp.int32)
    below = lax.bitcast_convert_type(jnp.where(b < 0, b + 0x10000, b - 0x10000), F32)
    return jnp.where(r > x, below, r)


def _bisect_threshold(count_fn, shape, keep, coarse_count_fn=None):
    keep_f = jnp.float32(keep)

    def bisect(passes, pivot_count, st):
        def step(_, st):
            lo_k, hi_k, n_hi = st
            mid_k = (lax.shift_right_arithmetic(lo_k, 1) + lax.shift_right_arithmetic(hi_k, 1)
                     + (lo_k & hi_k & 1))
            cnt = pivot_count(mid_k)
            raise_lo = jnp.logical_and(mid_k > lo_k, cnt >= keep_f)
            lower_hi = jnp.logical_and(mid_k > lo_k, cnt < keep_f)
            return (jnp.where(raise_lo, mid_k, lo_k), jnp.where(lower_hi, mid_k, hi_k),
                    jnp.where(lower_hi, cnt, n_hi))
        return lax.fori_loop(0, passes, step, st)

    def fine_count(mid_k):
        return count_fn(lambda s, idx: (_ind(s >= _ordered_bits(mid_k, F32)),))[0]

    lo = jnp.full(shape, -jnp.inf, F32)
    hi = jnp.full(shape, jnp.inf, F32)
    n_hi = jnp.zeros(shape, F32)
    if coarse_count_fn is None:
        lo_k, _, n_gt = bisect(FLOAT_BITS, fine_count,
                               (_ordered_bits(lo, jnp.int32), _ordered_bits(hi, jnp.int32) + 1, n_hi))
    else:
        half = FLOAT_BITS // 2
        coarse = lambda mid_h: coarse_count_fn(_from_ordered_bits16(mid_h).astype(BF16))
        lo_h, _, n_hi = bisect(half, coarse, (_ordered_bits16(lo), _ordered_bits16(hi) + 1, n_hi))
        st = (_ordered_bits(_from_ordered_bits16(lo_h), jnp.int32),
              _ordered_bits(_from_ordered_bits16(lo_h + 1), jnp.int32), n_hi)
        lo_k, _, n_gt = bisect(half, fine_count, st)
    return _ordered_bits(lo_k, F32), n_gt


def _select_threshold(count_fn, active, keep, index_bits):
    thr, n_gt = _bisect_threshold(count_fn, active.shape, keep)
    n_eq, = count_fn(lambda s, idx: (_ind(s == thr),))
    need = jnp.float32(keep) - n_gt
    excess = jnp.where(active, jnp.where(n_eq > need, 1, 0), 0)

    def tie_search():
        lo_i = jnp.full(thr.shape, -1, jnp.int32)
        hi_i = jnp.full(thr.shape, (1 << index_bits) - 1, jnp.int32)
        for _ in range(index_bits + 1):
            mid_i = lax.shift_right_arithmetic(lo_i + hi_i, 1)
            n_low, = count_fn(lambda s, idx: (jnp.where(s == thr, _ind(idx <= mid_i), 0.0),))
            ok = n_low >= need
            hi_i = jnp.where(ok, mid_i, hi_i)
            lo_i = jnp.where(ok, lo_i, mid_i)
        return hi_i

    cut = lax.cond(jnp.max(excess) > 0, tie_search, lambda: jnp.full(thr.shape, INDEX_BIG, jnp.int32))
    thr = jnp.where(active, thr, -jnp.inf)
    cut = jnp.where(active, cut, -1)
    return thr, cut


def _selected(s, idx, thr, cut):
    return jnp.where(s > thr, 1.0, jnp.where(s == thr, _ind(idx <= cut), 0.0))


def _dsa_prompt_kernel(q_ref, qi_ref, wit_ref, kb_ref, vt_ref, ki2_ref, o_ref,
                       sc_ref, sb_ref, m_ref, acc_ref, *, c_dh, idx_dh, keep, kc):
    tq = q_ref.shape[0]
    seq = kb_ref.shape[0]
    t0 = pl.program_id(1) * tq
    nch = (t0 + tq + kc - 1) // kc
    scale = c_dh ** -0.5
    neg_inf = jnp.float32(-jnp.inf)
    lane = lax.broadcasted_iota(jnp.int32, (tq, LANE), 1)
    key_iota = lax.broadcasted_iota(jnp.int32, (kc, tq), 0)
    qpos = t0 + lax.broadcasted_iota(jnp.int32, (1, tq), 1)

    qi = qi_ref[...]
    per_group = LANE // idx_dh
    qms = []
    for h in range(IDX_HEADS):
        grp = qi[:, (h // per_group) * LANE:(h // per_group + 1) * LANE]
        qms.append(jnp.where(_head_of(lane, idx_dh) == h % per_group, grp, jnp.zeros_like(grp)))
    wit = wit_ref[...]

    def score_chunk(c, _):
        start = pl.multiple_of(c * kc, kc)
        kic = ki2_ref[pl.ds(start, kc), :]
        acc = jnp.zeros((kc, tq), F32)
        for h in range(IDX_HEADS):
            acc = acc + jnp.maximum(_dot_nt(kic, qms[h]), 0.0) * wit[h:h + 1, :]
        score = acc * (idx_dh ** -0.5)
        adm = start + key_iota <= qpos
        masked = jnp.where(adm, score, neg_inf)
        sc_ref[pl.ds(start, kc), :] = masked
        sb_ref[pl.ds(start, kc), :] = _bf16_floor(masked).astype(BF16)
        return 0

    lax.fori_loop(0, nch, score_chunk, 0)

    def count_fn(pred, n_out=1):
        def body(c, cnts):
            start = pl.multiple_of(c * kc, kc)
            hits = pred(sc_ref[pl.ds(start, kc), :], start + key_iota)
            return tuple(cnt + jnp.sum(hit.reshape(kc // SUBLANE, SUBLANE, tq), axis=0) for cnt, hit in zip(cnts, hits))
        parts = lax.fori_loop(0, nch, body, tuple(jnp.zeros((SUBLANE, tq), F32) for _ in range(n_out)))
        return tuple(jnp.sum(part, axis=0, keepdims=True) for part in parts)

    def coarse_count_fn(pivot):
        one, zero = jnp.ones((), BF16), jnp.zeros((), BF16)
        rows = SUBLANE_BF16

        def body(c, cnt):
            hit = jnp.where(sb_ref[pl.ds(pl.multiple_of(c * kc, kc), kc), :] >= pivot, one, zero)
            part = hit[0:rows]
            for r in range(1, kc // rows):
                part = part + hit[r * rows:(r + 1) * rows]
            return cnt + part.astype(F32)
        part = lax.fori_loop(0, nch, body, jnp.zeros((rows, tq), F32))
        return jnp.sum(part, axis=0, keepdims=True)

    thr, n_gt = _bisect_threshold(count_fn, (1, tq), keep, coarse_count_fn)
    active = qpos + 1 > keep
    thr = jnp.where(active, thr, neg_inf)
    need = jnp.where(active, jnp.float32(keep) - n_gt, 0.0)
    prefix_ones = jnp.where(lax.broadcasted_iota(jnp.int32, (kc, kc), 0)
                            >= lax.broadcasted_iota(jnp.int32, (kc, kc), 1), 1.0, 0.0).astype(BF16)

    m_ref[...] = jnp.full(m_ref.shape, NEG_BIG, F32)
    acc_ref[...] = jnp.zeros(acc_ref.shape, F32)
    group = C_HEADS // C_KV_HEADS
    ext = c_dh + ONES_ROWS
    exp2_scale = scale * LOG2_E
    head_row = lax.broadcasted_iota(jnp.int32, m_ref.shape, 0)

    def attend_chunk(c, ties_before):
        start = pl.multiple_of(c * kc, kc)
        sc = sc_ref[pl.ds(start, kc), :]
        tie = _ind(sc == thr)
        tie_rank = ties_before + _dot(prefix_ones, tie.astype(BF16))
        keepm = jnp.where(sc > thr, 1.0, jnp.where(tie_rank <= need, tie, 0.0)) > 0.0
        ties_before = tie_rank[kc - 1:kc, :]
        kbc = kb_ref[pl.ds(start, kc), :]
        vtc = vt_ref[:, pl.ds(start, kc)]
        m_old = m_ref[...]
        m_all = jnp.zeros(m_old.shape, F32)
        scores = [_dot_nt(kbc[:, (h // group) * c_dh:(h // group + 1) * c_dh], q_ref[:, h * c_dh:(h + 1) * c_dh])
                  for h in range(C_HEADS)]
        alphas, pvs = [], []
        for kv in range(C_KV_HEADS):
            probs = []
            for h in range(kv * group, (kv + 1) * group):
                s = jnp.where(keepm, scores[h], neg_inf)
                m_new = jnp.maximum(m_old[h:h + 1, :], jnp.max(s, axis=0, keepdims=True))
                alphas.append(jnp.exp2((m_old[h:h + 1, :] - m_new) * exp2_scale))
                probs.append(jnp.exp2((s - m_new) * exp2_scale).astype(BF16))
                m_all = jnp.where(head_row == h, m_new, m_all)
            pvs += [_dot(vtc[kv * ext:(kv + 1) * ext, :], p) for p in probs]
        m_ref[...] = m_all
        for h in range(C_HEADS):
            acc_ref[h] = acc_ref[h] * alphas[h] + pvs[h]
        return ties_before

    lax.fori_loop(0, nch, attend_chunk, jnp.zeros((1, tq), F32))
    for h in range(C_HEADS):
        out_t = acc_ref[h, 0:c_dh, :] / acc_ref[h, c_dh:c_dh + 1, :]
        o_ref[:, h * c_dh:(h + 1) * c_dh] = out_t.T.astype(BF16)


def _dsa_prompt(q, qi, wit, kb, vt, ki2, batch, seq, c_dh, idx_dh, keep, tq, kc):
    n = q.shape[0]
    tiles = seq // tq
    assert c_dh == LANE and seq % kc == 0 and seq % tq == 0
    return pl.pallas_call(
        functools.partial(_dsa_prompt_kernel, c_dh=c_dh, idx_dh=idx_dh, keep=keep, kc=kc),
        grid=(batch, tiles),
        in_specs=[pl.BlockSpec((tq, q.shape[1]), lambda b, i: (b * tiles + i, 0)),
                  pl.BlockSpec((tq, qi.shape[1]), lambda b, i: (b * tiles + i, 0)),
                  pl.BlockSpec((SUBLANE, tq), lambda b, i: (0, b * tiles + i)),
                  pl.BlockSpec((seq, kb.shape[1]), lambda b, i: (b, 0)),
                  pl.BlockSpec((vt.shape[0], seq), lambda b, i: (0, b)),
                  pl.BlockSpec((seq, LANE), lambda b, i: (b, 0))],
        out_specs=pl.BlockSpec((tq, q.shape[1]), lambda b, i: (b * tiles + i, 0)),
        out_shape=jax.ShapeDtypeStruct(q.shape, BF16),
        scratch_shapes=[pltpu.VMEM((seq, tq), F32), pltpu.VMEM((seq, tq), BF16), pltpu.VMEM((SUBLANE, tq), F32),
                        pltpu.VMEM((C_HEADS, c_dh + ONES_ROWS, tq), F32)],
        compiler_params=_params("arbitrary", "arbitrary"),
        name="dsa_prompt",
    )(q, qi, wit, kb, vt, ki2)


PAGE_ROWS = 128
DECODE_SCORE_SEQS = 4
DECODE_ATTN_SEQS = 2
DECODE_MOBA_SEQS = 2


def _lane_bcast_col(row):
    return jnp.broadcast_to(row, (LANE, row.shape[1])).T


def _moba_decode_kernel(pt_ref, q_ref, kn_ref, vn_ref, *rest, n_pages, dh, seqs):
    for g in range(seqs):
        _moba_decode_seq(g, q_ref, kn_ref, vn_ref, rest[g * n_pages:(g + 1) * n_pages],
                         rest[(seqs + g) * n_pages:(seqs + g + 1) * n_pages], rest[-1], n_pages, dh)


def _moba_decode_seq(g, q_ref, kn_ref, vn_ref, kt_refs, vt_refs, o_ref, n_pages, dh):
    bw = q_ref.shape[2]
    heads = bw // dh
    ppb = MOBA_BLOCK // PAGE_ROWS
    nblk = n_pages // ppb
    neg_inf = jnp.float32(-jnp.inf)
    qb = _lane_bcast_col(q_ref[g].astype(F32))

    def head_sums(x):
        return jnp.sum(x.reshape(heads, dh, LANE), axis=1)

    def head_rows(x):
        return jnp.broadcast_to(x[:, None, :], (heads, dh, LANE)).reshape(bw, LANE)

    s_pages = [head_sums(kt_refs[j][...] * qb) for j in range(n_pages)]
    s_own = head_sums(_lane_bcast_col(kn_ref[g]) * qb)
    gates = []
    for n in range(nblk):
        gate = jnp.zeros((heads, 1), F32)
        for j in range(n * ppb, (n + 1) * ppb):
            gate = gate + jnp.sum(s_pages[j], axis=1, keepdims=True)
        gates.append(gate)
    sels = []
    for n in range(nblk):
        rank = jnp.zeros((heads, 1), F32)
        for m in range(nblk):
            if m != n:
                beats = gates[m] >= gates[n] if m < n else gates[m] > gates[n]
                rank = rank + jnp.where(beats, 1.0, 0.0)
        sels.append(rank < MOBA_TOPK)
    m = s_own
    masked = []
    for j in range(n_pages):
        sm = jnp.where(sels[j // ppb], s_pages[j], neg_inf)
        masked.append(sm)
        m = jnp.maximum(m, jnp.max(sm, axis=1, keepdims=True))
    e_own = jnp.exp(s_own - m)
    es = [jnp.exp(sm - m) for sm in masked]
    l = e_own
    for e in es:
        l = l + jnp.sum(e, axis=1, keepdims=True)
    rinv = 1.0 / l
    lane = lax.broadcasted_iota(jnp.int32, (bw, LANE), 1)
    acc = jnp.where(lane == 0, head_rows(e_own * rinv) * _lane_bcast_col(vn_ref[g]), 0.0)
    for j in range(n_pages):
        acc = acc + head_rows(es[j] * rinv) * vt_refs[j][...]
    out_col = jnp.sum(acc, axis=1, keepdims=True)
    o_ref[g] = jnp.broadcast_to(out_col, (bw, LANE)).T[0:1].astype(BF16)


def _moba_decode(q, k_new, v_new, cache_k, cache_v, page_table, dh):
    n, bw = q.shape
    n_pages = page_table.shape[1]
    assert cache_k.shape[1] == PAGE_ROWS == LANE and (n_pages * PAGE_ROWS) % MOBA_BLOCK == 0
    ckt = jnp.transpose(cache_k, (0, 2, 3, 1)).reshape(cache_k.shape[0], bw, PAGE_ROWS)
    cvt = jnp.transpose(cache_v, (0, 2, 3, 1)).reshape(cache_v.shape[0], bw, PAGE_ROWS)
    g = DECODE_MOBA_SEQS
    assert n % g == 0
    seq_row = pl.BlockSpec((g, 1, bw), lambda b, pt: (b, 0, 0))
    pages = [pl.BlockSpec((None, bw, PAGE_ROWS), lambda b, pt, j=j, k=k: (pt[b * g + k, j], 0, 0))
             for k in range(g) for j in range(n_pages)]
    out = pl.pallas_call(
        functools.partial(_moba_decode_kernel, n_pages=n_pages, dh=dh, seqs=g),
        grid_spec=pltpu.PrefetchScalarGridSpec(
            num_scalar_prefetch=1, grid=(n // g,),
            in_specs=[seq_row, seq_row, seq_row] + pages + pages,
            out_specs=seq_row),
        out_shape=jax.ShapeDtypeStruct((n, 1, bw), BF16),
        compiler_params=_params("arbitrary"),
        name="moba_decode",
    )(page_table, q.reshape(n, 1, bw), k_new.reshape(n, 1, bw), v_new.reshape(n, 1, bw),
      *([ckt] * (g * n_pages)), *([cvt] * (g * n_pages)))
    return out.reshape(n, bw)


def _dsa_decode_score_kernel(pt_ref, qi_ref, wi_ref, kin_ref, *rest, n_pages, idx_dh, seqs):
    for g in range(seqs):
        _dsa_decode_score_seq(g, qi_ref, wi_ref, kin_ref, rest[g * n_pages:(g + 1) * n_pages], rest[-1],
                              n_pages, idx_dh)


def _dsa_decode_score_seq(g, qi_ref, wi_ref, kin_ref, kit_refs, o_ref, n_pages, idx_dh):
    qb = _lane_bcast_col(qi_ref[g].astype(F32))
    wi8 = wi_ref[g]
    scale = idx_dh ** -0.5

    def score(kit):
        total = jnp.zeros((1, LANE), F32)
        for h in range(IDX_HEADS):
            sc = jnp.sum(kit * qb[h * idx_dh:(h + 1) * idx_dh], axis=0, keepdims=True)
            total = total + jnp.maximum(sc, 0.0) * wi8[h:h + 1, :]
        return total * scale

    o_ref[g] = jnp.full(o_ref.shape[1:], -jnp.inf, F32)
    for j in range(n_pages):
        o_ref[g, j:j + 1, :] = score(kit_refs[j][...])
    own = score(_lane_bcast_col(kin_ref[g]))
    lane = lax.broadcasted_iota(jnp.int32, (1, PAGE_ROWS), 1)
    o_ref[g, n_pages:n_pages + 1, :] = jnp.where(lane == 0, own, -jnp.inf)


def _dsa_decode_select_kernel(s_ref, m_ref, *, keep, n_valid):
    s = s_ref[...]
    idx = lax.broadcasted_iota(jnp.int32, s.shape, 1)
    count_fn = lambda pred, n_out=1: tuple(jnp.sum(hit, axis=1, keepdims=True) for hit in pred(s, idx))
    active = jnp.full((s.shape[0], 1), n_valid > keep, jnp.bool_)
    thr, cut = _select_threshold(count_fn, active, keep, (s.shape[1] - 1).bit_length())
    m_ref[...] = _selected(s, idx, thr, cut)


def _dsa_decode_attn_kernel(pt_ref, q_ref, kn_ref, vn_ref, mask_ref, *rest, n_pages, c_dh, seqs):
    staged = [_dsa_decode_attn_scores(g, q_ref, kn_ref, mask_ref, rest[g * n_pages:(g + 1) * n_pages],
                                      n_pages, c_dh) for g in range(seqs)]
    for g in range(seqs):
        _dsa_decode_attn_values(g, staged[g], vn_ref, rest[(seqs + g) * n_pages:(seqs + g + 1) * n_pages],
                                rest[-1], n_pages, c_dh)


def _dsa_decode_attn_scores(g, q_ref, kn_ref, mask_ref, k_refs, n_pages, c_dh):
    rows = k_refs[0].shape[0]
    group = C_HEADS // C_KV_HEADS
    scale = c_dh ** -0.5
    neg_inf = jnp.float32(-jnp.inf)
    qm = jnp.concatenate([q_ref[g].astype(F32), jnp.zeros((LANE - C_HEADS, c_dh), F32)], axis=0).astype(BF16)
    row_id = lax.broadcasted_iota(jnp.int32, (rows, LANE), 0)
    lane_id = lax.broadcasted_iota(jnp.int32, (rows, LANE), 1)
    own_kv = (row_id & (C_KV_HEADS - 1)) == _head_of(lane_id, group)
    mask = mask_ref[g]
    mask_t = jnp.concatenate([mask, jnp.zeros((LANE - mask.shape[0], PAGE_ROWS), F32)], axis=0).T
    repeat = jnp.where(_head_of(row_id, C_KV_HEADS) == lane_id, 1.0, 0.0).astype(BF16)
    row_mask = _dot(repeat, mask_t.astype(BF16))
    kn = jnp.concatenate([kn_ref[g], jnp.zeros((8 - C_KV_HEADS, c_dh), F32)], axis=0)
    s_own_kv = _dot_nt(kn.astype(BF16), qm)
    lane_kv = _head_of(lax.broadcasted_iota(jnp.int32, (1, LANE), 1), group)
    s_own = jnp.zeros((1, LANE), F32)
    for kv in range(C_KV_HEADS):
        s_own = jnp.where(lane_kv == kv, s_own_kv[kv:kv + 1, :], s_own)
    s_own = jnp.where(mask[n_pages:n_pages + 1, 0:1] > 0.0, s_own, neg_inf)
    m = jnp.maximum(s_own, NEG_BIG)
    masked = []
    for j in range(n_pages):
        s = _dot_nt(k_refs[j][...].astype(BF16), qm)
        sm = jnp.where(own_kv, jnp.where(row_mask[:, j:j + 1] > 0.5, s, neg_inf), neg_inf)
        masked.append(sm)
        m = jnp.maximum(m, jnp.max(sm, axis=0, keepdims=True))
    return masked, s_own, m, lane_kv


def _dsa_decode_attn_values(g, staged, vn_ref, v_refs, o_ref, n_pages, c_dh):
    masked, s_own, m, lane_kv = staged
    scale = c_dh ** -0.5
    own_rows = lax.broadcasted_iota(jnp.int32, (SUBLANE_BF16, LANE), 0)
    e_own = jnp.where(own_rows == lane_kv, jnp.exp((s_own - m) * scale), 0.0)
    v_own = jnp.concatenate([vn_ref[g], jnp.zeros((SUBLANE_BF16 - C_KV_HEADS, c_dh), F32)], axis=0)
    e_all = jnp.concatenate([jnp.exp((sm - m) * scale) for sm in masked] + [e_own], axis=0).astype(BF16)
    v_all = jnp.concatenate([v_refs[j][...] for j in range(n_pages)] + [v_own], axis=0).astype(BF16)
    v_all = jnp.concatenate([v_all, jnp.ones(v_all.shape, BF16)], axis=1)
    out = lax.dot_general(e_all, v_all, (((0,), (0,)), ((), ())), preferred_element_type=F32)
    o_ref[g] = (out[0:C_HEADS, 0:c_dh] / out[0:C_HEADS, c_dh:2 * c_dh]).astype(BF16)


def _dsa_decode(q, qi, wit, k_new, v_new, ki_new, cache_k, cache_v, cache_ki, page_table, c_dh, idx_dh, keep):
    n = q.shape[0]
    n_pages = page_table.shape[1]
    assert cache_k.shape[1] == PAGE_ROWS == LANE
    pool = cache_k.shape[0]
    ck = cache_k.reshape(pool, PAGE_ROWS * C_KV_HEADS, c_dh)
    cv = cache_v.reshape(pool, PAGE_ROWS * C_KV_HEADS, c_dh)
    ckit = jnp.transpose(cache_ki, (0, 2, 1))
    rows = -(-(n_pages + 1) // SUBLANE) * SUBLANE
    wi8 = wit.T.reshape(n, 8, 1)

    def seq3(g, a, b):
        return pl.BlockSpec((g, a, b), lambda s, pt: (s, 0, 0))

    def pages(g, r, w):
        return [pl.BlockSpec((None, r, w), lambda s, pt, j=j, k=k: (pt[s * g + k, j], 0, 0))
                for k in range(g) for j in range(n_pages)]

    gs = DECODE_SCORE_SEQS
    assert n % gs == 0 and n % DECODE_ATTN_SEQS == 0
    scores = pl.pallas_call(
        functools.partial(_dsa_decode_score_kernel, n_pages=n_pages, idx_dh=idx_dh, seqs=gs),
        grid_spec=pltpu.PrefetchScalarGridSpec(
            num_scalar_prefetch=1, grid=(n // gs,),
            in_specs=[seq3(gs, 1, IDX_HEADS * idx_dh), seq3(gs, 8, 1), seq3(gs, 1, idx_dh)]
                     + pages(gs, idx_dh, PAGE_ROWS),
            out_specs=seq3(gs, rows, PAGE_ROWS)),
        out_shape=jax.ShapeDtypeStruct((n, rows, PAGE_ROWS), F32),
        compiler_params=_params("arbitrary"),
        name="dsa_decode_scores",
    )(page_table, qi.reshape(n, 1, IDX_HEADS * idx_dh), wi8, ki_new.reshape(n, 1, idx_dh),
      *([ckit] * (gs * n_pages)))
    flat = rows * PAGE_ROWS
    mask = pl.pallas_call(
        functools.partial(_dsa_decode_select_kernel, keep=keep, n_valid=n_pages * PAGE_ROWS + 1),
        grid=(1,),
        in_specs=[_full((n, flat))],
        out_specs=_full((n, flat)),
        out_shape=jax.ShapeDtypeStruct((n, flat), F32),
        compiler_params=_params("arbitrary"),
        name="dsa_decode_select",
    )(scores.reshape(n, flat))
    cq = C_HEADS * c_dh
    ga = DECODE_ATTN_SEQS
    kv_rows = PAGE_ROWS * C_KV_HEADS
    out = pl.pallas_call(
        functools.partial(_dsa_decode_attn_kernel, n_pages=n_pages, c_dh=c_dh, seqs=ga),
        grid_spec=pltpu.PrefetchScalarGridSpec(
            num_scalar_prefetch=1, grid=(n // ga,),
            in_specs=[seq3(ga, C_HEADS, c_dh), seq3(ga, C_KV_HEADS, c_dh), seq3(ga, C_KV_HEADS, c_dh),
                      seq3(ga, rows, PAGE_ROWS)] + pages(ga, kv_rows, c_dh) + pages(ga, kv_rows, c_dh),
            out_specs=seq3(ga, C_HEADS, c_dh)),
        out_shape=jax.ShapeDtypeStruct((n, C_HEADS, c_dh), BF16),
        compiler_params=_params("arbitrary"),
        name="dsa_decode_attn",
    )(page_table, q.reshape(n, C_HEADS, c_dh), k_new.reshape(n, C_KV_HEADS, c_dh),
      v_new.reshape(n, C_KV_HEADS, c_dh), mask.reshape(n, rows, PAGE_ROWS),
      *([ck] * (ga * n_pages)), *([cv] * (ga * n_pages)))
    return out.reshape(n, cq)


def kernel(x_prompt, x_sample, cache_moba_k, cache_moba_v, cache_dsa_k, cache_dsa_v, cache_dsa_kidx,
           state_ffn_conv, page_table, w_in_even, g_sgu, w_sgu, b_sgu, w_out_even, w_in_odd, w_out_odd,
           g_mix, g_ffn, w_up, w_conv, b_conv, w_down, g_final):
    batch, seq, d = x_prompt.shape
    n_dec, dec_seq, _ = x_sample.shape
    depth = g_mix.shape[0]
    n_pages = page_table.shape[1]
    past_len = n_pages * cache_moba_k.shape[2]
    assert depth == 2 and dec_seq == 1, "one even + one odd layer, one decode token per sequence"
    b_dh = cache_moba_k.shape[-1]
    c_dh = cache_dsa_k.shape[-1]
    idx_dh = cache_dsa_kidx.shape[-1]
    keep_p = min(DSA_TOPK, seq // 4)
    keep_s = min(DSA_TOPK, (past_len + dec_seq) // 4)
    xp = x_prompt.reshape(batch * seq, d)
    xs = x_sample.reshape(n_dec, d)
    bf = lambda w: w.astype(BF16)

    tp64, half64 = _rope_tables(b_dh, seq, 0)
    ts64, _ = _rope_tables(b_dh, 1, past_len)
    tp128, half128 = _rope_tables(c_dh, seq, 0)
    ts128, _ = _rope_tables(c_dh, 1, past_len)
    assert idx_dh == b_dh

    w_in0, w_out0 = bf(w_in_even[0]), bf(w_out_even[0])
    a_p, q_p, kb_p, mkt_p, mvt_p, vt_p, km_p = _even_in_prompt(
        xp, seq, g_mix[0], w_in0, tp64, half64, g_sgu[0], w_sgu[0], b_sgu[0])
    bo_p = _moba_prompt(q_p, kb_p, vt_p, km_p, batch, seq, b_dh)
    cache_view = lambda t: jnp.transpose(t.reshape(batch, B_HEADS, b_dh, seq), (0, 3, 1, 2))[None]
    a_s, q_s, mk_s, mv_s, va_s = _even_in_decode(xs, g_mix[0], w_in0, ts64, half64, g_sgu[0], w_sgu[0], b_sgu[0])
    bo_s = _moba_decode(q_s, mk_s, mv_s, cache_moba_k[0], cache_moba_v[0], page_table, b_dh)
    w_up0, w_down0 = bf(w_up[0]), bf(w_down[0])
    xp, hp = _residual_proj(xp, [a_p, bo_p], w_out0, g_ffn[0], "next", ROW_TILE, "out_even_prompt")
    xs, hs = _residual_proj(xs, [a_s, bo_s], w_out0, g_ffn[0], "next", ROW_TILE, "out_even_decode")
    act_p, conv_p0 = _ffn_up_prompt(hp, seq, w_up0, w_conv[0], b_conv[0], ROW_TILE)
    act_s, conv_s0 = _ffn_up_decode(hs, state_ffn_conv[0], w_up0, w_conv[0], b_conv[0])
    xp, hp = _residual_proj(xp, [act_p], w_down0, g_mix[1], "next", ROW_TILE, "ffn_down0_prompt")
    xs, hs = _residual_proj(xs, [act_s], w_down0, g_mix[1], "next", ROW_TILE, "ffn_down0_decode")

    w_out1 = bf(w_out_odd[0])
    q_p, dk_p, dv_p, kb_p, vt_p, qi_p, kif_p, ki2_p, wit_p = _odd_in(
        hp, seq, w_in_odd[0], tp64, half64, tp128, half128, c_dh, idx_dh, ROW_TILE, True)
    o_p = _dsa_prompt(q_p, qi_p, wit_p, kb_p, vt_p, ki2_p, batch, seq, c_dh, idx_dh, keep_p, DSA_QUERY_TILE, DSA_KEY_CHUNK)
    q_s, dk_s, dv_s, _, _, qi_s, kif_s, _, wit_s = _odd_in(
        hs, None, w_in_odd[0], ts64, half64, ts128, half128, c_dh, idx_dh, 256, False)
    o_s = _dsa_decode(q_s, qi_s, wit_s, dk_s, dv_s, kif_s[:, :idx_dh], cache_dsa_k[0], cache_dsa_v[0],
                      cache_dsa_kidx[0], page_table, c_dh, idx_dh, keep_s)
    w_up1, w_down1 = bf(w_up[1]), bf(w_down[1])
    xp, hp = _residual_proj(xp, [o_p], w_out1, g_ffn[1], "next", ROW_TILE, "out_odd_prompt")
    xs, hs = _residual_proj(xs, [o_s], w_out1, g_ffn[1], "next", ROW_TILE, "out_odd_decode")
    act_p, conv_p1 = _ffn_up_prompt(hp, seq, w_up1, w_conv[1], b_conv[1], ROW_TILE)
    act_s, conv_s1 = _ffn_up_decode(hs, state_ffn_conv[1], w_up1, w_conv[1], b_conv[1])
    y_p, = _residual_proj(xp, [act_p], w_down1, g_final, "final", ROW_TILE, "ffn_down1_prompt")
    y_s, = _residual_proj(xs, [act_s], w_down1, g_final, "final", ROW_TILE, "ffn_down1_decode")

    ckv = C_KV_HEADS
    return (y_p.reshape(batch, seq, d), y_s.reshape(n_dec, dec_seq, d),
            cache_view(mkt_p), cache_view(mvt_p),
            mk_s.reshape(1, n_dec, dec_seq, B_HEADS, b_dh), mv_s.reshape(1, n_dec, dec_seq, B_HEADS, b_dh),
            va_s.reshape(1, n_dec, dec_seq, A_GROUPS, LANE),
            dk_p.reshape(1, batch, seq, ckv, c_dh), dv_p.reshape(1, batch, seq, ckv, c_dh),
            kif_p[:, :idx_dh].reshape(1, batch, seq, idx_dh),
            dk_s.reshape(1, n_dec, dec_seq, ckv, c_dh), dv_s.reshape(1, n_dec, dec_seq, ckv, c_dh),
            kif_s[:, :idx_dh].reshape(1, n_dec, dec_seq, idx_dh),
            jnp.stack([conv_p0, conv_p1]), jnp.stack([conv_s0, conv_s1]))
```

```python
import functools

import numpy as np
import jax
import jax.numpy as jnp
from jax import lax
from jax.experimental import pallas as pl
from jax.experimental.pallas import tpu as pltpu

F32 = jnp.float32
BF16 = jnp.bfloat16

A_GROUPS = 4
CHUNK = 128
B_HEADS = 8
MOBA_BLOCK = 256
MOBA_TOPK = 3
C_HEADS = 8
C_KV_HEADS = 2
IDX_HEADS = 4
DSA_TOPK = 256
ROPE_THETA = 500000.0
ROPE_FRAC = 4
CONV_W = 3
EPS = 1e-6

LANE = 128
SUBLANE = 8
SUBLANE_BF16 = 16
VMEM_LIMIT_BYTES = 56 * 1024 * 1024
NEG_BIG = -1e30
LOG2_E = 1.4426950408889634
ROW_TILE = 512
DSA_QUERY_TILE = 256
DSA_KEY_CHUNK = 256


def _params(*sem):
    return pltpu.CompilerParams(dimension_semantics=sem,
                                vmem_limit_bytes=VMEM_LIMIT_BYTES)


def _full(shape):
    nd = len(shape)
    return pl.BlockSpec(shape, lambda *_: (0,) * nd)


def _rmsnorm(x, g):
    ms = jnp.mean(x * x, axis=-1, keepdims=True)
    return x * lax.rsqrt(ms + EPS) * g


def _rope_apply(x, c, s1, s2, half):
    parts = []
    for j in range(x.shape[1] // LANE):
        xs = x[:, j * LANE:(j + 1) * LANE]
        parts.append(xs * c + pltpu.roll(xs, LANE - half, 1) * s1
                     + pltpu.roll(xs, half, 1) * s2)
    return parts[0] if len(parts) == 1 else jnp.concatenate(parts, axis=1)


def _head_of(index, dh):
    assert dh & (dh - 1) == 0
    return lax.shift_right_logical(index, dh.bit_length() - 1)


ONES_ROWS = SUBLANE_BF16


def _with_ones_rows(vt, dh):
    ones = jnp.ones((ONES_ROWS, vt.shape[1]), vt.dtype)
    parts = []
    for h in range(vt.shape[0] // dh):
        parts += [vt[h * dh:(h + 1) * dh], ones]
    return jnp.concatenate(parts, axis=0)


def _dot(a, b):
    return jnp.dot(a, b, preferred_element_type=F32)


def _dot_nt(a, b):
    return lax.dot_general(a, b, (((1,), (1,)), ((), ())),
                           preferred_element_type=F32)


def _rope_rows(dh):
    rd = dh // ROPE_FRAC
    half = rd // 2
    inv = ROPE_THETA ** (-jnp.arange(half, dtype=F32) * (2.0 / rd))
    r = np.arange(LANE) % dh
    j = np.where(r < half, r, np.where(r < rd, r - half, 0))
    inv_lane = jnp.where(jnp.asarray(r < rd), inv[j], 0.0).reshape(1, LANE)
    s1 = jnp.asarray(np.where(r < half, -1.0, 0.0), F32).reshape(1, LANE)
    s2 = jnp.asarray(np.where((r >= half) & (r < rd), 1.0, 0.0), F32).reshape(1, LANE)
    return inv_lane, s1, s2, half


def _rope_table_kernel(inv_ref, s1_ref, s2_ref, c_ref, a_ref, b_ref, *, base):
    rows = c_ref.shape[0]
    pos = lax.broadcasted_iota(jnp.int32, (rows, LANE), 0) + (base + pl.program_id(0) * rows)
    ang = pos.astype(F32) * inv_ref[...]
    s = jnp.sin(ang)
    c_ref[...] = jnp.cos(ang)
    a_ref[...] = s * s1_ref[...]
    b_ref[...] = s * s2_ref[...]


def _rope_tables(dh, n_pos, base):
    inv_lane, s1, s2, half = _rope_rows(dh)
    rows = min(n_pos, ROW_TILE)
    assert n_pos % rows == 0
    out = jax.ShapeDtypeStruct((n_pos, LANE), F32)
    spec = pl.BlockSpec((rows, LANE), lambda i: (i, 0))
    c, a, b = pl.pallas_call(
        functools.partial(_rope_table_kernel, base=base),
        grid=(n_pos // rows,),
        in_specs=[_full((1, LANE))] * 3,
        out_specs=[spec] * 3,
        out_shape=[out] * 3,
        compiler_params=_params("arbitrary"),
        name="rope_tables",
    )(inv_lane, s1, s2)
    return (c, a, b), half


def _sgu_values(h, w_ref, gs_ref, aw):
    u = jax.nn.gelu(_dot(h, w_ref[:, 0:aw]))
    zv = jax.nn.gelu(_dot(h, w_ref[:, aw:2 * aw]))
    vs = []
    for g in range(A_GROUPS):
        vg = zv[:, g * LANE:(g + 1) * LANE]
        vs.append(_rmsnorm(vg, gs_ref[:, g * LANE:(g + 1) * LANE]))
    return u, vs


def _even_in_prompt_kernel(x_ref, g_ref, w_ref, c_ref, s1_ref, s2_ref, gs_ref, ws_ref, bs_ref,
                           a_ref, q_ref, kb_ref, kt_ref, vt_ref, vtb_ref, km_ref, v_scr, *, half, scale):
    tm = x_ref.shape[0]
    aw = A_GROUPS * LANE
    bw = q_ref.shape[1]
    h = _rmsnorm(x_ref[...], g_ref[...]).astype(BF16)
    u, vs = _sgu_values(h, w_ref, gs_ref, aw)
    tri = (lax.broadcasted_iota(jnp.int32, (CHUNK, CHUNK), 0)
           >= lax.broadcasted_iota(jnp.int32, (CHUNK, CHUNK), 1))
    for g in range(A_GROUPS):
        wg = jnp.where(tri, ws_ref[g], 0.0).astype(BF16)
        for cc in range(tm // CHUNK):
            rows = slice(cc * CHUNK, (cc + 1) * CHUNK)
            mixed = _dot(wg, vs[g][rows].astype(BF16)) + bs_ref[g]
            a_ref[rows, g * LANE:(g + 1) * LANE] = (u[rows, g * LANE:(g + 1) * LANE] * mixed).astype(BF16)
    c, s1, s2 = c_ref[...], s1_ref[...], s2_ref[...]
    q = _rope_apply(_dot(h, w_ref[:, 2 * aw:2 * aw + bw]), c, s1, s2, half)
    q_ref[...] = (q * scale).astype(BF16)
    k = _rope_apply(_dot(h, w_ref[:, 2 * aw + bw:2 * aw + 2 * bw]), c, s1, s2, half)
    kt_ref[...] = k.T
    kb_ref[...] = k.astype(BF16)
    for r in range(tm // MOBA_BLOCK):
        block = k[r * MOBA_BLOCK:(r + 1) * MOBA_BLOCK]
        km_ref[r] = jnp.sum(block, axis=0, keepdims=True) * (1.0 / MOBA_BLOCK)
    v_scr[...] = _dot(h, w_ref[:, 2 * aw + 2 * bw:2 * aw + 3 * bw])
    vt = v_scr[...].T
    vt_ref[...] = vt
    vtb_ref[...] = _with_ones_rows(vt, bw // B_HEADS).astype(BF16)


def _even_in_decode_kernel(x_ref, g_ref, w_ref, c_ref, s1_ref, s2_ref, gs_ref, w00_ref, b0_ref,
                           a_ref, q_ref, k_ref, v_ref, va_ref, *, half, scale):
    aw = A_GROUPS * LANE
    bw = q_ref.shape[1]
    h = _rmsnorm(x_ref[...], g_ref[...]).astype(BF16)
    u, vs = _sgu_values(h, w_ref, gs_ref, aw)
    va = jnp.concatenate(vs, axis=1)
    va_ref[...] = va
    a_ref[...] = (u * (w00_ref[...] * va + b0_ref[...])).astype(BF16)
    c, s1, s2 = c_ref[...], s1_ref[...], s2_ref[...]
    q = _rope_apply(_dot(h, w_ref[:, 2 * aw:2 * aw + bw]), c, s1, s2, half)
    q_ref[...] = (q * scale).astype(BF16)
    k_ref[...] = _rope_apply(_dot(h, w_ref[:, 2 * aw + bw:2 * aw + 2 * bw]), c, s1, s2, half)
    v_ref[...] = _dot(h, w_ref[:, 2 * aw + 2 * bw:2 * aw + 3 * bw])


def _even_in_prompt(x, seq, g_mix, w_in, tables, half, g_sgu, w_sgu, b_sgu):
    n, d = x.shape
    tm = ROW_TILE
    assert tm % MOBA_BLOCK == 0 and seq % tm == 0
    blocks = tm // MOBA_BLOCK
    bw = (w_in.shape[1] - 2 * A_GROUPS * LANE) // 3
    dh = bw // B_HEADS
    tiles_per_seq = seq // tm
    row = lambda w, dt: (pl.BlockSpec((tm, w), lambda i: (i, 0)), jax.ShapeDtypeStruct((n, w), dt))
    tab = pl.BlockSpec((tm, LANE), lambda i: (i % tiles_per_seq, 0))
    col = lambda h, dt: (pl.BlockSpec((None, h, tm), lambda i: (i // tiles_per_seq, 0, i % tiles_per_seq)),
                         jax.ShapeDtypeStruct((n // seq, h, seq), dt))
    outs = [row(A_GROUPS * LANE, BF16), row(bw, BF16), row(bw, BF16), col(bw, F32), col(bw, F32),
            col(bw + B_HEADS * ONES_ROWS, BF16),
            (pl.BlockSpec((blocks, 1, bw), lambda i: (i, 0, 0)), jax.ShapeDtypeStruct((n // MOBA_BLOCK, 1, bw), F32))]
    bs = jnp.broadcast_to(b_sgu[:, :, None], (A_GROUPS, CHUNK, LANE))
    return pl.pallas_call(
        functools.partial(_even_in_prompt_kernel, half=half, scale=dh ** -0.5),
        grid=(n // tm,),
        in_specs=[pl.BlockSpec((tm, d), lambda i: (i, 0)), _full((1, d)), _full(w_in.shape),
                  tab, tab, tab, _full((1, A_GROUPS * LANE)), _full(w_sgu.shape), _full(bs.shape)],
        out_specs=[o[0] for o in outs],
        out_shape=[o[1] for o in outs],
        scratch_shapes=[pltpu.VMEM((tm, bw), F32)],
        compiler_params=_params("arbitrary"),
        name="even_in_prompt",
    )(x, g_mix.reshape(1, d), w_in, *tables, g_sgu.reshape(1, -1), w_sgu, bs)


def _even_in_decode(x, g_mix, w_in, tables, half, g_sgu, w_sgu, b_sgu):
    n, d = x.shape
    bw = (w_in.shape[1] - 2 * A_GROUPS * LANE) // 3
    dh = bw // B_HEADS
    aw = A_GROUPS * LANE
    w00 = jnp.repeat(w_sgu[:, 0, 0], LANE).reshape(1, aw)
    b0 = jnp.repeat(b_sgu[:, 0], LANE).reshape(1, aw)
    shapes = [(aw, BF16), (bw, BF16), (bw, F32), (bw, F32), (aw, F32)]
    return pl.pallas_call(
        functools.partial(_even_in_decode_kernel, half=half, scale=dh ** -0.5),
        grid=(1,),
        in_specs=[_full((n, d)), _full((1, d)), _full(w_in.shape)] + [_full((1, LANE))] * 3
                 + [_full((1, aw))] * 3,
        out_specs=[_full((n, w)) for w, _ in shapes],
        out_shape=[jax.ShapeDtypeStruct((n, w), dt) for w, dt in shapes],
        compiler_params=_params("arbitrary"),
        name="even_in_decode",
    )(x, g_mix.reshape(1, d), w_in, *tables, g_sgu.reshape(1, -1), w00, b0)


def _moba_prompt_kernel(q_ref, k_ref, vt_ref, km_ref, o_ref, sel_ref, m_ref, acc_ref, s_a, s_b, *, dh):
    blk = MOBA_BLOCK
    tq = q_ref.shape[0]
    i = pl.program_id(1)
    nblk = km_ref.shape[1]
    heads = q_ref.shape[1] // dh
    ext = dh + ONES_ROWS
    per_group = LANE // dh
    lane = lax.broadcasted_iota(jnp.int32, (tq, LANE), 1)
    blk_id = lax.broadcasted_iota(jnp.int32, (nblk, tq), 0)
    neg_inf = jnp.float32(-jnp.inf)
    qms = []
    for h in range(heads):
        grp = h // per_group
        q = q_ref[:, grp * LANE:(grp + 1) * LANE]
        qm = jnp.where(_head_of(lane, dh) == h % per_group, q, jnp.zeros_like(q))
        qms.append(qm)
        kmean = km_ref[0, :, grp * LANE:(grp + 1) * LANE].astype(BF16)
        gate = jnp.where(blk_id < i, _dot_nt(kmean, qm), neg_inf)
        sel = jnp.zeros((nblk, tq), F32)
        for _ in range(MOBA_TOPK):
            best = jnp.max(gate, axis=0, keepdims=True)
            first = jnp.min(jnp.where(gate == best, blk_id, nblk), axis=0, keepdims=True)
            hit = blk_id == first
            sel = jnp.where(hit, 1.0, sel)
            gate = jnp.where(hit, neg_inf, gate)
        sel_ref[h] = jnp.where(blk_id < i, sel, 0.0)

    causal = (lax.broadcasted_iota(jnp.int32, (blk, tq), 0)
              <= lax.broadcasted_iota(jnp.int32, (blk, tq), 1))

    head_row = lax.broadcasted_iota(jnp.int32, (heads, tq), 0)

    def score_matmuls(block):
        start = pl.multiple_of(block * blk, blk)
        kb = k_ref[pl.ds(start, blk), :]
        return [_dot_nt(kb[:, (h // per_group) * LANE:(h // per_group + 1) * LANE], qms[h])
                for h in range(heads)]

    def attend(scores, vtb, keeps, first):
        if not first:
            m_old = m_ref[...]
            acc_old = [acc_ref[h] for h in range(heads)]
        m_all = jnp.zeros((heads, tq), F32)
        probs, alphas = [], []
        for h in range(heads):
            if first:
                s = jnp.where(keeps[h], scores[h], neg_inf)
                m_new = jnp.max(s, axis=0, keepdims=True)
                shift = m_new
            else:
                s = scores[h]
                s_max = jnp.where(keeps[h], jnp.max(s, axis=0, keepdims=True), neg_inf)
                m_new = jnp.maximum(m_old[h:h + 1, :], s_max)
                alphas.append(jnp.exp(m_old[h:h + 1, :] - m_new))
                shift = jnp.where(keeps[h], m_new, jnp.inf)
            probs.append(jnp.exp(s - shift).astype(BF16))
            m_all = jnp.where(head_row == h, m_new, m_all)
        pvs = [_dot(vtb[h * ext:(h + 1) * ext, :], probs[h]) for h in range(heads)]
        m_ref[...] = m_all
        for h in range(heads):
            acc_ref[h] = pvs[h] if first else acc_old[h] * alphas[h] + pvs[h]

    start = pl.multiple_of(i * blk, blk)
    attend(score_matmuls(i), vt_ref[:, pl.ds(start, blk)], [causal] * heads, True)

    for h, s in enumerate(score_matmuls(0)):
        s_a[h] = s

    def step(n, read_scr, write_scr):
        for h, s in enumerate(score_matmuls(jnp.minimum(n + 1, jnp.maximum(i - 1, 0)))):
            write_scr[h] = s
        start = pl.multiple_of(n * blk, blk)
        keeps = [sel_ref[h, pl.ds(n, 1), :] > 0.0 for h in range(heads)]
        attend([read_scr[h] for h in range(heads)], vt_ref[:, pl.ds(start, blk)], keeps, False)

    def body(pair, _):
        step(2 * pair, s_a, s_b)

        @pl.when(2 * pair + 1 < i)
        def _():
            step(2 * pair + 1, s_b, s_a)
        return 0

    lax.fori_loop(0, (i + 1) // 2, body, 0)
    outs = [acc_ref[h, 0:dh, :] / acc_ref[h, dh:dh + 1, :] for h in range(heads)]
    o_ref[...] = jnp.concatenate(outs, axis=0).T.astype(BF16)


def _moba_prompt(q, kb, vt, kmean, batch, seq, dh):
    n, bw = q.shape
    tq = MOBA_BLOCK
    nblk = seq // MOBA_BLOCK
    heads = bw // dh
    km = kmean.reshape(batch, nblk, bw)
    tiles = seq // tq
    return pl.pallas_call(
        functools.partial(_moba_prompt_kernel, dh=dh),
        grid=(batch, tiles),
        in_specs=[pl.BlockSpec((tq, bw), lambda b, i: (b * tiles + i, 0)),
                  pl.BlockSpec((seq, bw), lambda b, i: (b, 0)),
                  pl.BlockSpec((None, vt.shape[1], seq), lambda b, i: (b, 0, 0)),
                  pl.BlockSpec((1, nblk, bw), lambda b, i: (b, 0, 0))],
        out_specs=pl.BlockSpec((tq, bw), lambda b, i: (b * tiles + i, 0)),
        out_shape=jax.ShapeDtypeStruct((n, bw), BF16),
        scratch_shapes=[pltpu.VMEM((heads, nblk, tq), F32), pltpu.VMEM((heads, tq), F32),
                        pltpu.VMEM((heads, dh + ONES_ROWS, tq), F32),
                        pltpu.VMEM((heads, MOBA_BLOCK, tq), F32), pltpu.VMEM((heads, MOBA_BLOCK, tq), F32)],
        compiler_params=_params("arbitrary", "arbitrary"),
        name="moba_prompt",
    )(q, kb, vt, km)


def _residual_proj_kernel(*refs, n_in, norm):
    x_ref = refs[0]
    in_refs = refs[1:1 + n_in]
    w_ref = refs[1 + n_in]
    y = x_ref[...]
    lo = 0
    for r in in_refs:
        y = y + _dot(r[...], w_ref[lo:lo + r.shape[1], :])
        lo += r.shape[1]
    if norm == "final":
        g_ref, o_ref = refs[2 + n_in:]
        o_ref[...] = _rmsnorm(y, g_ref[...])
    else:
        g_ref, o_ref, h_ref = refs[2 + n_in:]
        o_ref[...] = y
        h_ref[...] = _rmsnorm(y, g_ref[...]).astype(BF16)


def _residual_proj(x, ins, w, g, norm, tm, name):
    n, d = x.shape
    tm = min(tm, n)
    row = lambda width: pl.BlockSpec((tm, width), lambda i: (i, 0))
    out_specs = [row(d)] if norm == "final" else [row(d), row(d)]
    out_shape = ([jax.ShapeDtypeStruct((n, d), F32)] if norm == "final"
                 else [jax.ShapeDtypeStruct((n, d), F32), jax.ShapeDtypeStruct((n, d), BF16)])
    return pl.pallas_call(
        functools.partial(_residual_proj_kernel, n_in=len(ins), norm=norm),
        grid=(n // tm,),
        in_specs=[row(d)] + [row(a.shape[1]) for a in ins] + [_full(w.shape), _full((1, d))],
        out_specs=out_specs,
        out_shape=out_shape,
        compiler_params=_params("arbitrary"),
        name=name,
    )(x, *ins, w, g.reshape(1, d))


FFN_COLS = 256
FFN_HALO = SUBLANE_BF16


def _silu_gate(gate, value):
    return gate * (1.0 / (1.0 + jnp.exp(-gate))) * value


def _ffn_up_prompt_kernel(h_ref, halo_ref, w_ref, wc_ref, bc_ref, act_ref, tail_ref, up_scr, *, tiles_per_seq):
    tm = h_ref.shape[0]
    dff = act_ref.shape[1]
    seq_start = pl.program_id(0) % tiles_per_seq == 0
    halo = halo_ref[...]
    halo = jnp.where(seq_start, jnp.zeros_like(halo), halo)
    hh = jnp.concatenate([halo, h_ref[...]], axis=0)
    for c in range(dff // FFN_COLS):
        halves = []
        for k, lo in enumerate((c * FFN_COLS, dff + c * FFN_COLS)):
            cols = slice(lo, lo + FFN_COLS)
            up = _dot(hh, w_ref[:, cols])
            tail_ref[0, :, cols] = up[tm + FFN_HALO - SUBLANE:, :]
            up_scr[k] = up
            conv = bc_ref[:, cols] + up[FFN_HALO:, :] * wc_ref[CONV_W - 1:CONV_W, cols]
            for j in range(1, CONV_W):
                conv = conv + up_scr[k, pl.ds(FFN_HALO - j, tm), :] * wc_ref[CONV_W - 1 - j:CONV_W - j, cols]
            halves.append(conv)
        act_ref[:, c * FFN_COLS:(c + 1) * FFN_COLS] = _silu_gate(halves[1], halves[0]).astype(BF16)


def _ffn_up_decode_kernel(h_ref, s0_ref, s1_ref, w_ref, wc_ref, bc_ref, act_ref, up_ref):
    dff = act_ref.shape[1]
    h = h_ref[...]
    for c in range(dff // FFN_COLS):
        halves = []
        for lo in (c * FFN_COLS, dff + c * FFN_COLS):
            cols = slice(lo, lo + FFN_COLS)
            up = _dot(h, w_ref[:, cols])
            up_ref[:, cols] = up
            halves.append(bc_ref[:, cols] + s0_ref[:, cols] * wc_ref[0:1, cols]
                          + s1_ref[:, cols] * wc_ref[1:2, cols] + up * wc_ref[2:3, cols])
        act_ref[:, c * FFN_COLS:(c + 1) * FFN_COLS] = _silu_gate(halves[1], halves[0]).astype(BF16)


def _ffn_up_prompt(h, seq, w_up, w_conv, b_conv, tm):
    n, d = h.shape
    dff = w_up.shape[1] // 2
    assert dff % FFN_COLS == 0 and seq % tm == 0 and tm % FFN_HALO == 0
    ratio = tm // FFN_HALO
    act, tail = pl.pallas_call(
        functools.partial(_ffn_up_prompt_kernel, tiles_per_seq=seq // tm),
        grid=(n // tm,),
        in_specs=[pl.BlockSpec((tm, d), lambda i: (i, 0)),
                  pl.BlockSpec((FFN_HALO, d), lambda i: (jnp.maximum(i * ratio - 1, 0), 0)),
                  _full(w_up.shape), _full(w_conv.shape), _full((1, 2 * dff))],
        out_specs=[pl.BlockSpec((tm, dff), lambda i: (i, 0)),
                   pl.BlockSpec((1, SUBLANE, 2 * dff), lambda i: (i, 0, 0))],
        out_shape=[jax.ShapeDtypeStruct((n, dff), BF16),
                   jax.ShapeDtypeStruct((n // tm, SUBLANE, 2 * dff), F32)],
        scratch_shapes=[pltpu.VMEM((2, tm + FFN_HALO, FFN_COLS), F32)],
        compiler_params=_params("arbitrary"),
        name="ffn_up_prompt",
    )(h, h, w_up, w_conv, b_conv.reshape(1, -1))
    last = tail.reshape(n // seq, seq // tm, SUBLANE, 2 * dff)[:, -1, SUBLANE - (CONV_W - 1):, :]
    return act, last


def _ffn_up_decode(h, state, w_up, w_conv, b_conv):
    n, d = h.shape
    dff = w_up.shape[1] // 2
    act, up = pl.pallas_call(
        _ffn_up_decode_kernel,
        grid=(1,),
        in_specs=[_full((n, d)), _full((n, 2 * dff)), _full((n, 2 * dff)),
                  _full(w_up.shape), _full(w_conv.shape), _full((1, 2 * dff))],
        out_specs=[_full((n, dff)), _full((n, 2 * dff))],
        out_shape=[jax.ShapeDtypeStruct((n, dff), BF16), jax.ShapeDtypeStruct((n, 2 * dff), F32)],
        compiler_params=_params("arbitrary"),
        name="ffn_up_decode",
    )(h, state[:, 0], state[:, 1], w_up, w_conv, b_conv.reshape(1, -1))
    return act, jnp.stack([state[:, 1], up], axis=1)


def _odd_in_kernel(h_ref, w_ref, c64_ref, a64_ref, b64_ref, c128_ref, a128_ref, b128_ref,
                   q_ref, k_ref, v_ref, kb_ref, vt_ref, qi_ref, kif_ref, ki2_ref, wit_ref,
                   *, half64, half128, idx_dh):
    cq = q_ref.shape[1]
    ckv = kb_ref.shape[1]
    ciq = qi_ref.shape[1]
    h = h_ref[...]
    t64 = (c64_ref[...], a64_ref[...], b64_ref[...])
    t128 = (c128_ref[...], a128_ref[...], b128_ref[...])
    q_ref[...] = _rope_apply(_dot(h, w_ref[:, 0:cq]), *t128, half128).astype(BF16)
    k = _rope_apply(_dot(h, w_ref[:, cq:cq + ckv]), *t128, half128)
    kb_ref[...] = k.astype(BF16)
    v = _dot(h, w_ref[:, cq + ckv:cq + 2 * ckv])
    c_dh = ckv // C_KV_HEADS
    for kv in range(C_KV_HEADS):
        k_ref[pl.ds(kv, h.shape[0], stride=C_KV_HEADS), :] = k[:, kv * c_dh:(kv + 1) * c_dh]
        v_ref[pl.ds(kv, h.shape[0], stride=C_KV_HEADS), :] = v[:, kv * c_dh:(kv + 1) * c_dh]
    if vt_ref is not None:
        vt_ref[...] = _with_ones_rows(v.T, ckv // C_KV_HEADS).astype(BF16)
    lo = cq + 2 * ckv
    qi_ref[...] = _rope_apply(_dot(h, w_ref[:, lo:lo + ciq]), *t64, half64).astype(BF16)
    tail = _dot(h, w_ref[:, lo + ciq:lo + ciq + LANE])
    ki = _rope_apply(tail, *t64, half64)
    kif_ref[...] = ki
    lane = lax.broadcasted_iota(jnp.int32, ki.shape, 1)
    ki2_ref[...] = jnp.where(lane < idx_dh, ki, pltpu.roll(ki, idx_dh, 1)).astype(BF16)
    wit_ref[...] = tail.T[idx_dh:idx_dh + 8, :] * (IDX_HEADS ** -0.5)


def _odd_in(h, seq, w_in, tabs64, half64, tabs128, half128, c_dh, idx_dh, tm, with_vt):
    n, d = h.shape
    tm = min(tm, n)
    cq = C_HEADS * c_dh
    ckv = C_KV_HEADS * c_dh
    ciq = IDX_HEADS * idx_dh
    cols = cq + 2 * ckv + ciq + LANE
    assert 2 * idx_dh == LANE and w_in.shape[1] <= cols
    w = jnp.pad(w_in, ((0, 0), (0, cols - w_in.shape[1]))).astype(BF16)
    if seq is None:
        tab = pl.BlockSpec((1, LANE), lambda i: (0, 0))
    else:
        tiles_per_seq = seq // tm
        tab = pl.BlockSpec((tm, LANE), lambda i: (i % tiles_per_seq, 0))
    row = lambda width, dt: (pl.BlockSpec((tm, width), lambda i: (i, 0)), jax.ShapeDtypeStruct((n, width), dt))
    col = lambda height, dt: (pl.BlockSpec((height, tm), lambda i: (0, i)), jax.ShapeDtypeStruct((height, n), dt))
    pair_rows = (pl.BlockSpec((tm * C_KV_HEADS, c_dh), lambda i: (i, 0)),
                 jax.ShapeDtypeStruct((n * C_KV_HEADS, c_dh), F32))
    outs = [row(cq, BF16), pair_rows, pair_rows, row(ckv, BF16)]
    if with_vt:
        outs.append(col(ckv + C_KV_HEADS * ONES_ROWS, BF16))
    outs += [row(ciq, BF16), row(LANE, F32), row(LANE, BF16), col(8, F32)]

    def body(*refs):
        refs = list(refs)
        if not with_vt:
            refs.insert(8 + 4, None)
        _odd_in_kernel(*refs, half64=half64, half128=half128, idx_dh=idx_dh)

    res = pl.pallas_call(
        body,
        grid=(n // tm,),
        in_specs=[pl.BlockSpec((tm, d), lambda i: (i, 0)), _full(w.shape)] + [tab] * 6,
        out_specs=[o[0] for o in outs],
        out_shape=[o[1] for o in outs],
        compiler_params=_params("arbitrary"),
        name="odd_in_prompt" if with_vt else "odd_in_decode",
    )(h, w, *tabs64, *tabs128)
    res = list(res)
    if not with_vt:
        res.insert(4, None)
    return res


FLOAT_BITS = 32
INDEX_BIG = 2 ** 30


def _ind(cond):
    return jnp.where(cond, 1.0, 0.0)


def _ordered_bits(x, to):
    b = x if x.dtype == jnp.int32 else lax.bitcast_convert_type(x, jnp.int32)
    b = jnp.where(b < 0, b ^ 0x7FFFFFFF, b)
    return b if to == jnp.int32 else lax.bitcast_convert_type(b, F32)


def _ordered_bits16(x):
    b = lax.shift_right_arithmetic(lax.bitcast_convert_type(x, jnp.int32), 16)
    return jnp.where(b < 0, b ^ 0x7FFF, b)


def _from_ordered_bits16(k):
    b = jnp.where(k < 0, k ^ 0x7FFF, k)
    return lax.bitcast_convert_type(lax.shift_left(b, 16), F32)


def _bf16_floor(x):
    r = x.astype(BF16).astype(F32)
    b = lax.bitcast_convert_type(r, jnp.int32)
    below = lax.bitcast_convert_type(jnp.where(b < 0, b + 0x10000, b - 0x10000), F32)
    return jnp.where(r > x, below, r)


def _bisect_threshold(count_fn, shape, keep, coarse_count_fn=None):
    keep_f = jnp.float32(keep)

    def bisect(passes, pivot_count, st):
        def step(_, st):
            lo_k, hi_k, n_hi = st
            mid_k = (lax.shift_right_arithmetic(lo_k, 1) + lax.shift_right_arithmetic(hi_k, 1)
                     + (lo_k & hi_k & 1))
            cnt = pivot_count(mid_k)
            raise_lo = jnp.logical_and(mid_k > lo_k, cnt >= keep_f)
            lower_hi = jnp.logical_and(mid_k > lo_k, cnt < keep_f)
            return (jnp.where(raise_lo, mid_k, lo_k), jnp.where(lower_hi, mid_k, hi_k),
                    jnp.where(lower_hi, cnt, n_hi))
        return lax.fori_loop(0, passes, step, st)

    def fine_count(mid_k):
        return count_fn(lambda s, idx: (_ind(s >= _ordered_bits(mid_k, F32)),))[0]

    lo = jnp.full(shape, -jnp.inf, F32)
    hi = jnp.full(shape, jnp.inf, F32)
    n_hi = jnp.zeros(shape, F32)
    if coarse_count_fn is None:
        lo_k, _, n_gt = bisect(FLOAT_BITS, fine_count,
                               (_ordered_bits(lo, jnp.int32), _ordered_bits(hi, jnp.int32) + 1, n_hi))
    else:
        half = FLOAT_BITS // 2
        coarse = lambda mid_h: coarse_count_fn(_from_ordered_bits16(mid_h).astype(BF16))
        lo_h, _, n_hi = bisect(half, coarse, (_ordered_bits16(lo), _ordered_bits16(hi) + 1, n_hi))
        st = (_ordered_bits(_from_ordered_bits16(lo_h), jnp.int32),
              _ordered_bits(_from_ordered_bits16(lo_h + 1), jnp.int32), n_hi)
        lo_k, _, n_gt = bisect(half, fine_count, st)
    return _ordered_bits(lo_k, F32), n_gt


def _select_threshold(count_fn, active, keep, index_bits):
    thr, n_gt = _bisect_threshold(count_fn, active.shape, keep)
    n_eq, = count_fn(lambda s, idx: (_ind(s == thr),))
    need = jnp.float32(keep) - n_gt
    excess = jnp.where(active, jnp.where(n_eq > need, 1, 0), 0)

    def tie_search():
        lo_i = jnp.full(thr.shape, -1, jnp.int32)
        hi_i = jnp.full(thr.shape, (1 << index_bits) - 1, jnp.int32)
        for _ in range(index_bits + 1):
            mid_i = lax.shift_right_arithmetic(lo_i + hi_i, 1)
            n_low, = count_fn(lambda s, idx: (jnp.where(s == thr, _ind(idx <= mid_i), 0.0),))
            ok = n_low >= need
            hi_i = jnp.where(ok, mid_i, hi_i)
            lo_i = jnp.where(ok, lo_i, mid_i)
        return hi_i

    cut = lax.cond(jnp.max(excess) > 0, tie_search, lambda: jnp.full(thr.shape, INDEX_BIG, jnp.int32))
    thr = jnp.where(active, thr, -jnp.inf)
    cut = jnp.where(active, cut, -1)
    return thr, cut


def _selected(s, idx, thr, cut):
    return jnp.where(s > thr, 1.0, jnp.where(s == thr, _ind(idx <= cut), 0.0))


def _dsa_prompt_kernel(q_ref, qi_ref, wit_ref, kb_ref, vt_ref, ki2_ref, o_ref,
                       sc_ref, sb_ref, m_ref, acc_ref, *, c_dh, idx_dh, keep, kc):
    tq = q_ref.shape[0]
    seq = kb_ref.shape[0]
    t0 = pl.program_id(1) * tq
    nch = (t0 + tq + kc - 1) // kc
    scale = c_dh ** -0.5
    neg_inf = jnp.float32(-jnp.inf)
    lane = lax.broadcasted_iota(jnp.int32, (tq, LANE), 1)
    key_iota = lax.broadcasted_iota(jnp.int32, (kc, tq), 0)
    qpos = t0 + lax.broadcasted_iota(jnp.int32, (1, tq), 1)

    qi = qi_ref[...]
    per_group = LANE // idx_dh
    qms = []
    for h in range(IDX_HEADS):
        grp = qi[:, (h // per_group) * LANE:(h // per_group + 1) * LANE]
        qms.append(jnp.where(_head_of(lane, idx_dh) == h % per_group, grp, jnp.zeros_like(grp)))
    wit = wit_ref[...]

    def score_chunk(c, _):
        start = pl.multiple_of(c * kc, kc)
        kic = ki2_ref[pl.ds(start, kc), :]
        acc = jnp.zeros((kc, tq), F32)
        for h in range(IDX_HEADS):
            acc = acc + jnp.maximum(_dot_nt(kic, qms[h]), 0.0) * wit[h:h + 1, :]
        score = acc * (idx_dh ** -0.5)
        adm = start + key_iota <= qpos
        masked = jnp.where(adm, score, neg_inf)
        sc_ref[pl.ds(start, kc), :] = masked
        sb_ref[pl.ds(start, kc), :] = _bf16_floor(masked).astype(BF16)
        return 0

    lax.fori_loop(0, nch, score_chunk, 0)

    def count_fn(pred, n_out=1):
        def body(c, cnts):
            start = pl.multiple_of(c * kc, kc)
            hits = pred(sc_ref[pl.ds(start, kc), :], start + key_iota)
            return tuple(cnt + jnp.sum(hit.reshape(kc // SUBLANE, SUBLANE, tq), axis=0) for cnt, hit in zip(cnts, hits))
        parts = lax.fori_loop(0, nch, body, tuple(jnp.zeros((SUBLANE, tq), F32) for _ in range(n_out)))
        return tuple(jnp.sum(part, axis=0, keepdims=True) for part in parts)

    def coarse_count_fn(pivot):
        one, zero = jnp.ones((), BF16), jnp.zeros((), BF16)
        rows = SUBLANE_BF16

        def body(c, cnt):
            hit = jnp.where(sb_ref[pl.ds(pl.multiple_of(c * kc, kc), kc), :] >= pivot, one, zero)
            part = hit[0:rows]
            for r in range(1, kc // rows):
                part = part + hit[r * rows:(r + 1) * rows]
            return cnt + part.astype(F32)
        part = lax.fori_loop(0, nch, body, jnp.zeros((rows, tq), F32))
        return jnp.sum(part, axis=0, keepdims=True)

    thr, n_gt = _bisect_threshold(count_fn, (1, tq), keep, coarse_count_fn)
    active = qpos + 1 > keep
    thr = jnp.where(active, thr, neg_inf)
    need = jnp.where(active, jnp.float32(keep) - n_gt, 0.0)
    prefix_ones = jnp.where(lax.broadcasted_iota(jnp.int32, (kc, kc), 0)
                            >= lax.broadcasted_iota(jnp.int32, (kc, kc), 1), 1.0, 0.0).astype(BF16)

    m_ref[...] = jnp.full(m_ref.shape, NEG_BIG, F32)
    acc_ref[...] = jnp.zeros(acc_ref.shape, F32)
    group = C_HEADS // C_KV_HEADS
    ext = c_dh + ONES_ROWS
    exp2_scale = scale * LOG2_E
    head_row = lax.broadcasted_iota(jnp.int32, m_ref.shape, 0)

    def attend_chunk(c, ties_before):
        start = pl.multiple_of(c * kc, kc)
        sc = sc_ref[pl.ds(start, kc), :]
        tie = _ind(sc == thr)
        tie_rank = ties_before + _dot(prefix_ones, tie.astype(BF16))
        keepm = jnp.where(sc > thr, 1.0, jnp.where(tie_rank <= need, tie, 0.0)) > 0.0
        ties_before = tie_rank[kc - 1:kc, :]
        kbc = kb_ref[pl.ds(start, kc), :]
        vtc = vt_ref[:, pl.ds(start, kc)]
        m_old = m_ref[...]
        m_all = jnp.zeros(m_old.shape, F32)
        scores = [_dot_nt(kbc[:, (h // group) * c_dh:(h // group + 1) * c_dh], q_ref[:, h * c_dh:(h + 1) * c_dh])
                  for h in range(C_HEADS)]
        alphas, pvs = [], []
        for kv in range(C_KV_HEADS):
            probs = []
            for h in range(kv * group, (kv + 1) * group):
                s = jnp.where(keepm, scores[h], neg_inf)
                m_new = jnp.maximum(m_old[h:h + 1, :], jnp.max(s, axis=0, keepdims=True))
                alphas.append(jnp.exp2((m_old[h:h + 1, :] - m_new) * exp2_scale))
                probs.append(jnp.exp2((s - m_new) * exp2_scale).astype(BF16))
                m_all = jnp.where(head_row == h, m_new, m_all)
            pvs += [_dot(vtc[kv * ext:(kv + 1) * ext, :], p) for p in probs]
        m_ref[...] = m_all
        for h in range(C_HEADS):
            acc_ref[h] = acc_ref[h] * alphas[h] + pvs[h]
        return ties_before

    lax.fori_loop(0, nch, attend_chunk, jnp.zeros((1, tq), F32))
    for h in range(C_HEADS):
        out_t = acc_ref[h, 0:c_dh, :] / acc_ref[h, c_dh:c_dh + 1, :]
        o_ref[:, h * c_dh:(h + 1) * c_dh] = out_t.T.astype(BF16)


def _dsa_prompt(q, qi, wit, kb, vt, ki2, batch, seq, c_dh, idx_dh, keep, tq, kc):
    n = q.shape[0]
    tiles = seq // tq
    assert c_dh == LANE and seq % kc == 0 and seq % tq == 0
    return pl.pallas_call(
        functools.partial(_dsa_prompt_kernel, c_dh=c_dh, idx_dh=idx_dh, keep=keep, kc=kc),
        grid=(batch, tiles),
        in_specs=[pl.BlockSpec((tq, q.shape[1]), lambda b, i: (b * tiles + i, 0)),
                  pl.BlockSpec((tq, qi.shape[1]), lambda b, i: (b * tiles + i, 0)),
                  pl.BlockSpec((SUBLANE, tq), lambda b, i: (0, b * tiles + i)),
                  pl.BlockSpec((seq, kb.shape[1]), lambda b, i: (b, 0)),
                  pl.BlockSpec((vt.shape[0], seq), lambda b, i: (0, b)),
                  pl.BlockSpec((seq, LANE), lambda b, i: (b, 0))],
        out_specs=pl.BlockSpec((tq, q.shape[1]), lambda b, i: (b * tiles + i, 0)),
        out_shape=jax.ShapeDtypeStruct(q.shape, BF16),
        scratch_shapes=[pltpu.VMEM((seq, tq), F32), pltpu.VMEM((seq, tq), BF16), pltpu.VMEM((SUBLANE, tq), F32),
                        pltpu.VMEM((C_HEADS, c_dh + ONES_ROWS, tq), F32)],
        compiler_params=_params("arbitrary", "arbitrary"),
        name="dsa_prompt",
    )(q, qi, wit, kb, vt, ki2)


PAGE_ROWS = 128
DECODE_SCORE_SEQS = 4
DECODE_ATTN_SEQS = 2
DECODE_MOBA_SEQS = 2


def _lane_bcast_col(row):
    return jnp.broadcast_to(row, (LANE, row.shape[1])).T


def _moba_decode_kernel(pt_ref, q_ref, kn_ref, vn_ref, *rest, n_pages, dh, seqs):
    for g in range(seqs):
        _moba_decode_seq(g, q_ref, kn_ref, vn_ref, rest[g * n_pages:(g + 1) * n_pages],
                         rest[(seqs + g) * n_pages:(seqs + g + 1) * n_pages], rest[-1], n_pages, dh)


def _moba_decode_seq(g, q_ref, kn_ref, vn_ref, kt_refs, vt_refs, o_ref, n_pages, dh):
    bw = q_ref.shape[2]
    heads = bw // dh
    ppb = MOBA_BLOCK // PAGE_ROWS
    nblk = n_pages // ppb
    neg_inf = jnp.float32(-jnp.inf)
    qb = _lane_bcast_col(q_ref[g].astype(F32))

    def head_sums(x):
        return jnp.sum(x.reshape(heads, dh, LANE), axis=1)

    def head_rows(x):
        return jnp.broadcast_to(x[:, None, :], (heads, dh, LANE)).reshape(bw, LANE)

    s_pages = [head_sums(kt_refs[j][...] * qb) for j in range(n_pages)]
    s_own = head_sums(_lane_bcast_col(kn_ref[g]) * qb)
    gates = []
    for n in range(nblk):
        gate = jnp.zeros((heads, 1), F32)
        for j in range(n * ppb, (n + 1) * ppb):
            gate = gate + jnp.sum(s_pages[j], axis=1, keepdims=True)
        gates.append(gate)
    sels = []
    for n in range(nblk):
        rank = jnp.zeros((heads, 1), F32)
        for m in range(nblk):
            if m != n:
                beats = gates[m] >= gates[n] if m < n else gates[m] > gates[n]
                rank = rank + jnp.where(beats, 1.0, 0.0)
        sels.append(rank < MOBA_TOPK)
    m = s_own
    masked = []
    for j in range(n_pages):
        sm = jnp.where(sels[j // ppb], s_pages[j], neg_inf)
        masked.append(sm)
        m = jnp.maximum(m, jnp.max(sm, axis=1, keepdims=True))
    e_own = jnp.exp(s_own - m)
    es = [jnp.exp(sm - m) for sm in masked]
    l = e_own
    for e in es:
        l = l + jnp.sum(e, axis=1, keepdims=True)
    rinv = 1.0 / l
    lane = lax.broadcasted_iota(jnp.int32, (bw, LANE), 1)
    acc = jnp.where(lane == 0, head_rows(e_own * rinv) * _lane_bcast_col(vn_ref[g]), 0.0)
    for j in range(n_pages):
        acc = acc + head_rows(es[j] * rinv) * vt_refs[j][...]
    out_col = jnp.sum(acc, axis=1, keepdims=True)
    o_ref[g] = jnp.broadcast_to(out_col, (bw, LANE)).T[0:1].astype(BF16)


def _moba_decode(q, k_new, v_new, cache_k, cache_v, page_table, dh):
    n, bw = q.shape
    n_pages = page_table.shape[1]
    assert cache_k.shape[1] == PAGE_ROWS == LANE and (n_pages * PAGE_ROWS) % MOBA_BLOCK == 0
    ckt = jnp.transpose(cache_k, (0, 2, 3, 1)).reshape(cache_k.shape[0], bw, PAGE_ROWS)
    cvt = jnp.transpose(cache_v, (0, 2, 3, 1)).reshape(cache_v.shape[0], bw, PAGE_ROWS)
    g = DECODE_MOBA_SEQS
    assert n % g == 0
    seq_row = pl.BlockSpec((g, 1, bw), lambda b, pt: (b, 0, 0))
    pages = [pl.BlockSpec((None, bw, PAGE_ROWS), lambda b, pt, j=j, k=k: (pt[b * g + k, j], 0, 0))
             for k in range(g) for j in range(n_pages)]
    out = pl.pallas_call(
        functools.partial(_moba_decode_kernel, n_pages=n_pages, dh=dh, seqs=g),
        grid_spec=pltpu.PrefetchScalarGridSpec(
            num_scalar_prefetch=1, grid=(n // g,),
            in_specs=[seq_row, seq_row, seq_row] + pages + pages,
            out_specs=seq_row),
        out_shape=jax.ShapeDtypeStruct((n, 1, bw), BF16),
        compiler_params=_params("arbitrary"),
        name="moba_decode",
    )(page_table, q.reshape(n, 1, bw), k_new.reshape(n, 1, bw), v_new.reshape(n, 1, bw),
      *([ckt] * (g * n_pages)), *([cvt] * (g * n_pages)))
    return out.reshape(n, bw)


def _dsa_decode_score_kernel(pt_ref, qi_ref, wi_ref, kin_ref, *rest, n_pages, idx_dh, seqs):
    for g in range(seqs):
        _dsa_decode_score_seq(g, qi_ref, wi_ref, kin_ref, rest[g * n_pages:(g + 1) * n_pages], rest[-1],
                              n_pages, idx_dh)


def _dsa_decode_score_seq(g, qi_ref, wi_ref, kin_ref, kit_refs, o_ref, n_pages, idx_dh):
    qb = _lane_bcast_col(qi_ref[g].astype(F32))
    wi8 = wi_ref[g]
    scale = idx_dh ** -0.5

    def score(kit):
        total = jnp.zeros((1, LANE), F32)
        for h in range(IDX_HEADS):
            sc = jnp.sum(kit * qb[h * idx_dh:(h + 1) * idx_dh], axis=0, keepdims=True)
            total = total + jnp.maximum(sc, 0.0) * wi8[h:h + 1, :]
        return total * scale

    o_ref[g] = jnp.full(o_ref.shape[1:], -jnp.inf, F32)
    for j in range(n_pages):
        o_ref[g, j:j + 1, :] = score(kit_refs[j][...])
    own = score(_lane_bcast_col(kin_ref[g]))
    lane = lax.broadcasted_iota(jnp.int32, (1, PAGE_ROWS), 1)
    o_ref[g, n_pages:n_pages + 1, :] = jnp.where(lane == 0, own, -jnp.inf)


def _dsa_decode_select_kernel(s_ref, m_ref, *, keep, n_valid):
    s = s_ref[...]
    idx = lax.broadcasted_iota(jnp.int32, s.shape, 1)
    count_fn = lambda pred, n_out=1: tuple(jnp.sum(hit, axis=1, keepdims=True) for hit in pred(s, idx))
    active = jnp.full((s.shape[0], 1), n_valid > keep, jnp.bool_)
    thr, cut = _select_threshold(count_fn, active, keep, (s.shape[1] - 1).bit_length())
    m_ref[...] = _selected(s, idx, thr, cut)


def _dsa_decode_attn_kernel(pt_ref, q_ref, kn_ref, vn_ref, mask_ref, *rest, n_pages, c_dh, seqs):
    staged = [_dsa_decode_attn_scores(g, q_ref, kn_ref, mask_ref, rest[g * n_pages:(g + 1) * n_pages],
                                      n_pages, c_dh) for g in range(seqs)]
    for g in range(seqs):
        _dsa_decode_attn_values(g, staged[g], vn_ref, rest[(seqs + g) * n_pages:(seqs + g + 1) * n_pages],
                                rest[-1], n_pages, c_dh)


def _dsa_decode_attn_scores(g, q_ref, kn_ref, mask_ref, k_refs, n_pages, c_dh):
    rows = k_refs[0].shape[0]
    group = C_HEADS // C_KV_HEADS
    neg_inf = jnp.float32(-jnp.inf)
    qm = jnp.concatenate([q_ref[g].astype(F32), jnp.zeros((LANE - C_HEADS, c_dh), F32)], axis=0).astype(BF16)
    head_id = lax.broadcasted_iota(jnp.int32, (C_HEADS, rows), 0)
    row_id = lax.broadcasted_iota(jnp.int32, (C_HEADS, rows), 1)
    own_kv = (row_id & (C_KV_HEADS - 1)) == _head_of(head_id, group)
    mask = mask_ref[g]
    mask_pad = jnp.concatenate([mask, jnp.zeros((LANE - mask.shape[0], PAGE_ROWS), F32)], axis=0)
    key_id = lax.broadcasted_iota(jnp.int32, (PAGE_ROWS, rows), 0)
    key_of_row = _head_of(lax.broadcasted_iota(jnp.int32, (PAGE_ROWS, rows), 1), C_KV_HEADS)
    repeat = jnp.where(key_id == key_of_row, 1.0, 0.0).astype(BF16)
    row_mask = _dot(mask_pad.astype(BF16), repeat)
    kn = jnp.concatenate([kn_ref[g], jnp.zeros((SUBLANE - C_KV_HEADS, c_dh), F32)], axis=0)
    s_own_kv = _dot_nt(kn.astype(BF16), qm).T[0:C_HEADS, 0:SUBLANE]
    col_id = lax.broadcasted_iota(jnp.int32, (C_HEADS, SUBLANE), 1)
    own_col = col_id == _head_of(lax.broadcasted_iota(jnp.int32, (C_HEADS, SUBLANE), 0), group)
    own_on = jnp.where(own_col, mask[n_pages:n_pages + 1, 0:1], 0.0) > 0.0
    s_own = jnp.where(own_on, s_own_kv, neg_inf)
    m = jnp.maximum(jnp.max(s_own, axis=1, keepdims=True), NEG_BIG)
    masked = []
    for j in range(n_pages):
        s_t = _dot_nt(k_refs[j][...].astype(BF16), qm).T[0:C_HEADS]
        sm = jnp.where(own_kv, jnp.where(row_mask[j:j + 1, :] > 0.5, s_t, neg_inf), neg_inf)
        masked.append(sm)
        m = jnp.maximum(m, jnp.max(sm, axis=1, keepdims=True))
    return masked, s_own, m


def _dsa_decode_attn_values(g, staged, vn_ref, v_refs, o_ref, n_pages, c_dh):
    masked, s_own, m = staged
    scale = c_dh ** -0.5
    e_own = jnp.exp((s_own - m) * scale)
    e_own = jnp.concatenate([e_own, jnp.zeros((C_HEADS, LANE - SUBLANE), F32)], axis=1)
    e_all = jnp.concatenate([jnp.exp((sm - m) * scale) for sm in masked] + [e_own], axis=1)
    e_all = jnp.concatenate([e_all, jnp.zeros((SUBLANE_BF16 - C_HEADS, e_all.shape[1]), F32)], axis=0).astype(BF16)
    v_own = jnp.concatenate([vn_ref[g], jnp.zeros((LANE - C_KV_HEADS, c_dh), F32)], axis=0)
    v_all = jnp.concatenate([v_refs[j][...] for j in range(n_pages)] + [v_own], axis=0).astype(BF16)
    v_all = jnp.concatenate([v_all, jnp.ones(v_all.shape, BF16)], axis=1)
    out = _dot(e_all, v_all)
    o_ref[g] = (out[0:C_HEADS, 0:c_dh] / out[0:C_HEADS, c_dh:2 * c_dh]).astype(BF16)


def _dsa_decode(q, qi, wit, k_new, v_new, ki_new, cache_k, cache_v, cache_ki, page_table, c_dh, idx_dh, keep):
    n = q.shape[0]
    n_pages = page_table.shape[1]
    assert cache_k.shape[1] == PAGE_ROWS == LANE
    pool = cache_k.shape[0]
    ck = cache_k.reshape(pool, PAGE_ROWS * C_KV_HEADS, c_dh)
    cv = cache_v.reshape(pool, PAGE_ROWS * C_KV_HEADS, c_dh)
    ckit = jnp.transpose(cache_ki, (0, 2, 1))
    rows = -(-(n_pages + 1) // SUBLANE) * SUBLANE
    wi8 = wit.T.reshape(n, 8, 1)

    def seq3(g, a, b):
        return pl.BlockSpec((g, a, b), lambda s, pt: (s, 0, 0))

    def pages(g, r, w):
        return [pl.BlockSpec((None, r, w), lambda s, pt, j=j, k=k: (pt[s * g + k, j], 0, 0))
                for k in range(g) for j in range(n_pages)]

    gs = DECODE_SCORE_SEQS
    assert n % gs == 0 and n % DECODE_ATTN_SEQS == 0
    scores = pl.pallas_call(
        functools.partial(_dsa_decode_score_kernel, n_pages=n_pages, idx_dh=idx_dh, seqs=gs),
        grid_spec=pltpu.PrefetchScalarGridSpec(
            num_scalar_prefetch=1, grid=(n // gs,),
            in_specs=[seq3(gs, 1, IDX_HEADS * idx_dh), seq3(gs, 8, 1), seq3(gs, 1, idx_dh)]
                     + pages(gs, idx_dh, PAGE_ROWS),
            out_specs=seq3(gs, rows, PAGE_ROWS)),
        out_shape=jax.ShapeDtypeStruct((n, rows, PAGE_ROWS), F32),
        compiler_params=_params("arbitrary"),
        name="dsa_decode_scores",
    )(page_table, qi.reshape(n, 1, IDX_HEADS * idx_dh), wi8, ki_new.reshape(n, 1, idx_dh),
      *([ckit] * (gs * n_pages)))
    flat = rows * PAGE_ROWS
    mask = pl.pallas_call(
        functools.partial(_dsa_decode_select_kernel, keep=keep, n_valid=n_pages * PAGE_ROWS + 1),
        grid=(1,),
        in_specs=[_full((n, flat))],
        out_specs=_full((n, flat)),
        out_shape=jax.ShapeDtypeStruct((n, flat), F32),
        compiler_params=_params("arbitrary"),
        name="dsa_decode_select",
    )(scores.reshape(n, flat))
    cq = C_HEADS * c_dh
    ga = DECODE_ATTN_SEQS
    kv_rows = PAGE_ROWS * C_KV_HEADS
    out = pl.pallas_call(
        functools.partial(_dsa_decode_attn_kernel, n_pages=n_pages, c_dh=c_dh, seqs=ga),
        grid_spec=pltpu.PrefetchScalarGridSpec(
            num_scalar_prefetch=1, grid=(n // ga,),
            in_specs=[seq3(ga, C_HEADS, c_dh), seq3(ga, C_KV_HEADS, c_dh), seq3(ga, C_KV_HEADS, c_dh),
                      seq3(ga, rows, PAGE_ROWS)] + pages(ga, kv_rows, c_dh) + pages(ga, kv_rows, c_dh),
            out_specs=seq3(ga, C_HEADS, c_dh)),
        out_shape=jax.ShapeDtypeStruct((n, C_HEADS, c_dh), BF16),
        compiler_params=_params("arbitrary"),
        name="dsa_decode_attn",
    )(page_table, q.reshape(n, C_HEADS, c_dh), k_new.reshape(n, C_KV_HEADS, c_dh),
      v_new.reshape(n, C_KV_HEADS, c_dh), mask.reshape(n, rows, PAGE_ROWS),
      *([ck] * (ga * n_pages)), *([cv] * (ga * n_pages)))
    return out.reshape(n, cq)


def kernel(x_prompt, x_sample, cache_moba_k, cache_moba_v, cache_dsa_k, cache_dsa_v, cache_dsa_kidx,
           state_ffn_conv, page_table, w_in_even, g_sgu, w_sgu, b_sgu, w_out_even, w_in_odd, w_out_odd,
           g_mix, g_ffn, w_up, w_conv, b_conv, w_down, g_final):
    batch, seq, d = x_prompt.shape
    n_dec, dec_seq, _ = x_sample.shape
    depth = g_mix.shape[0]
    n_pages = page_table.shape[1]
    past_len = n_pages * cache_moba_k.shape[2]
    assert depth == 2 and dec_seq == 1, "one even + one odd layer, one decode token per sequence"
    b_dh = cache_moba_k.shape[-1]
    c_dh = cache_dsa_k.shape[-1]
    idx_dh = cache_dsa_kidx.shape[-1]
    keep_p = min(DSA_TOPK, seq // 4)
    keep_s = min(DSA_TOPK, (past_len + dec_seq) // 4)
    xp = x_prompt.reshape(batch * seq, d)
    xs = x_sample.reshape(n_dec, d)
    bf = lambda w: w.astype(BF16)

    tp64, half64 = _rope_tables(b_dh, seq, 0)
    ts64, _ = _rope_tables(b_dh, 1, past_len)
    tp128, half128 = _rope_tables(c_dh, seq, 0)
    ts128, _ = _rope_tables(c_dh, 1, past_len)
    assert idx_dh == b_dh

    w_in0, w_out0 = bf(w_in_even[0]), bf(w_out_even[0])
    a_p, q_p, kb_p, mkt_p, mvt_p, vt_p, km_p = _even_in_prompt(
        xp, seq, g_mix[0], w_in0, tp64, half64, g_sgu[0], w_sgu[0], b_sgu[0])
    bo_p = _moba_prompt(q_p, kb_p, vt_p, km_p, batch, seq, b_dh)
    cache_view = lambda t: jnp.transpose(t.reshape(batch, B_HEADS, b_dh, seq), (0, 3, 1, 2))[None]
    a_s, q_s, mk_s, mv_s, va_s = _even_in_decode(xs, g_mix[0], w_in0, ts64, half64, g_sgu[0], w_sgu[0], b_sgu[0])
    bo_s = _moba_decode(q_s, mk_s, mv_s, cache_moba_k[0], cache_moba_v[0], page_table, b_dh)
    w_up0, w_down0 = bf(w_up[0]), bf(w_down[0])
    xp, hp = _residual_proj(xp, [a_p, bo_p], w_out0, g_ffn[0], "next", ROW_TILE, "out_even_prompt")
    xs, hs = _residual_proj(xs, [a_s, bo_s], w_out0, g_ffn[0], "next", ROW_TILE, "out_even_decode")
    act_p, conv_p0 = _ffn_up_prompt(hp, seq, w_up0, w_conv[0], b_conv[0], ROW_TILE)
    act_s, conv_s0 = _ffn_up_decode(hs, state_ffn_conv[0], w_up0, w_conv[0], b_conv[0])
    xp, hp = _residual_proj(xp, [act_p], w_down0, g_mix[1], "next", ROW_TILE, "ffn_down0_prompt")
    xs, hs = _residual_proj(xs, [act_s], w_down0, g_mix[1], "next", ROW_TILE, "ffn_down0_decode")

    w_out1 = bf(w_out_odd[0])
    q_p, dk_p, dv_p, kb_p, vt_p, qi_p, kif_p, ki2_p, wit_p = _odd_in(
        hp, seq, w_in_odd[0], tp64, half64, tp128, half128, c_dh, idx_dh, ROW_TILE, True)
    o_p = _dsa_prompt(q_p, qi_p, wit_p, kb_p, vt_p, ki2_p, batch, seq, c_dh, idx_dh, keep_p, DSA_QUERY_TILE, DSA_KEY_CHUNK)
    q_s, dk_s, dv_s, _, _, qi_s, kif_s, _, wit_s = _odd_in(
        hs, None, w_in_odd[0], ts64, half64, ts128, half128, c_dh, idx_dh, 256, False)
    o_s = _dsa_decode(q_s, qi_s, wit_s, dk_s, dv_s, kif_s[:, :idx_dh], cache_dsa_k[0], cache_dsa_v[0],
                      cache_dsa_kidx[0], page_table, c_dh, idx_dh, keep_s)
    w_up1, w_down1 = bf(w_up[1]), bf(w_down[1])
    xp, hp = _residual_proj(xp, [o_p], w_out1, g_ffn[1], "next", ROW_TILE, "out_odd_prompt")
    xs, hs = _residual_proj(xs, [o_s], w_out1, g_ffn[1], "next", ROW_TILE, "out_odd_decode")
    act_p, conv_p1 = _ffn_up_prompt(hp, seq, w_up1, w_conv[1], b_conv[1], ROW_TILE)
    act_s, conv_s1 = _ffn_up_decode(hs, state_ffn_conv[1], w_up1, w_conv[1], b_conv[1])
    y_p, = _residual_proj(xp, [act_p], w_down1, g_final, "final", ROW_TILE, "ffn_down1_prompt")
    y_s, = _residual_proj(xs, [act_s], w_down1, g_final, "final", ROW_TILE, "ffn_down1_decode")

    ckv = C_KV_HEADS
    return (y_p.reshape(batch, seq, d), y_s.reshape(n_dec, dec_seq, d),
            cache_view(mkt_p), cache_view(mvt_p),
            mk_s.reshape(1, n_dec, dec_seq, B_HEADS, b_dh), mv_s.reshape(1, n_dec, dec_seq, B_HEADS, b_dh),
            va_s.reshape(1, n_dec, dec_seq, A_GROUPS, LANE),
            dk_p.reshape(1, batch, seq, ckv, c_dh), dv_p.reshape(1, batch, seq, ckv, c_dh),
            kif_p[:, :idx_dh].reshape(1, batch, seq, idx_dh),
            dk_s.reshape(1, n_dec, dec_seq, ckv, c_dh), dv_s.reshape(1, n_dec, dec_seq, ckv, c_dh),
            kif_s[:, :idx_dh].reshape(1, n_dec, dec_seq, idx_dh),
            jnp.stack([conv_p0, conv_p1]), jnp.stack([conv_s0, conv_s1]))
```

```python
import functools

import numpy as np
import jax
import jax.numpy as jnp
from jax import lax
from jax.experimental import pallas as pl
from jax.experimental.pallas import tpu as pltpu

F32 = jnp.float32
BF16 = jnp.bfloat16

A_GROUPS = 4
CHUNK = 128
B_HEADS = 8
MOBA_BLOCK = 256
MOBA_TOPK = 3
C_HEADS = 8
C_KV_HEADS = 2
IDX_HEADS = 4
DSA_TOPK = 256
ROPE_THETA = 500000.0
ROPE_FRAC = 4
CONV_W = 3
EPS = 1e-6

LANE = 128
SUBLANE = 8
SUBLANE_BF16 = 16
VMEM_LIMIT_BYTES = 56 * 1024 * 1024
NEG_BIG = -1e30
LOG2_E = 1.4426950408889634
ROW_TILE = 512
DSA_QUERY_TILE = 256
DSA_KEY_CHUNK = 256


def _params(*sem):
    return pltpu.CompilerParams(dimension_semantics=sem,
                                vmem_limit_bytes=VMEM_LIMIT_BYTES)


def _full(shape):
    nd = len(shape)
    return pl.BlockSpec(shape, lambda *_: (0,) * nd)


def _rmsnorm(x, g):
    ms = jnp.mean(x * x, axis=-1, keepdims=True)
    return x * lax.rsqrt(ms + EPS) * g


def _rope_apply(x, c, s1, s2, half):
    parts = []
    for j in range(x.shape[1] // LANE):
        xs = x[:, j * LANE:(j + 1) * LANE]
        parts.append(xs * c + pltpu.roll(xs, LANE - half, 1) * s1
                     + pltpu.roll(xs, half, 1) * s2)
    return parts[0] if len(parts) == 1 else jnp.concatenate(parts, axis=1)


def _head_of(index, dh):
    assert dh & (dh - 1) == 0
    return lax.shift_right_logical(index, dh.bit_length() - 1)


ONES_ROWS = SUBLANE_BF16


def _with_ones_rows(vt, dh):
    ones = jnp.ones((ONES_ROWS, vt.shape[1]), vt.dtype)
    parts = []
    for h in range(vt.shape[0] // dh):
        parts += [vt[h * dh:(h + 1) * dh], ones]
    return jnp.concatenate(parts, axis=0)


def _dot(a, b):
    return jnp.dot(a, b, preferred_element_type=F32)


def _dot_nt(a, b):
    return lax.dot_general(a, b, (((1,), (1,)), ((), ())),
                           preferred_element_type=F32)


def _rope_rows(dh):
    rd = dh // ROPE_FRAC
    half = rd // 2
    inv = ROPE_THETA ** (-jnp.arange(half, dtype=F32) * (2.0 / rd))
    r = np.arange(LANE) % dh
    j = np.where(r < half, r, np.where(r < rd, r - half, 0))
    inv_lane = jnp.where(jnp.asarray(r < rd), inv[j], 0.0).reshape(1, LANE)
    s1 = jnp.asarray(np.where(r < half, -1.0, 0.0), F32).reshape(1, LANE)
    s2 = jnp.asarray(np.where((r >= half) & (r < rd), 1.0, 0.0), F32).reshape(1, LANE)
    return inv_lane, s1, s2, half


def _rope_table_kernel(inv_ref, s1_ref, s2_ref, c_ref, a_ref, b_ref, *, base):
    rows = c_ref.shape[0]
    pos = lax.broadcasted_iota(jnp.int32, (rows, LANE), 0) + (base + pl.program_id(0) * rows)
    ang = pos.astype(F32) * inv_ref[...]
    s = jnp.sin(ang)
    c_ref[...] = jnp.cos(ang)
    a_ref[...] = s * s1_ref[...]
    b_ref[...] = s * s2_ref[...]


def _rope_tables(dh, n_pos, base):
    inv_lane, s1, s2, half = _rope_rows(dh)
    rows = min(n_pos, ROW_TILE)
    assert n_pos % rows == 0
    out = jax.ShapeDtypeStruct((n_pos, LANE), F32)
    spec = pl.BlockSpec((rows, LANE), lambda i: (i, 0))
    c, a, b = pl.pallas_call(
        functools.partial(_rope_table_kernel, base=base),
        grid=(n_pos // rows,),
        in_specs=[_full((1, LANE))] * 3,
        out_specs=[spec] * 3,
        out_shape=[out] * 3,
        compiler_params=_params("arbitrary"),
        name="rope_tables",
    )(inv_lane, s1, s2)
    return (c, a, b), half


def _sgu_values(h, w_ref, gs_ref, aw):
    u = jax.nn.gelu(_dot(h, w_ref[:, 0:aw]))
    zv = jax.nn.gelu(_dot(h, w_ref[:, aw:2 * aw]))
    vs = []
    for g in range(A_GROUPS):
        vg = zv[:, g * LANE:(g + 1) * LANE]
        vs.append(_rmsnorm(vg, gs_ref[:, g * LANE:(g + 1) * LANE]))
    return u, vs


def _even_in_prompt_kernel(x_ref, g_ref, w_ref, c_ref, s1_ref, s2_ref, gs_ref, ws_ref, bs_ref,
                           a_ref, q_ref, kb_ref, kt_ref, vt_ref, vtb_ref, km_ref, v_scr, *, half, scale):
    tm = x_ref.shape[0]
    aw = A_GROUPS * LANE
    bw = q_ref.shape[1]
    h = _rmsnorm(x_ref[...], g_ref[...]).astype(BF16)
    u, vs = _sgu_values(h, w_ref, gs_ref, aw)
    tri = (lax.broadcasted_iota(jnp.int32, (CHUNK, CHUNK), 0)
           >= lax.broadcasted_iota(jnp.int32, (CHUNK, CHUNK), 1))
    for g in range(A_GROUPS):
        wg = jnp.where(tri, ws_ref[g], 0.0).astype(BF16)
        for cc in range(tm // CHUNK):
            rows = slice(cc * CHUNK, (cc + 1) * CHUNK)
            mixed = _dot(wg, vs[g][rows].astype(BF16)) + bs_ref[g]
            a_ref[rows, g * LANE:(g + 1) * LANE] = (u[rows, g * LANE:(g + 1) * LANE] * mixed).astype(BF16)
    c, s1, s2 = c_ref[...], s1_ref[...], s2_ref[...]
    q = _rope_apply(_dot(h, w_ref[:, 2 * aw:2 * aw + bw]), c, s1, s2, half)
    q_ref[...] = (q * scale).astype(BF16)
    k = _rope_apply(_dot(h, w_ref[:, 2 * aw + bw:2 * aw + 2 * bw]), c, s1, s2, half)
    kt_ref[...] = k.T
    kb_ref[...] = k.astype(BF16)
    for r in range(tm // MOBA_BLOCK):
        block = k[r * MOBA_BLOCK:(r + 1) * MOBA_BLOCK]
        km_ref[r] = jnp.sum(block, axis=0, keepdims=True) * (1.0 / MOBA_BLOCK)
    v_scr[...] = _dot(h, w_ref[:, 2 * aw + 2 * bw:2 * aw + 3 * bw])
    vt = v_scr[...].T
    vt_ref[...] = vt
    vtb_ref[...] = _with_ones_rows(vt, bw // B_HEADS).astype(BF16)


def _even_in_decode_kernel(x_ref, g_ref, w_ref, c_ref, s1_ref, s2_ref, gs_ref, w00_ref, b0_ref,
                           a_ref, q_ref, k_ref, v_ref, va_ref, *, half, scale):
    aw = A_GROUPS * LANE
    bw = q_ref.shape[1]
    h = _rmsnorm(x_ref[...], g_ref[...]).astype(BF16)
    u, vs = _sgu_values(h, w_ref, gs_ref, aw)
    va = jnp.concatenate(vs, axis=1)
    va_ref[...] = va
    a_ref[...] = (u * (w00_ref[...] * va + b0_ref[...])).astype(BF16)
    c, s1, s2 = c_ref[...], s1_ref[...], s2_ref[...]
    q = _rope_apply(_dot(h, w_ref[:, 2 * aw:2 * aw + bw]), c, s1, s2, half)
    q_ref[...] = (q * scale).astype(BF16)
    k_ref[...] = _rope_apply(_dot(h, w_ref[:, 2 * aw + bw:2 * aw + 2 * bw]), c, s1, s2, half)
    v_ref[...] = _dot(h, w_ref[:, 2 * aw + 2 * bw:2 * aw + 3 * bw])


def _even_in_prompt(x, seq, g_mix, w_in, tables, half, g_sgu, w_sgu, b_sgu):
    n, d = x.shape
    tm = ROW_TILE
    assert tm % MOBA_BLOCK == 0 and seq % tm == 0
    blocks = tm // MOBA_BLOCK
    bw = (w_in.shape[1] - 2 * A_GROUPS * LANE) // 3
    dh = bw // B_HEADS
    tiles_per_seq = seq // tm
    row = lambda w, dt: (pl.BlockSpec((tm, w), lambda i: (i, 0)), jax.ShapeDtypeStruct((n, w), dt))
    tab = pl.BlockSpec((tm, LANE), lambda i: (i % tiles_per_seq, 0))
    col = lambda h, dt: (pl.BlockSpec((None, h, tm), lambda i: (i // tiles_per_seq, 0, i % tiles_per_seq)),
                         jax.ShapeDtypeStruct((n // seq, h, seq), dt))
    outs = [row(A_GROUPS * LANE, BF16), row(bw, BF16), row(bw, BF16), col(bw, F32), col(bw, F32),
            col(bw + B_HEADS * ONES_ROWS, BF16),
            (pl.BlockSpec((blocks, 1, bw), lambda i: (i, 0, 0)), jax.ShapeDtypeStruct((n // MOBA_BLOCK, 1, bw), F32))]
    bs = jnp.broadcast_to(b_sgu[:, :, None], (A_GROUPS, CHUNK, LANE))
    return pl.pallas_call(
        functools.partial(_even_in_prompt_kernel, half=half, scale=dh ** -0.5),
        grid=(n // tm,),
        in_specs=[pl.BlockSpec((tm, d), lambda i: (i, 0)), _full((1, d)), _full(w_in.shape),
                  tab, tab, tab, _full((1, A_GROUPS * LANE)), _full(w_sgu.shape), _full(bs.shape)],
        out_specs=[o[0] for o in outs],
        out_shape=[o[1] for o in outs],
        scratch_shapes=[pltpu.VMEM((tm, bw), F32)],
        compiler_params=_params("arbitrary"),
        name="even_in_prompt",
    )(x, g_mix.reshape(1, d), w_in, *tables, g_sgu.reshape(1, -1), w_sgu, bs)


def _even_in_decode(x, g_mix, w_in, tables, half, g_sgu, w_sgu, b_sgu):
    n, d = x.shape
    bw = (w_in.shape[1] - 2 * A_GROUPS * LANE) // 3
    dh = bw // B_HEADS
    aw = A_GROUPS * LANE
    w00 = jnp.repeat(w_sgu[:, 0, 0], LANE).reshape(1, aw)
    b0 = jnp.repeat(b_sgu[:, 0], LANE).reshape(1, aw)
    shapes = [(aw, BF16), (bw, BF16), (bw, F32), (bw, F32), (aw, F32)]
    return pl.pallas_call(
        functools.partial(_even_in_decode_kernel, half=half, scale=dh ** -0.5),
        grid=(1,),
        in_specs=[_full((n, d)), _full((1, d)), _full(w_in.shape)] + [_full((1, LANE))] * 3
                 + [_full((1, aw))] * 3,
        out_specs=[_full((n, w)) for w, _ in shapes],
        out_shape=[jax.ShapeDtypeStruct((n, w), dt) for w, dt in shapes],
        compiler_params=_params("arbitrary"),
        name="even_in_decode",
    )(x, g_mix.reshape(1, d), w_in, *tables, g_sgu.reshape(1, -1), w00, b0)


def _moba_prompt_kernel(q_ref, k_ref, vt_ref, km_ref, o_ref, sel_ref, m_ref, acc_ref, s_a, s_b, *, dh):
    blk = MOBA_BLOCK
    tq = q_ref.shape[0]
    i = pl.program_id(1)
    nblk = km_ref.shape[1]
    heads = q_ref.shape[1] // dh
    ext = dh + ONES_ROWS
    per_group = LANE // dh
    lane = lax.broadcasted_iota(jnp.int32, (tq, LANE), 1)
    blk_id = lax.broadcasted_iota(jnp.int32, (nblk, tq), 0)
    neg_inf = jnp.float32(-jnp.inf)
    qms = []
    for h in range(heads):
        grp = h // per_group
        q = q_ref[:, grp * LANE:(grp + 1) * LANE]
        qm = jnp.where(_head_of(lane, dh) == h % per_group, q, jnp.zeros_like(q))
        qms.append(qm)
        kmean = km_ref[0, :, grp * LANE:(grp + 1) * LANE].astype(BF16)
        gate = jnp.where(blk_id < i, _dot_nt(kmean, qm), neg_inf)
        sel = jnp.zeros((nblk, tq), F32)
        for _ in range(MOBA_TOPK):
            best = jnp.max(gate, axis=0, keepdims=True)
            first = jnp.min(jnp.where(gate == best, blk_id, nblk), axis=0, keepdims=True)
            hit = blk_id == first
            sel = jnp.where(hit, 1.0, sel)
            gate = jnp.where(hit, neg_inf, gate)
        sel_ref[h] = jnp.where(blk_id < i, sel, 0.0)

    causal = (lax.broadcasted_iota(jnp.int32, (blk, tq), 0)
              <= lax.broadcasted_iota(jnp.int32, (blk, tq), 1))

    head_row = lax.broadcasted_iota(jnp.int32, (heads, tq), 0)

    def score_matmuls(block):
        start = pl.multiple_of(block * blk, blk)
        kb = k_ref[pl.ds(start, blk), :]
        return [_dot_nt(kb[:, (h // per_group) * LANE:(h // per_group + 1) * LANE], qms[h])
                for h in range(heads)]

    def attend(scores, vtb, keeps, first):
        if not first:
            m_old = m_ref[...]
            acc_old = [acc_ref[h] for h in range(heads)]
        m_all = jnp.zeros((heads, tq), F32)
        probs, alphas = [], []
        for h in range(heads):
            if first:
                s = jnp.where(keeps[h], scores[h], neg_inf)
                m_new = jnp.max(s, axis=0, keepdims=True)
                shift = m_new
            else:
                s = scores[h]
                s_max = jnp.where(keeps[h], jnp.max(s, axis=0, keepdims=True), neg_inf)
                m_new = jnp.maximum(m_old[h:h + 1, :], s_max)
                alphas.append(jnp.exp(m_old[h:h + 1, :] - m_new))
                shift = jnp.where(keeps[h], m_new, jnp.inf)
            probs.append(jnp.exp(s - shift).astype(BF16))
            m_all = jnp.where(head_row == h, m_new, m_all)
        pvs = [_dot(vtb[h * ext:(h + 1) * ext, :], probs[h]) for h in range(heads)]
        m_ref[...] = m_all
        for h in range(heads):
            acc_ref[h] = pvs[h] if first else acc_old[h] * alphas[h] + pvs[h]

    start = pl.multiple_of(i * blk, blk)
    attend(score_matmuls(i), vt_ref[:, pl.ds(start, blk)], [causal] * heads, True)

    for h, s in enumerate(score_matmuls(0)):
        s_a[h] = s

    def step(n, read_scr, write_scr):
        for h, s in enumerate(score_matmuls(jnp.minimum(n + 1, jnp.maximum(i - 1, 0)))):
            write_scr[h] = s
        start = pl.multiple_of(n * blk, blk)
        keeps = [sel_ref[h, pl.ds(n, 1), :] > 0.0 for h in range(heads)]
        attend([read_scr[h] for h in range(heads)], vt_ref[:, pl.ds(start, blk)], keeps, False)

    def body(pair, _):
        step(2 * pair, s_a, s_b)

        @pl.when(2 * pair + 1 < i)
        def _():
            step(2 * pair + 1, s_b, s_a)
        return 0

    lax.fori_loop(0, (i + 1) // 2, body, 0)
    outs = [acc_ref[h, 0:dh, :] / acc_ref[h, dh:dh + 1, :] for h in range(heads)]
    o_ref[...] = jnp.concatenate(outs, axis=0).T.astype(BF16)


def _moba_prompt(q, kb, vt, kmean, batch, seq, dh):
    n, bw = q.shape
    tq = MOBA_BLOCK
    nblk = seq // MOBA_BLOCK
    heads = bw // dh
    km = kmean.reshape(batch, nblk, bw)
    tiles = seq // tq
    return pl.pallas_call(
        functools.partial(_moba_prompt_kernel, dh=dh),
        grid=(batch, tiles),
        in_specs=[pl.BlockSpec((tq, bw), lambda b, i: (b * tiles + i, 0)),
                  pl.BlockSpec((seq, bw), lambda b, i: (b, 0)),
                  pl.BlockSpec((None, vt.shape[1], seq), lambda b, i: (b, 0, 0)),
                  pl.BlockSpec((1, nblk, bw), lambda b, i: (b, 0, 0))],
        out_specs=pl.BlockSpec((tq, bw), lambda b, i: (b * tiles + i, 0)),
        out_shape=jax.ShapeDtypeStruct((n, bw), BF16),
        scratch_shapes=[pltpu.VMEM((heads, nblk, tq), F32), pltpu.VMEM((heads, tq), F32),
                        pltpu.VMEM((heads, dh + ONES_ROWS, tq), F32),
                        pltpu.VMEM((heads, MOBA_BLOCK, tq), F32), pltpu.VMEM((heads, MOBA_BLOCK, tq), F32)],
        compiler_params=_params("arbitrary", "arbitrary"),
        name="moba_prompt",
    )(q, kb, vt, km)


def _residual_proj_kernel(*refs, n_in, norm):
    x_ref = refs[0]
    in_refs = refs[1:1 + n_in]
    w_ref = refs[1 + n_in]
    y = x_ref[...]
    lo = 0
    for r in in_refs:
        y = y + _dot(r[...], w_ref[lo:lo + r.shape[1], :])
        lo += r.shape[1]
    if norm == "final":
        g_ref, o_ref = refs[2 + n_in:]
        o_ref[...] = _rmsnorm(y, g_ref[...])
    else:
        g_ref, o_ref, h_ref = refs[2 + n_in:]
        o_ref[...] = y
        h_ref[...] = _rmsnorm(y, g_ref[...]).astype(BF16)


def _residual_proj(x, ins, w, g, norm, tm, name):
    n, d = x.shape
    tm = min(tm, n)
    row = lambda width: pl.BlockSpec((tm, width), lambda i: (i, 0))
    out_specs = [row(d)] if norm == "final" else [row(d), row(d)]
    out_shape = ([jax.ShapeDtypeStruct((n, d), F32)] if norm == "final"
                 else [jax.ShapeDtypeStruct((n, d), F32), jax.ShapeDtypeStruct((n, d), BF16)])
    return pl.pallas_call(
        functools.partial(_residual_proj_kernel, n_in=len(ins), norm=norm),
        grid=(n // tm,),
        in_specs=[row(d)] + [row(a.shape[1]) for a in ins] + [_full(w.shape), _full((1, d))],
        out_specs=out_specs,
        out_shape=out_shape,
        compiler_params=_params("arbitrary"),
        name=name,
    )(x, *ins, w, g.reshape(1, d))


FFN_COLS = 256
FFN_HALO = SUBLANE_BF16


def _silu_gate(gate, value):
    return gate * (1.0 / (1.0 + jnp.exp(-gate))) * value


def _ffn_up_prompt_kernel(h_ref, halo_ref, w_ref, wc_ref, bc_ref, act_ref, tail_ref, up_scr, *, tiles_per_seq):
    tm = h_ref.shape[0]
    dff = act_ref.shape[1]
    seq_start = pl.program_id(0) % tiles_per_seq == 0
    halo = halo_ref[...]
    halo = jnp.where(seq_start, jnp.zeros_like(halo), halo)
    hh = jnp.concatenate([halo, h_ref[...]], axis=0)
    for c in range(dff // FFN_COLS):
        halves = []
        for k, lo in enumerate((c * FFN_COLS, dff + c * FFN_COLS)):
            cols = slice(lo, lo + FFN_COLS)
            up = _dot(hh, w_ref[:, cols])
            tail_ref[0, :, cols] = up[tm + FFN_HALO - SUBLANE:, :]
            up_scr[k] = up
            conv = bc_ref[:, cols] + up[FFN_HALO:, :] * wc_ref[CONV_W - 1:CONV_W, cols]
            for j in range(1, CONV_W):
                conv = conv + up_scr[k, pl.ds(FFN_HALO - j, tm), :] * wc_ref[CONV_W - 1 - j:CONV_W - j, cols]
            halves.append(conv)
        act_ref[:, c * FFN_COLS:(c + 1) * FFN_COLS] = _silu_gate(halves[1], halves[0]).astype(BF16)


def _ffn_up_decode_kernel(h_ref, s0_ref, s1_ref, w_ref, wc_ref, bc_ref, act_ref, up_ref):
    dff = act_ref.shape[1]
    h = h_ref[...]
    for c in range(dff // FFN_COLS):
        halves = []
        for lo in (c * FFN_COLS, dff + c * FFN_COLS):
            cols = slice(lo, lo + FFN_COLS)
            up = _dot(h, w_ref[:, cols])
            up_ref[:, cols] = up
            halves.append(bc_ref[:, cols] + s0_ref[:, cols] * wc_ref[0:1, cols]
                          + s1_ref[:, cols] * wc_ref[1:2, cols] + up * wc_ref[2:3, cols])
        act_ref[:, c * FFN_COLS:(c + 1) * FFN_COLS] = _silu_gate(halves[1], halves[0]).astype(BF16)


def _ffn_up_prompt(h, seq, w_up, w_conv, b_conv, tm):
    n, d = h.shape
    dff = w_up.shape[1] // 2
    assert dff % FFN_COLS == 0 and seq % tm == 0 and tm % FFN_HALO == 0
    ratio = tm // FFN_HALO
    act, tail = pl.pallas_call(
        functools.partial(_ffn_up_prompt_kernel, tiles_per_seq=seq // tm),
        grid=(n // tm,),
        in_specs=[pl.BlockSpec((tm, d), lambda i: (i, 0)),
                  pl.BlockSpec((FFN_HALO, d), lambda i: (jnp.maximum(i * ratio - 1, 0), 0)),
                  _full(w_up.shape), _full(w_conv.shape), _full((1, 2 * dff))],
        out_specs=[pl.BlockSpec((tm, dff), lambda i: (i, 0)),
                   pl.BlockSpec((1, SUBLANE, 2 * dff), lambda i: (i, 0, 0))],
        out_shape=[jax.ShapeDtypeStruct((n, dff), BF16),
                   jax.ShapeDtypeStruct((n // tm, SUBLANE, 2 * dff), F32)],
        scratch_shapes=[pltpu.VMEM((2, tm + FFN_HALO, FFN_COLS), F32)],
        compiler_params=_params("arbitrary"),
        name="ffn_up_prompt",
    )(h, h, w_up, w_conv, b_conv.reshape(1, -1))
    last = tail.reshape(n // seq, seq // tm, SUBLANE, 2 * dff)[:, -1, SUBLANE - (CONV_W - 1):, :]
    return act, last


def _ffn_up_decode(h, state, w_up, w_conv, b_conv):
    n, d = h.shape
    dff = w_up.shape[1] // 2
    act, up = pl.pallas_call(
        _ffn_up_decode_kernel,
        grid=(1,),
        in_specs=[_full((n, d)), _full((n, 2 * dff)), _full((n, 2 * dff)),
                  _full(w_up.shape), _full(w_conv.shape), _full((1, 2 * dff))],
        out_specs=[_full((n, dff)), _full((n, 2 * dff))],
        out_shape=[jax.ShapeDtypeStruct((n, dff), BF16), jax.ShapeDtypeStruct((n, 2 * dff), F32)],
        compiler_params=_params("arbitrary"),
        name="ffn_up_decode",
    )(h, state[:, 0], state[:, 1], w_up, w_conv, b_conv.reshape(1, -1))
    return act, jnp.stack([state[:, 1], up], axis=1)


def _odd_in_kernel(h_ref, w_ref, c64_ref, a64_ref, b64_ref, c128_ref, a128_ref, b128_ref,
                   q_ref, k_ref, v_ref, kb_ref, vt_ref, qi_ref, kif_ref, ki2_ref, wit_ref,
                   *, half64, half128, idx_dh):
    cq = q_ref.shape[1]
    ckv = kb_ref.shape[1]
    ciq = qi_ref.shape[1]
    h = h_ref[...]
    t64 = (c64_ref[...], a64_ref[...], b64_ref[...])
    t128 = (c128_ref[...], a128_ref[...], b128_ref[...])
    q_ref[...] = _rope_apply(_dot(h, w_ref[:, 0:cq]), *t128, half128).astype(BF16)
    k = _rope_apply(_dot(h, w_ref[:, cq:cq + ckv]), *t128, half128)
    kb_ref[...] = k.astype(BF16)
    v = _dot(h, w_ref[:, cq + ckv:cq + 2 * ckv])
    c_dh = ckv // C_KV_HEADS
    for kv in range(C_KV_HEADS):
        k_ref[pl.ds(kv, h.shape[0], stride=C_KV_HEADS), :] = k[:, kv * c_dh:(kv + 1) * c_dh]
        v_ref[pl.ds(kv, h.shape[0], stride=C_KV_HEADS), :] = v[:, kv * c_dh:(kv + 1) * c_dh]
    if vt_ref is not None:
        vt_ref[...] = _with_ones_rows(v.T, ckv // C_KV_HEADS).astype(BF16)
    lo = cq + 2 * ckv
    qi_ref[...] = _rope_apply(_dot(h, w_ref[:, lo:lo + ciq]), *t64, half64).astype(BF16)
    tail = _dot(h, w_ref[:, lo + ciq:lo + ciq + LANE])
    ki = _rope_apply(tail, *t64, half64)
    kif_ref[...] = ki
    lane = lax.broadcasted_iota(jnp.int32, ki.shape, 1)
    ki2_ref[...] = jnp.where(lane < idx_dh, ki, pltpu.roll(ki, idx_dh, 1)).astype(BF16)
    wit_ref[...] = tail.T[idx_dh:idx_dh + 8, :] * (IDX_HEADS ** -0.5)


def _odd_in(h, seq, w_in, tabs64, half64, tabs128, half128, c_dh, idx_dh, tm, with_vt):
    n, d = h.shape
    tm = min(tm, n)
    cq = C_HEADS * c_dh
    ckv = C_KV_HEADS * c_dh
    ciq = IDX_HEADS * idx_dh
    cols = cq + 2 * ckv + ciq + LANE
    assert 2 * idx_dh == LANE and w_in.shape[1] <= cols
    w = jnp.pad(w_in, ((0, 0), (0, cols - w_in.shape[1]))).astype(BF16)
    if seq is None:
        tab = pl.BlockSpec((1, LANE), lambda i: (0, 0))
    else:
        tiles_per_seq = seq // tm
        tab = pl.BlockSpec((tm, LANE), lambda i: (i % tiles_per_seq, 0))
    row = lambda width, dt: (pl.BlockSpec((tm, width), lambda i: (i, 0)), jax.ShapeDtypeStruct((n, width), dt))
    col = lambda height, dt: (pl.BlockSpec((height, tm), lambda i: (0, i)), jax.ShapeDtypeStruct((height, n), dt))
    pair_rows = (pl.BlockSpec((tm * C_KV_HEADS, c_dh), lambda i: (i, 0)),
                 jax.ShapeDtypeStruct((n * C_KV_HEADS, c_dh), F32))
    outs = [row(cq, BF16), pair_rows, pair_rows, row(ckv, BF16)]
    if with_vt:
        outs.append(col(ckv + C_KV_HEADS * ONES_ROWS, BF16))
    outs += [row(ciq, BF16), row(LANE, F32), row(LANE, BF16), col(8, F32)]

    def body(*refs):
        refs = list(refs)
        if not with_vt:
            refs.insert(8 + 4, None)
        _odd_in_kernel(*refs, half64=half64, half128=half128, idx_dh=idx_dh)

    res = pl.pallas_call(
        body,
        grid=(n // tm,),
        in_specs=[pl.BlockSpec((tm, d), lambda i: (i, 0)), _full(w.shape)] + [tab] * 6,
        out_specs=[o[0] for o in outs],
        out_shape=[o[1] for o in outs],
        compiler_params=_params("arbitrary"),
        name="odd_in_prompt" if with_vt else "odd_in_decode",
    )(h, w, *tabs64, *tabs128)
    res = list(res)
    if not with_vt:
        res.insert(4, None)
    return res


FLOAT_BITS = 32
INDEX_BIG = 2 ** 30


def _ind(cond):
    return jnp.where(cond, 1.0, 0.0)


def _ordered_bits(x, to):
    b = x if x.dtype == jnp.int32 else lax.bitcast_convert_type(x, jnp.int32)
    b = jnp.where(b < 0, b ^ 0x7FFFFFFF, b)
    return b if to == jnp.int32 else lax.bitcast_convert_type(b, F32)


def _ordered_bits16(x):
    b = lax.shift_right_arithmetic(lax.bitcast_convert_type(x, jnp.int32), 16)
    return jnp.where(b < 0, b ^ 0x7FFF, b)


def _from_ordered_bits16(k):
    b = jnp.where(k < 0, k ^ 0x7FFF, k)
    return lax.bitcast_convert_type(lax.shift_left(b, 16), F32)


def _bf16_floor(x):
    r = x.astype(BF16).astype(F32)
    b = lax.bitcast_convert_type(r, jnp.int32)
    below = lax.bitcast_convert_type(jnp.where(b < 0, b + 0x10000, b - 0x10000), F32)
    return jnp.where(r > x, below, r)


def _bisect_threshold(count_fn, shape, keep, coarse_count_fn=None):
    keep_f = jnp.float32(keep)

    def bisect(passes, pivot_count, st):
        def step(_, st):
            lo_k, hi_k, n_hi = st
            mid_k = (lax.shift_right_arithmetic(lo_k, 1) + lax.shift_right_arithmetic(hi_k, 1)
                     + (lo_k & hi_k & 1))
            cnt = pivot_count(mid_k)
            raise_lo = jnp.logical_and(mid_k > lo_k, cnt >= keep_f)
            lower_hi = jnp.logical_and(mid_k > lo_k, cnt < keep_f)
            return (jnp.where(raise_lo, mid_k, lo_k), jnp.where(lower_hi, mid_k, hi_k),
                    jnp.where(lower_hi, cnt, n_hi))
        return lax.fori_loop(0, passes, step, st)

    def fine_count(mid_k):
        return count_fn(lambda s, idx: (_ind(s >= _ordered_bits(mid_k, F32)),))[0]

    lo = jnp.full(shape, -jnp.inf, F32)
    hi = jnp.full(shape, jnp.inf, F32)
    n_hi = jnp.zeros(shape, F32)
    if coarse_count_fn is None:
        lo_k, _, n_gt = bisect(FLOAT_BITS, fine_count,
                               (_ordered_bits(lo, jnp.int32), _ordered_bits(hi, jnp.int32) + 1, n_hi))
    else:
        half = FLOAT_BITS // 2
        coarse = lambda mid_h: coarse_count_fn(_from_ordered_bits16(mid_h).astype(BF16))
        lo_h, _, n_hi = bisect(half, coarse, (_ordered_bits16(lo), _ordered_bits16(hi) + 1, n_hi))
        st = (_ordered_bits(_from_ordered_bits16(lo_h), jnp.int32),
              _ordered_bits(_from_ordered_bits16(lo_h + 1), jnp.int32), n_hi)
        lo_k, _, n_gt = bisect(half, fine_count, st)
    return _ordered_bits(lo_k, F32), n_gt


def _select_threshold(count_fn, active, keep, index_bits):
    thr, n_gt = _bisect_threshold(count_fn, active.shape, keep)
    n_eq, = count_fn(lambda s, idx: (_ind(s == thr),))
    need = jnp.float32(keep) - n_gt
    excess = jnp.where(active, jnp.where(n_eq > need, 1, 0), 0)

    def tie_search():
        lo_i = jnp.full(thr.shape, -1, jnp.int32)
        hi_i = jnp.full(thr.shape, (1 << index_bits) - 1, jnp.int32)
        for _ in range(index_bits + 1):
            mid_i = lax.shift_right_arithmetic(lo_i + hi_i, 1)
            n_low, = count_fn(lambda s, idx: (jnp.where(s == thr, _ind(idx <= mid_i), 0.0),))
            ok = n_low >= need
            hi_i = jnp.where(ok, mid_i, hi_i)
            lo_i = jnp.where(ok, lo_i, mid_i)
        return hi_i

    cut = lax.cond(jnp.max(excess) > 0, tie_search, lambda: jnp.full(thr.shape, INDEX_BIG, jnp.int32))
    thr = jnp.where(active, thr, -jnp.inf)
    cut = jnp.where(active, cut, -1)
    return thr, cut


def _selected(s, idx, thr, cut):
    return jnp.where(s > thr, 1.0, jnp.where(s == thr, _ind(idx <= cut), 0.0))


def _dsa_prompt_kernel(q_ref, qi_ref, wit_ref, kb_ref, vt_ref, ki2_ref, o_ref,
                       sc_ref, sb_ref, m_ref, acc_ref, *, c_dh, idx_dh, keep, kc):
    tq = q_ref.shape[0]
    seq = kb_ref.shape[0]
    t0 = pl.program_id(1) * tq
    nch = (t0 + tq + kc - 1) // kc
    scale = c_dh ** -0.5
    neg_inf = jnp.float32(-jnp.inf)
    lane = lax.broadcasted_iota(jnp.int32, (tq, LANE), 1)
    key_iota = lax.broadcasted_iota(jnp.int32, (kc, tq), 0)
    qpos = t0 + lax.broadcasted_iota(jnp.int32, (1, tq), 1)

    qi = qi_ref[...]
    per_group = LANE // idx_dh
    qms = []
    for h in range(IDX_HEADS):
        grp = qi[:, (h // per_group) * LANE:(h // per_group + 1) * LANE]
        qms.append(jnp.where(_head_of(lane, idx_dh) == h % per_group, grp, jnp.zeros_like(grp)))
    wit = wit_ref[...]

    def score_chunk(c, _):
        start = pl.multiple_of(c * kc, kc)
        kic = ki2_ref[pl.ds(start, kc), :]
        acc = jnp.zeros((kc, tq), F32)
        for h in range(IDX_HEADS):
            acc = acc + jnp.maximum(_dot_nt(kic, qms[h]), 0.0) * wit[h:h + 1, :]
        score = acc * (idx_dh ** -0.5)
        adm = start + key_iota <= qpos
        masked = jnp.where(adm, score, neg_inf)
        sc_ref[pl.ds(start, kc), :] = masked
        sb_ref[pl.ds(start, kc), :] = _bf16_floor(masked).astype(BF16)
        return 0

    lax.fori_loop(0, nch, score_chunk, 0)

    def count_fn(pred, n_out=1):
        def body(c, cnts):
            start = pl.multiple_of(c * kc, kc)
            hits = pred(sc_ref[pl.ds(start, kc), :], start + key_iota)
            return tuple(cnt + jnp.sum(hit.reshape(kc // SUBLANE, SUBLANE, tq), axis=0) for cnt, hit in zip(cnts, hits))
        parts = lax.fori_loop(0, nch, body, tuple(jnp.zeros((SUBLANE, tq), F32) for _ in range(n_out)))
        return tuple(jnp.sum(part, axis=0, keepdims=True) for part in parts)

    def coarse_count_fn(pivot):
        one, zero = jnp.ones((), BF16), jnp.zeros((), BF16)
        rows = SUBLANE_BF16

        def body(c, cnt):
            hit = jnp.where(sb_ref[pl.ds(pl.multiple_of(c * kc, kc), kc), :] >= pivot, one, zero)
            part = hit[0:rows]
            for r in range(1, kc // rows):
                part = part + hit[r * rows:(r + 1) * rows]
            return cnt + part.astype(F32)
        part = lax.fori_loop(0, nch, body, jnp.zeros((rows, tq), F32))
        return jnp.sum(part, axis=0, keepdims=True)

    thr, n_gt = _bisect_threshold(count_fn, (1, tq), keep, coarse_count_fn)
    active = qpos + 1 > keep
    thr = jnp.where(active, thr, neg_inf)
    need = jnp.where(active, jnp.float32(keep) - n_gt, 0.0)
    prefix_ones = jnp.where(lax.broadcasted_iota(jnp.int32, (kc, kc), 0)
                            >= lax.broadcasted_iota(jnp.int32, (kc, kc), 1), 1.0, 0.0).astype(BF16)

    m_ref[...] = jnp.full(m_ref.shape, NEG_BIG, F32)
    acc_ref[...] = jnp.zeros(acc_ref.shape, F32)
    group = C_HEADS // C_KV_HEADS
    ext = c_dh + ONES_ROWS
    exp2_scale = scale * LOG2_E
    head_row = lax.broadcasted_iota(jnp.int32, m_ref.shape, 0)

    def attend_chunk(c, ties_before):
        start = pl.multiple_of(c * kc, kc)
        sc = sc_ref[pl.ds(start, kc), :]
        tie = _ind(sc == thr)
        tie_rank = ties_before + _dot(prefix_ones, tie.astype(BF16))
        keepm = jnp.where(sc > thr, 1.0, jnp.where(tie_rank <= need, tie, 0.0)) > 0.0
        ties_before = tie_rank[kc - 1:kc, :]
        kbc = kb_ref[pl.ds(start, kc), :]
        vtc = vt_ref[:, pl.ds(start, kc)]
        m_old = m_ref[...]
        m_all = jnp.zeros(m_old.shape, F32)
        scores = [_dot_nt(kbc[:, (h // group) * c_dh:(h // group + 1) * c_dh], q_ref[:, h * c_dh:(h + 1) * c_dh])
                  for h in range(C_HEADS)]
        alphas, pvs = [], []
        for kv in range(C_KV_HEADS):
            probs = []
            for h in range(kv * group, (kv + 1) * group):
                s = jnp.where(keepm, scores[h], neg_inf)
                m_new = jnp.maximum(m_old[h:h + 1, :], jnp.max(s, axis=0, keepdims=True))
                alphas.append(jnp.exp2((m_old[h:h + 1, :] - m_new) * exp2_scale))
                probs.append(jnp.exp2((s - m_new) * exp2_scale).astype(BF16))
                m_all = jnp.where(head_row == h, m_new, m_all)
            pvs += [_dot(vtc[kv * ext:(kv + 1) * ext, :], p) for p in probs]
        m_ref[...] = m_all
        for h in range(C_HEADS):
            acc_ref[h] = acc_ref[h] * alphas[h] + pvs[h]
        return ties_before

    lax.fori_loop(0, nch, attend_chunk, jnp.zeros((1, tq), F32))
    for h in range(C_HEADS):
        out_t = acc_ref[h, 0:c_dh, :] / acc_ref[h, c_dh:c_dh + 1, :]
        o_ref[:, h * c_dh:(h + 1) * c_dh] = out_t.T.astype(BF16)


def _dsa_prompt(q, qi, wit, kb, vt, ki2, batch, seq, c_dh, idx_dh, keep, tq, kc):
    n = q.shape[0]
    tiles = seq // tq
    assert c_dh == LANE and seq % kc == 0 and seq % tq == 0
    return pl.pallas_call(
        functools.partial(_dsa_prompt_kernel, c_dh=c_dh, idx_dh=idx_dh, keep=keep, kc=kc),
        grid=(batch, tiles),
        in_specs=[pl.BlockSpec((tq, q.shape[1]), lambda b, i: (b * tiles + i, 0)),
                  pl.BlockSpec((tq, qi.shape[1]), lambda b, i: (b * tiles + i, 0)),
                  pl.BlockSpec((SUBLANE, tq), lambda b, i: (0, b * tiles + i)),
                  pl.BlockSpec((seq, kb.shape[1]), lambda b, i: (b, 0)),
                  pl.BlockSpec((vt.shape[0], seq), lambda b, i: (0, b)),
                  pl.BlockSpec((seq, LANE), lambda b, i: (b, 0))],
        out_specs=pl.BlockSpec((tq, q.shape[1]), lambda b, i: (b * tiles + i, 0)),
        out_shape=jax.ShapeDtypeStruct(q.shape, BF16),
        scratch_shapes=[pltpu.VMEM((seq, tq), F32), pltpu.VMEM((seq, tq), BF16), pltpu.VMEM((SUBLANE, tq), F32),
                        pltpu.VMEM((C_HEADS, c_dh + ONES_ROWS, tq), F32)],
        compiler_params=_params("arbitrary", "arbitrary"),
        name="dsa_prompt",
    )(q, qi, wit, kb, vt, ki2)


PAGE_ROWS = 128
DECODE_SCORE_SEQS = 4
DECODE_ATTN_SEQS = 4
DECODE_MOBA_SEQS = 2


def _lane_bcast_col(row):
    return jnp.broadcast_to(row, (LANE, row.shape[1])).T


def _moba_decode_kernel(pt_ref, q_ref, kn_ref, vn_ref, *rest, n_pages, dh, seqs):
    for g in range(seqs):
        _moba_decode_seq(g, q_ref, kn_ref, vn_ref, rest[g * n_pages:(g + 1) * n_pages],
                         rest[(seqs + g) * n_pages:(seqs + g + 1) * n_pages], rest[-1], n_pages, dh)


def _moba_decode_seq(g, q_ref, kn_ref, vn_ref, kt_refs, vt_refs, o_ref, n_pages, dh):
    bw = q_ref.shape[2]
    heads = bw // dh
    ppb = MOBA_BLOCK // PAGE_ROWS
    nblk = n_pages // ppb
    neg_inf = jnp.float32(-jnp.inf)
    qb = _lane_bcast_col(q_ref[g].astype(F32))

    def head_sums(x):
        return jnp.sum(x.reshape(heads, dh, LANE), axis=1)

    def head_rows(x):
        return jnp.broadcast_to(x[:, None, :], (heads, dh, LANE)).reshape(bw, LANE)

    s_pages = [head_sums(kt_refs[j][...] * qb) for j in range(n_pages)]
    s_own = head_sums(_lane_bcast_col(kn_ref[g]) * qb)
    gates = []
    for n in range(nblk):
        gate = jnp.zeros((heads, 1), F32)
        for j in range(n * ppb, (n + 1) * ppb):
            gate = gate + jnp.sum(s_pages[j], axis=1, keepdims=True)
        gates.append(gate)
    sels = []
    for n in range(nblk):
        rank = jnp.zeros((heads, 1), F32)
        for m in range(nblk):
            if m != n:
                beats = gates[m] >= gates[n] if m < n else gates[m] > gates[n]
                rank = rank + jnp.where(beats, 1.0, 0.0)
        sels.append(rank < MOBA_TOPK)
    m = s_own
    masked = []
    for j in range(n_pages):
        sm = jnp.where(sels[j // ppb], s_pages[j], neg_inf)
        masked.append(sm)
        m = jnp.maximum(m, jnp.max(sm, axis=1, keepdims=True))
    e_own = jnp.exp(s_own - m)
    es = [jnp.exp(sm - m) for sm in masked]
    l = e_own
    for e in es:
        l = l + jnp.sum(e, axis=1, keepdims=True)
    rinv = 1.0 / l
    lane = lax.broadcasted_iota(jnp.int32, (bw, LANE), 1)
    acc = jnp.where(lane == 0, head_rows(e_own * rinv) * _lane_bcast_col(vn_ref[g]), 0.0)
    for j in range(n_pages):
        acc = acc + head_rows(es[j] * rinv) * vt_refs[j][...]
    out_col = jnp.sum(acc, axis=1, keepdims=True)
    o_ref[g] = jnp.broadcast_to(out_col, (bw, LANE)).T[0:1].astype(BF16)


def _moba_decode(q, k_new, v_new, cache_k, cache_v, page_table, dh):
    n, bw = q.shape
    n_pages = page_table.shape[1]
    assert cache_k.shape[1] == PAGE_ROWS == LANE and (n_pages * PAGE_ROWS) % MOBA_BLOCK == 0
    ckt = jnp.transpose(cache_k, (0, 2, 3, 1)).reshape(cache_k.shape[0], bw, PAGE_ROWS)
    cvt = jnp.transpose(cache_v, (0, 2, 3, 1)).reshape(cache_v.shape[0], bw, PAGE_ROWS)
    g = DECODE_MOBA_SEQS
    assert n % g == 0
    seq_row = pl.BlockSpec((g, 1, bw), lambda b, pt: (b, 0, 0))
    pages = [pl.BlockSpec((None, bw, PAGE_ROWS), lambda b, pt, j=j, k=k: (pt[b * g + k, j], 0, 0))
             for k in range(g) for j in range(n_pages)]
    out = pl.pallas_call(
        functools.partial(_moba_decode_kernel, n_pages=n_pages, dh=dh, seqs=g),
        grid_spec=pltpu.PrefetchScalarGridSpec(
            num_scalar_prefetch=1, grid=(n // g,),
            in_specs=[seq_row, seq_row, seq_row] + pages + pages,
            out_specs=seq_row),
        out_shape=jax.ShapeDtypeStruct((n, 1, bw), BF16),
        compiler_params=_params("arbitrary"),
        name="moba_decode",
    )(page_table, q.reshape(n, 1, bw), k_new.reshape(n, 1, bw), v_new.reshape(n, 1, bw),
      *([ckt] * (g * n_pages)), *([cvt] * (g * n_pages)))
    return out.reshape(n, bw)


def _dsa_decode_score_kernel(pt_ref, qi_ref, wi_ref, kin_ref, *rest, n_pages, idx_dh, seqs):
    for g in range(seqs):
        _dsa_decode_score_seq(g, qi_ref, wi_ref, kin_ref, rest[g * n_pages:(g + 1) * n_pages], rest[-1],
                              n_pages, idx_dh)


def _dsa_decode_score_seq(g, qi_ref, wi_ref, kin_ref, kit_refs, o_ref, n_pages, idx_dh):
    qb = _lane_bcast_col(qi_ref[g].astype(F32))
    wi8 = wi_ref[g]
    scale = idx_dh ** -0.5

    def score(kit):
        total = jnp.zeros((1, LANE), F32)
        for h in range(IDX_HEADS):
            sc = jnp.sum(kit * qb[h * idx_dh:(h + 1) * idx_dh], axis=0, keepdims=True)
            total = total + jnp.maximum(sc, 0.0) * wi8[h:h + 1, :]
        return total * scale

    o_ref[g] = jnp.full(o_ref.shape[1:], -jnp.inf, F32)
    for j in range(n_pages):
        o_ref[g, j:j + 1, :] = score(kit_refs[j][...])
    own = score(_lane_bcast_col(kin_ref[g]))
    lane = lax.broadcasted_iota(jnp.int32, (1, PAGE_ROWS), 1)
    o_ref[g, n_pages:n_pages + 1, :] = jnp.where(lane == 0, own, -jnp.inf)


def _dsa_decode_select_kernel(s_ref, m_ref, *, keep, n_valid):
    s = s_ref[...]
    idx = lax.broadcasted_iota(jnp.int32, s.shape, 1)
    count_fn = lambda pred, n_out=1: tuple(jnp.sum(hit, axis=1, keepdims=True) for hit in pred(s, idx))
    active = jnp.full((s.shape[0], 1), n_valid > keep, jnp.bool_)
    thr, cut = _select_threshold(count_fn, active, keep, (s.shape[1] - 1).bit_length())
    m_ref[...] = _selected(s, idx, thr, cut)


def _dsa_decode_attn_kernel(pt_ref, q_ref, kn_ref, vn_ref, mask_ref, *rest, n_pages, c_dh, seqs):
    staged = [_dsa_decode_attn_scores(g, q_ref, kn_ref, mask_ref, rest[g * n_pages:(g + 1) * n_pages],
                                      n_pages, c_dh) for g in range(seqs)]
    for g in range(seqs):
        _dsa_decode_attn_values(g, staged[g], vn_ref, rest[(seqs + g) * n_pages:(seqs + g + 1) * n_pages],
                                rest[-1], n_pages, c_dh)


def _dsa_decode_attn_scores(g, q_ref, kn_ref, mask_ref, k_refs, n_pages, c_dh):
    rows = k_refs[0].shape[0]
    group = C_HEADS // C_KV_HEADS
    neg_inf = jnp.float32(-jnp.inf)
    qm = jnp.concatenate([q_ref[g].astype(F32), jnp.zeros((LANE - C_HEADS, c_dh), F32)], axis=0).astype(BF16)
    head_id = lax.broadcasted_iota(jnp.int32, (C_HEADS, rows), 0)
    row_id = lax.broadcasted_iota(jnp.int32, (C_HEADS, rows), 1)
    own_kv = (row_id & (C_KV_HEADS - 1)) == _head_of(head_id, group)
    mask = mask_ref[g]
    mask_pad = jnp.concatenate([mask, jnp.zeros((LANE - mask.shape[0], PAGE_ROWS), F32)], axis=0)
    key_id = lax.broadcasted_iota(jnp.int32, (PAGE_ROWS, rows), 0)
    key_of_row = _head_of(lax.broadcasted_iota(jnp.int32, (PAGE_ROWS, rows), 1), C_KV_HEADS)
    repeat = jnp.where(key_id == key_of_row, 1.0, 0.0).astype(BF16)
    row_mask = _dot(mask_pad.astype(BF16), repeat)
    kn = jnp.concatenate([kn_ref[g], jnp.zeros((SUBLANE - C_KV_HEADS, c_dh), F32)], axis=0)
    s_own_kv = _dot_nt(kn.astype(BF16), qm).T[0:C_HEADS, 0:SUBLANE]
    col_id = lax.broadcasted_iota(jnp.int32, (C_HEADS, SUBLANE), 1)
    own_col = col_id == _head_of(lax.broadcasted_iota(jnp.int32, (C_HEADS, SUBLANE), 0), group)
    own_on = jnp.where(own_col, mask[n_pages:n_pages + 1, 0:1], 0.0) > 0.0
    s_own = jnp.where(own_on, s_own_kv, neg_inf)
    m = jnp.maximum(jnp.max(s_own, axis=1, keepdims=True), NEG_BIG)
    masked = []
    for j in range(n_pages):
        s_t = _dot_nt(k_refs[j][...].astype(BF16), qm).T[0:C_HEADS]
        sm = jnp.where(own_kv, jnp.where(row_mask[j:j + 1, :] > 0.5, s_t, neg_inf), neg_inf)
        masked.append(sm)
        m = jnp.maximum(m, jnp.max(sm, axis=1, keepdims=True))
    return masked, s_own, m


def _dsa_decode_attn_values(g, staged, vn_ref, v_refs, o_ref, n_pages, c_dh):
    masked, s_own, m = staged
    scale = c_dh ** -0.5
    e_own = jnp.exp((s_own - m) * scale)
    e_own = jnp.concatenate([e_own, jnp.zeros((C_HEADS, LANE - SUBLANE), F32)], axis=1)
    e_all = jnp.concatenate([jnp.exp((sm - m) * scale) for sm in masked] + [e_own], axis=1)
    e_all = jnp.concatenate([e_all, jnp.zeros((SUBLANE_BF16 - C_HEADS, e_all.shape[1]), F32)], axis=0).astype(BF16)
    v_own = jnp.concatenate([vn_ref[g], jnp.zeros((LANE - C_KV_HEADS, c_dh), F32)], axis=0)
    v_all = jnp.concatenate([v_refs[j][...] for j in range(n_pages)] + [v_own], axis=0).astype(BF16)
    v_all = jnp.concatenate([v_all, jnp.ones(v_all.shape, BF16)], axis=1)
    out = _dot(e_all, v_all)
    o_ref[g] = (out[0:C_HEADS, 0:c_dh] / out[0:C_HEADS, c_dh:2 * c_dh]).astype(BF16)


def _dsa_decode(q, qi, wit, k_new, v_new, ki_new, cache_k, cache_v, cache_ki, page_table, c_dh, idx_dh, keep):
    n = q.shape[0]
    n_pages = page_table.shape[1]
    assert cache_k.shape[1] == PAGE_ROWS == LANE
    pool = cache_k.shape[0]
    ck = cache_k.reshape(pool, PAGE_ROWS * C_KV_HEADS, c_dh)
    cv = cache_v.reshape(pool, PAGE_ROWS * C_KV_HEADS, c_dh)
    ckit = jnp.transpose(cache_ki, (0, 2, 1))
    rows = -(-(n_pages + 1) // SUBLANE) * SUBLANE
    wi8 = wit.T.reshape(n, 8, 1)

    def seq3(g, a, b):
        return pl.BlockSpec((g, a, b), lambda s, pt: (s, 0, 0))

    def pages(g, r, w):
        return [pl.BlockSpec((None, r, w), lambda s, pt, j=j, k=k: (pt[s * g + k, j], 0, 0))
                for k in range(g) for j in range(n_pages)]

    gs = DECODE_SCORE_SEQS
    assert n % gs == 0 and n % DECODE_ATTN_SEQS == 0
    scores = pl.pallas_call(
        functools.partial(_dsa_decode_score_kernel, n_pages=n_pages, idx_dh=idx_dh, seqs=gs),
        grid_spec=pltpu.PrefetchScalarGridSpec(
            num_scalar_prefetch=1, grid=(n // gs,),
            in_specs=[seq3(gs, 1, IDX_HEADS * idx_dh), seq3(gs, 8, 1), seq3(gs, 1, idx_dh)]
                     + pages(gs, idx_dh, PAGE_ROWS),
            out_specs=seq3(gs, rows, PAGE_ROWS)),
        out_shape=jax.ShapeDtypeStruct((n, rows, PAGE_ROWS), F32),
        compiler_params=_params("arbitrary"),
        name="dsa_decode_scores",
    )(page_table, qi.reshape(n, 1, IDX_HEADS * idx_dh), wi8, ki_new.reshape(n, 1, idx_dh),
      *([ckit] * (gs * n_pages)))
    flat = rows * PAGE_ROWS
    mask = pl.pallas_call(
        functools.partial(_dsa_decode_select_kernel, keep=keep, n_valid=n_pages * PAGE_ROWS + 1),
        grid=(1,),
        in_specs=[_full((n, flat))],
        out_specs=_full((n, flat)),
        out_shape=jax.ShapeDtypeStruct((n, flat), F32),
        compiler_params=_params("arbitrary"),
        name="dsa_decode_select",
    )(scores.reshape(n, flat))
    cq = C_HEADS * c_dh
    ga = DECODE_ATTN_SEQS
    kv_rows = PAGE_ROWS * C_KV_HEADS
    out = pl.pallas_call(
        functools.partial(_dsa_decode_attn_kernel, n_pages=n_pages, c_dh=c_dh, seqs=ga),
        grid_spec=pltpu.PrefetchScalarGridSpec(
            num_scalar_prefetch=1, grid=(n // ga,),
            in_specs=[seq3(ga, C_HEADS, c_dh), seq3(ga, C_KV_HEADS, c_dh), seq3(ga, C_KV_HEADS, c_dh),
                      seq3(ga, rows, PAGE_ROWS)] + pages(ga, kv_rows, c_dh) + pages(ga, kv_rows, c_dh),
            out_specs=seq3(ga, C_HEADS, c_dh)),
        out_shape=jax.ShapeDtypeStruct((n, C_HEADS, c_dh), BF16),
        compiler_params=_params("arbitrary"),
        name="dsa_decode_attn",
    )(page_table, q.reshape(n, C_HEADS, c_dh), k_new.reshape(n, C_KV_HEADS, c_dh),
      v_new.reshape(n, C_KV_HEADS, c_dh), mask.reshape(n, rows, PAGE_ROWS),
      *([ck] * (ga * n_pages)), *([cv] * (ga * n_pages)))
    return out.reshape(n, cq)


def kernel(x_prompt, x_sample, cache_moba_k, cache_moba_v, cache_dsa_k, cache_dsa_v, cache_dsa_kidx,
           state_ffn_conv, page_table, w_in_even, g_sgu, w_sgu, b_sgu, w_out_even, w_in_odd, w_out_odd,
           g_mix, g_ffn, w_up, w_conv, b_conv, w_down, g_final):
    batch, seq, d = x_prompt.shape
    n_dec, dec_seq, _ = x_sample.shape
    depth = g_mix.shape[0]
    n_pages = page_table.shape[1]
    past_len = n_pages * cache_moba_k.shape[2]
    assert depth == 2 and dec_seq == 1, "one even + one odd layer, one decode token per sequence"
    b_dh = cache_moba_k.shape[-1]
    c_dh = cache_dsa_k.shape[-1]
    idx_dh = cache_dsa_kidx.shape[-1]
    keep_p = min(DSA_TOPK, seq // 4)
    keep_s = min(DSA_TOPK, (past_len + dec_seq) // 4)
    xp = x_prompt.reshape(batch * seq, d)
    xs = x_sample.reshape(n_dec, d)
    bf = lambda w: w.astype(BF16)

    tp64, half64 = _rope_tables(b_dh, seq, 0)
    ts64, _ = _rope_tables(b_dh, 1, past_len)
    tp128, half128 = _rope_tables(c_dh, seq, 0)
    ts128, _ = _rope_tables(c_dh, 1, past_len)
    assert idx_dh == b_dh

    w_in0, w_out0 = bf(w_in_even[0]), bf(w_out_even[0])
    a_p, q_p, kb_p, mkt_p, mvt_p, vt_p, km_p = _even_in_prompt(
        xp, seq, g_mix[0], w_in0, tp64, half64, g_sgu[0], w_sgu[0], b_sgu[0])
    bo_p = _moba_prompt(q_p, kb_p, vt_p, km_p, batch, seq, b_dh)
    cache_view = lambda t: jnp.transpose(t.reshape(batch, B_HEADS, b_dh, seq), (0, 3, 1, 2))[None]
    a_s, q_s, mk_s, mv_s, va_s = _even_in_decode(xs, g_mix[0], w_in0, ts64, half64, g_sgu[0], w_sgu[0], b_sgu[0])
    bo_s = _moba_decode(q_s, mk_s, mv_s, cache_moba_k[0], cache_moba_v[0], page_table, b_dh)
    w_up0, w_down0 = bf(w_up[0]), bf(w_down[0])
    xp, hp = _residual_proj(xp, [a_p, bo_p], w_out0, g_ffn[0], "next", ROW_TILE, "out_even_prompt")
    xs, hs = _residual_proj(xs, [a_s, bo_s], w_out0, g_ffn[0], "next", ROW_TILE, "out_even_decode")
    act_p, conv_p0 = _ffn_up_prompt(hp, seq, w_up0, w_conv[0], b_conv[0], ROW_TILE)
    act_s, conv_s0 = _ffn_up_decode(hs, state_ffn_conv[0], w_up0, w_conv[0], b_conv[0])
    xp, hp = _residual_proj(xp, [act_p], w_down0, g_mix[1], "next", ROW_TILE, "ffn_down0_prompt")
    xs, hs = _residual_proj(xs, [act_s], w_down0, g_mix[1], "next", ROW_TILE, "ffn_down0_decode")

    w_out1 = bf(w_out_odd[0])
    q_p, dk_p, dv_p, kb_p, vt_p, qi_p, kif_p, ki2_p, wit_p = _odd_in(
        hp, seq, w_in_odd[0], tp64, half64, tp128, half128, c_dh, idx_dh, ROW_TILE, True)
    o_p = _dsa_prompt(q_p, qi_p, wit_p, kb_p, vt_p, ki2_p, batch, seq, c_dh, idx_dh, keep_p, DSA_QUERY_TILE, DSA_KEY_CHUNK)
    q_s, dk_s, dv_s, _, _, qi_s, kif_s, _, wit_s = _odd_in(
        hs, None, w_in_odd[0], ts64, half64, ts128, half128, c_dh, idx_dh, 256, False)
    o_s = _dsa_decode(q_s, qi_s, wit_s, dk_s, dv_s, kif_s[:, :idx_dh], cache_dsa_k[0], cache_dsa_v[0],
                      cache_dsa_kidx[0], page_table, c_dh, idx_dh, keep_s)
    w_up1, w_down1 = bf(w_up[1]), bf(w_down[1])
    xp, hp = _residual_proj(xp, [o_p], w_out1, g_ffn[1], "next", ROW_TILE, "out_odd_prompt")
    xs, hs = _residual_proj(xs, [o_s], w_out1, g_ffn[1], "next", ROW_TILE, "out_odd_decode")
    act_p, conv_p1 = _ffn_up_prompt(hp, seq, w_up1, w_conv[1], b_conv[1], ROW_TILE)
    act_s, conv_s1 = _ffn_up_decode(hs, state_ffn_conv[1], w_up1, w_conv[1], b_conv[1])
    y_p, = _residual_proj(xp, [act_p], w_down1, g_final, "final", ROW_TILE, "ffn_down1_prompt")
    y_s, = _residual_proj(xs, [act_s], w_down1, g_final, "final", ROW_TILE, "ffn_down1_decode")

    ckv = C_KV_HEADS
    return (y_p.reshape(batch, seq, d), y_s.reshape(n_dec, dec_seq, d),
            cache_view(mkt_p), cache_view(mvt_p),
            mk_s.reshape(1, n_dec, dec_seq, B_HEADS, b_dh), mv_s.reshape(1, n_dec, dec_seq, B_HEADS, b_dh),
            va_s.reshape(1, n_dec, dec_seq, A_GROUPS, LANE),
            dk_p.reshape(1, batch, seq, ckv, c_dh), dv_p.reshape(1, batch, seq, ckv, c_dh),
            kif_p[:, :idx_dh].reshape(1, batch, seq, idx_dh),
            dk_s.reshape(1, n_dec, dec_seq, ckv, c_dh), dv_s.reshape(1, n_dec, dec_seq, ckv, c_dh),
            kif_s[:, :idx_dh].reshape(1, n_dec, dec_seq, idx_dh),
            jnp.stack([conv_p0, conv_p1]), jnp.stack([conv_s0, conv_s1]))
```
